```python
import math
import jax, jax.numpy as jnp
from jax import lax
import numpy as np

D_MODEL = 1024
BATCH = 8
SEQ = 2048
DEPTH = 1

N_ATTN_HEADS = 4
ATTN_HEAD_DIM = 64
ATTN_V_DIM = 2 * ATTN_HEAD_DIM
ATTN_WIDTH = N_ATTN_HEADS * ATTN_V_DIM
LRU_WIDTH = D_MODEL - ATTN_WIDTH
LRU_BLOCKS = 8
LRU_BLOCK_DIM = LRU_WIDTH // LRU_BLOCKS
LRU_CONV_WIDTH = 4
LRU_CONV_LEFT = 2
LRU_C = 8.0
N_DIRECTIONS = 2
IN_WIDTH = 3 * ATTN_WIDTH + 2 * LRU_WIDTH
D_FF = 2816
FFN_CONV_WIDTH = 3
FFN_CONV_LEFT = (FFN_CONV_WIDTH - 1) // 2
Q_BLOCK = 128
NORM_EPS = 1e-6

kernel_name = "hymba_diffattn_rglru_convglu_encoder"


def rms_norm(x, g):
    xf = x.astype(jnp.float32)
    y = xf * lax.rsqrt(jnp.mean(xf * xf, axis=-1, keepdims=True) + NORM_EPS)
    return (y * g.astype(jnp.float32)).astype(x.dtype)


def depthwise_conv(x, w, b, left):
    width = w.shape[0]
    s = x.shape[1]
    xp = jnp.pad(x, ((0, 0), (left, width - 1 - left), (0, 0)))
    out = b
    for tap in range(width):
        out = out + xp[:, tap:tap + s] * w[tap]
    return out


def diff_attention(q, k, v, lam, lambda_init, subln_g):
    b, s, _ = q.shape
    qh = q.reshape(b, s, N_ATTN_HEADS, 2, ATTN_HEAD_DIM)
    kh = k.reshape(b, s, N_ATTN_HEADS, 2, ATTN_HEAD_DIM)
    vh = v.reshape(b, s, N_ATTN_HEADS, ATTN_V_DIM).astype(jnp.float32)
    scale = ATTN_HEAD_DIM ** -0.5
    slopes = jnp.exp2(-8.0 * jnp.arange(1, N_ATTN_HEADS + 1, dtype=jnp.float32) / N_ATTN_HEADS)
    key_pos = jnp.arange(s)

    def block(start):
        qb = lax.dynamic_slice_in_dim(qh, start, Q_BLOCK, axis=1)
        sc = jnp.einsum('bqhcd,bkhcd->bhcqk', qb, kh).astype(jnp.float32) * scale
        dist = jnp.abs(start + jnp.arange(Q_BLOCK)[:, None] - key_pos[None, :]).astype(jnp.float32)
        sc = sc - slopes[:, None, None, None] * dist
        p = jax.nn.softmax(sc, axis=-1)
        w = p[:, :, 0] - lam * p[:, :, 1]
        return jnp.einsum('bhqk,bkhe->bqhe', w, vh)

    starts = jnp.arange(s // Q_BLOCK) * Q_BLOCK
    o = lax.map(block, starts)
    o = jnp.moveaxis(o, 0, 1).reshape(b, s, N_ATTN_HEADS, ATTN_V_DIM)
    o = rms_norm(o, subln_g) * (1.0 - lambda_init)
    return o.reshape(b, s, ATTN_WIDTH).astype(q.dtype)


def _linear_combine(c1, c2):
    a1, b1 = c1
    a2, b2 = c2
    return a1 * a2, a2 * b1 + b2


def rg_lru(xc, w_a, b_a, w_x, b_x, lru_lambda, reverse):
    b, s, c = xc.shape
    xb = xc.reshape(b, s, LRU_BLOCKS, LRU_BLOCK_DIM)
    r = jax.nn.sigmoid(jnp.einsum('bsni,nij->bsnj', xb, w_a.astype(jnp.float32)).reshape(b, s, c)
                       + b_a.astype(jnp.float32))
    i = jax.nn.sigmoid(jnp.einsum('bsni,nij->bsnj', xb, w_x.astype(jnp.float32)).reshape(b, s, c)
                       + b_x.astype(jnp.float32))
    log_a = -LRU_C * r * jax.nn.softplus(-lru_lambda.astype(jnp.float32))
    a = jnp.exp(log_a)
    u = jnp.sqrt(-jnp.expm1(2.0 * log_a)) * (i * xc)
    _, h = lax.associative_scan(_linear_combine, (a, u), axis=1, reverse=reverse)
    return h


def recurrent_group(xr, gr, conv_w, conv_b, w_a, b_a, w_x, b_x, lru_lambda):
    xc = depthwise_conv(xr, conv_w, conv_b, LRU_CONV_LEFT).astype(jnp.float32)
    y = (rg_lru(xc, w_a[0], b_a[0], w_x[0], b_x[0], lru_lambda[0], reverse=False)
         + rg_lru(xc, w_a[1], b_a[1], w_x[1], b_x[1], lru_lambda[1], reverse=True))
    return (jax.nn.gelu(gr.astype(jnp.float32)) * y).astype(xr.dtype)


def conv_glu_ffn(h, w_up, conv_w, conv_b, w_down):
    u = depthwise_conv(h @ w_up, conv_w, conv_b, FFN_CONV_LEFT)
    gate, val = jnp.split(u, 2, axis=-1)
    return (jax.nn.gelu(gate) * val) @ w_down


def setup_inputs(seed: int = 0) -> dict:
    key = jax.random.key(seed)
    ks = jax.random.split(key, 24)
    f32 = jnp.float32

    def nrm(k, shape, scale):
        return jax.random.normal(k, shape, f32) * scale

    a0 = jax.random.uniform(ks[12], (DEPTH, N_DIRECTIONS, LRU_WIDTH), f32, 0.9, 0.999)
    s0 = a0 ** (1.0 / LRU_C)
    lru_lambda = jnp.log(s0) - jnp.log1p(-s0)
    return {
        "x": jax.random.normal(ks[0], (BATCH, SEQ, D_MODEL), f32),
        "attn_norm_g": 1.0 + nrm(ks[1], (DEPTH, D_MODEL), 0.02),
        "w_in": nrm(ks[2], (DEPTH, D_MODEL, IN_WIDTH), D_MODEL ** -0.5),
        "lambda_q1": nrm(ks[3], (DEPTH, ATTN_HEAD_DIM), 0.1),
        "lambda_k1": nrm(ks[4], (DEPTH, ATTN_HEAD_DIM), 0.1),
        "lambda_q2": nrm(ks[5], (DEPTH, ATTN_HEAD_DIM), 0.1),
        "lambda_k2": nrm(ks[6], (DEPTH, ATTN_HEAD_DIM), 0.1),
        "subln_g": 1.0 + nrm(ks[7], (DEPTH, ATTN_V_DIM), 0.02),
        "lru_conv_w": nrm(ks[8], (DEPTH, LRU_CONV_WIDTH, LRU_WIDTH), LRU_CONV_WIDTH ** -0.5),
        "lru_conv_b": nrm(ks[9], (DEPTH, LRU_WIDTH), 0.02),
        "lru_w_a": nrm(ks[10], (DEPTH, N_DIRECTIONS, LRU_BLOCKS, LRU_BLOCK_DIM, LRU_BLOCK_DIM), LRU_BLOCK_DIM ** -0.5),
        "lru_b_a": nrm(ks[11], (DEPTH, N_DIRECTIONS, LRU_WIDTH), 0.1),
        "lru_w_x": nrm(ks[13], (DEPTH, N_DIRECTIONS, LRU_BLOCKS, LRU_BLOCK_DIM, LRU_BLOCK_DIM), LRU_BLOCK_DIM ** -0.5),
        "lru_b_x": nrm(ks[14], (DEPTH, N_DIRECTIONS, LRU_WIDTH), 0.1),
        "lru_lambda": lru_lambda,
        "w_out": nrm(ks[15], (DEPTH, D_MODEL, D_MODEL), D_MODEL ** -0.5),
        "ffn_norm_g": 1.0 + nrm(ks[16], (DEPTH, D_MODEL), 0.02),
        "w_up": nrm(ks[17], (DEPTH, D_MODEL, 2 * D_FF), D_MODEL ** -0.5),
        "ffn_conv_w": nrm(ks[18], (DEPTH, FFN_CONV_WIDTH, 2 * D_FF), FFN_CONV_WIDTH ** -0.5),
        "ffn_conv_b": nrm(ks[19], (DEPTH, 2 * D_FF), 0.02),
        "w_down": nrm(ks[20], (DEPTH, D_FF, D_MODEL), D_FF ** -0.5),
        "final_norm_g": 1.0 + nrm(ks[21], (D_MODEL,), 0.02),
    }


def reference(x, attn_norm_g, w_in, lambda_q1, lambda_k1, lambda_q2, lambda_k2, subln_g,
              lru_conv_w, lru_conv_b, lru_w_a, lru_b_a, lru_w_x, lru_b_x, lru_lambda,
              w_out, ffn_norm_g, w_up, ffn_conv_w, ffn_conv_b, w_down, final_norm_g):
    for l in range(DEPTH):
        h = rms_norm(x, attn_norm_g[l])
        proj = h @ w_in[l]
        q, k, v, xr, gr = jnp.split(
            proj, [ATTN_WIDTH, 2 * ATTN_WIDTH, 3 * ATTN_WIDTH, 3 * ATTN_WIDTH + LRU_WIDTH], axis=-1)
        lambda_init = 0.8 - 0.6 * math.exp(-0.3 * l)
        lam = (jnp.exp(jnp.sum(lambda_q1[l].astype(jnp.float32) * lambda_k1[l].astype(jnp.float32)))
               - jnp.exp(jnp.sum(lambda_q2[l].astype(jnp.float32) * lambda_k2[l].astype(jnp.float32)))
               + lambda_init)
        attn_out = diff_attention(q, k, v, lam, lambda_init, subln_g[l])
        lru_out = recurrent_group(xr, gr, lru_conv_w[l], lru_conv_b[l], lru_w_a[l], lru_b_a[l],
                                  lru_w_x[l], lru_b_x[l], lru_lambda[l])
        x = x + jnp.concatenate([attn_out, lru_out], axis=-1) @ w_out[l]
        x = x + conv_glu_ffn(rms_norm(x, ffn_norm_g[l]), w_up[l], ffn_conv_w[l], ffn_conv_b[l], w_down[l])
    return rms_norm(x, final_norm_g)
```

```python
import functools
import math

import jax
import jax.numpy as jnp
from jax import lax
from jax.experimental import pallas as pl
from jax.experimental.pallas import tpu as pltpu

F32 = jnp.float32
BF16 = jnp.bfloat16

N_HEADS = 4
HEAD_DIM = 64
V_DIM = 2 * HEAD_DIM
ATTN_WIDTH = N_HEADS * V_DIM
LRU_BLOCK = 64
LRU_CONV_WIDTH = 4
LRU_CONV_LEFT = 2
LRU_C = 8.0
FFN_CONV_LEFT = 1
NORM_EPS = 1e-6
LANES = 128
SUBLANES = 8
BF16_ROWS = 16
VMEM_LIMIT = 56 * 1024 * 1024


def _rms(x, g):
    return (x * lax.rsqrt(jnp.mean(x * x, axis=-1, keepdims=True) + NORM_EPS)) * g


def _dot(a, b):
    return jnp.dot(a, b, preferred_element_type=F32)


def _inproj_kernel(x_ref, g_ref, w_ref, q_ref, k_ref, v_ref, xr_ref, gr_ref):
    hb = _rms(x_ref[...], g_ref[...]).astype(BF16)
    aw = ATTN_WIDTH
    q_ref[...] = (_dot(hb, w_ref[:, 0:aw]) * (HEAD_DIM ** -0.5)).astype(BF16)
    k_ref[...] = _dot(hb, w_ref[:, aw:2 * aw]).astype(BF16)
    v_ref[...] = _dot(hb, w_ref[:, 2 * aw:3 * aw]).astype(BF16)
    lw = xr_ref.shape[-1]
    xr_ref[...] = _dot(hb, w_ref[:, 3 * aw:3 * aw + lw])
    gr_ref[...] = _dot(hb, w_ref[:, 3 * aw + lw:3 * aw + 2 * lw])


def _inproj(x2, g, w_bf, tm):
    n, d = x2.shape
    lw = (w_bf.shape[1] - 3 * ATTN_WIDTH) // 2
    row = lambda i: (i, 0)
    const = lambda i: (0, 0)
    return pl.pallas_call(
        _inproj_kernel,
        grid=(n // tm,),
        in_specs=[pl.BlockSpec((tm, d), row),
                  pl.BlockSpec((1, d), const),
                  pl.BlockSpec(w_bf.shape, const)],
        out_specs=[pl.BlockSpec((tm, ATTN_WIDTH), row)] * 3 + [pl.BlockSpec((tm, lw), row)] * 2,
        out_shape=[jax.ShapeDtypeStruct((n, ATTN_WIDTH), BF16)] * 3
        + [jax.ShapeDtypeStruct((n, lw), F32)] * 2,
        compiler_params=pltpu.CompilerParams(
            dimension_semantics=("arbitrary",), vmem_limit_bytes=VMEM_LIMIT),
        name="inproj",
    )(x2, g, w_bf)


def _attn_kernel(lq1_ref, lk1_ref, lq2_ref, lk2_ref, sg_ref, q_ref, k_ref, v_ref, o_ref,
                 *, tq, seq, lambda_init):
    h = pl.program_id(1)
    qi = pl.program_id(2)
    lam = (jnp.exp(jnp.sum(lq1_ref[...] * lk1_ref[...], axis=-1, keepdims=True))
           - jnp.exp(jnp.sum(lq2_ref[...] * lk2_ref[...], axis=-1, keepdims=True))
           + lambda_init)

    q = q_ref[...]
    k = k_ref[...]
    lane = lax.broadcasted_iota(jnp.int32, q.shape, 1)
    zero = jnp.zeros_like(q)
    nt = (((1,), (1,)), ((), ()))
    s1 = lax.dot_general(jnp.where(lane < HEAD_DIM, q, zero), k, nt, preferred_element_type=F32)
    s2 = lax.dot_general(jnp.where(lane >= HEAD_DIM, q, zero), k, nt, preferred_element_type=F32)

    slope = jnp.where(h == 0, 2.0 ** -2, jnp.where(h == 1, 2.0 ** -4,
                      jnp.where(h == 2, 2.0 ** -6, 2.0 ** -8))).astype(F32)
    row = qi * tq + lax.broadcasted_iota(jnp.int32, (tq, 1), 0)
    col = lax.broadcasted_iota(jnp.int32, (1, seq), 1)
    bias = slope * jnp.abs(row - col).astype(F32)

    s1 = s1 - bias
    s2 = s2 - bias
    e1 = jnp.exp(s1 - jnp.max(s1, axis=-1, keepdims=True))
    e2 = jnp.exp(s2 - jnp.max(s2, axis=-1, keepdims=True))
    c1 = 1.0 / jnp.sum(e1, axis=-1, keepdims=True)
    c2 = lam / jnp.sum(e2, axis=-1, keepdims=True)
    w = (e1 * c1 - e2 * c2).astype(BF16)
    o = _dot(w, v_ref[...])
    o_ref[...] = (_rms(o, sg_ref[...]) * (1.0 - lambda_init)).astype(o_ref.dtype)


def _attention(q, k, v, lq1, lk1, lq2, lk2, subln_g, lambda_init, tq):
    b, s, _ = q.shape
    vec = lambda bi, h, qi: (0, 0)
    kernel = functools.partial(_attn_kernel, tq=tq, seq=s, lambda_init=lambda_init)
    return pl.pallas_call(
        kernel,
        grid=(b, N_HEADS, s // tq),
        in_specs=[pl.BlockSpec((1, HEAD_DIM), vec)] * 4
        + [pl.BlockSpec((1, V_DIM), vec),
           pl.BlockSpec((None, tq, V_DIM), lambda bi, h, qi: (bi, qi, h)),
           pl.BlockSpec((None, s, V_DIM), lambda bi, h, qi: (bi, 0, h)),
           pl.BlockSpec((None, s, V_DIM), lambda bi, h, qi: (bi, 0, h))],
        out_specs=pl.BlockSpec((None, tq, V_DIM), lambda bi, h, qi: (bi, qi, h)),
        out_shape=jax.ShapeDtypeStruct((b, s, ATTN_WIDTH), BF16),
        compiler_params=pltpu.CompilerParams(
            dimension_semantics=("arbitrary",) * 3, vmem_limit_bytes=VMEM_LIMIT),
        name="attn",
    )(lq1, lk1, lq2, lk2, subln_g, q, k, v)


def _local_scan(a, u, reverse):
    row = lax.broadcasted_iota(jnp.int32, a.shape, 0)
    for d in (1, 2, 4):
        shift = SUBLANES - d if reverse else d
        valid = (row < SUBLANES - d) if reverse else (row >= d)
        a_s = jnp.where(valid, pltpu.roll(a, shift, 0), 1.0)
        u_s = jnp.where(valid, pltpu.roll(u, shift, 0), 0.0)
        u = a * u_s + u
        a = a * a_s
    return a, u


def _lru_kernel(xr_ref, gr_ref, cw_ref, cb_ref, wg_ref, bg_ref, lam_ref, o_ref,
                xp_ref, af_ref, uf_ref, ab_ref, ub_ref, *, seq, rows):
    width = xr_ref.shape[-1]
    pad = SUBLANES
    npair = width // LANES

    xp_ref[0:pad, :] = jnp.zeros((pad, width), F32)
    xp_ref[pad + seq:2 * pad + seq, :] = jnp.zeros((pad, width), F32)
    xp_ref[pad:pad + seq, :] = xr_ref[...]

    neg_lam = -lam_ref[...]
    softplus = jnp.maximum(neg_lam, 0.0) + jnp.log1p(jnp.exp(-jnp.abs(neg_lam)))
    decay = -LRU_C * softplus

    for c in range(seq // rows):
        r0 = c * rows
        xc = cb_ref[...]
        for tap in range(LRU_CONV_WIDTH):
            start = r0 + pad + tap - LRU_CONV_LEFT
            xc = xc + xp_ref[start:start + rows, :] * cw_ref[tap:tap + 1, :]
        xcb = xc.astype(BF16)
        for p in range(npair):
            cols = slice(p * LANES, (p + 1) * LANES)
            gates = _dot(xcb[:, cols], wg_ref[p]) + bg_ref[p:p + 1, :]
            xcp = xc[:, cols]
            for d, (a_ref, u_ref) in enumerate(((af_ref, uf_ref), (ab_ref, ub_ref))):
                r = jax.nn.sigmoid(gates[:, (2 * d) * LANES:(2 * d + 1) * LANES])
                i = jax.nn.sigmoid(gates[:, (2 * d + 1) * LANES:(2 * d + 2) * LANES])
                log_a = r * decay[d:d + 1, cols]
                t = jnp.tanh(-log_a)
                a_ref[r0:r0 + rows, cols] = jnp.exp(log_a)
                u_ref[r0:r0 + rows, cols] = jnp.sqrt(2.0 * t / (1.0 + t)) * (i * xcp)

    ngroups = seq // SUBLANES

    def step(g, carry):
        new = []
        rf = pl.multiple_of(g * SUBLANES, SUBLANES)
        rb = pl.multiple_of((ngroups - 1 - g) * SUBLANES, SUBLANES)
        for p in range(npair):
            cols = slice(p * LANES, (p + 1) * LANES)
            hf_prev, hb_prev = carry[2 * p], carry[2 * p + 1]
            a, u = _local_scan(af_ref[pl.ds(rf, SUBLANES), cols], uf_ref[pl.ds(rf, SUBLANES), cols], False)
            hf = u + a * jnp.broadcast_to(hf_prev[SUBLANES - 1:SUBLANES, :], a.shape)
            af_ref[pl.ds(rf, SUBLANES), cols] = hf
            a, u = _local_scan(ab_ref[pl.ds(rb, SUBLANES), cols], ub_ref[pl.ds(rb, SUBLANES), cols], True)
            hb = u + a * jnp.broadcast_to(hb_prev[0:1, :], a.shape)
            ab_ref[pl.ds(rb, SUBLANES), cols] = hb
            new += [hf, hb]
        return tuple(new)

    zero = jnp.zeros((SUBLANES, LANES), F32)
    lax.fori_loop(0, ngroups, step, (zero,) * (2 * npair))

    for c in range(seq // rows):
        rs = slice(c * rows, (c + 1) * rows)
        y = af_ref[rs, :] + ab_ref[rs, :]
        o_ref[rs, :] = (jax.nn.gelu(gr_ref[rs, :], approximate=True) * y).astype(o_ref.dtype)


def _lru(xr, gr, conv_w, conv_b, wg, bg, lru_lambda, rows):
    b, s, width = xr.shape
    seqblk = pl.BlockSpec((None, s, width), lambda bi: (bi, 0, 0))
    full = lambda a: pl.BlockSpec(a.shape, lambda bi: (0,) * a.ndim)
    kernel = functools.partial(_lru_kernel, seq=s, rows=rows)
    return pl.pallas_call(
        kernel,
        grid=(b,),
        in_specs=[seqblk, seqblk, full(conv_w), full(conv_b), full(wg), full(bg), full(lru_lambda)],
        out_specs=seqblk,
        out_shape=jax.ShapeDtypeStruct((b, s, width), BF16),
        scratch_shapes=[pltpu.VMEM((s + 2 * SUBLANES, width), F32)]
        + [pltpu.VMEM((s, width), F32)] * 4,
        compiler_params=pltpu.CompilerParams(
            dimension_semantics=("arbitrary",), vmem_limit_bytes=VMEM_LIMIT),
        name="lru",
    )(xr, gr, conv_w, conv_b, wg, bg, lru_lambda)


def _outproj_kernel(x_ref, a_ref, l_ref, wa_ref, wl_ref, g_ref, xm_ref, h_ref):
    xm = x_ref[...] + _dot(a_ref[...], wa_ref[...]) + _dot(l_ref[...], wl_ref[...])
    xm_ref[...] = xm
    h_ref[...] = _rms(xm, g_ref[...]).astype(h_ref.dtype)


def _outproj(x2, attn2, lru2, w_attn, w_lru, g, tm):
    n, d = x2.shape
    row = lambda i: (i, 0)
    const = lambda i: (0, 0)
    return pl.pallas_call(
        _outproj_kernel,
        grid=(n // tm,),
        in_specs=[pl.BlockSpec((tm, d), row),
                  pl.BlockSpec((tm, attn2.shape[1]), row),
                  pl.BlockSpec((tm, lru2.shape[1]), row),
                  pl.BlockSpec(w_attn.shape, const),
                  pl.BlockSpec(w_lru.shape, const),
                  pl.BlockSpec((1, d), const)],
        out_specs=[pl.BlockSpec((tm, d), row)] * 2,
        out_shape=[jax.ShapeDtypeStruct((n, d), F32), jax.ShapeDtypeStruct((n, d), BF16)],
        compiler_params=pltpu.CompilerParams(
            dimension_semantics=("arbitrary",), vmem_limit_bytes=VMEM_LIMIT),
        name="outproj",
    )(x2, attn2, lru2, w_attn, w_lru, g)


def _ffn_kernel(hp_ref, hm_ref, hn_ref, xm_ref, wup_ref, cw_ref, cb_ref, wdn_ref, fg_ref, o_ref,
                hext_ref, acc_ref, *, tile, chunk, d_ff):
    i = pl.program_id(1)
    halo = BF16_ROWS
    hext_ref[0:halo, :] = jnp.where(i > 0, hp_ref[...], jnp.zeros_like(hp_ref))
    hext_ref[halo:halo + tile, :] = hm_ref[...]
    hext_ref[halo + tile:2 * halo + tile, :] = jnp.where(
        i < pl.num_programs(1) - 1, hn_ref[...], jnp.zeros_like(hn_ref))
    acc_ref[...] = jnp.zeros_like(acc_ref)

    def conv(u, off):
        w = cw_ref[:, pl.ds(off, chunk)]
        out = cb_ref[:, pl.ds(off, chunk)]
        for tap in range(3):
            start = halo + tap - FFN_CONV_LEFT
            out = out + u[start:start + tile, :] * w[tap:tap + 1, :]
        return out

    def body(c, carry):
        og = pl.multiple_of(c * chunk, chunk)
        ov = pl.multiple_of(d_ff + c * chunk, chunk)
        hext = hext_ref[...]
        gate = conv(_dot(hext, wup_ref[:, pl.ds(og, chunk)]), og)
        val = conv(_dot(hext, wup_ref[:, pl.ds(ov, chunk)]), ov)
        act = (jax.nn.gelu(gate, approximate=True) * val).astype(BF16)
        acc_ref[...] += _dot(act, wdn_ref[pl.ds(og, chunk), :])
        return carry

    lax.fori_loop(0, d_ff // chunk, body, 0)
    o_ref[...] = _rms(xm_ref[...] + acc_ref[...], fg_ref[...])


def _ffn(h2, xm, w_up, conv_w, conv_b, w_down, final_g, tile, chunk):
    b, s, d = h2.shape
    d_ff = w_down.shape[0]
    nh = tile // BF16_ROWS
    last_halo = s // BF16_ROWS - 1
    main = lambda bi, i: (bi, i, 0)
    const = lambda bi, i: (0, 0)
    single = dict(pipeline_mode=pl.Buffered(1))
    kernel = functools.partial(_ffn_kernel, tile=tile, chunk=chunk, d_ff=d_ff)
    return pl.pallas_call(
        kernel,
        grid=(b, s // tile),
        in_specs=[pl.BlockSpec((None, BF16_ROWS, d), lambda bi, i: (bi, jnp.maximum(i * nh - 1, 0), 0)),
                  pl.BlockSpec((None, tile, d), main),
                  pl.BlockSpec((None, BF16_ROWS, d),
                               lambda bi, i: (bi, jnp.minimum((i + 1) * nh, last_halo), 0)),
                  pl.BlockSpec((None, tile, d), main),
                  pl.BlockSpec(w_up.shape, const, **single),
                  pl.BlockSpec(conv_w.shape, const),
                  pl.BlockSpec(conv_b.shape, const),
                  pl.BlockSpec(w_down.shape, const, **single),
                  pl.BlockSpec((1, d), const)],
        out_specs=pl.BlockSpec((None, tile, d), main),
        out_shape=jax.ShapeDtypeStruct((b, s, d), F32),
        scratch_shapes=[pltpu.VMEM((tile + 2 * BF16_ROWS, d), BF16),
                        pltpu.VMEM((tile, d), F32)],
        compiler_params=pltpu.CompilerParams(
            dimension_semantics=("arbitrary",) * 2, vmem_limit_bytes=VMEM_LIMIT),
        name="ffn",
    )(h2, h2, h2, xm, w_up, conv_w, conv_b, w_down, final_g)


def _gate_weights(w_a, b_a, w_x, b_x):
    ndir, nblk, bd, _ = w_a.shape
    per = LANES // bd
    npair = nblk // per

    def blockdiag(w):
        w = w.reshape(npair, per, bd, bd)
        eye = jnp.eye(per, dtype=w.dtype)
        return jnp.einsum('pbij,bc->pbicj', w, eye).reshape(npair, LANES, LANES)

    ws, bs = [], []
    for d in range(ndir):
        for w, bias in ((w_a, b_a), (w_x, b_x)):
            ws.append(blockdiag(w[d]))
            bs.append(bias[d].reshape(npair, LANES))
    return jnp.concatenate(ws, axis=-1).astype(BF16), jnp.concatenate(bs, axis=-1).astype(F32)


def kernel(x, attn_norm_g, w_in, lambda_q1, lambda_k1, lambda_q2, lambda_k2, subln_g,
           lru_conv_w, lru_conv_b, lru_w_a, lru_b_a, lru_w_x, lru_b_x, lru_lambda,
           w_out, ffn_norm_g, w_up, ffn_conv_w, ffn_conv_b, w_down, final_norm_g):
    b, s, d = x.shape
    depth = w_in.shape[0]
    x2 = x.reshape(b * s, d)
    for l in range(depth):
        lambda_init = 0.8 - 0.6 * math.exp(-0.3 * l)
        q, k, v, xr, gr = _inproj(x2, attn_norm_g[l][None], w_in[l].astype(BF16), tm=512)
        lw = xr.shape[-1]
        attn = _attention(q.reshape(b, s, -1), k.reshape(b, s, -1), v.reshape(b, s, -1),
                          lambda_q1[l][None], lambda_k1[l][None], lambda_q2[l][None],
                          lambda_k2[l][None], subln_g[l][None], lambda_init, tq=256)
        wg, bg = _gate_weights(lru_w_a[l], lru_b_a[l], lru_w_x[l], lru_b_x[l])
        lru = _lru(xr.reshape(b, s, lw), gr.reshape(b, s, lw), lru_conv_w[l], lru_conv_b[l][None],
                   wg, bg, lru_lambda[l], rows=256)
        wo = w_out[l].astype(BF16)
        xm, h2 = _outproj(x2, attn.reshape(b * s, -1), lru.reshape(b * s, -1),
                          wo[:ATTN_WIDTH], wo[ATTN_WIDTH:], ffn_norm_g[l][None], tm=512)
        assert depth == 1
        x2 = _ffn(h2.reshape(b, s, d), xm.reshape(b, s, d), w_up[l].astype(BF16), ffn_conv_w[l],
                  ffn_conv_b[l][None], w_down[l].astype(BF16), final_norm_g[None],
                  tile=512, chunk=256).reshape(b * s, d)
    return x2.reshape(b, s, d)
```

```python
import functools
import math

import jax
import jax.numpy as jnp
from jax import lax
from jax.experimental import pallas as pl
from jax.experimental.pallas import tpu as pltpu

F32 = jnp.float32
BF16 = jnp.bfloat16

N_HEADS = 4
HEAD_DIM = 64
V_DIM = 2 * HEAD_DIM
ATTN_WIDTH = N_HEADS * V_DIM
LRU_BLOCK = 64
LRU_CONV_WIDTH = 4
LRU_CONV_LEFT = 2
LRU_C = 8.0
FFN_CONV_LEFT = 1
NORM_EPS = 1e-6
LANES = 128
SUBLANES = 8
BF16_ROWS = 16
VMEM_LIMIT = 56 * 1024 * 1024


def _rms(x, g):
    return (x * lax.rsqrt(jnp.mean(x * x, axis=-1, keepdims=True) + NORM_EPS)) * g


def _dot(a, b):
    return jnp.dot(a, b, preferred_element_type=F32)


def _inproj_kernel(x_ref, g_ref, w_ref, q_ref, k_ref, v_ref, xr_ref, gr_ref):
    hb = _rms(x_ref[...], g_ref[...]).astype(BF16)
    aw = ATTN_WIDTH
    q_ref[...] = (_dot(hb, w_ref[:, 0:aw]) * (HEAD_DIM ** -0.5)).astype(BF16)
    k_ref[...] = _dot(hb, w_ref[:, aw:2 * aw]).astype(BF16)
    v_ref[...] = _dot(hb, w_ref[:, 2 * aw:3 * aw]).astype(BF16)
    lw = xr_ref.shape[-1]
    xr_ref[...] = _dot(hb, w_ref[:, 3 * aw:3 * aw + lw])
    gr_ref[...] = _dot(hb, w_ref[:, 3 * aw + lw:3 * aw + 2 * lw])


def _inproj(x2, g, w_bf, tm):
    n, d = x2.shape
    lw = (w_bf.shape[1] - 3 * ATTN_WIDTH) // 2
    row = lambda i: (i, 0)
    const = lambda i: (0, 0)
    return pl.pallas_call(
        _inproj_kernel,
        grid=(n // tm,),
        in_specs=[pl.BlockSpec((tm, d), row),
                  pl.BlockSpec((1, d), const),
                  pl.BlockSpec(w_bf.shape, const)],
        out_specs=[pl.BlockSpec((tm, ATTN_WIDTH), row)] * 3 + [pl.BlockSpec((tm, lw), row)] * 2,
        out_shape=[jax.ShapeDtypeStruct((n, ATTN_WIDTH), BF16)] * 3
        + [jax.ShapeDtypeStruct((n, lw), F32)] * 2,
        compiler_params=pltpu.CompilerParams(
            dimension_semantics=("arbitrary",), vmem_limit_bytes=VMEM_LIMIT),
        name="inproj",
    )(x2, g, w_bf)


def _attn_kernel(lq1_ref, lk1_ref, lq2_ref, lk2_ref, sg_ref, q_ref, k_ref, v_ref, o_ref,
                 *, tq, seq, lambda_init):
    h = pl.program_id(1)
    qi = pl.program_id(2)
    lam = (jnp.exp(jnp.sum(lq1_ref[...] * lk1_ref[...], axis=-1, keepdims=True))
           - jnp.exp(jnp.sum(lq2_ref[...] * lk2_ref[...], axis=-1, keepdims=True))
           + lambda_init)

    q = q_ref[...]
    k = k_ref[...]
    lane = lax.broadcasted_iota(jnp.int32, q.shape, 1)
    zero = jnp.zeros_like(q)
    nt = (((1,), (1,)), ((), ()))
    s1 = lax.dot_general(jnp.where(lane < HEAD_DIM, q, zero), k, nt, preferred_element_type=F32)
    s2 = lax.dot_general(jnp.where(lane >= HEAD_DIM, q, zero), k, nt, preferred_element_type=F32)

    slope = jnp.where(h == 0, 2.0 ** -2, jnp.where(h == 1, 2.0 ** -4,
                      jnp.where(h == 2, 2.0 ** -6, 2.0 ** -8))).astype(F32)
    row = qi * tq + lax.broadcasted_iota(jnp.int32, (tq, 1), 0)
    col = lax.broadcasted_iota(jnp.int32, (1, seq), 1)
    bias = slope * jnp.abs(row - col).astype(F32)

    s1 = s1 - bias
    s2 = s2 - bias
    e1 = jnp.exp(s1 - jnp.max(s1, axis=-1, keepdims=True))
    e2 = jnp.exp(s2 - jnp.max(s2, axis=-1, keepdims=True))
    c1 = 1.0 / jnp.sum(e1, axis=-1, keepdims=True)
    c2 = lam / jnp.sum(e2, axis=-1, keepdims=True)
    w = (e1 * c1 - e2 * c2).astype(BF16)
    o = _dot(w, v_ref[...])
    o_ref[...] = (_rms(o, sg_ref[...]) * (1.0 - lambda_init)).astype(o_ref.dtype)


def _attention(q, k, v, lq1, lk1, lq2, lk2, subln_g, lambda_init, tq):
    b, s, _ = q.shape
    vec = lambda bi, h, qi: (0, 0)
    kernel = functools.partial(_attn_kernel, tq=tq, seq=s, lambda_init=lambda_init)
    return pl.pallas_call(
        kernel,
        grid=(b, N_HEADS, s // tq),
        in_specs=[pl.BlockSpec((1, HEAD_DIM), vec)] * 4
        + [pl.BlockSpec((1, V_DIM), vec),
           pl.BlockSpec((None, tq, V_DIM), lambda bi, h, qi: (bi, qi, h)),
           pl.BlockSpec((None, s, V_DIM), lambda bi, h, qi: (bi, 0, h)),
           pl.BlockSpec((None, s, V_DIM), lambda bi, h, qi: (bi, 0, h))],
        out_specs=pl.BlockSpec((None, tq, V_DIM), lambda bi, h, qi: (bi, qi, h)),
        out_shape=jax.ShapeDtypeStruct((b, s, ATTN_WIDTH), BF16),
        compiler_params=pltpu.CompilerParams(
            dimension_semantics=("arbitrary",) * 3, vmem_limit_bytes=VMEM_LIMIT),
        name="attn",
    )(lq1, lk1, lq2, lk2, subln_g, q, k, v)


def _local_scan(a, u, reverse):
    row = lax.broadcasted_iota(jnp.int32, a.shape, 0)
    for d in (1, 2, 4):
        shift = SUBLANES - d if reverse else d
        valid = (row < SUBLANES - d) if reverse else (row >= d)
        a_s = jnp.where(valid, pltpu.roll(a, shift, 0), 1.0)
        u_s = jnp.where(valid, pltpu.roll(u, shift, 0), 0.0)
        u = a * u_s + u
        a = a * a_s
    return a, u


def _lru_kernel(xr_ref, gr_ref, cw_ref, cb_ref, wg_ref, bg_ref, lam_ref, o_ref,
                xp_ref, af_ref, uf_ref, ab_ref, ub_ref, *, seq, rows):
    width = xr_ref.shape[-1]
    pad = SUBLANES
    npair = width // LANES

    xp_ref[0:pad, :] = jnp.zeros((pad, width), F32)
    xp_ref[pad + seq:2 * pad + seq, :] = jnp.zeros((pad, width), F32)
    xp_ref[pad:pad + seq, :] = xr_ref[...]

    neg_lam = -lam_ref[...]
    softplus = jnp.maximum(neg_lam, 0.0) + jnp.log1p(jnp.exp(-jnp.abs(neg_lam)))
    decay = -LRU_C * softplus

    for c in range(seq // rows):
        r0 = c * rows
        xc = cb_ref[...]
        for tap in range(LRU_CONV_WIDTH):
            start = r0 + pad + tap - LRU_CONV_LEFT
            xc = xc + xp_ref[start:start + rows, :] * cw_ref[tap:tap + 1, :]
        xcb = xc.astype(BF16)
        for p in range(npair):
            cols = slice(p * LANES, (p + 1) * LANES)
            gates = _dot(xcb[:, cols], wg_ref[p]) + bg_ref[p:p + 1, :]
            xcp = xc[:, cols]
            for d, (a_ref, u_ref) in enumerate(((af_ref, uf_ref), (ab_ref, ub_ref))):
                r = jax.nn.sigmoid(gates[:, (2 * d) * LANES:(2 * d + 1) * LANES])
                i = jax.nn.sigmoid(gates[:, (2 * d + 1) * LANES:(2 * d + 2) * LANES])
                log_a = r * decay[d:d + 1, cols]
                t = jnp.tanh(-log_a)
                a_ref[r0:r0 + rows, cols] = jnp.exp(log_a)
                u_ref[r0:r0 + rows, cols] = jnp.sqrt(2.0 * t / (1.0 + t)) * (i * xcp)

    ngroups = seq // SUBLANES

    def step(g, carry):
        new = []
        rf = pl.multiple_of(g * SUBLANES, SUBLANES)
        rb = pl.multiple_of((ngroups - 1 - g) * SUBLANES, SUBLANES)
        for p in range(npair):
            cols = slice(p * LANES, (p + 1) * LANES)
            hf_prev, hb_prev = carry[2 * p], carry[2 * p + 1]
            a, u = _local_scan(af_ref[pl.ds(rf, SUBLANES), cols], uf_ref[pl.ds(rf, SUBLANES), cols], False)
            hf = u + a * jnp.broadcast_to(hf_prev[SUBLANES - 1:SUBLANES, :], a.shape)
            af_ref[pl.ds(rf, SUBLANES), cols] = hf
            a, u = _local_scan(ab_ref[pl.ds(rb, SUBLANES), cols], ub_ref[pl.ds(rb, SUBLANES), cols], True)
            hb = u + a * jnp.broadcast_to(hb_prev[0:1, :], a.shape)
            ab_ref[pl.ds(rb, SUBLANES), cols] = hb
            new += [hf, hb]
        return tuple(new)

    zero = jnp.zeros((SUBLANES, LANES), F32)
    lax.fori_loop(0, ngroups, step, (zero,) * (2 * npair))

    for c in range(seq // rows):
        rs = slice(c * rows, (c + 1) * rows)
        y = af_ref[rs, :] + ab_ref[rs, :]
        o_ref[rs, :] = (jax.nn.gelu(gr_ref[rs, :], approximate=True) * y).astype(o_ref.dtype)


def _lru(xr, gr, conv_w, conv_b, wg, bg, lru_lambda, rows):
    b, s, width = xr.shape
    seqblk = pl.BlockSpec((None, s, width), lambda bi: (bi, 0, 0))
    full = lambda a: pl.BlockSpec(a.shape, lambda bi: (0,) * a.ndim)
    kernel = functools.partial(_lru_kernel, seq=s, rows=rows)
    return pl.pallas_call(
        kernel,
        grid=(b,),
        in_specs=[seqblk, seqblk, full(conv_w), full(conv_b), full(wg), full(bg), full(lru_lambda)],
        out_specs=seqblk,
        out_shape=jax.ShapeDtypeStruct((b, s, width), BF16),
        scratch_shapes=[pltpu.VMEM((s + 2 * SUBLANES, width), F32)]
        + [pltpu.VMEM((s, width), F32)] * 4,
        compiler_params=pltpu.CompilerParams(
            dimension_semantics=("arbitrary",), vmem_limit_bytes=VMEM_LIMIT),
        name="lru",
    )(xr, gr, conv_w, conv_b, wg, bg, lru_lambda)


def _outproj_kernel(x_ref, a_ref, l_ref, wa_ref, wl_ref, g_ref, xm_ref, h_ref):
    xm = x_ref[...] + _dot(a_ref[...], wa_ref[...]) + _dot(l_ref[...], wl_ref[...])
    xm_ref[...] = xm
    h_ref[...] = _rms(xm, g_ref[...]).astype(h_ref.dtype)


def _outproj(x2, attn2, lru2, w_attn, w_lru, g, tm):
    n, d = x2.shape
    row = lambda i: (i, 0)
    const = lambda i: (0, 0)
    return pl.pallas_call(
        _outproj_kernel,
        grid=(n // tm,),
        in_specs=[pl.BlockSpec((tm, d), row),
                  pl.BlockSpec((tm, attn2.shape[1]), row),
                  pl.BlockSpec((tm, lru2.shape[1]), row),
                  pl.BlockSpec(w_attn.shape, const),
                  pl.BlockSpec(w_lru.shape, const),
                  pl.BlockSpec((1, d), const)],
        out_specs=[pl.BlockSpec((tm, d), row)] * 2,
        out_shape=[jax.ShapeDtypeStruct((n, d), F32), jax.ShapeDtypeStruct((n, d), BF16)],
        compiler_params=pltpu.CompilerParams(
            dimension_semantics=("arbitrary",), vmem_limit_bytes=VMEM_LIMIT),
        name="outproj",
    )(x2, attn2, lru2, w_attn, w_lru, g)


def _ffn_kernel(hp_ref, hm_ref, hn_ref, xm_ref, wup_ref, cw_ref, cb_ref, wdn_ref, fg_ref, o_ref,
                perm_ref, hext_ref, ua_ref, ub_ref, acta_ref, actb_ref, acc_ref,
                *, tile, chunk, d_ff):
    i = pl.program_id(1)
    nchunks = d_ff // chunk
    ngrp = tile // SUBLANES
    nslab = perm_ref.shape[0]
    d = nslab * LANES

    for s in range(SUBLANES):
        rows = hm_ref[s * ngrp:(s + 1) * ngrp, :].astype(F32)
        for n in range(nslab):
            perm_ref[n, pl.ds(s, ngrp, stride=SUBLANES), :] = rows[:, n * LANES:(n + 1) * LANES]
    for n in range(nslab):
        hext_ref[0:tile, n * LANES:(n + 1) * LANES] = perm_ref[n].astype(BF16)
    prev = jnp.where(i > 0, hp_ref[BF16_ROWS - 1:BF16_ROWS, :].astype(F32), 0.0)
    nxt = jnp.where(i < pl.num_programs(1) - 1, hn_ref[0:1, :].astype(F32), 0.0)
    hrow = lax.broadcasted_iota(jnp.int32, (BF16_ROWS, d), 0)
    halo = jnp.where(hrow == 0, prev, jnp.where(hrow == 1, nxt, 0.0))
    hext_ref[tile:tile + BF16_ROWS, :] = halo.astype(BF16)
    acc_ref[...] = jnp.zeros_like(acc_ref)

    def offsets(c):
        og, ov = c * chunk, d_ff + c * chunk
        if isinstance(c, int):
            return og, ov
        return pl.multiple_of(og, chunk), pl.multiple_of(ov, chunk)

    def up(c, u_ref):
        og, ov = offsets(c)
        hext = hext_ref[...]
        u_ref[:, 0:chunk] = _dot(hext, wup_ref[:, pl.ds(og, chunk)])
        u_ref[:, chunk:2 * chunk] = _dot(hext, wup_ref[:, pl.ds(ov, chunk)])

    def glu(c, u_ref, act_ref):
        sub = lax.broadcasted_iota(jnp.int32, (SUBLANES, chunk), 0)
        blk = BF16_ROWS

        def conv(col0, off, r0):
            cols = slice(col0, col0 + chunk)
            w = cw_ref[:, pl.ds(off, chunk)]
            cur = u_ref[r0:r0 + blk, cols]
            if r0 == 0:
                first = jnp.where(sub == 0, u_ref[tile:tile + 1, cols],
                                  pltpu.roll(u_ref[tile - SUBLANES:tile, cols], 1, 0))
                um1 = jnp.concatenate([first, cur[0:blk - SUBLANES]], axis=0)
            else:
                um1 = u_ref[r0 - SUBLANES:r0 + blk - SUBLANES, cols]
            if r0 + blk == tile:
                last = jnp.where(sub == SUBLANES - 1, u_ref[tile + 1:tile + 2, cols],
                                 pltpu.roll(u_ref[0:SUBLANES, cols], SUBLANES - 1, 0))
                up1 = jnp.concatenate([cur[SUBLANES:blk], last], axis=0)
            else:
                up1 = u_ref[r0 + SUBLANES:r0 + blk + SUBLANES, cols]
            return (cb_ref[:, pl.ds(off, chunk)] + um1 * w[0:1, :] + cur * w[1:2, :]
                    + up1 * w[2:3, :])

        og, ov = offsets(c)
        for r0 in range(0, tile, blk):
            gate = conv(0, og, r0)
            val = conv(chunk, ov, r0)
            act_ref[r0:r0 + blk, :] = (jax.nn.gelu(gate, approximate=True) * val).astype(BF16)

    def down(c, act_ref):
        og, _ = offsets(c)
        y = _dot(act_ref[...], wdn_ref[pl.ds(og, chunk), :])
        for n in range(nslab):
            acc_ref[n] += y[:, n * LANES:(n + 1) * LANES]

    u_refs, act_refs = (ua_ref, ub_ref), (acta_ref, actb_ref)

    def step(t, parity, first=False, last=False):
        if not last:
            up(t + 1, u_refs[1 - parity])
        glu(t, u_refs[parity], act_refs[parity])
        if not first:
            down(t - 1, act_refs[1 - parity])

    def body(t, carry):
        lax.cond(t % 2 == 1, lambda: step(t, 1), lambda: step(t, 0))
        return carry

    assert nchunks % 2 == 1 and nchunks >= 3
    up(0, ua_ref)
    step(0, 0, first=True)
    lax.fori_loop(1, nchunks - 1, body, 0)
    step(nchunks - 1, 0, last=True)
    down(nchunks - 1, acta_ref)
    for s in range(SUBLANES):
        rs = slice(s * ngrp, (s + 1) * ngrp)
        y = jnp.concatenate([acc_ref[n, pl.ds(s, ngrp, stride=SUBLANES), :] for n in range(nslab)],
                            axis=1)
        o_ref[rs, :] = _rms(xm_ref[rs, :] + y, fg_ref[...])


def _ffn(h2, xm, w_up, conv_w, conv_b, w_down, final_g, tile, chunk):
    b, s, d = h2.shape
    d_ff = w_down.shape[0]
    nh = tile // BF16_ROWS
    last_halo = s // BF16_ROWS - 1
    main = lambda bi, i: (bi, i, 0)
    const = lambda bi, i: (0, 0)
    single = dict(pipeline_mode=pl.Buffered(1))
    kernel = functools.partial(_ffn_kernel, tile=tile, chunk=chunk, d_ff=d_ff)
    return pl.pallas_call(
        kernel,
        grid=(b, s // tile),
        in_specs=[pl.BlockSpec((None, BF16_ROWS, d), lambda bi, i: (bi, jnp.maximum(i * nh - 1, 0), 0)),
                  pl.BlockSpec((None, tile, d), main),
                  pl.BlockSpec((None, BF16_ROWS, d),
                               lambda bi, i: (bi, jnp.minimum((i + 1) * nh, last_halo), 0)),
                  pl.BlockSpec((None, tile, d), main),
                  pl.BlockSpec(w_up.shape, const, **single),
                  pl.BlockSpec(conv_w.shape, const),
                  pl.BlockSpec(conv_b.shape, const),
                  pl.BlockSpec(w_down.shape, const, **single),
                  pl.BlockSpec((1, d), const)],
        out_specs=pl.BlockSpec((None, tile, d), main),
        out_shape=jax.ShapeDtypeStruct((b, s, d), F32),
        scratch_shapes=[pltpu.VMEM((d // LANES, tile, LANES), F32),
                        pltpu.VMEM((tile + BF16_ROWS, d), BF16)]
        + [pltpu.VMEM((tile + BF16_ROWS, 2 * chunk), F32)] * 2
        + [pltpu.VMEM((tile, chunk), BF16)] * 2
        + [pltpu.VMEM((d // LANES, tile, LANES), F32)],
        compiler_params=pltpu.CompilerParams(
            dimension_semantics=("arbitrary",) * 2, vmem_limit_bytes=VMEM_LIMIT),
        name="ffn",
    )(h2, h2, h2, xm, w_up, conv_w, conv_b, w_down, final_g)


def _gate_weights(w_a, b_a, w_x, b_x):
    ndir, nblk, bd, _ = w_a.shape
    per = LANES // bd
    npair = nblk // per

    def blockdiag(w):
        w = w.reshape(npair, per, bd, bd)
        eye = jnp.eye(per, dtype=w.dtype)
        return jnp.einsum('pbij,bc->pbicj', w, eye).reshape(npair, LANES, LANES)

    ws, bs = [], []
    for d in range(ndir):
        for w, bias in ((w_a, b_a), (w_x, b_x)):
            ws.append(blockdiag(w[d]))
            bs.append(bias[d].reshape(npair, LANES))
    return jnp.concatenate(ws, axis=-1).astype(BF16), jnp.concatenate(bs, axis=-1).astype(F32)


def kernel(x, attn_norm_g, w_in, lambda_q1, lambda_k1, lambda_q2, lambda_k2, subln_g,
           lru_conv_w, lru_conv_b, lru_w_a, lru_b_a, lru_w_x, lru_b_x, lru_lambda,
           w_out, ffn_norm_g, w_up, ffn_conv_w, ffn_conv_b, w_down, final_norm_g):
    b, s, d = x.shape
    depth = w_in.shape[0]
    x2 = x.reshape(b * s, d)
    for l in range(depth):
        lambda_init = 0.8 - 0.6 * math.exp(-0.3 * l)
        q, k, v, xr, gr = _inproj(x2, attn_norm_g[l][None], w_in[l].astype(BF16), tm=512)
        lw = xr.shape[-1]
        attn = _attention(q.reshape(b, s, -1), k.reshape(b, s, -1), v.reshape(b, s, -1),
                          lambda_q1[l][None], lambda_k1[l][None], lambda_q2[l][None],
                          lambda_k2[l][None], subln_g[l][None], lambda_init, tq=256)
        wg, bg = _gate_weights(lru_w_a[l], lru_b_a[l], lru_w_x[l], lru_b_x[l])
        lru = _lru(xr.reshape(b, s, lw), gr.reshape(b, s, lw), lru_conv_w[l], lru_conv_b[l][None],
                   wg, bg, lru_lambda[l], rows=256)
        wo = w_out[l].astype(BF16)
        xm, h2 = _outproj(x2, attn.reshape(b * s, -1), lru.reshape(b * s, -1),
                          wo[:ATTN_WIDTH], wo[ATTN_WIDTH:], ffn_norm_g[l][None], tm=512)
        assert depth == 1
        x2 = _ffn(h2.reshape(b, s, d), xm.reshape(b, s, d), w_up[l].astype(BF16), ffn_conv_w[l],
                  ffn_conv_b[l][None], w_down[l].astype(BF16), final_norm_g[None],
                  tile=512, chunk=256).reshape(b * s, d)
    return x2.reshape(b, s, d)
```

```python
import functools
import math

import jax
import jax.numpy as jnp
from jax import lax
from jax.experimental import pallas as pl
from jax.experimental.pallas import tpu as pltpu

F32 = jnp.float32
BF16 = jnp.bfloat16

N_HEADS = 4
HEAD_DIM = 64
V_DIM = 2 * HEAD_DIM
ATTN_WIDTH = N_HEADS * V_DIM
LRU_BLOCK = 64
LRU_CONV_WIDTH = 4
LRU_CONV_LEFT = 2
LRU_C = 8.0
FFN_CONV_LEFT = 1
NORM_EPS = 1e-6
LANES = 128
SUBLANES = 8
BF16_ROWS = 16
VMEM_LIMIT = 56 * 1024 * 1024


def _rms(x, g):
    return (x * lax.rsqrt(jnp.mean(x * x, axis=-1, keepdims=True) + NORM_EPS)) * g


def _dot(a, b):
    return jnp.dot(a, b, preferred_element_type=F32)


_NT = (((1,), (1,)), ((), ()))


def _inproj_kernel(x_ref, g_ref, w_ref, wvt_ref, q_ref, k_ref, vt_ref, xr_ref, gr_ref):
    hb = _rms(x_ref[...], g_ref[...]).astype(BF16)
    aw = ATTN_WIDTH
    q_ref[...] = (_dot(hb, w_ref[:, 0:aw]) * (HEAD_DIM ** -0.5)).astype(BF16)
    k_ref[...] = _dot(hb, w_ref[:, aw:2 * aw]).astype(BF16)
    vt_ref[...] = lax.dot_general(wvt_ref[...], hb, _NT, preferred_element_type=F32).astype(BF16)
    lw = xr_ref.shape[-1]
    xr_ref[...] = _dot(hb, w_ref[:, 3 * aw:3 * aw + lw])
    gr_ref[...] = _dot(hb, w_ref[:, 3 * aw + lw:3 * aw + 2 * lw])


def _inproj(x2, g, w_bf, wvt_bf, tm):
    n, d = x2.shape
    lw = (w_bf.shape[1] - 3 * ATTN_WIDTH) // 2
    row = lambda i: (i, 0)
    const = lambda i: (0, 0)
    return pl.pallas_call(
        _inproj_kernel,
        grid=(n // tm,),
        in_specs=[pl.BlockSpec((tm, d), row),
                  pl.BlockSpec((1, d), const),
                  pl.BlockSpec(w_bf.shape, const),
                  pl.BlockSpec(wvt_bf.shape, const)],
        out_specs=[pl.BlockSpec((tm, ATTN_WIDTH), row)] * 2
        + [pl.BlockSpec((ATTN_WIDTH, tm), lambda i: (0, i))]
        + [pl.BlockSpec((tm, lw), row)] * 2,
        out_shape=[jax.ShapeDtypeStruct((n, ATTN_WIDTH), BF16)] * 2
        + [jax.ShapeDtypeStruct((ATTN_WIDTH, n), BF16)]
        + [jax.ShapeDtypeStruct((n, lw), F32)] * 2,
        compiler_params=pltpu.CompilerParams(
            dimension_semantics=("arbitrary",), vmem_limit_bytes=VMEM_LIMIT),
        name="inproj",
    )(x2, g, w_bf, wvt_bf)


def _attn_kernel(lq1_ref, lk1_ref, lq2_ref, lk2_ref, sg_ref, q_ref, k_ref, vt_ref, o_ref,
                 kf_ref, dist_ref, *se_refs, tq, nsub, seq, lambda_init):
    h = pl.program_id(1)
    qi = pl.program_id(2)
    s_refs, e_refs = se_refs[:nsub], se_refs[nsub:]
    nblk = seq // tq
    assert nblk & (nblk - 1) == 0 and 3 * nblk <= LANES and tq <= 256
    shift = nblk.bit_length() - 1

    @pl.when((pl.program_id(0) == 0) & (h == 0) & (qi == 0))
    def _init():
        lane = lax.broadcasted_iota(jnp.int32, (seq, LANES), 1)
        row = lax.broadcasted_iota(jnp.int32, (seq, LANES), 0)
        grp = lane >> shift
        hit = (row // tq) == (lane & (nblk - 1))
        dj = (row % tq).astype(F32)
        kf_ref[...] = jnp.where(hit & (grp < 2), 1.0,
                                jnp.where(hit & (grp == 2), dj, 0.0)).astype(BF16)
        r = lax.broadcasted_iota(jnp.int32, (tq, tq), 0)
        c = lax.broadcasted_iota(jnp.int32, (tq, tq), 1)
        dist_ref[...] = jnp.abs(r - c).astype(F32)

    lam = (jnp.exp(jnp.sum(lq1_ref[...] * lk1_ref[...], axis=-1, keepdims=True))
           - jnp.exp(jnp.sum(lq2_ref[...] * lk2_ref[...], axis=-1, keepdims=True))
           + lambda_init)
    slope = jnp.where(h == 0, 2.0 ** -2, jnp.where(h == 1, 2.0 ** -4,
                      jnp.where(h == 2, 2.0 ** -6, 2.0 ** -8))).astype(F32)

    kaug = jnp.concatenate([k_ref[...], kf_ref[...]], axis=1)
    lane = lax.broadcasted_iota(jnp.int32, (tq, LANES), 1)
    di = lax.broadcasted_iota(jnp.int32, (tq, LANES), 0).astype(F32)
    grp = lane >> shift
    strip = 4 * SUBLANES

    def scores(sb):
        blk = qi * nsub + sb
        diff = blk - (lane & (nblk - 1))
        sign = jnp.where(diff > 0, 1.0, jnp.where(diff < 0, -1.0, 0.0))
        qf = jnp.where(grp == 0, -slope * sign * di,
                       jnp.where(grp == 1, -slope * tq * jnp.abs(diff).astype(F32),
                                 jnp.where(grp == 2, slope * sign, 0.0))).astype(BF16)
        q = q_ref[sb * tq:(sb + 1) * tq, :]
        zero = jnp.zeros_like(q)
        qaug = jnp.concatenate(
            [jnp.concatenate([jnp.where(lane < HEAD_DIM, q, zero), qf], axis=1),
             jnp.concatenate([jnp.where(lane >= HEAD_DIM, q, zero), qf], axis=1)], axis=0)
        s_ref = s_refs[sb]
        s_ref[...] = lax.dot_general(kaug, qaug, _NT, preferred_element_type=F32)
        diag = pl.ds(pl.multiple_of(blk * tq, tq), tq)
        diag_bias = slope * dist_ref[...]
        s_ref[diag, 0:tq] = s_ref[diag, 0:tq] - diag_bias
        s_ref[diag, tq:2 * tq] = s_ref[diag, tq:2 * tq] - diag_bias

    def finish(sb):
        s_ref, e_ref = s_refs[sb], e_refs[sb]
        mx = s_ref[0:strip, :]
        for r in range(strip, seq, strip):
            mx = jnp.maximum(mx, s_ref[r:r + strip, :])
        mx = jnp.max(mx, axis=0, keepdims=True)
        acc = jnp.zeros((strip, 2 * tq), F32)
        for r in range(0, seq, strip):
            e = jnp.exp(s_ref[r:r + strip, :] - mx)
            acc = acc + e
            e_ref[r:r + strip, :] = e.astype(BF16)
        norm = jnp.sum(acc, axis=0, keepdims=True)
        o12 = _dot(vt_ref[...], e_ref[...])
        o = o12[:, 0:tq] * (1.0 / norm[:, 0:tq]) - o12[:, tq:2 * tq] * (lam / norm[:, tq:2 * tq])
        o = o * lax.rsqrt(jnp.mean(o * o, axis=0, keepdims=True) + NORM_EPS)
        o = o * sg_ref[...] * (1.0 - lambda_init)
        o_ref[sb * tq:(sb + 1) * tq, :] = o.T.astype(o_ref.dtype)

    scores(0)
    for sb in range(nsub):
        if sb + 1 < nsub:
            scores(sb + 1)
        finish(sb)


def _attention(q, k, vt, lq1, lk1, lq2, lk2, subln_g, lambda_init, tq, nsub):
    b, s, _ = q.shape
    vec = lambda bi, h, qi: (0, 0)
    tstep = tq * nsub
    kernel = functools.partial(_attn_kernel, tq=tq, nsub=nsub, seq=s, lambda_init=lambda_init)
    return pl.pallas_call(
        kernel,
        grid=(b, N_HEADS, s // tstep),
        in_specs=[pl.BlockSpec((1, HEAD_DIM), vec)] * 4
        + [pl.BlockSpec((V_DIM, 1), vec),
           pl.BlockSpec((None, tstep, V_DIM), lambda bi, h, qi: (bi, qi, h)),
           pl.BlockSpec((None, s, V_DIM), lambda bi, h, qi: (bi, 0, h)),
           pl.BlockSpec((V_DIM, s), lambda bi, h, qi: (h, bi))],
        out_specs=pl.BlockSpec((None, tstep, V_DIM), lambda bi, h, qi: (bi, qi, h)),
        out_shape=jax.ShapeDtypeStruct((b, s, ATTN_WIDTH), BF16),
        scratch_shapes=[pltpu.VMEM((s, LANES), BF16), pltpu.VMEM((tq, tq), F32)]
        + [pltpu.VMEM((s, 2 * tq), F32)] * nsub + [pltpu.VMEM((s, 2 * tq), BF16)] * nsub,
        compiler_params=pltpu.CompilerParams(
            dimension_semantics=("arbitrary",) * 3, vmem_limit_bytes=VMEM_LIMIT),
        name="attn",
    )(lq1, lk1, lq2, lk2, subln_g, q, k, vt)


def _local_scan(a, u, reverse):
    row = lax.broadcasted_iota(jnp.int32, a.shape, 0)
    for d in (1, 2, 4):
        shift = SUBLANES - d if reverse else d
        valid = (row < SUBLANES - d) if reverse else (row >= d)
        a_s = jnp.where(valid, pltpu.roll(a, shift, 0), 1.0)
        u_s = jnp.where(valid, pltpu.roll(u, shift, 0), 0.0)
        u = a * u_s + u
        a = a * a_s
    return a, u


def _lru_kernel(xr_ref, gr_ref, cw_ref, cb_ref, wg_ref, bg_ref, lam_ref, o_ref,
                xp_ref, af_ref, uf_ref, ab_ref, ub_ref, *, seq, rows):
    width = xr_ref.shape[-1]
    pad = SUBLANES
    npair = width // LANES

    xp_ref[0:pad, :] = jnp.zeros((pad, width), F32)
    xp_ref[pad + seq:2 * pad + seq, :] = jnp.zeros((pad, width), F32)
    xp_ref[pad:pad + seq, :] = xr_ref[...]

    neg_lam = -lam_ref[...]
    softplus = jnp.maximum(neg_lam, 0.0) + jnp.log1p(jnp.exp(-jnp.abs(neg_lam)))
    decay = -LRU_C * softplus

    for c in range(seq // rows):
        r0 = c * rows
        xc = cb_ref[...]
        for tap in range(LRU_CONV_WIDTH):
            start = r0 + pad + tap - LRU_CONV_LEFT
            xc = xc + xp_ref[start:start + rows, :] * cw_ref[tap:tap + 1, :]
        xcb = xc.astype(BF16)
        for p in range(npair):
            cols = slice(p * LANES, (p + 1) * LANES)
            gates = _dot(xcb[:, cols], wg_ref[p]) + bg_ref[p:p + 1, :]
            xcp = xc[:, cols]
            for d, (a_ref, u_ref) in enumerate(((af_ref, uf_ref), (ab_ref, ub_ref))):
                r = jax.nn.sigmoid(gates[:, (2 * d) * LANES:(2 * d + 1) * LANES])
                i = jax.nn.sigmoid(gates[:, (2 * d + 1) * LANES:(2 * d + 2) * LANES])
                log_a = r * decay[d:d + 1, cols]
                t = jnp.tanh(-log_a)
                a_ref[r0:r0 + rows, cols] = jnp.exp(log_a)
                u_ref[r0:r0 + rows, cols] = jnp.sqrt(2.0 * t / (1.0 + t)) * (i * xcp)

    ngroups = seq // SUBLANES

    def step(g, carry):
        new = []
        rf = pl.multiple_of(g * SUBLANES, SUBLANES)
        rb = pl.multiple_of((ngroups - 1 - g) * SUBLANES, SUBLANES)
        for p in range(npair):
            cols = slice(p * LANES, (p + 1) * LANES)
            hf_prev, hb_prev = carry[2 * p], carry[2 * p + 1]
            a, u = _local_scan(af_ref[pl.ds(rf, SUBLANES), cols], uf_ref[pl.ds(rf, SUBLANES), cols], False)
            hf = u + a * jnp.broadcast_to(hf_prev[SUBLANES - 1:SUBLANES, :], a.shape)
            af_ref[pl.ds(rf, SUBLANES), cols] = hf
            a, u = _local_scan(ab_ref[pl.ds(rb, SUBLANES), cols], ub_ref[pl.ds(rb, SUBLANES), cols], True)
            hb = u + a * jnp.broadcast_to(hb_prev[0:1, :], a.shape)
            ab_ref[pl.ds(rb, SUBLANES), cols] = hb
            new += [hf, hb]
        return tuple(new)

    zero = jnp.zeros((SUBLANES, LANES), F32)
    lax.fori_loop(0, ngroups, step, (zero,) * (2 * npair))

    for c in range(seq // rows):
        rs = slice(c * rows, (c + 1) * rows)
        y = af_ref[rs, :] + ab_ref[rs, :]
        o_ref[rs, :] = (jax.nn.gelu(gr_ref[rs, :], approximate=True) * y).astype(o_ref.dtype)


def _lru(xr, gr, conv_w, conv_b, wg, bg, lru_lambda, rows):
    b, s, width = xr.shape
    seqblk = pl.BlockSpec((None, s, width), lambda bi: (bi, 0, 0))
    full = lambda a: pl.BlockSpec(a.shape, lambda bi: (0,) * a.ndim)
    kernel = functools.partial(_lru_kernel, seq=s, rows=rows)
    return pl.pallas_call(
        kernel,
        grid=(b,),
        in_specs=[seqblk, seqblk, full(conv_w), full(conv_b), full(wg), full(bg), full(lru_lambda)],
        out_specs=seqblk,
        out_shape=jax.ShapeDtypeStruct((b, s, width), BF16),
        scratch_shapes=[pltpu.VMEM((s + 2 * SUBLANES, width), F32)]
        + [pltpu.VMEM((s, width), F32)] * 4,
        compiler_params=pltpu.CompilerParams(
            dimension_semantics=("arbitrary",), vmem_limit_bytes=VMEM_LIMIT),
        name="lru",
    )(xr, gr, conv_w, conv_b, wg, bg, lru_lambda)


def _outproj_kernel(x_ref, a_ref, l_ref, wa_ref, wl_ref, g_ref, xm_ref, h_ref):
    xm = x_ref[...] + _dot(a_ref[...], wa_ref[...]) + _dot(l_ref[...], wl_ref[...])
    xm_ref[...] = xm
    h_ref[...] = _rms(xm, g_ref[...]).astype(h_ref.dtype)


def _outproj(x2, attn2, lru2, w_attn, w_lru, g, tm):
    n, d = x2.shape
    row = lambda i: (i, 0)
    const = lambda i: (0, 0)
    return pl.pallas_call(
        _outproj_kernel,
        grid=(n // tm,),
        in_specs=[pl.BlockSpec((tm, d), row),
                  pl.BlockSpec((tm, attn2.shape[1]), row),
                  pl.BlockSpec((tm, lru2.shape[1]), row),
                  pl.BlockSpec(w_attn.shape, const),
                  pl.BlockSpec(w_lru.shape, const),
                  pl.BlockSpec((1, d), const)],
        out_specs=[pl.BlockSpec((tm, d), row)] * 2,
        out_shape=[jax.ShapeDtypeStruct((n, d), F32), jax.ShapeDtypeStruct((n, d), BF16)],
        compiler_params=pltpu.CompilerParams(
            dimension_semantics=("arbitrary",), vmem_limit_bytes=VMEM_LIMIT),
        name="outproj",
    )(x2, attn2, lru2, w_attn, w_lru, g)


def _ffn_kernel(hp_ref, hm_ref, hn_ref, xm_ref, wup_ref, cw_ref, cb_ref, wdn_ref, fg_ref, o_ref,
                perm_ref, hext_ref, ua_ref, ub_ref, acta_ref, actb_ref, acc_ref,
                *, tile, chunk, d_ff):
    i = pl.program_id(1)
    nchunks = d_ff // chunk
    ngrp = tile // SUBLANES
    nslab = perm_ref.shape[0]
    d = nslab * LANES

    for s in range(SUBLANES):
        rows = hm_ref[s * ngrp:(s + 1) * ngrp, :].astype(F32)
        for n in range(nslab):
            perm_ref[n, pl.ds(s, ngrp, stride=SUBLANES), :] = rows[:, n * LANES:(n + 1) * LANES]
    for n in range(nslab):
        hext_ref[0:tile, n * LANES:(n + 1) * LANES] = perm_ref[n].astype(BF16)
    prev = jnp.where(i > 0, hp_ref[BF16_ROWS - 1:BF16_ROWS, :].astype(F32), 0.0)
    nxt = jnp.where(i < pl.num_programs(1) - 1, hn_ref[0:1, :].astype(F32), 0.0)
    hrow = lax.broadcasted_iota(jnp.int32, (BF16_ROWS, d), 0)
    halo = jnp.where(hrow == 0, prev, jnp.where(hrow == 1, nxt, 0.0))
    hext_ref[tile:tile + BF16_ROWS, :] = halo.astype(BF16)
    acc_ref[...] = jnp.zeros_like(acc_ref)

    def offsets(c):
        og, ov = c * chunk, d_ff + c * chunk
        if isinstance(c, int):
            return og, ov
        return pl.multiple_of(og, chunk), pl.multiple_of(ov, chunk)

    def up(c, u_ref):
        og, ov = offsets(c)
        hext = hext_ref[...]
        u_ref[:, 0:chunk] = _dot(hext, wup_ref[:, pl.ds(og, chunk)])
        u_ref[:, chunk:2 * chunk] = _dot(hext, wup_ref[:, pl.ds(ov, chunk)])

    def glu(c, u_ref, act_ref):
        sub = lax.broadcasted_iota(jnp.int32, (SUBLANES, chunk), 0)
        blk = BF16_ROWS

        def conv(col0, off, r0):
            cols = slice(col0, col0 + chunk)
            w = cw_ref[:, pl.ds(off, chunk)]
            cur = u_ref[r0:r0 + blk, cols]
            if r0 == 0:
                first = jnp.where(sub == 0, u_ref[tile:tile + 1, cols],
                                  pltpu.roll(u_ref[tile - SUBLANES:tile, cols], 1, 0))
                um1 = jnp.concatenate([first, cur[0:blk - SUBLANES]], axis=0)
            else:
                um1 = u_ref[r0 - SUBLANES:r0 + blk - SUBLANES, cols]
            if r0 + blk == tile:
                last = jnp.where(sub == SUBLANES - 1, u_ref[tile + 1:tile + 2, cols],
                                 pltpu.roll(u_ref[0:SUBLANES, cols], SUBLANES - 1, 0))
                up1 = jnp.concatenate([cur[SUBLANES:blk], last], axis=0)
            else:
                up1 = u_ref[r0 + SUBLANES:r0 + blk + SUBLANES, cols]
            return (cb_ref[:, pl.ds(off, chunk)] + um1 * w[0:1, :] + cur * w[1:2, :]
                    + up1 * w[2:3, :])

        og, ov = offsets(c)
        for r0 in range(0, tile, blk):
            gate = conv(0, og, r0)
            val = conv(chunk, ov, r0)
            act_ref[r0:r0 + blk, :] = (jax.nn.gelu(gate, approximate=True) * val).astype(BF16)

    def down(c, act_ref):
        og, _ = offsets(c)
        y = _dot(act_ref[...], wdn_ref[pl.ds(og, chunk), :])
        for n in range(nslab):
            acc_ref[n] += y[:, n * LANES:(n + 1) * LANES]

    u_refs, act_refs = (ua_ref, ub_ref), (acta_ref, actb_ref)

    def step(t, parity, first=False, last=False):
        if not last:
            up(t + 1, u_refs[1 - parity])
        glu(t, u_refs[parity], act_refs[parity])
        if not first:
            down(t - 1, act_refs[1 - parity])

    def body(t, carry):
        lax.cond(t % 2 == 1, lambda: step(t, 1), lambda: step(t, 0))
        return carry

    assert nchunks % 2 == 1 and nchunks >= 3
    up(0, ua_ref)
    step(0, 0, first=True)
    lax.fori_loop(1, nchunks - 1, body, 0)
    step(nchunks - 1, 0, last=True)
    down(nchunks - 1, acta_ref)
    for s in range(SUBLANES):
        rs = slice(s * ngrp, (s + 1) * ngrp)
        y = jnp.concatenate([acc_ref[n, pl.ds(s, ngrp, stride=SUBLANES), :] for n in range(nslab)],
                            axis=1)
        o_ref[rs, :] = _rms(xm_ref[rs, :] + y, fg_ref[...])


def _ffn(h2, xm, w_up, conv_w, conv_b, w_down, final_g, tile, chunk):
    b, s, d = h2.shape
    d_ff = w_down.shape[0]
    nh = tile // BF16_ROWS
    last_halo = s // BF16_ROWS - 1
    main = lambda bi, i: (bi, i, 0)
    const = lambda bi, i: (0, 0)
    single = dict(pipeline_mode=pl.Buffered(1))
    kernel = functools.partial(_ffn_kernel, tile=tile, chunk=chunk, d_ff=d_ff)
    return pl.pallas_call(
        kernel,
        grid=(b, s // tile),
        in_specs=[pl.BlockSpec((None, BF16_ROWS, d), lambda bi, i: (bi, jnp.maximum(i * nh - 1, 0), 0)),
                  pl.BlockSpec((None, tile, d), main),
                  pl.BlockSpec((None, BF16_ROWS, d),
                               lambda bi, i: (bi, jnp.minimum((i + 1) * nh, last_halo), 0)),
                  pl.BlockSpec((None, tile, d), main),
                  pl.BlockSpec(w_up.shape, const, **single),
                  pl.BlockSpec(conv_w.shape, const),
                  pl.BlockSpec(conv_b.shape, const),
                  pl.BlockSpec(w_down.shape, const, **single),
                  pl.BlockSpec((1, d), const)],
        out_specs=pl.BlockSpec((None, tile, d), main),
        out_shape=jax.ShapeDtypeStruct((b, s, d), F32),
        scratch_shapes=[pltpu.VMEM((d // LANES, tile, LANES), F32),
                        pltpu.VMEM((tile + BF16_ROWS, d), BF16)]
        + [pltpu.VMEM((tile + BF16_ROWS, 2 * chunk), F32)] * 2
        + [pltpu.VMEM((tile, chunk), BF16)] * 2
        + [pltpu.VMEM((d // LANES, tile, LANES), F32)],
        compiler_params=pltpu.CompilerParams(
            dimension_semantics=("arbitrary",) * 2, vmem_limit_bytes=VMEM_LIMIT),
        name="ffn",
    )(h2, h2, h2, xm, w_up, conv_w, conv_b, w_down, final_g)


def _gate_weights(w_a, b_a, w_x, b_x):
    ndir, nblk, bd, _ = w_a.shape
    per = LANES // bd
    npair = nblk // per

    def blockdiag(w):
        w = w.reshape(npair, per, bd, bd)
        eye = jnp.eye(per, dtype=w.dtype)
        return jnp.einsum('pbij,bc->pbicj', w, eye).reshape(npair, LANES, LANES)

    ws, bs = [], []
    for d in range(ndir):
        for w, bias in ((w_a, b_a), (w_x, b_x)):
            ws.append(blockdiag(w[d]))
            bs.append(bias[d].reshape(npair, LANES))
    return jnp.concatenate(ws, axis=-1).astype(BF16), jnp.concatenate(bs, axis=-1).astype(F32)


def kernel(x, attn_norm_g, w_in, lambda_q1, lambda_k1, lambda_q2, lambda_k2, subln_g,
           lru_conv_w, lru_conv_b, lru_w_a, lru_b_a, lru_w_x, lru_b_x, lru_lambda,
           w_out, ffn_norm_g, w_up, ffn_conv_w, ffn_conv_b, w_down, final_norm_g):
    b, s, d = x.shape
    depth = w_in.shape[0]
    x2 = x.reshape(b * s, d)
    for l in range(depth):
        lambda_init = 0.8 - 0.6 * math.exp(-0.3 * l)
        w_in_bf = w_in[l].astype(BF16)
        q, k, vt, xr, gr = _inproj(x2, attn_norm_g[l][None], w_in_bf,
                                   w_in_bf[:, 2 * ATTN_WIDTH:3 * ATTN_WIDTH].T, tm=512)
        lw = xr.shape[-1]
        attn = _attention(q.reshape(b, s, -1), k.reshape(b, s, -1), vt,
                          lambda_q1[l][None], lambda_k1[l][None], lambda_q2[l][None],
                          lambda_k2[l][None], subln_g[l][:, None], lambda_init, tq=256, nsub=2)
        wg, bg = _gate_weights(lru_w_a[l], lru_b_a[l], lru_w_x[l], lru_b_x[l])
        lru = _lru(xr.reshape(b, s, lw), gr.reshape(b, s, lw), lru_conv_w[l], lru_conv_b[l][None],
                   wg, bg, lru_lambda[l], rows=256)
        wo = w_out[l].astype(BF16)
        xm, h2 = _outproj(x2, attn.reshape(b * s, -1), lru.reshape(b * s, -1),
                          wo[:ATTN_WIDTH], wo[ATTN_WIDTH:], ffn_norm_g[l][None], tm=512)
        assert depth == 1
        x2 = _ffn(h2.reshape(b, s, d), xm.reshape(b, s, d), w_up[l].astype(BF16), ffn_conv_w[l],
                  ffn_conv_b[l][None], w_down[l].astype(BF16), final_norm_g[None],
                  tile=512, chunk=256).reshape(b * s, d)
    return x2.reshape(b, s, d)
```

```python
import functools
import math

import jax
import jax.numpy as jnp
from jax import lax
from jax.experimental import pallas as pl
from jax.experimental.pallas import tpu as pltpu

F32 = jnp.float32
BF16 = jnp.bfloat16

N_HEADS = 4
HEAD_DIM = 64
V_DIM = 2 * HEAD_DIM
ATTN_WIDTH = N_HEADS * V_DIM
LRU_BLOCK = 64
LRU_CONV_WIDTH = 4
LRU_CONV_LEFT = 2
LRU_C = 8.0
FFN_CONV_LEFT = 1
NORM_EPS = 1e-6
LANES = 128
SUBLANES = 8
BF16_ROWS = 16
VMEM_LIMIT = 56 * 1024 * 1024


def _rms(x, g):
    return (x * lax.rsqrt(jnp.mean(x * x, axis=-1, keepdims=True) + NORM_EPS)) * g


def _dot(a, b):
    return jnp.dot(a, b, preferred_element_type=F32)


_NT = (((1,), (1,)), ((), ()))


def _inproj_kernel(x_ref, g_ref, w_ref, wvt_ref, q_ref, k_ref, vt_ref, xr_ref, gr_ref):
    hb = _rms(x_ref[...], g_ref[...]).astype(BF16)
    aw = ATTN_WIDTH
    q_ref[...] = (_dot(hb, w_ref[:, 0:aw]) * (HEAD_DIM ** -0.5)).astype(BF16)
    k_ref[...] = _dot(hb, w_ref[:, aw:2 * aw]).astype(BF16)
    vt_ref[...] = lax.dot_general(wvt_ref[...], hb, _NT, preferred_element_type=F32).astype(BF16)
    lw = xr_ref.shape[-1]
    xr_ref[...] = _dot(hb, w_ref[:, 3 * aw:3 * aw + lw])
    gr_ref[...] = _dot(hb, w_ref[:, 3 * aw + lw:3 * aw + 2 * lw])


def _inproj(x2, g, w_bf, wvt_bf, tm):
    n, d = x2.shape
    lw = (w_bf.shape[1] - 3 * ATTN_WIDTH) // 2
    row = lambda i: (i, 0)
    const = lambda i: (0, 0)
    return pl.pallas_call(
        _inproj_kernel,
        grid=(n // tm,),
        in_specs=[pl.BlockSpec((tm, d), row),
                  pl.BlockSpec((1, d), const),
                  pl.BlockSpec(w_bf.shape, const),
                  pl.BlockSpec(wvt_bf.shape, const)],
        out_specs=[pl.BlockSpec((tm, ATTN_WIDTH), row)] * 2
        + [pl.BlockSpec((ATTN_WIDTH, tm), lambda i: (0, i))]
        + [pl.BlockSpec((tm, lw), row)] * 2,
        out_shape=[jax.ShapeDtypeStruct((n, ATTN_WIDTH), BF16)] * 2
        + [jax.ShapeDtypeStruct((ATTN_WIDTH, n), BF16)]
        + [jax.ShapeDtypeStruct((n, lw), F32)] * 2,
        compiler_params=pltpu.CompilerParams(
            dimension_semantics=("arbitrary",), vmem_limit_bytes=VMEM_LIMIT),
        name="inproj",
    )(x2, g, w_bf, wvt_bf)


def _attn_kernel(lq1_ref, lk1_ref, lq2_ref, lk2_ref, sg_ref, q_ref, k_ref, vt_ref, o_ref,
                 kf_ref, dist_ref, *se_refs, tq, nsub, seq, lambda_init):
    h = pl.program_id(1)
    qi = pl.program_id(2)
    s_refs, e_refs = se_refs[:nsub], se_refs[nsub:]
    nblk = seq // tq
    assert nblk & (nblk - 1) == 0 and 3 * nblk <= LANES and tq <= 256
    shift = nblk.bit_length() - 1

    @pl.when((pl.program_id(0) == 0) & (h == 0) & (qi == 0))
    def _init():
        lane = lax.broadcasted_iota(jnp.int32, (seq, LANES), 1)
        row = lax.broadcasted_iota(jnp.int32, (seq, LANES), 0)
        grp = lane >> shift
        hit = (row // tq) == (lane & (nblk - 1))
        dj = (row % tq).astype(F32)
        kf_ref[...] = jnp.where(hit & (grp < 2), 1.0,
                                jnp.where(hit & (grp == 2), dj, 0.0)).astype(BF16)
        r = lax.broadcasted_iota(jnp.int32, (tq, tq), 0)
        c = lax.broadcasted_iota(jnp.int32, (tq, tq), 1)
        dist_ref[...] = jnp.abs(r - c).astype(F32)

    lam = (jnp.exp(jnp.sum(lq1_ref[...] * lk1_ref[...], axis=-1, keepdims=True))
           - jnp.exp(jnp.sum(lq2_ref[...] * lk2_ref[...], axis=-1, keepdims=True))
           + lambda_init)
    slope = jnp.where(h == 0, 2.0 ** -2, jnp.where(h == 1, 2.0 ** -4,
                      jnp.where(h == 2, 2.0 ** -6, 2.0 ** -8))).astype(F32)

    kaug = jnp.concatenate([k_ref[...], kf_ref[...]], axis=1)
    vt_ones = jnp.concatenate([vt_ref[...], jnp.ones((BF16_ROWS, seq), BF16)], axis=0)
    lane = lax.broadcasted_iota(jnp.int32, (tq, LANES), 1)
    di = lax.broadcasted_iota(jnp.int32, (tq, LANES), 0).astype(F32)
    grp = lane >> shift
    strip = 4 * SUBLANES

    def scores(sb):
        blk = qi * nsub + sb
        diff = blk - (lane & (nblk - 1))
        sign = jnp.where(diff > 0, 1.0, jnp.where(diff < 0, -1.0, 0.0))
        qf = jnp.where(grp == 0, -slope * sign * di,
                       jnp.where(grp == 1, -slope * tq * jnp.abs(diff).astype(F32),
                                 jnp.where(grp == 2, slope * sign, 0.0))).astype(BF16)
        q = q_ref[sb * tq:(sb + 1) * tq, :]
        zero = jnp.zeros_like(q)
        qaug = jnp.concatenate(
            [jnp.concatenate([jnp.where(lane < HEAD_DIM, q, zero), qf], axis=1),
             jnp.concatenate([jnp.where(lane >= HEAD_DIM, q, zero), qf], axis=1)], axis=0)
        s_ref = s_refs[sb]
        s_ref[...] = lax.dot_general(kaug, qaug, _NT, preferred_element_type=F32)
        diag = pl.ds(pl.multiple_of(blk * tq, tq), tq)
        diag_bias = slope * dist_ref[...]
        s_ref[diag, 0:tq] = s_ref[diag, 0:tq] - diag_bias
        s_ref[diag, tq:2 * tq] = s_ref[diag, tq:2 * tq] - diag_bias

    def finish(sb):
        s_ref, e_ref = s_refs[sb], e_refs[sb]
        mx = s_ref[0:strip, :]
        for r in range(strip, seq, strip):
            mx = jnp.maximum(mx, s_ref[r:r + strip, :])
        mx = jnp.max(mx, axis=0, keepdims=True)
        for r in range(0, seq, strip):
            e_ref[r:r + strip, :] = jnp.exp(s_ref[r:r + strip, :] - mx).astype(BF16)
        o12 = _dot(vt_ones, e_ref[...])
        norm = o12[V_DIM:V_DIM + 1, :]
        o12 = o12[0:V_DIM, :]
        o = o12[:, 0:tq] * (1.0 / norm[:, 0:tq]) - o12[:, tq:2 * tq] * (lam / norm[:, tq:2 * tq])
        o = o * lax.rsqrt(jnp.mean(o * o, axis=0, keepdims=True) + NORM_EPS)
        o = o * sg_ref[...] * (1.0 - lambda_init)
        o_ref[sb * tq:(sb + 1) * tq, :] = o.T.astype(o_ref.dtype)

    scores(0)
    for sb in range(nsub):
        if sb + 1 < nsub:
            scores(sb + 1)
        finish(sb)


def _attention(q, k, vt, lq1, lk1, lq2, lk2, subln_g, lambda_init, tq, nsub):
    b, s, _ = q.shape
    vec = lambda bi, h, qi: (0, 0)
    tstep = tq * nsub
    kernel = functools.partial(_attn_kernel, tq=tq, nsub=nsub, seq=s, lambda_init=lambda_init)
    return pl.pallas_call(
        kernel,
        grid=(b, N_HEADS, s // tstep),
        in_specs=[pl.BlockSpec((1, HEAD_DIM), vec)] * 4
        + [pl.BlockSpec((V_DIM, 1), vec),
           pl.BlockSpec((None, tstep, V_DIM), lambda bi, h, qi: (bi, qi, h)),
           pl.BlockSpec((None, s, V_DIM), lambda bi, h, qi: (bi, 0, h)),
           pl.BlockSpec((V_DIM, s), lambda bi, h, qi: (h, bi))],
        out_specs=pl.BlockSpec((None, tstep, V_DIM), lambda bi, h, qi: (bi, qi, h)),
        out_shape=jax.ShapeDtypeStruct((b, s, ATTN_WIDTH), BF16),
        scratch_shapes=[pltpu.VMEM((s, LANES), BF16), pltpu.VMEM((tq, tq), F32)]
        + [pltpu.VMEM((s, 2 * tq), F32)] * nsub + [pltpu.VMEM((s, 2 * tq), BF16)] * nsub,
        compiler_params=pltpu.CompilerParams(
            dimension_semantics=("arbitrary",) * 3, vmem_limit_bytes=VMEM_LIMIT),
        name="attn",
    )(lq1, lk1, lq2, lk2, subln_g, q, k, vt)


def _local_scan(a, u, reverse):
    row = lax.broadcasted_iota(jnp.int32, a.shape, 0)
    for d in (1, 2, 4):
        shift = SUBLANES - d if reverse else d
        valid = (row < SUBLANES - d) if reverse else (row >= d)
        a_s = jnp.where(valid, pltpu.roll(a, shift, 0), 1.0)
        u_s = jnp.where(valid, pltpu.roll(u, shift, 0), 0.0)
        u = a * u_s + u
        a = a * a_s
    return a, u


def _lru_kernel(xr_ref, gr_ref, cw_ref, cb_ref, wg_ref, bg_ref, lam_ref, o_ref,
                xp_ref, xc_ref, h_ref, cum_ref, *, seq, rows):
    width = xr_ref.shape[-1]
    ntile = width // LANES
    grp = seq // SUBLANES
    left = LRU_CONV_LEFT
    right = LRU_CONV_WIDTH - 1 - LRU_CONV_LEFT
    top = left * SUBLANES
    sub = lax.broadcasted_iota(jnp.int32, (SUBLANES, LANES), 0)

    for s in range(SUBLANES):
        blk = xr_ref[s * grp:(s + 1) * grp, :]
        for p in range(ntile):
            xp_ref[p, pl.ds(top + s, grp, stride=SUBLANES), :] = blk[:, p * LANES:(p + 1) * LANES]
    for p in range(ntile):
        for k in range(left):
            src = xp_ref[p, top + (grp - 1 - k) * SUBLANES:top + (grp - k) * SUBLANES, :]
            xp_ref[p, top - (k + 1) * SUBLANES:top - k * SUBLANES, :] = jnp.where(
                sub == 0, 0.0, pltpu.roll(src, 1, 0))
        for k in range(right):
            src = xp_ref[p, top + k * SUBLANES:top + (k + 1) * SUBLANES, :]
            xp_ref[p, top + (grp + k) * SUBLANES:top + (grp + k + 1) * SUBLANES, :] = jnp.where(
                sub == SUBLANES - 1, 0.0, pltpu.roll(src, SUBLANES - 1, 0))

    neg_lam = -lam_ref[...]
    softplus = jnp.maximum(neg_lam, 0.0) + jnp.log1p(jnp.exp(-jnp.abs(neg_lam)))
    rate = LRU_C * softplus
    rate_log2 = -rate * math.log2(math.e)

    nchunk = seq // rows
    for c in range(nchunk):
        r0 = c * rows
        for p in range(ntile):
            cols = slice(p * LANES, (p + 1) * LANES)
            xc = cb_ref[:, cols]
            for tap in range(LRU_CONV_WIDTH):
                start = top + r0 + (tap - left) * SUBLANES
                xc = xc + xp_ref[p, start:start + rows, :] * cw_ref[tap:tap + 1, cols]
            xc_ref[p, r0:r0 + rows, :] = xc

    zero = jnp.zeros((SUBLANES, LANES), F32)
    one = jnp.ones((SUBLANES, LANES), F32)
    ends = [[(zero, one), (zero, one)] for _ in range(ntile)]
    for c in range(nchunk):
        for p in range(ntile):
            cols = slice(p * LANES, (p + 1) * LANES)
            for d in range(2):
                r0 = (c if d == 0 else nchunk - 1 - c) * rows
                gcols = slice(2 * d * LANES, (2 * d + 2) * LANES)
                xc = xc_ref[p, r0:r0 + rows, :]
                gates = _dot(xc.astype(BF16), wg_ref[p, :, gcols]) + bg_ref[p:p + 1, gcols]
                r = jax.nn.sigmoid(gates[:, 0:LANES])
                i = jax.nn.sigmoid(gates[:, LANES:2 * LANES])
                t = jnp.tanh(r * rate[d:d + 1, cols])
                tt = t + t
                prod = tt * (1.0 + t)
                mult = jnp.where(prod > 0.0, tt * lax.rsqrt(prod), 0.0)
                a = jnp.exp2(r * rate_log2[d:d + 1, cols])
                u = mult * (i * xc)
                h, cum = ends[p][d]
                vrows = range(rows // SUBLANES)
                for j in (vrows if d == 0 else reversed(vrows)):
                    rs = slice(j * SUBLANES, (j + 1) * SUBLANES)
                    h = a[rs] * h + u[rs]
                    cum = a[rs] * cum
                    h_ref[d, p, r0 + j * SUBLANES:r0 + (j + 1) * SUBLANES, :] = h
                    cum_ref[d, p, r0 + j * SUBLANES:r0 + (j + 1) * SUBLANES, :] = cum
                ends[p][d] = (h, cum)

    enter = []
    for p in range(ntile):
        for d in range(2):
            h_end, cum_end = ends[p][d]
            _, chained = _local_scan(cum_end, h_end, reverse=(d == 1))
            if d == 0:
                enter.append(jnp.where(sub == 0, 0.0, pltpu.roll(chained, 1, 0)))
            else:
                enter.append(jnp.where(sub == SUBLANES - 1, 0.0,
                                       pltpu.roll(chained, SUBLANES - 1, 0)))

    for c in range(seq // rows):
        rs = slice(c * rows, (c + 1) * rows)
        for p in range(ntile):
            y = None
            for d in range(2):
                init = jnp.tile(enter[2 * p + d], (rows // SUBLANES, 1))
                part = h_ref[d, p, rs, :] + cum_ref[d, p, rs, :] * init
                y = part if y is None else y + part
            xc_ref[p, rs, :] = y

    for s in range(SUBLANES):
        rs = slice(s * grp, (s + 1) * grp)
        for p in range(ntile):
            cols = slice(p * LANES, (p + 1) * LANES)
            y = xc_ref[p, pl.ds(s, grp, stride=SUBLANES), :]
            o_ref[rs, cols] = (jax.nn.gelu(gr_ref[rs, cols], approximate=True) * y).astype(o_ref.dtype)


def _lru(xr, gr, conv_w, conv_b, wg, bg, lru_lambda, rows):
    b, s, width = xr.shape
    ntile = width // LANES
    seqblk = pl.BlockSpec((None, s, width), lambda bi: (bi, 0, 0))
    full = lambda a: pl.BlockSpec(a.shape, lambda bi: (0,) * a.ndim)
    kernel = functools.partial(_lru_kernel, seq=s, rows=rows)
    halo_rows = (LRU_CONV_WIDTH - 1) * SUBLANES
    return pl.pallas_call(
        kernel,
        grid=(b,),
        in_specs=[seqblk, seqblk, full(conv_w), full(conv_b), full(wg), full(bg), full(lru_lambda)],
        out_specs=seqblk,
        out_shape=jax.ShapeDtypeStruct((b, s, width), BF16),
        scratch_shapes=[pltpu.VMEM((ntile, s + halo_rows, LANES), F32),
                        pltpu.VMEM((ntile, s, LANES), F32),
                        pltpu.VMEM((2, ntile, s, LANES), F32),
                        pltpu.VMEM((2, ntile, s, LANES), F32)],
        compiler_params=pltpu.CompilerParams(
            dimension_semantics=("arbitrary",), vmem_limit_bytes=VMEM_LIMIT),
        name="lru",
    )(xr, gr, conv_w, conv_b, wg, bg, lru_lambda)


def _outproj_kernel(x_ref, a_ref, l_ref, wa_ref, wl_ref, g_ref, xm_ref, h_ref):
    xm = x_ref[...] + _dot(a_ref[...], wa_ref[...]) + _dot(l_ref[...], wl_ref[...])
    xm_ref[...] = xm
    h_ref[...] = _rms(xm, g_ref[...]).astype(h_ref.dtype)


def _outproj(x2, attn2, lru2, w_attn, w_lru, g, tm):
    n, d = x2.shape
    row = lambda i: (i, 0)
    const = lambda i: (0, 0)
    return pl.pallas_call(
        _outproj_kernel,
        grid=(n // tm,),
        in_specs=[pl.BlockSpec((tm, d), row),
                  pl.BlockSpec((tm, attn2.shape[1]), row),
                  pl.BlockSpec((tm, lru2.shape[1]), row),
                  pl.BlockSpec(w_attn.shape, const),
                  pl.BlockSpec(w_lru.shape, const),
                  pl.BlockSpec((1, d), const)],
        out_specs=[pl.BlockSpec((tm, d), row)] * 2,
        out_shape=[jax.ShapeDtypeStruct((n, d), F32), jax.ShapeDtypeStruct((n, d), BF16)],
        compiler_params=pltpu.CompilerParams(
            dimension_semantics=("arbitrary",), vmem_limit_bytes=VMEM_LIMIT),
        name="outproj",
    )(x2, attn2, lru2, w_attn, w_lru, g)


def _ffn_kernel(hp_ref, hm_ref, hn_ref, xm_ref, wup_ref, cw_ref, cb_ref, wdn_ref, fg_ref, o_ref,
                perm_ref, hext_ref, ua_ref, ub_ref, acta_ref, actb_ref, acc_ref,
                *, tile, chunk, d_ff):
    i = pl.program_id(1)
    nchunks = d_ff // chunk
    ngrp = tile // SUBLANES
    nslab = perm_ref.shape[0]
    d = nslab * LANES

    for s in range(SUBLANES):
        rows = hm_ref[s * ngrp:(s + 1) * ngrp, :].astype(F32)
        for n in range(nslab):
            perm_ref[n, pl.ds(s, ngrp, stride=SUBLANES), :] = rows[:, n * LANES:(n + 1) * LANES]
    for n in range(nslab):
        hext_ref[0:tile, n * LANES:(n + 1) * LANES] = perm_ref[n].astype(BF16)
    prev = jnp.where(i > 0, hp_ref[BF16_ROWS - 1:BF16_ROWS, :].astype(F32), 0.0)
    nxt = jnp.where(i < pl.num_programs(1) - 1, hn_ref[0:1, :].astype(F32), 0.0)
    hrow = lax.broadcasted_iota(jnp.int32, (BF16_ROWS, d), 0)
    halo = jnp.where(hrow == 0, prev, jnp.where(hrow == 1, nxt, 0.0))
    hext_ref[tile:tile + BF16_ROWS, :] = halo.astype(BF16)
    acc_ref[...] = jnp.zeros_like(acc_ref)

    def offsets(c):
        og, ov = c * chunk, d_ff + c * chunk
        if isinstance(c, int):
            return og, ov
        return pl.multiple_of(og, chunk), pl.multiple_of(ov, chunk)

    def up(c, u_ref):
        og, ov = offsets(c)
        hext = hext_ref[...]
        u_ref[:, 0:chunk] = _dot(hext, wup_ref[:, pl.ds(og, chunk)])
        u_ref[:, chunk:2 * chunk] = _dot(hext, wup_ref[:, pl.ds(ov, chunk)])

    def glu(c, u_ref, act_ref):
        sub = lax.broadcasted_iota(jnp.int32, (SUBLANES, chunk), 0)
        blk = BF16_ROWS

        def conv(col0, off, r0):
            cols = slice(col0, col0 + chunk)
            w = cw_ref[:, pl.ds(off, chunk)]
            cur = u_ref[r0:r0 + blk, cols]
            if r0 == 0:
                first = jnp.where(sub == 0, u_ref[tile:tile + 1, cols],
                                  pltpu.roll(u_ref[tile - SUBLANES:tile, cols], 1, 0))
                um1 = jnp.concatenate([first, cur[0:blk - SUBLANES]], axis=0)
            else:
                um1 = u_ref[r0 - SUBLANES:r0 + blk - SUBLANES, cols]
            if r0 + blk == tile:
                last = jnp.where(sub == SUBLANES - 1, u_ref[tile + 1:tile + 2, cols],
                                 pltpu.roll(u_ref[0:SUBLANES, cols], SUBLANES - 1, 0))
                up1 = jnp.concatenate([cur[SUBLANES:blk], last], axis=0)
            else:
                up1 = u_ref[r0 + SUBLANES:r0 + blk + SUBLANES, cols]
            return (cb_ref[:, pl.ds(off, chunk)] + um1 * w[0:1, :] + cur * w[1:2, :]
                    + up1 * w[2:3, :])

        og, ov = offsets(c)
        for r0 in range(0, tile, blk):
            gate = conv(0, og, r0)
            val = conv(chunk, ov, r0)
            act_ref[r0:r0 + blk, :] = (jax.nn.gelu(gate, approximate=True) * val).astype(BF16)

    def down(c, act_ref):
        og, _ = offsets(c)
        y = _dot(act_ref[...], wdn_ref[pl.ds(og, chunk), :])
        for n in range(nslab):
            acc_ref[n] += y[:, n * LANES:(n + 1) * LANES]

    u_refs, act_refs = (ua_ref, ub_ref), (acta_ref, actb_ref)

    def step(t, parity, first=False, last=False):
        if not last:
            up(t + 1, u_refs[1 - parity])
        glu(t, u_refs[parity], act_refs[parity])
        if not first:
            down(t - 1, act_refs[1 - parity])

    def body(t, carry):
        lax.cond(t % 2 == 1, lambda: step(t, 1), lambda: step(t, 0))
        return carry

    assert nchunks % 2 == 1 and nchunks >= 3
    up(0, ua_ref)
    step(0, 0, first=True)
    lax.fori_loop(1, nchunks - 1, body, 0)
    step(nchunks - 1, 0, last=True)
    down(nchunks - 1, acta_ref)
    for s in range(SUBLANES):
        rs = slice(s * ngrp, (s + 1) * ngrp)
        y = jnp.concatenate([acc_ref[n, pl.ds(s, ngrp, stride=SUBLANES), :] for n in range(nslab)],
                            axis=1)
        o_ref[rs, :] = _rms(xm_ref[rs, :] + y, fg_ref[...])


def _ffn(h2, xm, w_up, conv_w, conv_b, w_down, final_g, tile, chunk):
    b, s, d = h2.shape
    d_ff = w_down.shape[0]
    nh = tile // BF16_ROWS
    last_halo = s // BF16_ROWS - 1
    main = lambda bi, i: (bi, i, 0)
    const = lambda bi, i: (0, 0)
    single = dict(pipeline_mode=pl.Buffered(1))
    kernel = functools.partial(_ffn_kernel, tile=tile, chunk=chunk, d_ff=d_ff)
    return pl.pallas_call(
        kernel,
        grid=(b, s // tile),
        in_specs=[pl.BlockSpec((None, BF16_ROWS, d), lambda bi, i: (bi, jnp.maximum(i * nh - 1, 0), 0)),
                  pl.BlockSpec((None, tile, d), main),
                  pl.BlockSpec((None, BF16_ROWS, d),
                               lambda bi, i: (bi, jnp.minimum((i + 1) * nh, last_halo), 0)),
                  pl.BlockSpec((None, tile, d), main),
                  pl.BlockSpec(w_up.shape, const, **single),
                  pl.BlockSpec(conv_w.shape, const),
                  pl.BlockSpec(conv_b.shape, const),
                  pl.BlockSpec(w_down.shape, const, **single),
                  pl.BlockSpec((1, d), const)],
        out_specs=pl.BlockSpec((None, tile, d), main),
        out_shape=jax.ShapeDtypeStruct((b, s, d), F32),
        scratch_shapes=[pltpu.VMEM((d // LANES, tile, LANES), F32),
                        pltpu.VMEM((tile + BF16_ROWS, d), BF16)]
        + [pltpu.VMEM((tile + BF16_ROWS, 2 * chunk), F32)] * 2
        + [pltpu.VMEM((tile, chunk), BF16)] * 2
        + [pltpu.VMEM((d // LANES, tile, LANES), F32)],
        compiler_params=pltpu.CompilerParams(
            dimension_semantics=("arbitrary",) * 2, vmem_limit_bytes=VMEM_LIMIT),
        name="ffn",
    )(h2, h2, h2, xm, w_up, conv_w, conv_b, w_down, final_g)


def _gate_weights(w_a, b_a, w_x, b_x):
    ndir, nblk, bd, _ = w_a.shape
    per = LANES // bd
    npair = nblk // per

    def blockdiag(w):
        w = w.reshape(npair, per, bd, bd)
        eye = jnp.eye(per, dtype=w.dtype)
        return jnp.einsum('pbij,bc->pbicj', w, eye).reshape(npair, LANES, LANES)

    ws, bs = [], []
    for d in range(ndir):
        for w, bias in ((w_a, b_a), (w_x, b_x)):
            ws.append(blockdiag(w[d]))
            bs.append(bias[d].reshape(npair, LANES))
    return jnp.concatenate(ws, axis=-1).astype(BF16), jnp.concatenate(bs, axis=-1).astype(F32)


def kernel(x, attn_norm_g, w_in, lambda_q1, lambda_k1, lambda_q2, lambda_k2, subln_g,
           lru_conv_w, lru_conv_b, lru_w_a, lru_b_a, lru_w_x, lru_b_x, lru_lambda,
           w_out, ffn_norm_g, w_up, ffn_conv_w, ffn_conv_b, w_down, final_norm_g):
    b, s, d = x.shape
    depth = w_in.shape[0]
    x2 = x.reshape(b * s, d)
    for l in range(depth):
        lambda_init = 0.8 - 0.6 * math.exp(-0.3 * l)
        w_in_bf = w_in[l].astype(BF16)
        q, k, vt, xr, gr = _inproj(x2, attn_norm_g[l][None], w_in_bf,
                                   w_in_bf[:, 2 * ATTN_WIDTH:3 * ATTN_WIDTH].T, tm=512)
        lw = xr.shape[-1]
        attn = _attention(q.reshape(b, s, -1), k.reshape(b, s, -1), vt,
                          lambda_q1[l][None], lambda_k1[l][None], lambda_q2[l][None],
                          lambda_k2[l][None], subln_g[l][:, None], lambda_init, tq=256, nsub=2)
        wg, bg = _gate_weights(lru_w_a[l], lru_b_a[l], lru_w_x[l], lru_b_x[l])
        lru = _lru(xr.reshape(b, s, lw), gr.reshape(b, s, lw), lru_conv_w[l], lru_conv_b[l][None],
                   wg, bg, lru_lambda[l], rows=256)
        wo = w_out[l].astype(BF16)
        xm, h2 = _outproj(x2, attn.reshape(b * s, -1), lru.reshape(b * s, -1),
                          wo[:ATTN_WIDTH], wo[ATTN_WIDTH:], ffn_norm_g[l][None], tm=512)
        assert depth == 1
        x2 = _ffn(h2.reshape(b, s, d), xm.reshape(b, s, d), w_up[l].astype(BF16), ffn_conv_w[l],
                  ffn_conv_b[l][None], w_down[l].astype(BF16), final_norm_g[None],
                  tile=512, chunk=256).reshape(b * s, d)
    return x2.reshape(b, s, d)
```

```python
import functools
import math

import jax
import jax.numpy as jnp
from jax import lax
from jax.experimental import pallas as pl
from jax.experimental.pallas import tpu as pltpu

F32 = jnp.float32
BF16 = jnp.bfloat16

N_HEADS = 4
HEAD_DIM = 64
V_DIM = 2 * HEAD_DIM
ATTN_WIDTH = N_HEADS * V_DIM
LRU_BLOCK = 64
LRU_CONV_WIDTH = 4
LRU_CONV_LEFT = 2
LRU_C = 8.0
FFN_CONV_LEFT = 1
NORM_EPS = 1e-6
LANES = 128
SUBLANES = 8
BF16_ROWS = 16
VMEM_LIMIT = 56 * 1024 * 1024


def _rms(x, g):
    return (x * lax.rsqrt(jnp.mean(x * x, axis=-1, keepdims=True) + NORM_EPS)) * g


def _dot(a, b):
    return jnp.dot(a, b, preferred_element_type=F32)


_NT = (((1,), (1,)), ((), ()))


def _inproj_kernel(x_ref, g_ref, w_ref, wvt_ref, q_ref, k_ref, vt_ref, xr_ref, gr_ref):
    hb = _rms(x_ref[...], g_ref[...]).astype(BF16)
    aw = ATTN_WIDTH
    q_ref[...] = (_dot(hb, w_ref[:, 0:aw]) * (HEAD_DIM ** -0.5)).astype(BF16)
    k_ref[...] = _dot(hb, w_ref[:, aw:2 * aw]).astype(BF16)
    vt_ref[...] = lax.dot_general(wvt_ref[...], hb, _NT, preferred_element_type=F32).astype(BF16)
    lw = xr_ref.shape[-1]
    xr_ref[...] = _dot(hb, w_ref[:, 3 * aw:3 * aw + lw])
    gr_ref[...] = _dot(hb, w_ref[:, 3 * aw + lw:3 * aw + 2 * lw])


def _inproj(x2, g, w_bf, wvt_bf, tm):
    n, d = x2.shape
    lw = (w_bf.shape[1] - 3 * ATTN_WIDTH) // 2
    row = lambda i: (i, 0)
    const = lambda i: (0, 0)
    return pl.pallas_call(
        _inproj_kernel,
        grid=(n // tm,),
        in_specs=[pl.BlockSpec((tm, d), row),
                  pl.BlockSpec((1, d), const),
                  pl.BlockSpec(w_bf.shape, const),
                  pl.BlockSpec(wvt_bf.shape, const)],
        out_specs=[pl.BlockSpec((tm, ATTN_WIDTH), row)] * 2
        + [pl.BlockSpec((ATTN_WIDTH, tm), lambda i: (0, i))]
        + [pl.BlockSpec((tm, lw), row)] * 2,
        out_shape=[jax.ShapeDtypeStruct((n, ATTN_WIDTH), BF16)] * 2
        + [jax.ShapeDtypeStruct((ATTN_WIDTH, n), BF16)]
        + [jax.ShapeDtypeStruct((n, lw), F32)] * 2,
        compiler_params=pltpu.CompilerParams(
            dimension_semantics=("arbitrary",), vmem_limit_bytes=VMEM_LIMIT),
        name="inproj",
    )(x2, g, w_bf, wvt_bf)


def _attn_kernel(lq1_ref, lk1_ref, lq2_ref, lk2_ref, sg_ref, q_ref, k_ref, vt_ref, o_ref,
                 kf_ref, dist_ref, *se_refs, tq, nsub, seq, lambda_init):
    h = pl.program_id(1)
    qi = pl.program_id(2)
    s_refs, e_refs, m_refs = se_refs[:nsub], se_refs[nsub:2 * nsub], se_refs[2 * nsub:]
    nblk = seq // tq
    assert nblk & (nblk - 1) == 0 and 3 * nblk <= LANES and tq <= 256
    shift = nblk.bit_length() - 1

    @pl.when((pl.program_id(0) == 0) & (h == 0) & (qi == 0))
    def _init():
        lane = lax.broadcasted_iota(jnp.int32, (seq, LANES), 1)
        row = lax.broadcasted_iota(jnp.int32, (seq, LANES), 0)
        grp = lane >> shift
        hit = (row // tq) == (lane & (nblk - 1))
        dj = (row % tq).astype(F32)
        kf_ref[...] = jnp.where(hit & (grp < 2), 1.0,
                                jnp.where(hit & (grp == 2), dj, 0.0)).astype(BF16)
        r = lax.broadcasted_iota(jnp.int32, (tq, tq), 0)
        c = lax.broadcasted_iota(jnp.int32, (tq, tq), 1)
        dist_ref[...] = jnp.abs(r - c).astype(F32)

    lam = (jnp.exp(jnp.sum(lq1_ref[...] * lk1_ref[...], axis=-1, keepdims=True))
           - jnp.exp(jnp.sum(lq2_ref[...] * lk2_ref[...], axis=-1, keepdims=True))
           + lambda_init)
    slope = jnp.where(h == 0, 2.0 ** -2, jnp.where(h == 1, 2.0 ** -4,
                      jnp.where(h == 2, 2.0 ** -6, 2.0 ** -8))).astype(F32)

    kaug = jnp.concatenate([k_ref[...], kf_ref[...]], axis=1)
    vt_ones = jnp.concatenate([vt_ref[...], jnp.ones((BF16_ROWS, seq), BF16)], axis=0)
    lane = lax.broadcasted_iota(jnp.int32, (tq, LANES), 1)
    di = lax.broadcasted_iota(jnp.int32, (tq, LANES), 0).astype(F32)
    grp = lane >> shift
    strip = 4 * SUBLANES

    def scores(sb):
        blk = qi * nsub + sb
        diff = blk - (lane & (nblk - 1))
        sign = jnp.where(diff > 0, 1.0, jnp.where(diff < 0, -1.0, 0.0))
        qf = jnp.where(grp == 0, -slope * sign * di,
                       jnp.where(grp == 1, -slope * tq * jnp.abs(diff).astype(F32),
                                 jnp.where(grp == 2, slope * sign, 0.0))).astype(BF16)
        q = q_ref[sb * tq:(sb + 1) * tq, :]
        zero = jnp.zeros_like(q)
        qaug = jnp.concatenate(
            [jnp.concatenate([jnp.where(lane < HEAD_DIM, q, zero), qf], axis=1),
             jnp.concatenate([jnp.where(lane >= HEAD_DIM, q, zero), qf], axis=1)], axis=0)
        s_ref = s_refs[sb]
        s = lax.dot_general(kaug, qaug, _NT, preferred_element_type=F32)
        s_ref[...] = s
        mx = s[0:strip, :]
        for r in range(strip, seq, strip):
            mx = jnp.maximum(mx, s[r:r + strip, :])
        m_refs[sb][...] = mx
        diag = pl.ds(pl.multiple_of(blk * tq, tq), tq)
        diag_bias = slope * dist_ref[...]
        s_ref[diag, 0:tq] = s_ref[diag, 0:tq] - diag_bias
        s_ref[diag, tq:2 * tq] = s_ref[diag, tq:2 * tq] - diag_bias

    def finish(sb):
        s_ref, e_ref = s_refs[sb], e_refs[sb]
        mx = jnp.max(m_refs[sb][...], axis=0, keepdims=True)
        for r in range(0, seq, strip):
            e_ref[r:r + strip, :] = jnp.exp(s_ref[r:r + strip, :] - mx).astype(BF16)
        o12 = _dot(vt_ones, e_ref[...])
        norm = o12[V_DIM:V_DIM + 1, :]
        o12 = o12[0:V_DIM, :]
        o = o12[:, 0:tq] * (1.0 / norm[:, 0:tq]) - o12[:, tq:2 * tq] * (lam / norm[:, tq:2 * tq])
        o = o * lax.rsqrt(jnp.mean(o * o, axis=0, keepdims=True) + NORM_EPS)
        o = o * sg_ref[...] * (1.0 - lambda_init)
        o_ref[sb * tq:(sb + 1) * tq, :] = o.T.astype(o_ref.dtype)

    scores(0)
    for sb in range(nsub):
        if sb + 1 < nsub:
            scores(sb + 1)
        finish(sb)


def _attention(q, k, vt, lq1, lk1, lq2, lk2, subln_g, lambda_init, tq, nsub):
    b, s, _ = q.shape
    vec = lambda bi, h, qi: (0, 0)
    tstep = tq * nsub
    kernel = functools.partial(_attn_kernel, tq=tq, nsub=nsub, seq=s, lambda_init=lambda_init)
    return pl.pallas_call(
        kernel,
        grid=(b, N_HEADS, s // tstep),
        in_specs=[pl.BlockSpec((1, HEAD_DIM), vec)] * 4
        + [pl.BlockSpec((V_DIM, 1), vec),
           pl.BlockSpec((None, tstep, V_DIM), lambda bi, h, qi: (bi, qi, h)),
           pl.BlockSpec((None, s, V_DIM), lambda bi, h, qi: (bi, 0, h)),
           pl.BlockSpec((V_DIM, s), lambda bi, h, qi: (h, bi))],
        out_specs=pl.BlockSpec((None, tstep, V_DIM), lambda bi, h, qi: (bi, qi, h)),
        out_shape=jax.ShapeDtypeStruct((b, s, ATTN_WIDTH), BF16),
        scratch_shapes=[pltpu.VMEM((s, LANES), BF16), pltpu.VMEM((tq, tq), F32)]
        + [pltpu.VMEM((s, 2 * tq), F32)] * nsub + [pltpu.VMEM((s, 2 * tq), BF16)] * nsub
        + [pltpu.VMEM((4 * SUBLANES, 2 * tq), F32)] * nsub,
        compiler_params=pltpu.CompilerParams(
            dimension_semantics=("arbitrary",) * 3, vmem_limit_bytes=VMEM_LIMIT),
        name="attn",
    )(lq1, lk1, lq2, lk2, subln_g, q, k, vt)


def _local_scan(a, u, reverse):
    row = lax.broadcasted_iota(jnp.int32, a.shape, 0)
    for d in (1, 2, 4):
        shift = SUBLANES - d if reverse else d
        valid = (row < SUBLANES - d) if reverse else (row >= d)
        a_s = jnp.where(valid, pltpu.roll(a, shift, 0), 1.0)
        u_s = jnp.where(valid, pltpu.roll(u, shift, 0), 0.0)
        u = a * u_s + u
        a = a * a_s
    return a, u


def _lru_kernel(xr_ref, gr_ref, cw_ref, cb_ref, wg_ref, bg_ref, lam_ref, o_ref,
                xp_ref, xc_ref, h_ref, cum_ref, *, seq, rows):
    width = xr_ref.shape[-1]
    ntile = width // LANES
    grp = seq // SUBLANES
    left = LRU_CONV_LEFT
    right = LRU_CONV_WIDTH - 1 - LRU_CONV_LEFT
    top = left * SUBLANES
    sub = lax.broadcasted_iota(jnp.int32, (SUBLANES, LANES), 0)

    for s in range(SUBLANES):
        blk = xr_ref[s * grp:(s + 1) * grp, :]
        for p in range(ntile):
            xp_ref[p, pl.ds(top + s, grp, stride=SUBLANES), :] = blk[:, p * LANES:(p + 1) * LANES]
    for p in range(ntile):
        for k in range(left):
            src = xp_ref[p, top + (grp - 1 - k) * SUBLANES:top + (grp - k) * SUBLANES, :]
            xp_ref[p, top - (k + 1) * SUBLANES:top - k * SUBLANES, :] = jnp.where(
                sub == 0, 0.0, pltpu.roll(src, 1, 0))
        for k in range(right):
            src = xp_ref[p, top + k * SUBLANES:top + (k + 1) * SUBLANES, :]
            xp_ref[p, top + (grp + k) * SUBLANES:top + (grp + k + 1) * SUBLANES, :] = jnp.where(
                sub == SUBLANES - 1, 0.0, pltpu.roll(src, SUBLANES - 1, 0))

    neg_lam = -lam_ref[...]
    softplus = jnp.maximum(neg_lam, 0.0) + jnp.log1p(jnp.exp(-jnp.abs(neg_lam)))
    rate = LRU_C * softplus
    rate_log2 = -rate * math.log2(math.e)

    nchunk = seq // rows
    for c in range(nchunk):
        r0 = c * rows
        for p in range(ntile):
            cols = slice(p * LANES, (p + 1) * LANES)
            xc = cb_ref[:, cols]
            for tap in range(LRU_CONV_WIDTH):
                start = top + r0 + (tap - left) * SUBLANES
                xc = xc + xp_ref[p, start:start + rows, :] * cw_ref[tap:tap + 1, cols]
            xc_ref[p, r0:r0 + rows, :] = xc

    zero = jnp.zeros((SUBLANES, LANES), F32)
    one = jnp.ones((SUBLANES, LANES), F32)
    ends = [[(zero, one), (zero, one)] for _ in range(ntile)]
    for c in range(nchunk):
        for p in range(ntile):
            cols = slice(p * LANES, (p + 1) * LANES)
            for d in range(2):
                r0 = (c if d == 0 else nchunk - 1 - c) * rows
                gcols = slice(2 * d * LANES, (2 * d + 2) * LANES)
                xc = xc_ref[p, r0:r0 + rows, :]
                gates = _dot(xc.astype(BF16), wg_ref[p, :, gcols]) + bg_ref[p:p + 1, gcols]
                r = jax.nn.sigmoid(gates[:, 0:LANES])
                i = jax.nn.sigmoid(gates[:, LANES:2 * LANES])
                t = jnp.tanh(r * rate[d:d + 1, cols])
                tt = t + t
                prod = tt * (1.0 + t)
                mult = jnp.where(prod > 0.0, tt * lax.rsqrt(prod), 0.0)
                a = jnp.exp2(r * rate_log2[d:d + 1, cols])
                u = mult * (i * xc)
                h, cum = ends[p][d]
                vrows = range(rows // SUBLANES)
                for j in (vrows if d == 0 else reversed(vrows)):
                    rs = slice(j * SUBLANES, (j + 1) * SUBLANES)
                    h = a[rs] * h + u[rs]
                    cum = a[rs] * cum
                    h_ref[d, p, r0 + j * SUBLANES:r0 + (j + 1) * SUBLANES, :] = h
                    cum_ref[d, p, r0 + j * SUBLANES:r0 + (j + 1) * SUBLANES, :] = cum
                ends[p][d] = (h, cum)

    enter = []
    for p in range(ntile):
        for d in range(2):
            h_end, cum_end = ends[p][d]
            _, chained = _local_scan(cum_end, h_end, reverse=(d == 1))
            if d == 0:
                enter.append(jnp.where(sub == 0, 0.0, pltpu.roll(chained, 1, 0)))
            else:
                enter.append(jnp.where(sub == SUBLANES - 1, 0.0,
                                       pltpu.roll(chained, SUBLANES - 1, 0)))

    for c in range(seq // rows):
        rs = slice(c * rows, (c + 1) * rows)
        for p in range(ntile):
            y = None
            for d in range(2):
                init = jnp.tile(enter[2 * p + d], (rows // SUBLANES, 1))
                part = h_ref[d, p, rs, :] + cum_ref[d, p, rs, :] * init
                y = part if y is None else y + part
            xc_ref[p, rs, :] = y

    for s in range(SUBLANES):
        rs = slice(s * grp, (s + 1) * grp)
        for p in range(ntile):
            cols = slice(p * LANES, (p + 1) * LANES)
            y = xc_ref[p, pl.ds(s, grp, stride=SUBLANES), :]
            o_ref[rs, cols] = (jax.nn.gelu(gr_ref[rs, cols], approximate=True) * y).astype(o_ref.dtype)


def _lru(xr, gr, conv_w, conv_b, wg, bg, lru_lambda, rows):
    b, s, width = xr.shape
    ntile = width // LANES
    seqblk = pl.BlockSpec((None, s, width), lambda bi: (bi, 0, 0))
    full = lambda a: pl.BlockSpec(a.shape, lambda bi: (0,) * a.ndim)
    kernel = functools.partial(_lru_kernel, seq=s, rows=rows)
    halo_rows = (LRU_CONV_WIDTH - 1) * SUBLANES
    return pl.pallas_call(
        kernel,
        grid=(b,),
        in_specs=[seqblk, seqblk, full(conv_w), full(conv_b), full(wg), full(bg), full(lru_lambda)],
        out_specs=seqblk,
        out_shape=jax.ShapeDtypeStruct((b, s, width), BF16),
        scratch_shapes=[pltpu.VMEM((ntile, s + halo_rows, LANES), F32),
                        pltpu.VMEM((ntile, s, LANES), F32),
                        pltpu.VMEM((2, ntile, s, LANES), F32),
                        pltpu.VMEM((2, ntile, s, LANES), F32)],
        compiler_params=pltpu.CompilerParams(
            dimension_semantics=("arbitrary",), vmem_limit_bytes=VMEM_LIMIT),
        name="lru",
    )(xr, gr, conv_w, conv_b, wg, bg, lru_lambda)


def _outproj_kernel(x_ref, a_ref, l_ref, wa_ref, wl_ref, g_ref, xm_ref, h_ref):
    xm = x_ref[...] + _dot(a_ref[...], wa_ref[...]) + _dot(l_ref[...], wl_ref[...])
    xm_ref[...] = xm
    h_ref[...] = _rms(xm, g_ref[...]).astype(h_ref.dtype)


def _outproj(x2, attn2, lru2, w_attn, w_lru, g, tm):
    n, d = x2.shape
    row = lambda i: (i, 0)
    const = lambda i: (0, 0)
    return pl.pallas_call(
        _outproj_kernel,
        grid=(n // tm,),
        in_specs=[pl.BlockSpec((tm, d), row),
                  pl.BlockSpec((tm, attn2.shape[1]), row),
                  pl.BlockSpec((tm, lru2.shape[1]), row),
                  pl.BlockSpec(w_attn.shape, const),
                  pl.BlockSpec(w_lru.shape, const),
                  pl.BlockSpec((1, d), const)],
        out_specs=[pl.BlockSpec((tm, d), row)] * 2,
        out_shape=[jax.ShapeDtypeStruct((n, d), F32), jax.ShapeDtypeStruct((n, d), BF16)],
        compiler_params=pltpu.CompilerParams(
            dimension_semantics=("arbitrary",), vmem_limit_bytes=VMEM_LIMIT),
        name="outproj",
    )(x2, attn2, lru2, w_attn, w_lru, g)


def _ffn_kernel(hp_ref, hm_ref, hn_ref, xm_ref, wup_ref, cw_ref, cb_ref, wdn_ref, fg_ref, o_ref,
                perm_ref, hext_ref, ua_ref, ub_ref, acta_ref, actb_ref, acc_ref,
                *, tile, chunk, d_ff):
    i = pl.program_id(1)
    nchunks = d_ff // chunk
    ngrp = tile // SUBLANES
    nslab = perm_ref.shape[0]
    d = nslab * LANES

    for s in range(SUBLANES):
        rows = hm_ref[s * ngrp:(s + 1) * ngrp, :].astype(F32)
        for n in range(nslab):
            perm_ref[n, pl.ds(s, ngrp, stride=SUBLANES), :] = rows[:, n * LANES:(n + 1) * LANES]
    for n in range(nslab):
        hext_ref[0:tile, n * LANES:(n + 1) * LANES] = perm_ref[n].astype(BF16)
    prev = jnp.where(i > 0, hp_ref[BF16_ROWS - 1:BF16_ROWS, :].astype(F32), 0.0)
    nxt = jnp.where(i < pl.num_programs(1) - 1, hn_ref[0:1, :].astype(F32), 0.0)
    hrow = lax.broadcasted_iota(jnp.int32, (BF16_ROWS, d), 0)
    halo = jnp.where(hrow == 0, prev, jnp.where(hrow == 1, nxt, 0.0))
    hext_ref[tile:tile + BF16_ROWS, :] = halo.astype(BF16)
    acc_ref[...] = jnp.zeros_like(acc_ref)

    def offsets(c):
        og, ov = c * chunk, d_ff + c * chunk
        if isinstance(c, int):
            return og, ov
        return pl.multiple_of(og, chunk), pl.multiple_of(ov, chunk)

    def up(c, u_ref):
        og, ov = offsets(c)
        hext = hext_ref[...]
        u_ref[:, 0:chunk] = _dot(hext, wup_ref[:, pl.ds(og, chunk)])
        u_ref[:, chunk:2 * chunk] = _dot(hext, wup_ref[:, pl.ds(ov, chunk)])

    def glu(c, u_ref, act_ref):
        sub = lax.broadcasted_iota(jnp.int32, (SUBLANES, chunk), 0)
        blk = BF16_ROWS

        def conv(col0, off, r0):
            cols = slice(col0, col0 + chunk)
            w = cw_ref[:, pl.ds(off, chunk)]
            cur = u_ref[r0:r0 + blk, cols]
            if r0 == 0:
                first = jnp.where(sub == 0, u_ref[tile:tile + 1, cols],
                                  pltpu.roll(u_ref[tile - SUBLANES:tile, cols], 1, 0))
                um1 = jnp.concatenate([first, cur[0:blk - SUBLANES]], axis=0)
            else:
                um1 = u_ref[r0 - SUBLANES:r0 + blk - SUBLANES, cols]
            if r0 + blk == tile:
                last = jnp.where(sub == SUBLANES - 1, u_ref[tile + 1:tile + 2, cols],
                                 pltpu.roll(u_ref[0:SUBLANES, cols], SUBLANES - 1, 0))
                up1 = jnp.concatenate([cur[SUBLANES:blk], last], axis=0)
            else:
                up1 = u_ref[r0 + SUBLANES:r0 + blk + SUBLANES, cols]
            return (cb_ref[:, pl.ds(off, chunk)] + um1 * w[0:1, :] + cur * w[1:2, :]
                    + up1 * w[2:3, :])

        og, ov = offsets(c)
        for r0 in range(0, tile, blk):
            gate = conv(0, og, r0)
            val = conv(chunk, ov, r0)
            act_ref[r0:r0 + blk, :] = (jax.nn.gelu(gate, approximate=True) * val).astype(BF16)

    def down(c, act_ref):
        og, _ = offsets(c)
        y = _dot(act_ref[...], wdn_ref[pl.ds(og, chunk), :])
        for n in range(nslab):
            acc_ref[n] += y[:, n * LANES:(n + 1) * LANES]

    u_refs, act_refs = (ua_ref, ub_ref), (acta_ref, actb_ref)

    def step(t, parity, first=False, last=False):
        if not last:
            up(t + 1, u_refs[1 - parity])
        glu(t, u_refs[parity], act_refs[parity])
        if not first:
            down(t - 1, act_refs[1 - parity])

    def body(t, carry):
        lax.cond(t % 2 == 1, lambda: step(t, 1), lambda: step(t, 0))
        return carry

    assert nchunks % 2 == 1 and nchunks >= 3
    up(0, ua_ref)
    step(0, 0, first=True)
    lax.fori_loop(1, nchunks - 1, body, 0)
    step(nchunks - 1, 0, last=True)
    down(nchunks - 1, acta_ref)
    for s in range(SUBLANES):
        rs = slice(s * ngrp, (s + 1) * ngrp)
        y = jnp.concatenate([acc_ref[n, pl.ds(s, ngrp, stride=SUBLANES), :] for n in range(nslab)],
                            axis=1)
        o_ref[rs, :] = _rms(xm_ref[rs, :] + y, fg_ref[...])


def _ffn(h2, xm, w_up, conv_w, conv_b, w_down, final_g, tile, chunk):
    b, s, d = h2.shape
    d_ff = w_down.shape[0]
    nh = tile // BF16_ROWS
    last_halo = s // BF16_ROWS - 1
    main = lambda bi, i: (bi, i, 0)
    const = lambda bi, i: (0, 0)
    single = dict(pipeline_mode=pl.Buffered(1))
    kernel = functools.partial(_ffn_kernel, tile=tile, chunk=chunk, d_ff=d_ff)
    return pl.pallas_call(
        kernel,
        grid=(b, s // tile),
        in_specs=[pl.BlockSpec((None, BF16_ROWS, d), lambda bi, i: (bi, jnp.maximum(i * nh - 1, 0), 0)),
                  pl.BlockSpec((None, tile, d), main),
                  pl.BlockSpec((None, BF16_ROWS, d),
                               lambda bi, i: (bi, jnp.minimum((i + 1) * nh, last_halo), 0)),
                  pl.BlockSpec((None, tile, d), main),
                  pl.BlockSpec(w_up.shape, const, **single),
                  pl.BlockSpec(conv_w.shape, const),
                  pl.BlockSpec(conv_b.shape, const),
                  pl.BlockSpec(w_down.shape, const, **single),
                  pl.BlockSpec((1, d), const)],
        out_specs=pl.BlockSpec((None, tile, d), main),
        out_shape=jax.ShapeDtypeStruct((b, s, d), F32),
        scratch_shapes=[pltpu.VMEM((d // LANES, tile, LANES), F32),
                        pltpu.VMEM((tile + BF16_ROWS, d), BF16)]
        + [pltpu.VMEM((tile + BF16_ROWS, 2 * chunk), F32)] * 2
        + [pltpu.VMEM((tile, chunk), BF16)] * 2
        + [pltpu.VMEM((d // LANES, tile, LANES), F32)],
        compiler_params=pltpu.CompilerParams(
            dimension_semantics=("arbitrary",) * 2, vmem_limit_bytes=VMEM_LIMIT),
        name="ffn",
    )(h2, h2, h2, xm, w_up, conv_w, conv_b, w_down, final_g)


def _gate_weights(w_a, b_a, w_x, b_x):
    ndir, nblk, bd, _ = w_a.shape
    per = LANES // bd
    npair = nblk // per

    def blockdiag(w):
        w = w.reshape(npair, per, bd, bd)
        eye = jnp.eye(per, dtype=w.dtype)
        return jnp.einsum('pbij,bc->pbicj', w, eye).reshape(npair, LANES, LANES)

    ws, bs = [], []
    for d in range(ndir):
        for w, bias in ((w_a, b_a), (w_x, b_x)):
            ws.append(blockdiag(w[d]))
            bs.append(bias[d].reshape(npair, LANES))
    return jnp.concatenate(ws, axis=-1).astype(BF16), jnp.concatenate(bs, axis=-1).astype(F32)


def kernel(x, attn_norm_g, w_in, lambda_q1, lambda_k1, lambda_q2, lambda_k2, subln_g,
           lru_conv_w, lru_conv_b, lru_w_a, lru_b_a, lru_w_x, lru_b_x, lru_lambda,
           w_out, ffn_norm_g, w_up, ffn_conv_w, ffn_conv_b, w_down, final_norm_g):
    b, s, d = x.shape
    depth = w_in.shape[0]
    x2 = x.reshape(b * s, d)
    for l in range(depth):
        lambda_init = 0.8 - 0.6 * math.exp(-0.3 * l)
        w_in_bf = w_in[l].astype(BF16)
        q, k, vt, xr, gr = _inproj(x2, attn_norm_g[l][None], w_in_bf,
                                   w_in_bf[:, 2 * ATTN_WIDTH:3 * ATTN_WIDTH].T, tm=512)
        lw = xr.shape[-1]
        attn = _attention(q.reshape(b, s, -1), k.reshape(b, s, -1), vt,
                          lambda_q1[l][None], lambda_k1[l][None], lambda_q2[l][None],
                          lambda_k2[l][None], subln_g[l][:, None], lambda_init, tq=256, nsub=2)
        wg, bg = _gate_weights(lru_w_a[l], lru_b_a[l], lru_w_x[l], lru_b_x[l])
        lru = _lru(xr.reshape(b, s, lw), gr.reshape(b, s, lw), lru_conv_w[l], lru_conv_b[l][None],
                   wg, bg, lru_lambda[l], rows=256)
        wo = w_out[l].astype(BF16)
        xm, h2 = _outproj(x2, attn.reshape(b * s, -1), lru.reshape(b * s, -1),
                          wo[:ATTN_WIDTH], wo[ATTN_WIDTH:], ffn_norm_g[l][None], tm=512)
        assert depth == 1
        x2 = _ffn(h2.reshape(b, s, d), xm.reshape(b, s, d), w_up[l].astype(BF16), ffn_conv_w[l],
                  ffn_conv_b[l][None], w_down[l].astype(BF16), final_norm_g[None],
                  tile=512, chunk=256).reshape(b * s, d)
    return x2.reshape(b, s, d)
```

```python
import functools
import math

import jax
import jax.numpy as jnp
from jax import lax
from jax.experimental import pallas as pl
from jax.experimental.pallas import tpu as pltpu

F32 = jnp.float32
BF16 = jnp.bfloat16

N_HEADS = 4
HEAD_DIM = 64
V_DIM = 2 * HEAD_DIM
ATTN_WIDTH = N_HEADS * V_DIM
LRU_BLOCK = 64
LRU_CONV_WIDTH = 4
LRU_CONV_LEFT = 2
LRU_C = 8.0
FFN_CONV_LEFT = 1
NORM_EPS = 1e-6
LANES = 128
SUBLANES = 8
BF16_ROWS = 16
VMEM_LIMIT = 56 * 1024 * 1024


def _rms(x, g):
    return (x * lax.rsqrt(jnp.mean(x * x, axis=-1, keepdims=True) + NORM_EPS)) * g


def _dot(a, b):
    return jnp.dot(a, b, preferred_element_type=F32)


_NT = (((1,), (1,)), ((), ()))


def _inproj_kernel(x_ref, g_ref, w_ref, q_ref, k_ref, vt_ref, xr_ref, gr_ref, wb_ref, wvt_ref):
    aw = ATTN_WIDTH
    d = x_ref.shape[-1]

    @pl.when(pl.program_id(0) == 0)
    def _cast_weights():
        rows = 2 * LANES
        for r in range(0, d, rows):
            w = w_ref[r:r + rows, :]
            wb_ref[r:r + rows, 0:aw] = (w[:, 0:aw] * (HEAD_DIM ** -0.5)).astype(BF16)
            wb_ref[r:r + rows, aw:] = w[:, aw:].astype(BF16)
        for c in range(aw // LANES):
            cols = slice(2 * aw + c * LANES, 2 * aw + (c + 1) * LANES)
            wvt_ref[c * LANES:(c + 1) * LANES, :] = w_ref[:, cols].T.astype(BF16)

    hb = _rms(x_ref[...], g_ref[...]).astype(BF16)
    q_ref[...] = _dot(hb, wb_ref[:, 0:aw]).astype(BF16)
    k_ref[...] = _dot(hb, wb_ref[:, aw:2 * aw]).astype(BF16)
    vt_ref[...] = lax.dot_general(wvt_ref[...], hb, _NT, preferred_element_type=F32).astype(BF16)
    lw = xr_ref.shape[-1]
    xr_ref[...] = _dot(hb, wb_ref[:, 3 * aw:3 * aw + lw])
    gr_ref[...] = _dot(hb, wb_ref[:, 3 * aw + lw:3 * aw + 2 * lw])


def _inproj(x2, g, w, tm):
    n, d = x2.shape
    lw = (w.shape[1] - 3 * ATTN_WIDTH) // 2
    row = lambda i: (i, 0)
    const = lambda i: (0, 0)
    return pl.pallas_call(
        _inproj_kernel,
        grid=(n // tm,),
        in_specs=[pl.BlockSpec((tm, d), row),
                  pl.BlockSpec((1, d), const),
                  pl.BlockSpec(w.shape, const, pipeline_mode=pl.Buffered(1))],
        out_specs=[pl.BlockSpec((tm, ATTN_WIDTH), row)] * 2
        + [pl.BlockSpec((ATTN_WIDTH, tm), lambda i: (0, i))]
        + [pl.BlockSpec((tm, lw), row)] * 2,
        out_shape=[jax.ShapeDtypeStruct((n, ATTN_WIDTH), BF16)] * 2
        + [jax.ShapeDtypeStruct((ATTN_WIDTH, n), BF16)]
        + [jax.ShapeDtypeStruct((n, lw), F32)] * 2,
        scratch_shapes=[pltpu.VMEM(w.shape, BF16), pltpu.VMEM((ATTN_WIDTH, d), BF16)],
        compiler_params=pltpu.CompilerParams(
            dimension_semantics=("arbitrary",), vmem_limit_bytes=VMEM_LIMIT),
        name="inproj",
    )(x2, g, w)


def _attn_kernel(lq1_ref, lk1_ref, lq2_ref, lk2_ref, sg_ref, q_ref, k_ref, vt_ref, o_ref,
                 kf_ref, dist_ref, *se_refs, tq, nsub, seq, lambda_init):
    h = pl.program_id(1)
    qi = pl.program_id(2)
    s_refs, e_refs, m_refs = se_refs[:nsub], se_refs[nsub:2 * nsub], se_refs[2 * nsub:]
    nblk = seq // tq
    assert nblk & (nblk - 1) == 0 and 3 * nblk <= LANES and tq <= 256
    shift = nblk.bit_length() - 1

    @pl.when((pl.program_id(0) == 0) & (h == 0) & (qi == 0))
    def _init():
        lane = lax.broadcasted_iota(jnp.int32, (seq, LANES), 1)
        row = lax.broadcasted_iota(jnp.int32, (seq, LANES), 0)
        grp = lane >> shift
        hit = (row // tq) == (lane & (nblk - 1))
        dj = (row % tq).astype(F32)
        kf_ref[...] = jnp.where(hit & (grp < 2), 1.0,
                                jnp.where(hit & (grp == 2), dj, 0.0)).astype(BF16)
        r = lax.broadcasted_iota(jnp.int32, (tq, tq), 0)
        c = lax.broadcasted_iota(jnp.int32, (tq, tq), 1)
        dist_ref[...] = jnp.abs(r - c).astype(F32)

    lam = (jnp.exp(jnp.sum(lq1_ref[...] * lk1_ref[...], axis=-1, keepdims=True))
           - jnp.exp(jnp.sum(lq2_ref[...] * lk2_ref[...], axis=-1, keepdims=True))
           + lambda_init)
    slope = jnp.where(h == 0, 2.0 ** -2, jnp.where(h == 1, 2.0 ** -4,
                      jnp.where(h == 2, 2.0 ** -6, 2.0 ** -8))).astype(F32)

    kaug = jnp.concatenate([k_ref[...], kf_ref[...]], axis=1)
    vt_ones = jnp.concatenate([vt_ref[...], jnp.ones((BF16_ROWS, seq), BF16)], axis=0)
    lane = lax.broadcasted_iota(jnp.int32, (tq, LANES), 1)
    di = lax.broadcasted_iota(jnp.int32, (tq, LANES), 0).astype(F32)
    grp = lane >> shift
    strip = 4 * SUBLANES

    def scores(sb):
        blk = qi * nsub + sb
        diff = blk - (lane & (nblk - 1))
        sign = jnp.where(diff > 0, 1.0, jnp.where(diff < 0, -1.0, 0.0))
        qf = jnp.where(grp == 0, -slope * sign * di,
                       jnp.where(grp == 1, -slope * tq * jnp.abs(diff).astype(F32),
                                 jnp.where(grp == 2, slope * sign, 0.0))).astype(BF16)
        q = q_ref[sb * tq:(sb + 1) * tq, :]
        zero = jnp.zeros_like(q)
        qaug = jnp.concatenate(
            [jnp.concatenate([jnp.where(lane < HEAD_DIM, q, zero), qf], axis=1),
             jnp.concatenate([jnp.where(lane >= HEAD_DIM, q, zero), qf], axis=1)], axis=0)
        s_ref = s_refs[sb]
        s = lax.dot_general(kaug, qaug, _NT, preferred_element_type=F32)
        s_ref[...] = s
        mx = s[0:strip, :]
        for r in range(strip, seq, strip):
            mx = jnp.maximum(mx, s[r:r + strip, :])
        m_refs[sb][...] = mx
        diag = pl.ds(pl.multiple_of(blk * tq, tq), tq)
        diag_bias = slope * dist_ref[...]
        s_ref[diag, 0:tq] = s_ref[diag, 0:tq] - diag_bias
        s_ref[diag, tq:2 * tq] = s_ref[diag, tq:2 * tq] - diag_bias

    def finish(sb):
        s_ref, e_ref = s_refs[sb], e_refs[sb]
        mx = jnp.max(m_refs[sb][...], axis=0, keepdims=True)
        for r in range(0, seq, strip):
            e_ref[r:r + strip, :] = jnp.exp(s_ref[r:r + strip, :] - mx).astype(BF16)
        o12 = _dot(vt_ones, e_ref[...])
        norm = o12[V_DIM:V_DIM + 1, :]
        o12 = o12[0:V_DIM, :]
        o = o12[:, 0:tq] * (1.0 / norm[:, 0:tq]) - o12[:, tq:2 * tq] * (lam / norm[:, tq:2 * tq])
        o = o * lax.rsqrt(jnp.mean(o * o, axis=0, keepdims=True) + NORM_EPS)
        o = o * sg_ref[...] * (1.0 - lambda_init)
        o_ref[sb * tq:(sb + 1) * tq, :] = o.T.astype(o_ref.dtype)

    scores(0)
    for sb in range(nsub):
        if sb + 1 < nsub:
            scores(sb + 1)
        finish(sb)


def _attention(q, k, vt, lq1, lk1, lq2, lk2, subln_g, lambda_init, tq, nsub):
    b, s, _ = q.shape
    vec = lambda bi, h, qi: (0, 0)
    tstep = tq * nsub
    kernel = functools.partial(_attn_kernel, tq=tq, nsub=nsub, seq=s, lambda_init=lambda_init)
    return pl.pallas_call(
        kernel,
        grid=(b, N_HEADS, s // tstep),
        in_specs=[pl.BlockSpec((1, HEAD_DIM), vec)] * 4
        + [pl.BlockSpec((V_DIM, 1), vec),
           pl.BlockSpec((None, tstep, V_DIM), lambda bi, h, qi: (bi, qi, h)),
           pl.BlockSpec((None, s, V_DIM), lambda bi, h, qi: (bi, 0, h)),
           pl.BlockSpec((V_DIM, s), lambda bi, h, qi: (h, bi))],
        out_specs=pl.BlockSpec((None, tstep, V_DIM), lambda bi, h, qi: (bi, qi, h)),
        out_shape=jax.ShapeDtypeStruct((b, s, ATTN_WIDTH), BF16),
        scratch_shapes=[pltpu.VMEM((s, LANES), BF16), pltpu.VMEM((tq, tq), F32)]
        + [pltpu.VMEM((s, 2 * tq), F32)] * nsub + [pltpu.VMEM((s, 2 * tq), BF16)] * nsub
        + [pltpu.VMEM((4 * SUBLANES, 2 * tq), F32)] * nsub,
        compiler_params=pltpu.CompilerParams(
            dimension_semantics=("arbitrary",) * 3, vmem_limit_bytes=VMEM_LIMIT),
        name="attn",
    )(lq1, lk1, lq2, lk2, subln_g, q, k, vt)


def _local_scan(a, u, reverse):
    row = lax.broadcasted_iota(jnp.int32, a.shape, 0)
    for d in (1, 2, 4):
        shift = SUBLANES - d if reverse else d
        valid = (row < SUBLANES - d) if reverse else (row >= d)
        a_s = jnp.where(valid, pltpu.roll(a, shift, 0), 1.0)
        u_s = jnp.where(valid, pltpu.roll(u, shift, 0), 0.0)
        u = a * u_s + u
        a = a * a_s
    return a, u


def _lru_kernel(xr_ref, gr_ref, cw_ref, cb_ref, wg_ref, bg_ref, lam_ref, o_ref,
                xp_ref, xc_ref, h_ref, cum_ref, *, seq, rows):
    width = xr_ref.shape[-1]
    ntile = width // LANES
    grp = seq // SUBLANES
    left = LRU_CONV_LEFT
    right = LRU_CONV_WIDTH - 1 - LRU_CONV_LEFT
    top = left * SUBLANES
    sub = lax.broadcasted_iota(jnp.int32, (SUBLANES, LANES), 0)

    for s in range(SUBLANES):
        blk = xr_ref[s * grp:(s + 1) * grp, :]
        for p in range(ntile):
            xp_ref[p, pl.ds(top + s, grp, stride=SUBLANES), :] = blk[:, p * LANES:(p + 1) * LANES]
    for p in range(ntile):
        for k in range(left):
            src = xp_ref[p, top + (grp - 1 - k) * SUBLANES:top + (grp - k) * SUBLANES, :]
            xp_ref[p, top - (k + 1) * SUBLANES:top - k * SUBLANES, :] = jnp.where(
                sub == 0, 0.0, pltpu.roll(src, 1, 0))
        for k in range(right):
            src = xp_ref[p, top + k * SUBLANES:top + (k + 1) * SUBLANES, :]
            xp_ref[p, top + (grp + k) * SUBLANES:top + (grp + k + 1) * SUBLANES, :] = jnp.where(
                sub == SUBLANES - 1, 0.0, pltpu.roll(src, SUBLANES - 1, 0))

    neg_lam = -lam_ref[...]
    softplus = jnp.maximum(neg_lam, 0.0) + jnp.log1p(jnp.exp(-jnp.abs(neg_lam)))
    rate = LRU_C * softplus
    rate_log2 = -rate * math.log2(math.e)

    nchunk = seq // rows
    for c in range(nchunk):
        r0 = c * rows
        for p in range(ntile):
            cols = slice(p * LANES, (p + 1) * LANES)
            xc = cb_ref[:, cols]
            for tap in range(LRU_CONV_WIDTH):
                start = top + r0 + (tap - left) * SUBLANES
                xc = xc + xp_ref[p, start:start + rows, :] * cw_ref[tap:tap + 1, cols]
            xc_ref[p, r0:r0 + rows, :] = xc

    zero = jnp.zeros((SUBLANES, LANES), F32)
    one = jnp.ones((SUBLANES, LANES), F32)
    ends = [[(zero, one), (zero, one)] for _ in range(ntile)]
    for c in range(nchunk):
        for p in range(ntile):
            cols = slice(p * LANES, (p + 1) * LANES)
            for d in range(2):
                r0 = (c if d == 0 else nchunk - 1 - c) * rows
                gcols = slice(2 * d * LANES, (2 * d + 2) * LANES)
                xc = xc_ref[p, r0:r0 + rows, :]
                gates = _dot(xc.astype(BF16), wg_ref[p, :, gcols]) + bg_ref[p:p + 1, gcols]
                r = jax.nn.sigmoid(gates[:, 0:LANES])
                i = jax.nn.sigmoid(gates[:, LANES:2 * LANES])
                t = jnp.tanh(r * rate[d:d + 1, cols])
                tt = t + t
                prod = tt * (1.0 + t)
                mult = jnp.where(prod > 0.0, tt * lax.rsqrt(prod), 0.0)
                a = jnp.exp2(r * rate_log2[d:d + 1, cols])
                u = mult * (i * xc)
                h, cum = ends[p][d]
                vrows = range(rows // SUBLANES)
                for j in (vrows if d == 0 else reversed(vrows)):
                    rs = slice(j * SUBLANES, (j + 1) * SUBLANES)
                    h = a[rs] * h + u[rs]
                    cum = a[rs] * cum
                    h_ref[d, p, r0 + j * SUBLANES:r0 + (j + 1) * SUBLANES, :] = h
                    cum_ref[d, p, r0 + j * SUBLANES:r0 + (j + 1) * SUBLANES, :] = cum
                ends[p][d] = (h, cum)

    enter = []
    for p in range(ntile):
        for d in range(2):
            h_end, cum_end = ends[p][d]
            _, chained = _local_scan(cum_end, h_end, reverse=(d == 1))
            if d == 0:
                enter.append(jnp.where(sub == 0, 0.0, pltpu.roll(chained, 1, 0)))
            else:
                enter.append(jnp.where(sub == SUBLANES - 1, 0.0,
                                       pltpu.roll(chained, SUBLANES - 1, 0)))

    for c in range(seq // rows):
        rs = slice(c * rows, (c + 1) * rows)
        for p in range(ntile):
            y = None
            for d in range(2):
                init = jnp.tile(enter[2 * p + d], (rows // SUBLANES, 1))
                part = h_ref[d, p, rs, :] + cum_ref[d, p, rs, :] * init
                y = part if y is None else y + part
            xc_ref[p, rs, :] = y

    for s in range(SUBLANES):
        rs = slice(s * grp, (s + 1) * grp)
        for p in range(ntile):
            cols = slice(p * LANES, (p + 1) * LANES)
            y = xc_ref[p, pl.ds(s, grp, stride=SUBLANES), :]
            o_ref[rs, cols] = (jax.nn.gelu(gr_ref[rs, cols], approximate=True) * y).astype(o_ref.dtype)


def _lru(xr, gr, conv_w, conv_b, wg, bg, lru_lambda, rows):
    b, s, width = xr.shape
    ntile = width // LANES
    seqblk = pl.BlockSpec((None, s, width), lambda bi: (bi, 0, 0))
    full = lambda a: pl.BlockSpec(a.shape, lambda bi: (0,) * a.ndim)
    kernel = functools.partial(_lru_kernel, seq=s, rows=rows)
    halo_rows = (LRU_CONV_WIDTH - 1) * SUBLANES
    return pl.pallas_call(
        kernel,
        grid=(b,),
        in_specs=[seqblk, seqblk, full(conv_w), full(conv_b), full(wg), full(bg), full(lru_lambda)],
        out_specs=seqblk,
        out_shape=jax.ShapeDtypeStruct((b, s, width), BF16),
        scratch_shapes=[pltpu.VMEM((ntile, s + halo_rows, LANES), F32),
                        pltpu.VMEM((ntile, s, LANES), F32),
                        pltpu.VMEM((2, ntile, s, LANES), F32),
                        pltpu.VMEM((2, ntile, s, LANES), F32)],
        compiler_params=pltpu.CompilerParams(
            dimension_semantics=("arbitrary",), vmem_limit_bytes=VMEM_LIMIT),
        name="lru",
    )(xr, gr, conv_w, conv_b, wg, bg, lru_lambda)


def _outproj_kernel(x_ref, a_ref, l_ref, w_ref, g_ref, xm_ref, h_ref, wb_ref):
    @pl.when(pl.program_id(0) == 0)
    def _cast_weights():
        wb_ref[...] = w_ref[...].astype(BF16)

    aw = a_ref.shape[-1]
    xm = x_ref[...] + _dot(a_ref[...], wb_ref[0:aw, :]) + _dot(l_ref[...], wb_ref[aw:, :])
    xm_ref[...] = xm
    h_ref[...] = _rms(xm, g_ref[...]).astype(h_ref.dtype)


def _outproj(x2, attn2, lru2, w, g, tm):
    n, d = x2.shape
    row = lambda i: (i, 0)
    const = lambda i: (0, 0)
    return pl.pallas_call(
        _outproj_kernel,
        grid=(n // tm,),
        in_specs=[pl.BlockSpec((tm, d), row),
                  pl.BlockSpec((tm, attn2.shape[1]), row),
                  pl.BlockSpec((tm, lru2.shape[1]), row),
                  pl.BlockSpec(w.shape, const, pipeline_mode=pl.Buffered(1)),
                  pl.BlockSpec((1, d), const)],
        out_specs=[pl.BlockSpec((tm, d), row)] * 2,
        out_shape=[jax.ShapeDtypeStruct((n, d), F32), jax.ShapeDtypeStruct((n, d), BF16)],
        scratch_shapes=[pltpu.VMEM(w.shape, BF16)],
        compiler_params=pltpu.CompilerParams(
            dimension_semantics=("arbitrary",), vmem_limit_bytes=VMEM_LIMIT),
        name="outproj",
    )(x2, attn2, lru2, w, g)


def _ffn_kernel(hp_ref, hm_ref, hn_ref, xm_ref, wup_ref, cw_ref, cb_ref, wdn_ref, fg_ref, o_ref,
                perm_ref, hext_ref, ua_ref, ub_ref, acta_ref, actb_ref, acc_ref,
                *, tile, chunk, d_ff):
    i = pl.program_id(1)
    nchunks = d_ff // chunk
    ngrp = tile // SUBLANES
    nslab = perm_ref.shape[0]
    d = nslab * LANES

    for s in range(SUBLANES):
        rows = hm_ref[s * ngrp:(s + 1) * ngrp, :].astype(F32)
        for n in range(nslab):
            perm_ref[n, pl.ds(s, ngrp, stride=SUBLANES), :] = rows[:, n * LANES:(n + 1) * LANES]
    for n in range(nslab):
        hext_ref[0:tile, n * LANES:(n + 1) * LANES] = perm_ref[n].astype(BF16)
    prev = jnp.where(i > 0, hp_ref[BF16_ROWS - 1:BF16_ROWS, :].astype(F32), 0.0)
    nxt = jnp.where(i < pl.num_programs(1) - 1, hn_ref[0:1, :].astype(F32), 0.0)
    hrow = lax.broadcasted_iota(jnp.int32, (BF16_ROWS, d), 0)
    halo = jnp.where(hrow == 0, prev, jnp.where(hrow == 1, nxt, 0.0))
    hext_ref[tile:tile + BF16_ROWS, :] = halo.astype(BF16)
    acc_ref[...] = jnp.zeros_like(acc_ref)

    def offsets(c):
        og, ov = c * chunk, d_ff + c * chunk
        if isinstance(c, int):
            return og, ov
        return pl.multiple_of(og, chunk), pl.multiple_of(ov, chunk)

    def up(c, u_ref):
        og, ov = offsets(c)
        hext = hext_ref[...]
        u_ref[:, 0:chunk] = _dot(hext, wup_ref[:, pl.ds(og, chunk)])
        u_ref[:, chunk:2 * chunk] = _dot(hext, wup_ref[:, pl.ds(ov, chunk)])

    def glu(c, u_ref, act_ref):
        sub = lax.broadcasted_iota(jnp.int32, (SUBLANES, chunk), 0)
        blk = BF16_ROWS

        def conv(col0, off, r0):
            cols = slice(col0, col0 + chunk)
            w = cw_ref[:, pl.ds(off, chunk)]
            cur = u_ref[r0:r0 + blk, cols]
            if r0 == 0:
                first = jnp.where(sub == 0, u_ref[tile:tile + 1, cols],
                                  pltpu.roll(u_ref[tile - SUBLANES:tile, cols], 1, 0))
                um1 = jnp.concatenate([first, cur[0:blk - SUBLANES]], axis=0)
            else:
                um1 = u_ref[r0 - SUBLANES:r0 + blk - SUBLANES, cols]
            if r0 + blk == tile:
                last = jnp.where(sub == SUBLANES - 1, u_ref[tile + 1:tile + 2, cols],
                                 pltpu.roll(u_ref[0:SUBLANES, cols], SUBLANES - 1, 0))
                up1 = jnp.concatenate([cur[SUBLANES:blk], last], axis=0)
            else:
                up1 = u_ref[r0 + SUBLANES:r0 + blk + SUBLANES, cols]
            return (cb_ref[:, pl.ds(off, chunk)] + um1 * w[0:1, :] + cur * w[1:2, :]
                    + up1 * w[2:3, :])

        og, ov = offsets(c)
        for r0 in range(0, tile, blk):
            gate = conv(0, og, r0)
            val = conv(chunk, ov, r0)
            act_ref[r0:r0 + blk, :] = (jax.nn.gelu(gate, approximate=True) * val).astype(BF16)

    def down(c, act_ref):
        og, _ = offsets(c)
        y = _dot(act_ref[...], wdn_ref[pl.ds(og, chunk), :])
        for n in range(nslab):
            acc_ref[n, 0:tile, :] += y[:, n * LANES:(n + 1) * LANES]

    u_refs, act_refs = (ua_ref, ub_ref), (acta_ref, actb_ref)

    def step(t, parity, first=False, last=False):
        if not last:
            up(t + 1, u_refs[1 - parity])
        glu(t, u_refs[parity], act_refs[parity])
        if not first:
            down(t - 1, act_refs[1 - parity])

    def body(t, carry):
        lax.cond(t % 2 == 1, lambda: step(t, 1), lambda: step(t, 0))
        return carry

    assert nchunks % 2 == 1 and nchunks >= 3
    up(0, ua_ref)
    step(0, 0, first=True)
    lax.fori_loop(1, nchunks - 1, body, 0)
    step(nchunks - 1, 0, last=True)
    down(nchunks - 1, acta_ref)
    for s in range(SUBLANES):
        rs = slice(s * ngrp, (s + 1) * ngrp)
        y = jnp.concatenate([acc_ref[n, pl.ds(s, ngrp, stride=SUBLANES), :] for n in range(nslab)],
                            axis=1)
        o_ref[rs, :] = _rms(xm_ref[rs, :] + y, fg_ref[...])


def _ffn(h2, xm, w_up, conv_w, conv_b, w_down, final_g, tile, chunk):
    b, s, d = h2.shape
    d_ff = w_down.shape[0]
    nh = tile // BF16_ROWS
    last_halo = s // BF16_ROWS - 1
    main = lambda bi, i: (bi, i, 0)
    const = lambda bi, i: (0, 0)
    single = dict(pipeline_mode=pl.Buffered(1))
    kernel = functools.partial(_ffn_kernel, tile=tile, chunk=chunk, d_ff=d_ff)
    return pl.pallas_call(
        kernel,
        grid=(b, s // tile),
        in_specs=[pl.BlockSpec((None, BF16_ROWS, d), lambda bi, i: (bi, jnp.maximum(i * nh - 1, 0), 0)),
                  pl.BlockSpec((None, tile, d), main),
                  pl.BlockSpec((None, BF16_ROWS, d),
                               lambda bi, i: (bi, jnp.minimum((i + 1) * nh, last_halo), 0)),
                  pl.BlockSpec((None, tile, d), main),
                  pl.BlockSpec(w_up.shape, const, **single),
                  pl.BlockSpec(conv_w.shape, const),
                  pl.BlockSpec(conv_b.shape, const),
                  pl.BlockSpec(w_down.shape, const, **single),
                  pl.BlockSpec((1, d), const)],
        out_specs=pl.BlockSpec((None, tile, d), main),
        out_shape=jax.ShapeDtypeStruct((b, s, d), F32),
        scratch_shapes=[pltpu.VMEM((d // LANES, tile, LANES), F32),
                        pltpu.VMEM((tile + BF16_ROWS, d), BF16)]
        + [pltpu.VMEM((tile + BF16_ROWS, 2 * chunk), F32)] * 2
        + [pltpu.VMEM((tile, chunk), BF16)] * 2
        + [pltpu.VMEM((d // LANES, tile + SUBLANES, LANES), F32)],
        compiler_params=pltpu.CompilerParams(
            dimension_semantics=("arbitrary",) * 2, vmem_limit_bytes=VMEM_LIMIT),
        name="ffn",
    )(h2, h2, h2, xm, w_up, conv_w, conv_b, w_down, final_g)


def _gate_weights(w_a, b_a, w_x, b_x):
    ndir, nblk, bd, _ = w_a.shape
    per = LANES // bd
    npair = nblk // per

    def blockdiag(w):
        w = w.reshape(npair, per, bd, bd)
        eye = jnp.eye(per, dtype=w.dtype)
        return jnp.einsum('pbij,bc->pbicj', w, eye).reshape(npair, LANES, LANES)

    ws, bs = [], []
    for d in range(ndir):
        for w, bias in ((w_a, b_a), (w_x, b_x)):
            ws.append(blockdiag(w[d]))
            bs.append(bias[d].reshape(npair, LANES))
    return jnp.concatenate(ws, axis=-1).astype(BF16), jnp.concatenate(bs, axis=-1).astype(F32)


def kernel(x, attn_norm_g, w_in, lambda_q1, lambda_k1, lambda_q2, lambda_k2, subln_g,
           lru_conv_w, lru_conv_b, lru_w_a, lru_b_a, lru_w_x, lru_b_x, lru_lambda,
           w_out, ffn_norm_g, w_up, ffn_conv_w, ffn_conv_b, w_down, final_norm_g):
    b, s, d = x.shape
    depth = w_in.shape[0]
    x2 = x.reshape(b * s, d)
    for l in range(depth):
        lambda_init = 0.8 - 0.6 * math.exp(-0.3 * l)
        q, k, vt, xr, gr = _inproj(x2, attn_norm_g[l][None], w_in[l], tm=512)
        lw = xr.shape[-1]
        attn = _attention(q.reshape(b, s, -1), k.reshape(b, s, -1), vt,
                          lambda_q1[l][None], lambda_k1[l][None], lambda_q2[l][None],
                          lambda_k2[l][None], subln_g[l][:, None], lambda_init, tq=256, nsub=2)
        wg, bg = _gate_weights(lru_w_a[l], lru_b_a[l], lru_w_x[l], lru_b_x[l])
        lru = _lru(xr.reshape(b, s, lw), gr.reshape(b, s, lw), lru_conv_w[l], lru_conv_b[l][None],
                   wg, bg, lru_lambda[l], rows=256)
        xm, h2 = _outproj(x2, attn.reshape(b * s, -1), lru.reshape(b * s, -1),
                          w_out[l], ffn_norm_g[l][None], tm=512)
        assert depth == 1
        x2 = _ffn(h2.reshape(b, s, d), xm.reshape(b, s, d), w_up[l].astype(BF16), ffn_conv_w[l],
                  ffn_conv_b[l][None], w_down[l].astype(BF16), final_norm_g[None],
                  tile=512, chunk=256).reshape(b * s, d)
    return x2.reshape(b, s, d)
```

```python
import functools
import math

import jax
import jax.numpy as jnp
from jax import lax
from jax.experimental import pallas as pl
from jax.experimental.pallas import tpu as pltpu

F32 = jnp.float32
BF16 = jnp.bfloat16

N_HEADS = 4
HEAD_DIM = 64
V_DIM = 2 * HEAD_DIM
ATTN_WIDTH = N_HEADS * V_DIM
LRU_BLOCK = 64
LRU_CONV_WIDTH = 4
LRU_CONV_LEFT = 2
LRU_C = 8.0
FFN_CONV_LEFT = 1
NORM_EPS = 1e-6
LANES = 128
SUBLANES = 8
BF16_ROWS = 16
VMEM_LIMIT = 56 * 1024 * 1024


def _rms(x, g):
    return (x * lax.rsqrt(jnp.mean(x * x, axis=-1, keepdims=True) + NORM_EPS)) * g


def _dot(a, b):
    return jnp.dot(a, b, preferred_element_type=F32)


_NT = (((1,), (1,)), ((), ()))


def _inproj_kernel(x_ref, g_ref, w_ref, q_ref, k_ref, vt_ref, xr_ref, gr_ref, wb_ref, wvt_ref):
    aw = ATTN_WIDTH
    d = x_ref.shape[-1]

    @pl.when(pl.program_id(0) == 0)
    def _cast_weights():
        rows = 2 * LANES
        for r in range(0, d, rows):
            w = w_ref[r:r + rows, :]
            wb_ref[r:r + rows, 0:aw] = (w[:, 0:aw] * (HEAD_DIM ** -0.5)).astype(BF16)
            wb_ref[r:r + rows, aw:] = w[:, aw:].astype(BF16)
        for c in range(aw // LANES):
            cols = slice(2 * aw + c * LANES, 2 * aw + (c + 1) * LANES)
            wvt_ref[c * LANES:(c + 1) * LANES, :] = w_ref[:, cols].T.astype(BF16)

    hb = _rms(x_ref[...], g_ref[...]).astype(BF16)
    q_ref[...] = _dot(hb, wb_ref[:, 0:aw]).astype(BF16)
    k_ref[...] = _dot(hb, wb_ref[:, aw:2 * aw]).astype(BF16)
    vt_ref[...] = lax.dot_general(wvt_ref[...], hb, _NT, preferred_element_type=F32).astype(BF16)
    lw = xr_ref.shape[-1]
    xr_ref[...] = _dot(hb, wb_ref[:, 3 * aw:3 * aw + lw])
    gr_ref[...] = _dot(hb, wb_ref[:, 3 * aw + lw:3 * aw + 2 * lw])


def _inproj(x2, g, w, tm):
    n, d = x2.shape
    lw = (w.shape[1] - 3 * ATTN_WIDTH) // 2
    row = lambda i: (i, 0)
    const = lambda i: (0, 0)
    return pl.pallas_call(
        _inproj_kernel,
        grid=(n // tm,),
        in_specs=[pl.BlockSpec((tm, d), row),
                  pl.BlockSpec((1, d), const),
                  pl.BlockSpec(w.shape, const, pipeline_mode=pl.Buffered(1))],
        out_specs=[pl.BlockSpec((tm, ATTN_WIDTH), row)] * 2
        + [pl.BlockSpec((ATTN_WIDTH, tm), lambda i: (0, i))]
        + [pl.BlockSpec((tm, lw), row)] * 2,
        out_shape=[jax.ShapeDtypeStruct((n, ATTN_WIDTH), BF16)] * 2
        + [jax.ShapeDtypeStruct((ATTN_WIDTH, n), BF16)]
        + [jax.ShapeDtypeStruct((n, lw), F32)] * 2,
        scratch_shapes=[pltpu.VMEM(w.shape, BF16), pltpu.VMEM((ATTN_WIDTH, d), BF16)],
        compiler_params=pltpu.CompilerParams(
            dimension_semantics=("arbitrary",), vmem_limit_bytes=VMEM_LIMIT),
        name="inproj",
    )(x2, g, w)


def _attn_kernel(lq1_ref, lk1_ref, lq2_ref, lk2_ref, sg_ref, q_ref, k_ref, vt_ref, o_ref,
                 kf_ref, dist_ref, *se_refs, tq, nsub, seq, lambda_init):
    h = pl.program_id(1)
    qi = pl.program_id(2)
    s_refs, e_refs, m_refs = se_refs[:nsub], se_refs[nsub:2 * nsub], se_refs[2 * nsub:]
    nblk = seq // tq
    assert nblk & (nblk - 1) == 0 and 3 * nblk <= LANES and tq <= 256
    shift = nblk.bit_length() - 1

    @pl.when((pl.program_id(0) == 0) & (h == 0) & (qi == 0))
    def _init():
        lane = lax.broadcasted_iota(jnp.int32, (seq, LANES), 1)
        row = lax.broadcasted_iota(jnp.int32, (seq, LANES), 0)
        grp = lane >> shift
        hit = (row // tq) == (lane & (nblk - 1))
        dj = (row % tq).astype(F32)
        kf_ref[...] = jnp.where(hit & (grp < 2), 1.0,
                                jnp.where(hit & (grp == 2), dj, 0.0)).astype(BF16)
        r = lax.broadcasted_iota(jnp.int32, (tq, tq), 0)
        c = lax.broadcasted_iota(jnp.int32, (tq, tq), 1)
        dist_ref[...] = jnp.abs(r - c).astype(F32)

    lam = (jnp.exp(jnp.sum(lq1_ref[...] * lk1_ref[...], axis=-1, keepdims=True))
           - jnp.exp(jnp.sum(lq2_ref[...] * lk2_ref[...], axis=-1, keepdims=True))
           + lambda_init)
    slope = jnp.where(h == 0, 2.0 ** -2, jnp.where(h == 1, 2.0 ** -4,
                      jnp.where(h == 2, 2.0 ** -6, 2.0 ** -8))).astype(F32)

    kaug = jnp.concatenate([k_ref[...], kf_ref[...]], axis=1)
    vt_ones = jnp.concatenate([vt_ref[...], jnp.ones((BF16_ROWS, seq), BF16)], axis=0)
    lane = lax.broadcasted_iota(jnp.int32, (tq, LANES), 1)
    di = lax.broadcasted_iota(jnp.int32, (tq, LANES), 0).astype(F32)
    grp = lane >> shift
    strip = 4 * SUBLANES

    def scores(sb):
        blk = qi * nsub + sb
        diff = blk - (lane & (nblk - 1))
        sign = jnp.where(diff > 0, 1.0, jnp.where(diff < 0, -1.0, 0.0))
        qf = jnp.where(grp == 0, -slope * sign * di,
                       jnp.where(grp == 1, -slope * tq * jnp.abs(diff).astype(F32),
                                 jnp.where(grp == 2, slope * sign, 0.0))).astype(BF16)
        q = q_ref[sb * tq:(sb + 1) * tq, :]
        zero = jnp.zeros_like(q)
        qaug = jnp.concatenate(
            [jnp.concatenate([jnp.where(lane < HEAD_DIM, q, zero), qf], axis=1),
             jnp.concatenate([jnp.where(lane >= HEAD_DIM, q, zero), qf], axis=1)], axis=0)
        s_ref = s_refs[sb]
        s = lax.dot_general(kaug, qaug, _NT, preferred_element_type=F32)
        s_ref[...] = s
        mx = s[0:strip, :]
        for r in range(strip, seq, strip):
            mx = jnp.maximum(mx, s[r:r + strip, :])
        m_refs[sb][...] = mx
        diag = pl.ds(pl.multiple_of(blk * tq, tq), tq)
        diag_bias = slope * dist_ref[...]
        s_ref[diag, 0:tq] = s_ref[diag, 0:tq] - diag_bias
        s_ref[diag, tq:2 * tq] = s_ref[diag, tq:2 * tq] - diag_bias

    def finish(sb):
        s_ref, e_ref = s_refs[sb], e_refs[sb]
        mx = jnp.max(m_refs[sb][...], axis=0, keepdims=True)
        for r in range(0, seq, strip):
            e_ref[r:r + strip, :] = jnp.exp(s_ref[r:r + strip, :] - mx).astype(BF16)
        o12 = _dot(vt_ones, e_ref[...])
        norm = o12[V_DIM:V_DIM + 1, :]
        o12 = o12[0:V_DIM, :]
        o = o12[:, 0:tq] * (1.0 / norm[:, 0:tq]) - o12[:, tq:2 * tq] * (lam / norm[:, tq:2 * tq])
        o = o * lax.rsqrt(jnp.mean(o * o, axis=0, keepdims=True) + NORM_EPS)
        o = o * sg_ref[...] * (1.0 - lambda_init)
        o_ref[sb * tq:(sb + 1) * tq, :] = o.T.astype(o_ref.dtype)

    scores(0)
    for sb in range(nsub):
        if sb + 1 < nsub:
            scores(sb + 1)
        finish(sb)


def _attention(q, k, vt, lq1, lk1, lq2, lk2, subln_g, lambda_init, tq, nsub):
    b, s, _ = q.shape
    vec = lambda bi, h, qi: (0, 0)
    tstep = tq * nsub
    kernel = functools.partial(_attn_kernel, tq=tq, nsub=nsub, seq=s, lambda_init=lambda_init)
    return pl.pallas_call(
        kernel,
        grid=(b, N_HEADS, s // tstep),
        in_specs=[pl.BlockSpec((1, HEAD_DIM), vec)] * 4
        + [pl.BlockSpec((V_DIM, 1), vec),
           pl.BlockSpec((None, tstep, V_DIM), lambda bi, h, qi: (bi, qi, h)),
           pl.BlockSpec((None, s, V_DIM), lambda bi, h, qi: (bi, 0, h)),
           pl.BlockSpec((V_DIM, s), lambda bi, h, qi: (h, bi))],
        out_specs=pl.BlockSpec((None, tstep, V_DIM), lambda bi, h, qi: (bi, qi, h)),
        out_shape=jax.ShapeDtypeStruct((b, s, ATTN_WIDTH), BF16),
        scratch_shapes=[pltpu.VMEM((s, LANES), BF16), pltpu.VMEM((tq, tq), F32)]
        + [pltpu.VMEM((s, 2 * tq), F32)] * nsub + [pltpu.VMEM((s, 2 * tq), BF16)] * nsub
        + [pltpu.VMEM((4 * SUBLANES, 2 * tq), F32)] * nsub,
        compiler_params=pltpu.CompilerParams(
            dimension_semantics=("arbitrary",) * 3, vmem_limit_bytes=VMEM_LIMIT),
        name="attn",
    )(lq1, lk1, lq2, lk2, subln_g, q, k, vt)


def _local_scan(a, u, reverse):
    row = lax.broadcasted_iota(jnp.int32, a.shape, 0)
    for d in (1, 2, 4):
        shift = SUBLANES - d if reverse else d
        valid = (row < SUBLANES - d) if reverse else (row >= d)
        a_s = jnp.where(valid, pltpu.roll(a, shift, 0), 1.0)
        u_s = jnp.where(valid, pltpu.roll(u, shift, 0), 0.0)
        u = a * u_s + u
        a = a * a_s
    return a, u


def _lru_kernel(xr_ref, gr_ref, cw_ref, cb_ref, wg_ref, bg_ref, lam_ref, o_ref,
                xp_ref, xc_ref, h_ref, cum_ref, *, seq, rows):
    width = xr_ref.shape[-1]
    ntile = width // LANES
    grp = seq // SUBLANES
    left = LRU_CONV_LEFT
    right = LRU_CONV_WIDTH - 1 - LRU_CONV_LEFT
    top = left * SUBLANES
    sub = lax.broadcasted_iota(jnp.int32, (SUBLANES, LANES), 0)

    for s in range(SUBLANES):
        blk = xr_ref[s * grp:(s + 1) * grp, :]
        for p in range(ntile):
            xp_ref[p, pl.ds(top + s, grp, stride=SUBLANES), :] = blk[:, p * LANES:(p + 1) * LANES]
    for p in range(ntile):
        for k in range(left):
            src = xp_ref[p, top + (grp - 1 - k) * SUBLANES:top + (grp - k) * SUBLANES, :]
            xp_ref[p, top - (k + 1) * SUBLANES:top - k * SUBLANES, :] = jnp.where(
                sub == 0, 0.0, pltpu.roll(src, 1, 0))
        for k in range(right):
            src = xp_ref[p, top + k * SUBLANES:top + (k + 1) * SUBLANES, :]
            xp_ref[p, top + (grp + k) * SUBLANES:top + (grp + k + 1) * SUBLANES, :] = jnp.where(
                sub == SUBLANES - 1, 0.0, pltpu.roll(src, SUBLANES - 1, 0))

    neg_lam = -lam_ref[...]
    softplus = jnp.maximum(neg_lam, 0.0) + jnp.log1p(jnp.exp(-jnp.abs(neg_lam)))
    rate = LRU_C * softplus
    rate_log2 = -rate * math.log2(math.e)

    nchunk = seq // rows
    for c in range(nchunk):
        r0 = c * rows
        for p in range(ntile):
            cols = slice(p * LANES, (p + 1) * LANES)
            xc = cb_ref[:, cols]
            for tap in range(LRU_CONV_WIDTH):
                start = top + r0 + (tap - left) * SUBLANES
                xc = xc + xp_ref[p, start:start + rows, :] * cw_ref[tap:tap + 1, cols]
            xc_ref[p, r0:r0 + rows, :] = xc

    zero = jnp.zeros((SUBLANES, LANES), F32)
    one = jnp.ones((SUBLANES, LANES), F32)
    ends = [[(zero, one), (zero, one)] for _ in range(ntile)]
    for c in range(nchunk):
        for p in range(ntile):
            cols = slice(p * LANES, (p + 1) * LANES)
            for d in range(2):
                r0 = (c if d == 0 else nchunk - 1 - c) * rows
                gcols = slice(2 * d * LANES, (2 * d + 2) * LANES)
                xc = xc_ref[p, r0:r0 + rows, :]
                gates = _dot(xc.astype(BF16), wg_ref[p, :, gcols]) + bg_ref[p:p + 1, gcols]
                r = jax.nn.sigmoid(gates[:, 0:LANES])
                i = jax.nn.sigmoid(gates[:, LANES:2 * LANES])
                t = jnp.tanh(r * rate[d:d + 1, cols])
                tt = t + t
                prod = tt * (1.0 + t)
                mult = jnp.where(prod > 0.0, tt * lax.rsqrt(prod), 0.0)
                a = jnp.exp2(r * rate_log2[d:d + 1, cols])
                u = mult * (i * xc)
                h, cum = ends[p][d]
                vrows = range(rows // SUBLANES)
                for j in (vrows if d == 0 else reversed(vrows)):
                    rs = slice(j * SUBLANES, (j + 1) * SUBLANES)
                    h = a[rs] * h + u[rs]
                    cum = a[rs] * cum
                    h_ref[d, p, r0 + j * SUBLANES:r0 + (j + 1) * SUBLANES, :] = h
                    cum_ref[d, p, r0 + j * SUBLANES:r0 + (j + 1) * SUBLANES, :] = cum
                ends[p][d] = (h, cum)

    enter = []
    for p in range(ntile):
        for d in range(2):
            h_end, cum_end = ends[p][d]
            _, chained = _local_scan(cum_end, h_end, reverse=(d == 1))
            if d == 0:
                enter.append(jnp.where(sub == 0, 0.0, pltpu.roll(chained, 1, 0)))
            else:
                enter.append(jnp.where(sub == SUBLANES - 1, 0.0,
                                       pltpu.roll(chained, SUBLANES - 1, 0)))

    for c in range(seq // rows):
        rs = slice(c * rows, (c + 1) * rows)
        for p in range(ntile):
            y = None
            for d in range(2):
                init = jnp.tile(enter[2 * p + d], (rows // SUBLANES, 1))
                part = h_ref[d, p, rs, :] + cum_ref[d, p, rs, :] * init
                y = part if y is None else y + part
            xc_ref[p, rs, :] = y

    for s in range(SUBLANES):
        rs = slice(s * grp, (s + 1) * grp)
        for p in range(ntile):
            cols = slice(p * LANES, (p + 1) * LANES)
            y = xc_ref[p, pl.ds(s, grp, stride=SUBLANES), :]
            o_ref[rs, cols] = (jax.nn.gelu(gr_ref[rs, cols], approximate=True) * y).astype(o_ref.dtype)


def _lru(xr, gr, conv_w, conv_b, wg, bg, lru_lambda, rows):
    b, s, width = xr.shape
    ntile = width // LANES
    seqblk = pl.BlockSpec((None, s, width), lambda bi: (bi, 0, 0))
    full = lambda a: pl.BlockSpec(a.shape, lambda bi: (0,) * a.ndim)
    kernel = functools.partial(_lru_kernel, seq=s, rows=rows)
    halo_rows = (LRU_CONV_WIDTH - 1) * SUBLANES
    return pl.pallas_call(
        kernel,
        grid=(b,),
        in_specs=[seqblk, seqblk, full(conv_w), full(conv_b), full(wg), full(bg), full(lru_lambda)],
        out_specs=seqblk,
        out_shape=jax.ShapeDtypeStruct((b, s, width), BF16),
        scratch_shapes=[pltpu.VMEM((ntile, s + halo_rows, LANES), F32),
                        pltpu.VMEM((ntile, s, LANES), F32),
                        pltpu.VMEM((2, ntile, s, LANES), F32),
                        pltpu.VMEM((2, ntile, s, LANES), F32)],
        compiler_params=pltpu.CompilerParams(
            dimension_semantics=("arbitrary",), vmem_limit_bytes=VMEM_LIMIT),
        name="lru",
    )(xr, gr, conv_w, conv_b, wg, bg, lru_lambda)


def _outproj_kernel(x_ref, a_ref, l_ref, w_ref, g_ref, xm_ref, h_ref, wb_ref):
    @pl.when(pl.program_id(0) == 0)
    def _cast_weights():
        wb_ref[...] = w_ref[...].astype(BF16)

    aw = a_ref.shape[-1]
    xm = x_ref[...] + _dot(a_ref[...], wb_ref[0:aw, :]) + _dot(l_ref[...], wb_ref[aw:, :])
    xm_ref[...] = xm
    h_ref[...] = _rms(xm, g_ref[...]).astype(h_ref.dtype)


def _outproj(x2, attn2, lru2, w, g, tm):
    n, d = x2.shape
    row = lambda i: (i, 0)
    const = lambda i: (0, 0)
    return pl.pallas_call(
        _outproj_kernel,
        grid=(n // tm,),
        in_specs=[pl.BlockSpec((tm, d), row),
                  pl.BlockSpec((tm, attn2.shape[1]), row),
                  pl.BlockSpec((tm, lru2.shape[1]), row),
                  pl.BlockSpec(w.shape, const, pipeline_mode=pl.Buffered(1)),
                  pl.BlockSpec((1, d), const)],
        out_specs=[pl.BlockSpec((tm, d), row)] * 2,
        out_shape=[jax.ShapeDtypeStruct((n, d), F32), jax.ShapeDtypeStruct((n, d), BF16)],
        scratch_shapes=[pltpu.VMEM(w.shape, BF16)],
        compiler_params=pltpu.CompilerParams(
            dimension_semantics=("arbitrary",), vmem_limit_bytes=VMEM_LIMIT),
        name="outproj",
    )(x2, attn2, lru2, w, g)


def _ffn_kernel(hp_ref, hm_ref, hn_ref, xm_ref, wup_ref, cw_ref, cb_ref, wdn_ref, fg_ref, o_ref,
                perm_ref, hext_ref, ua_ref, ub_ref, acta_ref, actb_ref, acc_ref,
                *, tile, chunk, d_ff):
    i = pl.program_id(1)
    nchunks = d_ff // chunk
    ngrp = tile // SUBLANES
    nslab = perm_ref.shape[0]
    d = nslab * LANES

    for s in range(SUBLANES):
        rows = hm_ref[s * ngrp:(s + 1) * ngrp, :].astype(F32)
        for n in range(nslab):
            perm_ref[n, pl.ds(s, ngrp, stride=SUBLANES), :] = rows[:, n * LANES:(n + 1) * LANES]
    for n in range(nslab):
        hext_ref[0:tile, n * LANES:(n + 1) * LANES] = perm_ref[n].astype(BF16)
    prev = jnp.where(i > 0, hp_ref[BF16_ROWS - 1:BF16_ROWS, :].astype(F32), 0.0)
    nxt = jnp.where(i < pl.num_programs(1) - 1, hn_ref[0:1, :].astype(F32), 0.0)
    hrow = lax.broadcasted_iota(jnp.int32, (BF16_ROWS, d), 0)
    halo = jnp.where(hrow == 0, prev, jnp.where(hrow == 1, nxt, 0.0))
    hext_ref[tile:tile + BF16_ROWS, :] = halo.astype(BF16)
    acc_ref[...] = jnp.zeros_like(acc_ref)

    def offsets(c):
        og, ov = c * chunk, d_ff + c * chunk
        if isinstance(c, int):
            return og, ov
        return pl.multiple_of(og, chunk), pl.multiple_of(ov, chunk)

    def up(c, u_ref):
        og, ov = offsets(c)
        hext = hext_ref[...]
        u_ref[:, 0:chunk] = _dot(hext, wup_ref[:, pl.ds(og, chunk)])
        u_ref[:, chunk:2 * chunk] = _dot(hext, wup_ref[:, pl.ds(ov, chunk)])

    def glu(c, u_ref, act_ref):
        sub = lax.broadcasted_iota(jnp.int32, (SUBLANES, chunk), 0)
        blk = BF16_ROWS

        def conv(col0, off, r0):
            cols = slice(col0, col0 + chunk)
            w = cw_ref[:, pl.ds(off, chunk)]
            cur = u_ref[r0:r0 + blk, cols]
            if r0 == 0:
                first = jnp.where(sub == 0, u_ref[tile:tile + 1, cols],
                                  pltpu.roll(u_ref[tile - SUBLANES:tile, cols], 1, 0))
                um1 = jnp.concatenate([first, cur[0:blk - SUBLANES]], axis=0)
            else:
                um1 = u_ref[r0 - SUBLANES:r0 + blk - SUBLANES, cols]
            if r0 + blk == tile:
                last = jnp.where(sub == SUBLANES - 1, u_ref[tile + 1:tile + 2, cols],
                                 pltpu.roll(u_ref[0:SUBLANES, cols], SUBLANES - 1, 0))
                up1 = jnp.concatenate([cur[SUBLANES:blk], last], axis=0)
            else:
                up1 = u_ref[r0 + SUBLANES:r0 + blk + SUBLANES, cols]
            return (cb_ref[:, pl.ds(off, chunk)] + um1 * w[0:1, :] + cur * w[1:2, :]
                    + up1 * w[2:3, :])

        og, ov = offsets(c)
        for r0 in range(0, tile, blk):
            gate = conv(0, og, r0)
            val = conv(chunk, ov, r0)
            act_ref[r0:r0 + blk, :] = (jax.nn.gelu(gate, approximate=True) * val).astype(BF16)

    def down(c, act_ref):
        og, _ = offsets(c)
        y = _dot(act_ref[...], wdn_ref[pl.ds(og, chunk), :])
        for n in range(nslab):
            acc_ref[n, 0:tile, :] += y[:, n * LANES:(n + 1) * LANES]

    u_refs, act_refs = (ua_ref, ub_ref), (acta_ref, actb_ref)

    def step(t, parity, first=False, last=False):
        if not last:
            up(t + 1, u_refs[1 - parity])
        glu(t, u_refs[parity], act_refs[parity])
        if not first:
            down(t - 1, act_refs[1 - parity])

    def body(t, carry):
        lax.cond(t % 2 == 1, lambda: step(t, 1), lambda: step(t, 0))
        return carry

    assert nchunks % 2 == 1 and nchunks >= 3
    up(0, ua_ref)

    def pair(j, carry):
        c = 2 * j
        up(c + 1, ub_ref)
        glu(c, ua_ref, acta_ref)
        down(c, acta_ref)
        up(c + 2, ua_ref)
        glu(c + 1, ub_ref, actb_ref)
        down(c + 1, actb_ref)
        return carry

    lax.fori_loop(0, nchunks // 2, pair, 0)
    glu(nchunks - 1, ua_ref, acta_ref)
    down(nchunks - 1, acta_ref)
    for s in range(SUBLANES):
        rs = slice(s * ngrp, (s + 1) * ngrp)
        y = jnp.concatenate([acc_ref[n, pl.ds(s, ngrp, stride=SUBLANES), :] for n in range(nslab)],
                            axis=1)
        o_ref[rs, :] = _rms(xm_ref[rs, :] + y, fg_ref[...])


def _ffn(h2, xm, w_up, conv_w, conv_b, w_down, final_g, tile, chunk):
    b, s, d = h2.shape
    d_ff = w_down.shape[0]
    nh = tile // BF16_ROWS
    last_halo = s // BF16_ROWS - 1
    main = lambda bi, i: (bi, i, 0)
    const = lambda bi, i: (0, 0)
    single = dict(pipeline_mode=pl.Buffered(1))
    kernel = functools.partial(_ffn_kernel, tile=tile, chunk=chunk, d_ff=d_ff)
    return pl.pallas_call(
        kernel,
        grid=(b, s // tile),
        in_specs=[pl.BlockSpec((None, BF16_ROWS, d), lambda bi, i: (bi, jnp.maximum(i * nh - 1, 0), 0)),
                  pl.BlockSpec((None, tile, d), main),
                  pl.BlockSpec((None, BF16_ROWS, d),
                               lambda bi, i: (bi, jnp.minimum((i + 1) * nh, last_halo), 0)),
                  pl.BlockSpec((None, tile, d), main),
                  pl.BlockSpec(w_up.shape, const, **single),
                  pl.BlockSpec(conv_w.shape, const),
                  pl.BlockSpec(conv_b.shape, const),
                  pl.BlockSpec(w_down.shape, const, **single),
                  pl.BlockSpec((1, d), const)],
        out_specs=pl.BlockSpec((None, tile, d), main),
        out_shape=jax.ShapeDtypeStruct((b, s, d), F32),
        scratch_shapes=[pltpu.VMEM((d // LANES, tile, LANES), F32),
                        pltpu.VMEM((tile + BF16_ROWS, d), BF16)]
        + [pltpu.VMEM((tile + BF16_ROWS, 2 * chunk), F32)] * 2
        + [pltpu.VMEM((tile, chunk), BF16)] * 2
        + [pltpu.VMEM((d // LANES, tile + SUBLANES, LANES), F32)],
        compiler_params=pltpu.CompilerParams(
            dimension_semantics=("arbitrary",) * 2, vmem_limit_bytes=VMEM_LIMIT),
        name="ffn",
    )(h2, h2, h2, xm, w_up, conv_w, conv_b, w_down, final_g)


def _gate_weights(w_a, b_a, w_x, b_x):
    ndir, nblk, bd, _ = w_a.shape
    per = LANES // bd
    npair = nblk // per

    def blockdiag(w):
        w = w.reshape(npair, per, bd, bd)
        eye = jnp.eye(per, dtype=w.dtype)
        return jnp.einsum('pbij,bc->pbicj', w, eye).reshape(npair, LANES, LANES)

    ws, bs = [], []
    for d in range(ndir):
        for w, bias in ((w_a, b_a), (w_x, b_x)):
            ws.append(blockdiag(w[d]))
            bs.append(bias[d].reshape(npair, LANES))
    return jnp.concatenate(ws, axis=-1).astype(BF16), jnp.concatenate(bs, axis=-1).astype(F32)


def kernel(x, attn_norm_g, w_in, lambda_q1, lambda_k1, lambda_q2, lambda_k2, subln_g,
           lru_conv_w, lru_conv_b, lru_w_a, lru_b_a, lru_w_x, lru_b_x, lru_lambda,
           w_out, ffn_norm_g, w_up, ffn_conv_w, ffn_conv_b, w_down, final_norm_g):
    b, s, d = x.shape
    depth = w_in.shape[0]
    x2 = x.reshape(b * s, d)
    for l in range(depth):
        lambda_init = 0.8 - 0.6 * math.exp(-0.3 * l)
        q, k, vt, xr, gr = _inproj(x2, attn_norm_g[l][None], w_in[l], tm=512)
        lw = xr.shape[-1]
        attn = _attention(q.reshape(b, s, -1), k.reshape(b, s, -1), vt,
                          lambda_q1[l][None], lambda_k1[l][None], lambda_q2[l][None],
                          lambda_k2[l][None], subln_g[l][:, None], lambda_init, tq=256, nsub=2)
        wg, bg = _gate_weights(lru_w_a[l], lru_b_a[l], lru_w_x[l], lru_b_x[l])
        lru = _lru(xr.reshape(b, s, lw), gr.reshape(b, s, lw), lru_conv_w[l], lru_conv_b[l][None],
                   wg, bg, lru_lambda[l], rows=256)
        xm, h2 = _outproj(x2, attn.reshape(b * s, -1), lru.reshape(b * s, -1),
                          w_out[l], ffn_norm_g[l][None], tm=512)
        assert depth == 1
        x2 = _ffn(h2.reshape(b, s, d), xm.reshape(b, s, d), w_up[l].astype(BF16), ffn_conv_w[l],
                  ffn_conv_b[l][None], w_down[l].astype(BF16), final_norm_g[None],
                  tile=512, chunk=256).reshape(b * s, d)
    return x2.reshape(b, s, d)
```

```python
import functools
import math

import jax
import jax.numpy as jnp
from jax import lax
from jax.experimental import pallas as pl
from jax.experimental.pallas import tpu as pltpu

F32 = jnp.float32
BF16 = jnp.bfloat16

N_HEADS = 4
HEAD_DIM = 64
V_DIM = 2 * HEAD_DIM
ATTN_WIDTH = N_HEADS * V_DIM
LRU_BLOCK = 64
LRU_CONV_WIDTH = 4
LRU_CONV_LEFT = 2
LRU_C = 8.0
FFN_CONV_LEFT = 1
NORM_EPS = 1e-6
LANES = 128
SUBLANES = 8
BF16_ROWS = 16
VMEM_LIMIT = 56 * 1024 * 1024


def _rms(x, g):
    return (x * lax.rsqrt(jnp.mean(x * x, axis=-1, keepdims=True) + NORM_EPS)) * g


def _dot(a, b):
    return jnp.dot(a, b, preferred_element_type=F32)


_NT = (((1,), (1,)), ((), ()))


def _inproj_kernel(x_ref, g_ref, w_ref, q_ref, k_ref, vt_ref, xr_ref, gr_ref, wb_ref, wvt_ref):
    aw = ATTN_WIDTH
    d = x_ref.shape[-1]

    @pl.when(pl.program_id(0) == 0)
    def _cast_weights():
        rows = 2 * LANES
        for r in range(0, d, rows):
            w = w_ref[r:r + rows, :]
            wb_ref[r:r + rows, 0:aw] = (w[:, 0:aw] * (HEAD_DIM ** -0.5)).astype(BF16)
            wb_ref[r:r + rows, aw:] = w[:, aw:].astype(BF16)
        for c in range(aw // LANES):
            cols = slice(2 * aw + c * LANES, 2 * aw + (c + 1) * LANES)
            wvt_ref[c * LANES:(c + 1) * LANES, :] = w_ref[:, cols].T.astype(BF16)

    hb = _rms(x_ref[...], g_ref[...]).astype(BF16)
    q_ref[...] = _dot(hb, wb_ref[:, 0:aw]).astype(BF16)
    k_ref[...] = _dot(hb, wb_ref[:, aw:2 * aw]).astype(BF16)
    vt_ref[...] = lax.dot_general(wvt_ref[...], hb, _NT, preferred_element_type=F32).astype(BF16)
    lw = xr_ref.shape[-1]
    xr_ref[...] = _dot(hb, wb_ref[:, 3 * aw:3 * aw + lw])
    gr_ref[...] = _dot(hb, wb_ref[:, 3 * aw + lw:3 * aw + 2 * lw])


def _inproj(x2, g, w, tm):
    n, d = x2.shape
    lw = (w.shape[1] - 3 * ATTN_WIDTH) // 2
    row = lambda i: (i, 0)
    const = lambda i: (0, 0)
    return pl.pallas_call(
        _inproj_kernel,
        grid=(n // tm,),
        in_specs=[pl.BlockSpec((tm, d), row),
                  pl.BlockSpec((1, d), const),
                  pl.BlockSpec(w.shape, const, pipeline_mode=pl.Buffered(1))],
        out_specs=[pl.BlockSpec((tm, ATTN_WIDTH), row)] * 2
        + [pl.BlockSpec((ATTN_WIDTH, tm), lambda i: (0, i))]
        + [pl.BlockSpec((tm, lw), row)] * 2,
        out_shape=[jax.ShapeDtypeStruct((n, ATTN_WIDTH), BF16)] * 2
        + [jax.ShapeDtypeStruct((ATTN_WIDTH, n), BF16)]
        + [jax.ShapeDtypeStruct((n, lw), F32)] * 2,
        scratch_shapes=[pltpu.VMEM(w.shape, BF16), pltpu.VMEM((ATTN_WIDTH, d), BF16)],
        compiler_params=pltpu.CompilerParams(
            dimension_semantics=("arbitrary",), vmem_limit_bytes=VMEM_LIMIT),
        name="inproj",
    )(x2, g, w)


def _attn_kernel(lq1_ref, lk1_ref, lq2_ref, lk2_ref, sg_ref, q_ref, k_ref, vt_ref, o_ref,
                 kf_ref, dist_ref, *se_refs, tq, nsub, seq, lambda_init):
    h = pl.program_id(1)
    qi = pl.program_id(2)
    s_refs, e_refs, m_refs = se_refs[:nsub], se_refs[nsub:2 * nsub], se_refs[2 * nsub:]
    nblk = seq // tq
    assert nblk & (nblk - 1) == 0 and 3 * nblk <= LANES and tq <= 256
    shift = nblk.bit_length() - 1

    @pl.when((pl.program_id(0) == 0) & (h == 0) & (qi == 0))
    def _init():
        lane = lax.broadcasted_iota(jnp.int32, (seq, LANES), 1)
        row = lax.broadcasted_iota(jnp.int32, (seq, LANES), 0)
        grp = lane >> shift
        hit = (row // tq) == (lane & (nblk - 1))
        dj = (row % tq).astype(F32)
        kf_ref[...] = jnp.where(hit & (grp < 2), 1.0,
                                jnp.where(hit & (grp == 2), dj, 0.0)).astype(BF16)
        r = lax.broadcasted_iota(jnp.int32, (tq, tq), 0)
        c = lax.broadcasted_iota(jnp.int32, (tq, tq), 1)
        dist_ref[...] = jnp.abs(r - c).astype(F32)

    lam = (jnp.exp(jnp.sum(lq1_ref[...] * lk1_ref[...], axis=-1, keepdims=True))
           - jnp.exp(jnp.sum(lq2_ref[...] * lk2_ref[...], axis=-1, keepdims=True))
           + lambda_init)
    slope = jnp.where(h == 0, 2.0 ** -2, jnp.where(h == 1, 2.0 ** -4,
                      jnp.where(h == 2, 2.0 ** -6, 2.0 ** -8))).astype(F32)

    kaug = jnp.concatenate([k_ref[...], kf_ref[...]], axis=1)
    vt_ones = jnp.concatenate([vt_ref[...], jnp.ones((BF16_ROWS, seq), BF16)], axis=0)
    lane = lax.broadcasted_iota(jnp.int32, (tq, LANES), 1)
    di = lax.broadcasted_iota(jnp.int32, (tq, LANES), 0).astype(F32)
    grp = lane >> shift
    strip = 4 * SUBLANES

    def scores(sb):
        blk = qi * nsub + sb
        diff = blk - (lane & (nblk - 1))
        sign = jnp.where(diff > 0, 1.0, jnp.where(diff < 0, -1.0, 0.0))
        qf = jnp.where(grp == 0, -slope * sign * di,
                       jnp.where(grp == 1, -slope * tq * jnp.abs(diff).astype(F32),
                                 jnp.where(grp == 2, slope * sign, 0.0))).astype(BF16)
        q = q_ref[sb * tq:(sb + 1) * tq, :]
        zero = jnp.zeros_like(q)
        qaug = jnp.concatenate(
            [jnp.concatenate([jnp.where(lane < HEAD_DIM, q, zero), qf], axis=1),
             jnp.concatenate([jnp.where(lane >= HEAD_DIM, q, zero), qf], axis=1)], axis=0)
        s_ref = s_refs[sb]
        s = lax.dot_general(kaug, qaug, _NT, preferred_element_type=F32)
        s_ref[...] = s
        mx = s[0:strip, :]
        for r in range(strip, seq, strip):
            mx = jnp.maximum(mx, s[r:r + strip, :])
        m_refs[sb][...] = mx
        diag = pl.ds(pl.multiple_of(blk * tq, tq), tq)
        diag_bias = slope * dist_ref[...]
        s_ref[diag, 0:tq] = s_ref[diag, 0:tq] - diag_bias
        s_ref[diag, tq:2 * tq] = s_ref[diag, tq:2 * tq] - diag_bias

    def finish(sb):
        s_ref, e_ref = s_refs[sb], e_refs[sb]
        mx = jnp.max(m_refs[sb][...], axis=0, keepdims=True)
        for r in range(0, seq, strip):
            e_ref[r:r + strip, :] = jnp.exp(s_ref[r:r + strip, :] - mx).astype(BF16)
        o12 = _dot(vt_ones, e_ref[...])
        norm = o12[V_DIM:V_DIM + 1, :]
        o12 = o12[0:V_DIM, :]
        o = o12[:, 0:tq] * (1.0 / norm[:, 0:tq]) - o12[:, tq:2 * tq] * (lam / norm[:, tq:2 * tq])
        o = o * lax.rsqrt(jnp.mean(o * o, axis=0, keepdims=True) + NORM_EPS)
        o = o * sg_ref[...] * (1.0 - lambda_init)
        o_ref[sb * tq:(sb + 1) * tq, :] = o.T.astype(o_ref.dtype)

    scores(0)
    for sb in range(nsub):
        if sb + 1 < nsub:
            scores(sb + 1)
        finish(sb)


def _attention(q, k, vt, lq1, lk1, lq2, lk2, subln_g, lambda_init, tq, nsub):
    b, s, _ = q.shape
    vec = lambda bi, h, qi: (0, 0)
    tstep = tq * nsub
    kernel = functools.partial(_attn_kernel, tq=tq, nsub=nsub, seq=s, lambda_init=lambda_init)
    return pl.pallas_call(
        kernel,
        grid=(b, N_HEADS, s // tstep),
        in_specs=[pl.BlockSpec((1, HEAD_DIM), vec)] * 4
        + [pl.BlockSpec((V_DIM, 1), vec),
           pl.BlockSpec((None, tstep, V_DIM), lambda bi, h, qi: (bi, qi, h)),
           pl.BlockSpec((None, s, V_DIM), lambda bi, h, qi: (bi, 0, h)),
           pl.BlockSpec((V_DIM, s), lambda bi, h, qi: (h, bi))],
        out_specs=pl.BlockSpec((None, tstep, V_DIM), lambda bi, h, qi: (bi, qi, h)),
        out_shape=jax.ShapeDtypeStruct((b, s, ATTN_WIDTH), BF16),
        scratch_shapes=[pltpu.VMEM((s, LANES), BF16), pltpu.VMEM((tq, tq), F32)]
        + [pltpu.VMEM((s, 2 * tq), F32)] * nsub + [pltpu.VMEM((s, 2 * tq), BF16)] * nsub
        + [pltpu.VMEM((4 * SUBLANES, 2 * tq), F32)] * nsub,
        compiler_params=pltpu.CompilerParams(
            dimension_semantics=("arbitrary",) * 3, vmem_limit_bytes=VMEM_LIMIT),
        name="attn",
    )(lq1, lk1, lq2, lk2, subln_g, q, k, vt)


def _local_scan(a, u, reverse):
    row = lax.broadcasted_iota(jnp.int32, a.shape, 0)
    for d in (1, 2, 4):
        shift = SUBLANES - d if reverse else d
        valid = (row < SUBLANES - d) if reverse else (row >= d)
        a_s = jnp.where(valid, pltpu.roll(a, shift, 0), 1.0)
        u_s = jnp.where(valid, pltpu.roll(u, shift, 0), 0.0)
        u = a * u_s + u
        a = a * a_s
    return a, u


def _lru_kernel(xr_ref, gr_ref, cw_ref, cb_ref, wg_ref, bg_ref, lam_ref, o_ref,
                xp_ref, xc_ref, h_ref, cum_ref, *, seq, rows):
    width = xr_ref.shape[-1]
    ntile = width // LANES
    grp = seq // SUBLANES
    left = LRU_CONV_LEFT
    right = LRU_CONV_WIDTH - 1 - LRU_CONV_LEFT
    top = left * SUBLANES
    sub = lax.broadcasted_iota(jnp.int32, (SUBLANES, LANES), 0)

    for s in range(SUBLANES):
        blk = xr_ref[s * grp:(s + 1) * grp, :]
        for p in range(ntile):
            xp_ref[p, pl.ds(top + s, grp, stride=SUBLANES), :] = blk[:, p * LANES:(p + 1) * LANES]
    for p in range(ntile):
        for k in range(left):
            src = xp_ref[p, top + (grp - 1 - k) * SUBLANES:top + (grp - k) * SUBLANES, :]
            xp_ref[p, top - (k + 1) * SUBLANES:top - k * SUBLANES, :] = jnp.where(
                sub == 0, 0.0, pltpu.roll(src, 1, 0))
        for k in range(right):
            src = xp_ref[p, top + k * SUBLANES:top + (k + 1) * SUBLANES, :]
            xp_ref[p, top + (grp + k) * SUBLANES:top + (grp + k + 1) * SUBLANES, :] = jnp.where(
                sub == SUBLANES - 1, 0.0, pltpu.roll(src, SUBLANES - 1, 0))

    neg_lam = -lam_ref[...]
    softplus = jnp.maximum(neg_lam, 0.0) + jnp.log1p(jnp.exp(-jnp.abs(neg_lam)))
    rate = LRU_C * softplus
    rate_log2 = -rate * math.log2(math.e)

    nchunk = seq // rows
    for c in range(nchunk):
        r0 = c * rows
        for p in range(ntile):
            cols = slice(p * LANES, (p + 1) * LANES)
            xc = cb_ref[:, cols]
            for tap in range(LRU_CONV_WIDTH):
                start = top + r0 + (tap - left) * SUBLANES
                xc = xc + xp_ref[p, start:start + rows, :] * cw_ref[tap:tap + 1, cols]
            xc_ref[p, r0:r0 + rows, :] = xc

    zero = jnp.zeros((SUBLANES, LANES), F32)
    one = jnp.ones((SUBLANES, LANES), F32)
    ends = [[(zero, one), (zero, one)] for _ in range(ntile)]
    for c in range(nchunk):
        for p in range(ntile):
            cols = slice(p * LANES, (p + 1) * LANES)
            for d in range(2):
                r0 = (c if d == 0 else nchunk - 1 - c) * rows
                gcols = slice(2 * d * LANES, (2 * d + 2) * LANES)
                xc = xc_ref[p, r0:r0 + rows, :]
                gates = _dot(xc.astype(BF16), wg_ref[p, :, gcols]) + bg_ref[p:p + 1, gcols]
                r = jax.nn.sigmoid(gates[:, 0:LANES])
                i = jax.nn.sigmoid(gates[:, LANES:2 * LANES])
                t = jnp.tanh(r * rate[d:d + 1, cols])
                tt = t + t
                prod = tt * (1.0 + t)
                mult = jnp.where(prod > 0.0, tt * lax.rsqrt(prod), 0.0)
                a = jnp.exp2(r * rate_log2[d:d + 1, cols])
                u = mult * (i * xc)
                h, cum = ends[p][d]
                vrows = range(rows // SUBLANES)
                for j in (vrows if d == 0 else reversed(vrows)):
                    rs = slice(j * SUBLANES, (j + 1) * SUBLANES)
                    h = a[rs] * h + u[rs]
                    cum = a[rs] * cum
                    h_ref[d, p, r0 + j * SUBLANES:r0 + (j + 1) * SUBLANES, :] = h
                    cum_ref[d, p, r0 + j * SUBLANES:r0 + (j + 1) * SUBLANES, :] = cum
                ends[p][d] = (h, cum)

    enter = []
    for p in range(ntile):
        for d in range(2):
            h_end, cum_end = ends[p][d]
            _, chained = _local_scan(cum_end, h_end, reverse=(d == 1))
            if d == 0:
                enter.append(jnp.where(sub == 0, 0.0, pltpu.roll(chained, 1, 0)))
            else:
                enter.append(jnp.where(sub == SUBLANES - 1, 0.0,
                                       pltpu.roll(chained, SUBLANES - 1, 0)))

    for c in range(seq // rows):
        rs = slice(c * rows, (c + 1) * rows)
        for p in range(ntile):
            y = None
            for d in range(2):
                init = jnp.tile(enter[2 * p + d], (rows // SUBLANES, 1))
                part = h_ref[d, p, rs, :] + cum_ref[d, p, rs, :] * init
                y = part if y is None else y + part
            xc_ref[p, rs, :] = y

    for s in range(SUBLANES):
        rs = slice(s * grp, (s + 1) * grp)
        for p in range(ntile):
            cols = slice(p * LANES, (p + 1) * LANES)
            y = xc_ref[p, pl.ds(s, grp, stride=SUBLANES), :]
            o_ref[rs, cols] = (jax.nn.gelu(gr_ref[rs, cols], approximate=True) * y).astype(o_ref.dtype)


def _lru(xr, gr, conv_w, conv_b, wg, bg, lru_lambda, rows):
    b, s, width = xr.shape
    ntile = width // LANES
    seqblk = pl.BlockSpec((None, s, width), lambda bi: (bi, 0, 0))
    full = lambda a: pl.BlockSpec(a.shape, lambda bi: (0,) * a.ndim)
    kernel = functools.partial(_lru_kernel, seq=s, rows=rows)
    halo_rows = (LRU_CONV_WIDTH - 1) * SUBLANES
    return pl.pallas_call(
        kernel,
        grid=(b,),
        in_specs=[seqblk, seqblk, full(conv_w), full(conv_b), full(wg), full(bg), full(lru_lambda)],
        out_specs=seqblk,
        out_shape=jax.ShapeDtypeStruct((b, s, width), BF16),
        scratch_shapes=[pltpu.VMEM((ntile, s + halo_rows, LANES), F32),
                        pltpu.VMEM((ntile, s, LANES), F32),
                        pltpu.VMEM((2, ntile, s, LANES), F32),
                        pltpu.VMEM((2, ntile, s, LANES), F32)],
        compiler_params=pltpu.CompilerParams(
            dimension_semantics=("arbitrary",), vmem_limit_bytes=VMEM_LIMIT),
        name="lru",
    )(xr, gr, conv_w, conv_b, wg, bg, lru_lambda)


def _outproj_kernel(x_ref, a_ref, l_ref, w_ref, g_ref, xm_ref, h_ref, wb_ref):
    @pl.when(pl.program_id(0) == 0)
    def _cast_weights():
        wb_ref[...] = w_ref[...].astype(BF16)

    aw = a_ref.shape[-1]
    xm = x_ref[...] + _dot(a_ref[...], wb_ref[0:aw, :]) + _dot(l_ref[...], wb_ref[aw:, :])
    xm_ref[...] = xm
    h_ref[...] = _rms(xm, g_ref[...]).astype(h_ref.dtype)


def _outproj(x2, attn2, lru2, w, g, tm):
    n, d = x2.shape
    row = lambda i: (i, 0)
    const = lambda i: (0, 0)
    return pl.pallas_call(
        _outproj_kernel,
        grid=(n // tm,),
        in_specs=[pl.BlockSpec((tm, d), row),
                  pl.BlockSpec((tm, attn2.shape[1]), row),
                  pl.BlockSpec((tm, lru2.shape[1]), row),
                  pl.BlockSpec(w.shape, const, pipeline_mode=pl.Buffered(1)),
                  pl.BlockSpec((1, d), const)],
        out_specs=[pl.BlockSpec((tm, d), row)] * 2,
        out_shape=[jax.ShapeDtypeStruct((n, d), F32), jax.ShapeDtypeStruct((n, d), BF16)],
        scratch_shapes=[pltpu.VMEM(w.shape, BF16)],
        compiler_params=pltpu.CompilerParams(
            dimension_semantics=("arbitrary",), vmem_limit_bytes=VMEM_LIMIT),
        name="outproj",
    )(x2, attn2, lru2, w, g)


def _ffn_kernel(hp_ref, hm_ref, hn_ref, xm_ref, wup_ref, cw_ref, cb_ref, wdn_ref, fg_ref, o_ref,
                perm_ref, hext_ref, ua_ref, ub_ref, acta_ref, actb_ref, acc_ref,
                *, tile, chunk, d_ff):
    i = pl.program_id(1)
    nchunks = d_ff // chunk
    ngrp = tile // SUBLANES
    nslab = perm_ref.shape[0]
    d = nslab * LANES

    for s in range(SUBLANES):
        rows = hm_ref[s * ngrp:(s + 1) * ngrp, :].astype(F32)
        for n in range(nslab):
            perm_ref[n, pl.ds(s, ngrp, stride=SUBLANES), :] = rows[:, n * LANES:(n + 1) * LANES]
    for n in range(nslab):
        hext_ref[0:tile, n * LANES:(n + 1) * LANES] = perm_ref[n].astype(BF16)
    prev = jnp.where(i > 0, hp_ref[BF16_ROWS - 1:BF16_ROWS, :].astype(F32), 0.0)
    nxt = jnp.where(i < pl.num_programs(1) - 1, hn_ref[0:1, :].astype(F32), 0.0)
    hrow = lax.broadcasted_iota(jnp.int32, (BF16_ROWS, d), 0)
    halo = jnp.where(hrow == 0, prev, jnp.where(hrow == 1, nxt, 0.0))
    hext_ref[tile:tile + BF16_ROWS, :] = halo.astype(BF16)
    acc_ref[...] = jnp.zeros_like(acc_ref)

    def offsets(c):
        og, ov = c * chunk, d_ff + c * chunk
        if isinstance(c, int):
            return og, ov
        return pl.multiple_of(og, chunk), pl.multiple_of(ov, chunk)

    def up(c, u_ref):
        og, ov = offsets(c)
        hext = hext_ref[...]
        u_ref[:, 0:chunk] = _dot(hext, wup_ref[:, pl.ds(og, chunk)])
        u_ref[:, chunk:2 * chunk] = _dot(hext, wup_ref[:, pl.ds(ov, chunk)])

    def glu(c, u_ref, act_ref):
        sub = lax.broadcasted_iota(jnp.int32, (SUBLANES, chunk), 0)
        blk = BF16_ROWS

        def conv(col0, off, r0):
            cols = slice(col0, col0 + chunk)
            w = cw_ref[:, pl.ds(off, chunk)]
            cur = u_ref[r0:r0 + blk, cols]
            if r0 == 0:
                first = jnp.where(sub == 0, u_ref[tile:tile + 1, cols],
                                  pltpu.roll(u_ref[tile - SUBLANES:tile, cols], 1, 0))
                um1 = jnp.concatenate([first, cur[0:blk - SUBLANES]], axis=0)
            else:
                um1 = u_ref[r0 - SUBLANES:r0 + blk - SUBLANES, cols]
            if r0 + blk == tile:
                last = jnp.where(sub == SUBLANES - 1, u_ref[tile + 1:tile + 2, cols],
                                 pltpu.roll(u_ref[0:SUBLANES, cols], SUBLANES - 1, 0))
                up1 = jnp.concatenate([cur[SUBLANES:blk], last], axis=0)
            else:
                up1 = u_ref[r0 + SUBLANES:r0 + blk + SUBLANES, cols]
            return (cb_ref[:, pl.ds(off, chunk)] + um1 * w[0:1, :] + cur * w[1:2, :]
                    + up1 * w[2:3, :])

        og, ov = offsets(c)
        for r0 in range(0, tile, blk):
            gate = conv(0, og, r0)
            val = conv(chunk, ov, r0)
            act_ref[r0:r0 + blk, :] = (jax.nn.gelu(gate, approximate=True) * val).astype(BF16)

    def down(c, act_ref):
        og, _ = offsets(c)
        y = _dot(act_ref[...], wdn_ref[pl.ds(og, chunk), :])
        for n in range(nslab):
            acc_ref[n, 0:tile, :] += y[:, n * LANES:(n + 1) * LANES]

    assert nchunks % 2 == 1 and nchunks >= 3
    up(0, ua_ref)

    def pair(j, carry):
        c = 2 * j
        up(c + 1, ub_ref)
        glu(c, ua_ref, acta_ref)
        down(c, acta_ref)
        up(c + 2, ua_ref)
        glu(c + 1, ub_ref, actb_ref)
        down(c + 1, actb_ref)
        return carry

    lax.fori_loop(0, nchunks // 2, pair, 0)
    glu(nchunks - 1, ua_ref, acta_ref)
    down(nchunks - 1, acta_ref)
    for s in range(SUBLANES):
        rs = slice(s * ngrp, (s + 1) * ngrp)
        y = jnp.concatenate([acc_ref[n, pl.ds(s, ngrp, stride=SUBLANES), :] for n in range(nslab)],
                            axis=1)
        o_ref[rs, :] = _rms(xm_ref[rs, :] + y, fg_ref[...])


def _ffn(h2, xm, w_up, conv_w, conv_b, w_down, final_g, tile, chunk):
    b, s, d = h2.shape
    d_ff = w_down.shape[0]
    nh = tile // BF16_ROWS
    last_halo = s // BF16_ROWS - 1
    main = lambda bi, i: (bi, i, 0)
    const = lambda bi, i: (0, 0)
    single = dict(pipeline_mode=pl.Buffered(1))
    kernel = functools.partial(_ffn_kernel, tile=tile, chunk=chunk, d_ff=d_ff)
    return pl.pallas_call(
        kernel,
        grid=(b, s // tile),
        in_specs=[pl.BlockSpec((None, BF16_ROWS, d), lambda bi, i: (bi, jnp.maximum(i * nh - 1, 0), 0)),
                  pl.BlockSpec((None, tile, d), main),
                  pl.BlockSpec((None, BF16_ROWS, d),
                               lambda bi, i: (bi, jnp.minimum((i + 1) * nh, last_halo), 0)),
                  pl.BlockSpec((None, tile, d), main),
                  pl.BlockSpec(w_up.shape, const, **single),
                  pl.BlockSpec(conv_w.shape, const),
                  pl.BlockSpec(conv_b.shape, const),
                  pl.BlockSpec(w_down.shape, const, **single),
                  pl.BlockSpec((1, d), const)],
        out_specs=pl.BlockSpec((None, tile, d), main),
        out_shape=jax.ShapeDtypeStruct((b, s, d), F32),
        scratch_shapes=[pltpu.VMEM((d // LANES, tile, LANES), F32),
                        pltpu.VMEM((tile + BF16_ROWS, d), BF16)]
        + [pltpu.VMEM((tile + BF16_ROWS, 2 * chunk), F32)] * 2
        + [pltpu.VMEM((tile, chunk), BF16)] * 2
        + [pltpu.VMEM((d // LANES, tile + SUBLANES, LANES), F32)],
        compiler_params=pltpu.CompilerParams(
            dimension_semantics=("arbitrary",) * 2, vmem_limit_bytes=VMEM_LIMIT),
        name="ffn",
    )(h2, h2, h2, xm, w_up, conv_w, conv_b, w_down, final_g)


def _gate_weights(w_a, b_a, w_x, b_x):
    ndir, nblk, bd, _ = w_a.shape
    per = LANES // bd
    npair = nblk // per

    def blockdiag(w):
        w = w.reshape(npair, per, bd, bd)
        eye = jnp.eye(per, dtype=w.dtype)
        return jnp.einsum('pbij,bc->pbicj', w, eye).reshape(npair, LANES, LANES)

    ws, bs = [], []
    for d in range(ndir):
        for w, bias in ((w_a, b_a), (w_x, b_x)):
            ws.append(blockdiag(w[d]))
            bs.append(bias[d].reshape(npair, LANES))
    return jnp.concatenate(ws, axis=-1).astype(BF16), jnp.concatenate(bs, axis=-1).astype(F32)


def kernel(x, attn_norm_g, w_in, lambda_q1, lambda_k1, lambda_q2, lambda_k2, subln_g,
           lru_conv_w, lru_conv_b, lru_w_a, lru_b_a, lru_w_x, lru_b_x, lru_lambda,
           w_out, ffn_norm_g, w_up, ffn_conv_w, ffn_conv_b, w_down, final_norm_g):
    b, s, d = x.shape
    depth = w_in.shape[0]
    x2 = x.reshape(b * s, d)
    for l in range(depth):
        lambda_init = 0.8 - 0.6 * math.exp(-0.3 * l)
        q, k, vt, xr, gr = _inproj(x2, attn_norm_g[l][None], w_in[l], tm=512)
        lw = xr.shape[-1]
        attn = _attention(q.reshape(b, s, -1), k.reshape(b, s, -1), vt,
                          lambda_q1[l][None], lambda_k1[l][None], lambda_q2[l][None],
                          lambda_k2[l][None], subln_g[l][:, None], lambda_init, tq=256, nsub=4)
        wg, bg = _gate_weights(lru_w_a[l], lru_b_a[l], lru_w_x[l], lru_b_x[l])
        lru = _lru(xr.reshape(b, s, lw), gr.reshape(b, s, lw), lru_conv_w[l], lru_conv_b[l][None],
                   wg, bg, lru_lambda[l], rows=256)
        xm, h2 = _outproj(x2, attn.reshape(b * s, -1), lru.reshape(b * s, -1),
                          w_out[l], ffn_norm_g[l][None], tm=512)
        assert depth == 1
        x2 = _ffn(h2.reshape(b, s, d), xm.reshape(b, s, d), w_up[l].astype(BF16), ffn_conv_w[l],
                  ffn_conv_b[l][None], w_down[l].astype(BF16), final_norm_g[None],
                  tile=512, chunk=256).reshape(b * s, d)
    return x2.reshape(b, s, d)
```

```python
import functools
import math

import jax
import jax.numpy as jnp
from jax import lax
from jax.experimental import pallas as pl
from jax.experimental.pallas import tpu as pltpu

F32 = jnp.float32
BF16 = jnp.bfloat16

N_HEADS = 4
HEAD_DIM = 64
V_DIM = 2 * HEAD_DIM
ATTN_WIDTH = N_HEADS * V_DIM
LRU_BLOCK = 64
LRU_CONV_WIDTH = 4
LRU_CONV_LEFT = 2
LRU_C = 8.0
FFN_CONV_LEFT = 1
NORM_EPS = 1e-6
LANES = 128
SUBLANES = 8
BF16_ROWS = 16
VMEM_LIMIT = 56 * 1024 * 1024


def _rms(x, g):
    return (x * lax.rsqrt(jnp.mean(x * x, axis=-1, keepdims=True) + NORM_EPS)) * g


def _dot(a, b):
    return jnp.dot(a, b, preferred_element_type=F32)


_NT = (((1,), (1,)), ((), ()))


def _inproj_kernel(x_ref, g_ref, w_ref, q_ref, k_ref, vt_ref, xr_ref, gr_ref, wb_ref, wvt_ref):
    aw = ATTN_WIDTH
    d = x_ref.shape[-1]

    @pl.when(pl.program_id(0) == 0)
    def _cast_weights():
        rows = 2 * LANES
        for r in range(0, d, rows):
            w = w_ref[r:r + rows, :]
            wb_ref[r:r + rows, 0:aw] = (w[:, 0:aw] * (HEAD_DIM ** -0.5)).astype(BF16)
            wb_ref[r:r + rows, aw:] = w[:, aw:].astype(BF16)
        for c in range(aw // LANES):
            cols = slice(2 * aw + c * LANES, 2 * aw + (c + 1) * LANES)
            wvt_ref[c * LANES:(c + 1) * LANES, :] = w_ref[:, cols].T.astype(BF16)

    hb = _rms(x_ref[...], g_ref[...]).astype(BF16)
    q_ref[...] = _dot(hb, wb_ref[:, 0:aw]).astype(BF16)
    k_ref[...] = _dot(hb, wb_ref[:, aw:2 * aw]).astype(BF16)
    vt_ref[...] = lax.dot_general(wvt_ref[...], hb, _NT, preferred_element_type=F32).astype(BF16)
    lw = xr_ref.shape[-1]
    xr_ref[...] = _dot(hb, wb_ref[:, 3 * aw:3 * aw + lw])
    gr_ref[...] = _dot(hb, wb_ref[:, 3 * aw + lw:3 * aw + 2 * lw])


def _inproj(x2, g, w, tm):
    n, d = x2.shape
    lw = (w.shape[1] - 3 * ATTN_WIDTH) // 2
    row = lambda i: (i, 0)
    const = lambda i: (0, 0)
    return pl.pallas_call(
        _inproj_kernel,
        grid=(n // tm,),
        in_specs=[pl.BlockSpec((tm, d), row),
                  pl.BlockSpec((1, d), const),
                  pl.BlockSpec(w.shape, const, pipeline_mode=pl.Buffered(1))],
        out_specs=[pl.BlockSpec((tm, ATTN_WIDTH), row)] * 2
        + [pl.BlockSpec((ATTN_WIDTH, tm), lambda i: (0, i))]
        + [pl.BlockSpec((tm, lw), row)] * 2,
        out_shape=[jax.ShapeDtypeStruct((n, ATTN_WIDTH), BF16)] * 2
        + [jax.ShapeDtypeStruct((ATTN_WIDTH, n), BF16)]
        + [jax.ShapeDtypeStruct((n, lw), F32)] * 2,
        scratch_shapes=[pltpu.VMEM(w.shape, BF16), pltpu.VMEM((ATTN_WIDTH, d), BF16)],
        compiler_params=pltpu.CompilerParams(
            dimension_semantics=("arbitrary",), vmem_limit_bytes=VMEM_LIMIT),
        name="inproj",
    )(x2, g, w)


def _attn_kernel(lq1_ref, lk1_ref, lq2_ref, lk2_ref, sg_ref, q_ref, k_ref, vt_ref, o_ref,
                 kf_ref, dist_ref, *se_refs, tq, nsub, seq, lambda_init):
    h = pl.program_id(1)
    qi = pl.program_id(2)
    s_refs, e_refs, m_refs = se_refs[:nsub], se_refs[nsub:2 * nsub], se_refs[2 * nsub:]
    nblk = seq // tq
    assert nblk & (nblk - 1) == 0 and 3 * nblk <= LANES and tq <= 256
    shift = nblk.bit_length() - 1

    @pl.when((pl.program_id(0) == 0) & (h == 0) & (qi == 0))
    def _init():
        lane = lax.broadcasted_iota(jnp.int32, (seq, LANES), 1)
        row = lax.broadcasted_iota(jnp.int32, (seq, LANES), 0)
        grp = lane >> shift
        hit = (row // tq) == (lane & (nblk - 1))
        dj = (row % tq).astype(F32)
        kf_ref[...] = jnp.where(hit & (grp < 2), 1.0,
                                jnp.where(hit & (grp == 2), dj, 0.0)).astype(BF16)
        r = lax.broadcasted_iota(jnp.int32, (tq, tq), 0)
        c = lax.broadcasted_iota(jnp.int32, (tq, tq), 1)
        dist_ref[...] = jnp.abs(r - c).astype(F32)

    lam = (jnp.exp(jnp.sum(lq1_ref[...] * lk1_ref[...], axis=-1, keepdims=True))
           - jnp.exp(jnp.sum(lq2_ref[...] * lk2_ref[...], axis=-1, keepdims=True))
           + lambda_init)
    slope = jnp.where(h == 0, 2.0 ** -2, jnp.where(h == 1, 2.0 ** -4,
                      jnp.where(h == 2, 2.0 ** -6, 2.0 ** -8))).astype(F32)

    kaug = jnp.concatenate([k_ref[...], kf_ref[...]], axis=1)
    vt_ones = jnp.concatenate([vt_ref[...], jnp.ones((BF16_ROWS, seq), BF16)], axis=0)
    lane = lax.broadcasted_iota(jnp.int32, (tq, LANES), 1)
    di = lax.broadcasted_iota(jnp.int32, (tq, LANES), 0).astype(F32)
    grp = lane >> shift
    strip = 4 * SUBLANES

    def scores(sb):
        blk = qi * nsub + sb
        diff = blk - (lane & (nblk - 1))
        sign = jnp.where(diff > 0, 1.0, jnp.where(diff < 0, -1.0, 0.0))
        qf = jnp.where(grp == 0, -slope * sign * di,
                       jnp.where(grp == 1, -slope * tq * jnp.abs(diff).astype(F32),
                                 jnp.where(grp == 2, slope * sign, 0.0))).astype(BF16)
        q = q_ref[sb * tq:(sb + 1) * tq, :]
        zero = jnp.zeros_like(q)
        qaug = jnp.concatenate(
            [jnp.concatenate([jnp.where(lane < HEAD_DIM, q, zero), qf], axis=1),
             jnp.concatenate([jnp.where(lane >= HEAD_DIM, q, zero), qf], axis=1)], axis=0)
        s_ref = s_refs[sb]
        s = lax.dot_general(kaug, qaug, _NT, preferred_element_type=F32)
        s_ref[...] = s
        mx = s[0:strip, :]
        for r in range(strip, seq, strip):
            mx = jnp.maximum(mx, s[r:r + strip, :])
        m_refs[sb][...] = mx
        diag = pl.ds(pl.multiple_of(blk * tq, tq), tq)
        diag_bias = slope * dist_ref[...]
        s_ref[diag, 0:tq] = s_ref[diag, 0:tq] - diag_bias
        s_ref[diag, tq:2 * tq] = s_ref[diag, tq:2 * tq] - diag_bias

    def finish(sb):
        s_ref, e_ref = s_refs[sb], e_refs[sb]
        mx = jnp.max(m_refs[sb][...], axis=0, keepdims=True)
        for r in range(0, seq, strip):
            e_ref[r:r + strip, :] = jnp.exp(s_ref[r:r + strip, :] - mx).astype(BF16)
        o12 = _dot(vt_ones, e_ref[...])
        norm = o12[V_DIM:V_DIM + 1, :]
        o12 = o12[0:V_DIM, :]
        o = o12[:, 0:tq] * (1.0 / norm[:, 0:tq]) - o12[:, tq:2 * tq] * (lam / norm[:, tq:2 * tq])
        o = o * lax.rsqrt(jnp.mean(o * o, axis=0, keepdims=True) + NORM_EPS)
        o = o * sg_ref[...] * (1.0 - lambda_init)
        o_ref[sb * tq:(sb + 1) * tq, :] = o.T.astype(o_ref.dtype)

    scores(0)
    for sb in range(nsub):
        if sb + 1 < nsub:
            scores(sb + 1)
        finish(sb)


def _attention(q, k, vt, lq1, lk1, lq2, lk2, subln_g, lambda_init, tq, nsub):
    b, s, _ = q.shape
    vec = lambda bi, h, qi: (0, 0)
    tstep = tq * nsub
    kernel = functools.partial(_attn_kernel, tq=tq, nsub=nsub, seq=s, lambda_init=lambda_init)
    return pl.pallas_call(
        kernel,
        grid=(b, N_HEADS, s // tstep),
        in_specs=[pl.BlockSpec((1, HEAD_DIM), vec)] * 4
        + [pl.BlockSpec((V_DIM, 1), vec),
           pl.BlockSpec((None, tstep, V_DIM), lambda bi, h, qi: (bi, qi, h)),
           pl.BlockSpec((None, s, V_DIM), lambda bi, h, qi: (bi, 0, h)),
           pl.BlockSpec((V_DIM, s), lambda bi, h, qi: (h, bi))],
        out_specs=pl.BlockSpec((None, tstep, V_DIM), lambda bi, h, qi: (bi, qi, h)),
        out_shape=jax.ShapeDtypeStruct((b, s, ATTN_WIDTH), BF16),
        scratch_shapes=[pltpu.VMEM((s, LANES), BF16), pltpu.VMEM((tq, tq), F32)]
        + [pltpu.VMEM((s, 2 * tq), F32)] * nsub + [pltpu.VMEM((s, 2 * tq), BF16)] * nsub
        + [pltpu.VMEM((4 * SUBLANES, 2 * tq), F32)] * nsub,
        compiler_params=pltpu.CompilerParams(
            dimension_semantics=("arbitrary",) * 3, vmem_limit_bytes=VMEM_LIMIT),
        name="attn",
    )(lq1, lk1, lq2, lk2, subln_g, q, k, vt)


def _local_scan(a, u, reverse):
    row = lax.broadcasted_iota(jnp.int32, a.shape, 0)
    for d in (1, 2, 4):
        shift = SUBLANES - d if reverse else d
        valid = (row < SUBLANES - d) if reverse else (row >= d)
        a_s = jnp.where(valid, pltpu.roll(a, shift, 0), 1.0)
        u_s = jnp.where(valid, pltpu.roll(u, shift, 0), 0.0)
        u = a * u_s + u
        a = a * a_s
    return a, u


def _lru_kernel(xr_ref, gr_ref, cw_ref, cb_ref, wg_ref, bg_ref, lam_ref, o_ref,
                xp_ref, xc_ref, h_ref, cum_ref, *, seq, rows):
    width = xr_ref.shape[-1]
    ntile = width // LANES
    grp = seq // SUBLANES
    left = LRU_CONV_LEFT
    right = LRU_CONV_WIDTH - 1 - LRU_CONV_LEFT
    top = left * SUBLANES
    sub = lax.broadcasted_iota(jnp.int32, (SUBLANES, LANES), 0)

    for s in range(SUBLANES):
        blk = xr_ref[s * grp:(s + 1) * grp, :]
        for p in range(ntile):
            xp_ref[p, pl.ds(top + s, grp, stride=SUBLANES), :] = blk[:, p * LANES:(p + 1) * LANES]
    for p in range(ntile):
        for k in range(left):
            src = xp_ref[p, top + (grp - 1 - k) * SUBLANES:top + (grp - k) * SUBLANES, :]
            xp_ref[p, top - (k + 1) * SUBLANES:top - k * SUBLANES, :] = jnp.where(
                sub == 0, 0.0, pltpu.roll(src, 1, 0))
        for k in range(right):
            src = xp_ref[p, top + k * SUBLANES:top + (k + 1) * SUBLANES, :]
            xp_ref[p, top + (grp + k) * SUBLANES:top + (grp + k + 1) * SUBLANES, :] = jnp.where(
                sub == SUBLANES - 1, 0.0, pltpu.roll(src, SUBLANES - 1, 0))

    neg_lam = -lam_ref[...]
    softplus = jnp.maximum(neg_lam, 0.0) + jnp.log1p(jnp.exp(-jnp.abs(neg_lam)))
    rate = LRU_C * softplus
    rate_log2 = -rate * math.log2(math.e)

    nchunk = seq // rows
    for c in range(nchunk):
        r0 = c * rows
        for p in range(ntile):
            cols = slice(p * LANES, (p + 1) * LANES)
            xc = cb_ref[:, cols]
            for tap in range(LRU_CONV_WIDTH):
                start = top + r0 + (tap - left) * SUBLANES
                xc = xc + xp_ref[p, start:start + rows, :] * cw_ref[tap:tap + 1, cols]
            xc_ref[p, r0:r0 + rows, :] = xc

    zero = jnp.zeros((SUBLANES, LANES), F32)
    one = jnp.ones((SUBLANES, LANES), F32)
    ends = [[(zero, one), (zero, one)] for _ in range(ntile)]
    for c in range(nchunk):
        for p in range(ntile):
            cols = slice(p * LANES, (p + 1) * LANES)
            for d in range(2):
                r0 = (c if d == 0 else nchunk - 1 - c) * rows
                gcols = slice(2 * d * LANES, (2 * d + 2) * LANES)
                xc = xc_ref[p, r0:r0 + rows, :]
                gates = _dot(xc.astype(BF16), wg_ref[p, :, gcols]) + bg_ref[p:p + 1, gcols]
                r = jax.nn.sigmoid(gates[:, 0:LANES])
                i = jax.nn.sigmoid(gates[:, LANES:2 * LANES])
                t = jnp.tanh(r * rate[d:d + 1, cols])
                tt = t + t
                prod = tt * (1.0 + t)
                mult = jnp.where(prod > 0.0, tt * lax.rsqrt(prod), 0.0)
                a = jnp.exp2(r * rate_log2[d:d + 1, cols])
                u = mult * (i * xc)
                h, cum = ends[p][d]
                vrows = range(rows // SUBLANES)
                for j in (vrows if d == 0 else reversed(vrows)):
                    rs = slice(j * SUBLANES, (j + 1) * SUBLANES)
                    h = a[rs] * h + u[rs]
                    cum = a[rs] * cum
                    h_ref[d, p, r0 + j * SUBLANES:r0 + (j + 1) * SUBLANES, :] = h
                    cum_ref[d, p, r0 + j * SUBLANES:r0 + (j + 1) * SUBLANES, :] = cum
                ends[p][d] = (h, cum)

    enter = []
    for p in range(ntile):
        for d in range(2):
            h_end, cum_end = ends[p][d]
            _, chained = _local_scan(cum_end, h_end, reverse=(d == 1))
            if d == 0:
                enter.append(jnp.where(sub == 0, 0.0, pltpu.roll(chained, 1, 0)))
            else:
                enter.append(jnp.where(sub == SUBLANES - 1, 0.0,
                                       pltpu.roll(chained, SUBLANES - 1, 0)))

    for c in range(seq // rows):
        rs = slice(c * rows, (c + 1) * rows)
        for p in range(ntile):
            y = None
            for d in range(2):
                init = jnp.tile(enter[2 * p + d], (rows // SUBLANES, 1))
                part = h_ref[d, p, rs, :] + cum_ref[d, p, rs, :] * init
                y = part if y is None else y + part
            xc_ref[p, rs, :] = y

    for s in range(SUBLANES):
        rs = slice(s * grp, (s + 1) * grp)
        for p in range(ntile):
            cols = slice(p * LANES, (p + 1) * LANES)
            y = xc_ref[p, pl.ds(s, grp, stride=SUBLANES), :]
            o_ref[rs, cols] = (jax.nn.gelu(gr_ref[rs, cols], approximate=True) * y).astype(o_ref.dtype)


def _lru(xr, gr, conv_w, conv_b, wg, bg, lru_lambda, rows):
    b, s, width = xr.shape
    ntile = width // LANES
    seqblk = pl.BlockSpec((None, s, width), lambda bi: (bi, 0, 0))
    full = lambda a: pl.BlockSpec(a.shape, lambda bi: (0,) * a.ndim)
    kernel = functools.partial(_lru_kernel, seq=s, rows=rows)
    halo_rows = (LRU_CONV_WIDTH - 1) * SUBLANES
    return pl.pallas_call(
        kernel,
        grid=(b,),
        in_specs=[seqblk, seqblk, full(conv_w), full(conv_b), full(wg), full(bg), full(lru_lambda)],
        out_specs=seqblk,
        out_shape=jax.ShapeDtypeStruct((b, s, width), BF16),
        scratch_shapes=[pltpu.VMEM((ntile, s + halo_rows, LANES), F32),
                        pltpu.VMEM((ntile, s, LANES), F32),
                        pltpu.VMEM((2, ntile, s, LANES), F32),
                        pltpu.VMEM((2, ntile, s, LANES), F32)],
        compiler_params=pltpu.CompilerParams(
            dimension_semantics=("arbitrary",), vmem_limit_bytes=VMEM_LIMIT),
        name="lru",
    )(xr, gr, conv_w, conv_b, wg, bg, lru_lambda)


def _outproj_kernel(x_ref, a_ref, l_ref, w_ref, g_ref, xm_ref, h_ref, wb_ref):
    @pl.when(pl.program_id(0) == 0)
    def _cast_weights():
        wb_ref[...] = w_ref[...].astype(BF16)

    aw = a_ref.shape[-1]
    xm = x_ref[...] + _dot(a_ref[...], wb_ref[0:aw, :]) + _dot(l_ref[...], wb_ref[aw:, :])
    xm_ref[...] = xm
    h_ref[...] = _rms(xm, g_ref[...]).astype(h_ref.dtype)


def _outproj(x2, attn2, lru2, w, g, tm):
    n, d = x2.shape
    row = lambda i: (i, 0)
    const = lambda i: (0, 0)
    return pl.pallas_call(
        _outproj_kernel,
        grid=(n // tm,),
        in_specs=[pl.BlockSpec((tm, d), row),
                  pl.BlockSpec((tm, attn2.shape[1]), row),
                  pl.BlockSpec((tm, lru2.shape[1]), row),
                  pl.BlockSpec(w.shape, const, pipeline_mode=pl.Buffered(1)),
                  pl.BlockSpec((1, d), const)],
        out_specs=[pl.BlockSpec((tm, d), row)] * 2,
        out_shape=[jax.ShapeDtypeStruct((n, d), F32), jax.ShapeDtypeStruct((n, d), BF16)],
        scratch_shapes=[pltpu.VMEM(w.shape, BF16)],
        compiler_params=pltpu.CompilerParams(
            dimension_semantics=("arbitrary",), vmem_limit_bytes=VMEM_LIMIT),
        name="outproj",
    )(x2, attn2, lru2, w, g)


def _ffn_kernel(hp_ref, hm_ref, hn_ref, xm_ref, wup_ref, cw_ref, cb_ref, wdn_ref, fg_ref, o_ref,
                perm_ref, hext_ref, ua_ref, ub_ref, acta_ref, actb_ref, acc_ref,
                *, tile, chunk, d_ff):
    i = pl.program_id(1)
    nchunks = d_ff // chunk
    ngrp = tile // SUBLANES
    nslab = perm_ref.shape[0]
    d = nslab * LANES

    for s in range(SUBLANES):
        rows = hm_ref[s * ngrp:(s + 1) * ngrp, :].astype(F32)
        for n in range(nslab):
            perm_ref[n, pl.ds(s, ngrp, stride=SUBLANES), :] = rows[:, n * LANES:(n + 1) * LANES]
    for n in range(nslab):
        hext_ref[0:tile, n * LANES:(n + 1) * LANES] = perm_ref[n].astype(BF16)
    prev = jnp.where(i > 0, hp_ref[BF16_ROWS - 1:BF16_ROWS, :].astype(F32), 0.0)
    nxt = jnp.where(i < pl.num_programs(1) - 1, hn_ref[0:1, :].astype(F32), 0.0)
    hrow = lax.broadcasted_iota(jnp.int32, (BF16_ROWS, d), 0)
    halo = jnp.where(hrow == 0, prev, jnp.where(hrow == 1, nxt, 0.0))
    hext_ref[tile:tile + BF16_ROWS, :] = halo.astype(BF16)
    acc_ref[...] = jnp.zeros_like(acc_ref)

    def offsets(c):
        og, ov = c * chunk, d_ff + c * chunk
        if isinstance(c, int):
            return og, ov
        return pl.multiple_of(og, chunk), pl.multiple_of(ov, chunk)

    def up(c, u_ref):
        og, ov = offsets(c)
        hext = hext_ref[...]
        u_ref[:, 0:chunk] = _dot(hext, wup_ref[:, pl.ds(og, chunk)])
        u_ref[:, chunk:2 * chunk] = _dot(hext, wup_ref[:, pl.ds(ov, chunk)])

    def glu(c, u_ref, act_ref):
        sub = lax.broadcasted_iota(jnp.int32, (SUBLANES, chunk), 0)
        blk = BF16_ROWS

        def conv(col0, off, r0):
            cols = slice(col0, col0 + chunk)
            reps = (blk // SUBLANES, 1)
            w = [jnp.tile(cw_ref[k * SUBLANES:(k + 1) * SUBLANES, pl.ds(off, chunk)], reps)
                 for k in range(3)]
            bias = jnp.tile(cb_ref[:, pl.ds(off, chunk)], reps)
            cur = u_ref[r0:r0 + blk, cols]
            if r0 == 0:
                first = jnp.where(sub == 0, u_ref[tile:tile + 1, cols],
                                  pltpu.roll(u_ref[tile - SUBLANES:tile, cols], 1, 0))
                um1 = jnp.concatenate([first, cur[0:blk - SUBLANES]], axis=0)
            else:
                um1 = u_ref[r0 - SUBLANES:r0 + blk - SUBLANES, cols]
            if r0 + blk == tile:
                last = jnp.where(sub == SUBLANES - 1, u_ref[tile + 1:tile + 2, cols],
                                 pltpu.roll(u_ref[0:SUBLANES, cols], SUBLANES - 1, 0))
                up1 = jnp.concatenate([cur[SUBLANES:blk], last], axis=0)
            else:
                up1 = u_ref[r0 + SUBLANES:r0 + blk + SUBLANES, cols]
            return bias + um1 * w[0] + cur * w[1] + up1 * w[2]

        og, ov = offsets(c)
        for r0 in range(0, tile, blk):
            gate = conv(0, og, r0)
            val = conv(chunk, ov, r0)
            act_ref[r0:r0 + blk, :] = (jax.nn.gelu(gate, approximate=True) * val).astype(BF16)

    def down(c, act_ref):
        og, _ = offsets(c)
        y = _dot(act_ref[...], wdn_ref[pl.ds(og, chunk), :])
        for n in range(nslab):
            acc_ref[n, 0:tile, :] += y[:, n * LANES:(n + 1) * LANES]

    assert nchunks % 2 == 1 and nchunks >= 3
    up(0, ua_ref)

    def pair(j, carry):
        c = 2 * j
        up(c + 1, ub_ref)
        glu(c, ua_ref, acta_ref)
        down(c, acta_ref)
        up(c + 2, ua_ref)
        glu(c + 1, ub_ref, actb_ref)
        down(c + 1, actb_ref)
        return carry

    lax.fori_loop(0, nchunks // 2, pair, 0)
    glu(nchunks - 1, ua_ref, acta_ref)
    down(nchunks - 1, acta_ref)
    for s in range(SUBLANES):
        rs = slice(s * ngrp, (s + 1) * ngrp)
        y = jnp.concatenate([acc_ref[n, pl.ds(s, ngrp, stride=SUBLANES), :] for n in range(nslab)],
                            axis=1)
        o_ref[rs, :] = _rms(xm_ref[rs, :] + y, fg_ref[...])


def _ffn(h2, xm, w_up, conv_w, conv_b, w_down, final_g, tile, chunk):
    b, s, d = h2.shape
    d_ff = w_down.shape[0]
    nh = tile // BF16_ROWS
    last_halo = s // BF16_ROWS - 1
    main = lambda bi, i: (bi, i, 0)
    const = lambda bi, i: (0, 0)
    single = dict(pipeline_mode=pl.Buffered(1))
    kernel = functools.partial(_ffn_kernel, tile=tile, chunk=chunk, d_ff=d_ff)
    return pl.pallas_call(
        kernel,
        grid=(b, s // tile),
        in_specs=[pl.BlockSpec((None, BF16_ROWS, d), lambda bi, i: (bi, jnp.maximum(i * nh - 1, 0), 0)),
                  pl.BlockSpec((None, tile, d), main),
                  pl.BlockSpec((None, BF16_ROWS, d),
                               lambda bi, i: (bi, jnp.minimum((i + 1) * nh, last_halo), 0)),
                  pl.BlockSpec((None, tile, d), main),
                  pl.BlockSpec(w_up.shape, const, **single),
                  pl.BlockSpec(conv_w.shape, const),
                  pl.BlockSpec(conv_b.shape, const),
                  pl.BlockSpec(w_down.shape, const, **single),
                  pl.BlockSpec((1, d), const)],
        out_specs=pl.BlockSpec((None, tile, d), main),
        out_shape=jax.ShapeDtypeStruct((b, s, d), F32),
        scratch_shapes=[pltpu.VMEM((d // LANES, tile, LANES), F32),
                        pltpu.VMEM((tile + BF16_ROWS, d), BF16)]
        + [pltpu.VMEM((tile + BF16_ROWS, 2 * chunk), F32)] * 2
        + [pltpu.VMEM((tile, chunk), BF16)] * 2
        + [pltpu.VMEM((d // LANES, tile + SUBLANES, LANES), F32)],
        compiler_params=pltpu.CompilerParams(
            dimension_semantics=("arbitrary",) * 2, vmem_limit_bytes=VMEM_LIMIT),
        name="ffn",
    )(h2, h2, h2, xm, w_up, conv_w, conv_b, w_down, final_g)


def _gate_weights(w_a, b_a, w_x, b_x):
    ndir, nblk, bd, _ = w_a.shape
    per = LANES // bd
    npair = nblk // per

    def blockdiag(w):
        w = w.reshape(npair, per, bd, bd)
        eye = jnp.eye(per, dtype=w.dtype)
        return jnp.einsum('pbij,bc->pbicj', w, eye).reshape(npair, LANES, LANES)

    ws, bs = [], []
    for d in range(ndir):
        for w, bias in ((w_a, b_a), (w_x, b_x)):
            ws.append(blockdiag(w[d]))
            bs.append(bias[d].reshape(npair, LANES))
    return jnp.concatenate(ws, axis=-1).astype(BF16), jnp.concatenate(bs, axis=-1).astype(F32)


def kernel(x, attn_norm_g, w_in, lambda_q1, lambda_k1, lambda_q2, lambda_k2, subln_g,
           lru_conv_w, lru_conv_b, lru_w_a, lru_b_a, lru_w_x, lru_b_x, lru_lambda,
           w_out, ffn_norm_g, w_up, ffn_conv_w, ffn_conv_b, w_down, final_norm_g):
    b, s, d = x.shape
    depth = w_in.shape[0]
    x2 = x.reshape(b * s, d)
    for l in range(depth):
        lambda_init = 0.8 - 0.6 * math.exp(-0.3 * l)
        q, k, vt, xr, gr = _inproj(x2, attn_norm_g[l][None], w_in[l], tm=512)
        lw = xr.shape[-1]
        attn = _attention(q.reshape(b, s, -1), k.reshape(b, s, -1), vt,
                          lambda_q1[l][None], lambda_k1[l][None], lambda_q2[l][None],
                          lambda_k2[l][None], subln_g[l][:, None], lambda_init, tq=256, nsub=4)
        wg, bg = _gate_weights(lru_w_a[l], lru_b_a[l], lru_w_x[l], lru_b_x[l])
        lru = _lru(xr.reshape(b, s, lw), gr.reshape(b, s, lw), lru_conv_w[l], lru_conv_b[l][None],
                   wg, bg, lru_lambda[l], rows=256)
        xm, h2 = _outproj(x2, attn.reshape(b * s, -1), lru.reshape(b * s, -1),
                          w_out[l], ffn_norm_g[l][None], tm=1024)
        assert depth == 1
        x2 = _ffn(h2.reshape(b, s, d), xm.reshape(b, s, d), w_up[l].astype(BF16),
                  jnp.repeat(ffn_conv_w[l], SUBLANES, axis=0),
                  jnp.broadcast_to(ffn_conv_b[l][None], (SUBLANES, ffn_conv_b.shape[-1])),
                  w_down[l].astype(BF16), final_norm_g[None],
                  tile=512, chunk=256).reshape(b * s, d)
    return x2.reshape(b, s, d)
```

```python
import functools
import math

import jax
import jax.numpy as jnp
from jax import lax
from jax.experimental import pallas as pl
from jax.experimental.pallas import tpu as pltpu

F32 = jnp.float32
BF16 = jnp.bfloat16

N_HEADS = 4
HEAD_DIM = 64
V_DIM = 2 * HEAD_DIM
ATTN_WIDTH = N_HEADS * V_DIM
LRU_BLOCK = 64
LRU_CONV_WIDTH = 4
LRU_CONV_LEFT = 2
LRU_C = 8.0
FFN_CONV_LEFT = 1
NORM_EPS = 1e-6
LANES = 128
SUBLANES = 8
BF16_ROWS = 16
VMEM_LIMIT = 56 * 1024 * 1024


def _rms(x, g):
    return (x * lax.rsqrt(jnp.mean(x * x, axis=-1, keepdims=True) + NORM_EPS)) * g


def _dot(a, b):
    return jnp.dot(a, b, preferred_element_type=F32)


_NT = (((1,), (1,)), ((), ()))


def _inproj_kernel(x_ref, g_ref, w_ref, q_ref, k_ref, vt_ref, xr_ref, gr_ref, wb_ref, wvt_ref):
    aw = ATTN_WIDTH
    d = x_ref.shape[-1]

    @pl.when(pl.program_id(0) == 0)
    def _cast_weights():
        rows = 2 * LANES
        for r in range(0, d, rows):
            w = w_ref[r:r + rows, :]
            wb_ref[r:r + rows, 0:aw] = (w[:, 0:aw] * (HEAD_DIM ** -0.5)).astype(BF16)
            wb_ref[r:r + rows, aw:] = w[:, aw:].astype(BF16)
        for c in range(aw // LANES):
            cols = slice(2 * aw + c * LANES, 2 * aw + (c + 1) * LANES)
            wvt_ref[c * LANES:(c + 1) * LANES, :] = w_ref[:, cols].T.astype(BF16)

    hb = _rms(x_ref[...], g_ref[...]).astype(BF16)
    q_ref[...] = _dot(hb, wb_ref[:, 0:aw]).astype(BF16)
    k_ref[...] = _dot(hb, wb_ref[:, aw:2 * aw]).astype(BF16)
    vt_ref[...] = lax.dot_general(wvt_ref[...], hb, _NT, preferred_element_type=F32).astype(BF16)
    lw = xr_ref.shape[-1]
    xr_ref[...] = _dot(hb, wb_ref[:, 3 * aw:3 * aw + lw])
    gr_ref[...] = _dot(hb, wb_ref[:, 3 * aw + lw:3 * aw + 2 * lw])


def _inproj(x2, g, w, tm):
    n, d = x2.shape
    lw = (w.shape[1] - 3 * ATTN_WIDTH) // 2
    row = lambda i: (i, 0)
    const = lambda i: (0, 0)
    return pl.pallas_call(
        _inproj_kernel,
        grid=(n // tm,),
        in_specs=[pl.BlockSpec((tm, d), row),
                  pl.BlockSpec((1, d), const),
                  pl.BlockSpec(w.shape, const, pipeline_mode=pl.Buffered(1))],
        out_specs=[pl.BlockSpec((tm, ATTN_WIDTH), row)] * 2
        + [pl.BlockSpec((ATTN_WIDTH, tm), lambda i: (0, i))]
        + [pl.BlockSpec((tm, lw), row)] * 2,
        out_shape=[jax.ShapeDtypeStruct((n, ATTN_WIDTH), BF16)] * 2
        + [jax.ShapeDtypeStruct((ATTN_WIDTH, n), BF16)]
        + [jax.ShapeDtypeStruct((n, lw), F32)] * 2,
        scratch_shapes=[pltpu.VMEM(w.shape, BF16), pltpu.VMEM((ATTN_WIDTH, d), BF16)],
        compiler_params=pltpu.CompilerParams(
            dimension_semantics=("arbitrary",), vmem_limit_bytes=VMEM_LIMIT),
        name="inproj",
    )(x2, g, w)


def _attn_kernel(lq1_ref, lk1_ref, lq2_ref, lk2_ref, sg_ref, q_ref, k_ref, vt_ref, o_ref,
                 kf_ref, dist_ref, *se_refs, tq, nsub, seq, lambda_init):
    h = pl.program_id(1)
    qi = pl.program_id(2)
    s_refs, e_refs, m_refs = se_refs[:nsub], se_refs[nsub:2 * nsub], se_refs[2 * nsub:]
    nblk = seq // tq
    assert nblk & (nblk - 1) == 0 and 3 * nblk <= LANES and tq <= 256
    shift = nblk.bit_length() - 1

    @pl.when((pl.program_id(0) == 0) & (h == 0) & (qi == 0))
    def _init():
        lane = lax.broadcasted_iota(jnp.int32, (seq, LANES), 1)
        row = lax.broadcasted_iota(jnp.int32, (seq, LANES), 0)
        grp = lane >> shift
        hit = (row // tq) == (lane & (nblk - 1))
        dj = (row % tq).astype(F32)
        kf_ref[...] = jnp.where(hit & (grp < 2), 1.0,
                                jnp.where(hit & (grp == 2), dj, 0.0)).astype(BF16)
        r = lax.broadcasted_iota(jnp.int32, (tq, tq), 0)
        c = lax.broadcasted_iota(jnp.int32, (tq, tq), 1)
        dist_ref[...] = jnp.abs(r - c).astype(F32)

    lam = (jnp.exp(jnp.sum(lq1_ref[...] * lk1_ref[...], axis=-1, keepdims=True))
           - jnp.exp(jnp.sum(lq2_ref[...] * lk2_ref[...], axis=-1, keepdims=True))
           + lambda_init)
    slope = jnp.where(h == 0, 2.0 ** -2, jnp.where(h == 1, 2.0 ** -4,
                      jnp.where(h == 2, 2.0 ** -6, 2.0 ** -8))).astype(F32)

    kaug = jnp.concatenate([k_ref[...], kf_ref[...]], axis=1)
    vt_ones = jnp.concatenate([vt_ref[...], jnp.ones((BF16_ROWS, seq), BF16)], axis=0)
    lane = lax.broadcasted_iota(jnp.int32, (tq, LANES), 1)
    di = lax.broadcasted_iota(jnp.int32, (tq, LANES), 0).astype(F32)
    grp = lane >> shift
    strip = 4 * SUBLANES

    def scores(sb):
        blk = qi * nsub + sb
        diff = blk - (lane & (nblk - 1))
        sign = jnp.where(diff > 0, 1.0, jnp.where(diff < 0, -1.0, 0.0))
        qf = jnp.where(grp == 0, -slope * sign * di,
                       jnp.where(grp == 1, -slope * tq * jnp.abs(diff).astype(F32),
                                 jnp.where(grp == 2, slope * sign, 0.0))).astype(BF16)
        q = q_ref[sb * tq:(sb + 1) * tq, :]
        zero = jnp.zeros_like(q)
        qaug = jnp.concatenate(
            [jnp.concatenate([jnp.where(lane < HEAD_DIM, q, zero), qf], axis=1),
             jnp.concatenate([jnp.where(lane >= HEAD_DIM, q, zero), qf], axis=1)], axis=0)
        s_ref = s_refs[sb]
        s = lax.dot_general(kaug, qaug, _NT, preferred_element_type=F32)
        s_ref[...] = s
        mx = s[0:strip, :]
        for r in range(strip, seq, strip):
            mx = jnp.maximum(mx, s[r:r + strip, :])
        m_refs[sb][...] = mx
        diag = pl.ds(pl.multiple_of(blk * tq, tq), tq)
        diag_bias = slope * dist_ref[...]
        s_ref[diag, 0:tq] = s_ref[diag, 0:tq] - diag_bias
        s_ref[diag, tq:2 * tq] = s_ref[diag, tq:2 * tq] - diag_bias

    def finish(sb):
        s_ref, e_ref = s_refs[sb], e_refs[sb]
        mx = jnp.max(m_refs[sb][...], axis=0, keepdims=True)
        for r in range(0, seq, strip):
            e_ref[r:r + strip, :] = jnp.exp(s_ref[r:r + strip, :] - mx).astype(BF16)
        o12 = _dot(vt_ones, e_ref[...])
        norm = o12[V_DIM:V_DIM + 1, :]
        o12 = o12[0:V_DIM, :]
        o = o12[:, 0:tq] * (1.0 / norm[:, 0:tq]) - o12[:, tq:2 * tq] * (lam / norm[:, tq:2 * tq])
        o = o * lax.rsqrt(jnp.mean(o * o, axis=0, keepdims=True) + NORM_EPS)
        o = o * sg_ref[...] * (1.0 - lambda_init)
        o_ref[sb * tq:(sb + 1) * tq, :] = o.T.astype(o_ref.dtype)

    scores(0)
    for sb in range(nsub):
        if sb + 1 < nsub:
            scores(sb + 1)
        finish(sb)


def _attention(q, k, vt, lq1, lk1, lq2, lk2, subln_g, lambda_init, tq, nsub):
    b, s, _ = q.shape
    vec = lambda bi, h, qi: (0, 0)
    tstep = tq * nsub
    kernel = functools.partial(_attn_kernel, tq=tq, nsub=nsub, seq=s, lambda_init=lambda_init)
    return pl.pallas_call(
        kernel,
        grid=(b, N_HEADS, s // tstep),
        in_specs=[pl.BlockSpec((1, HEAD_DIM), vec)] * 4
        + [pl.BlockSpec((V_DIM, 1), vec),
           pl.BlockSpec((None, tstep, V_DIM), lambda bi, h, qi: (bi, qi, h)),
           pl.BlockSpec((None, s, V_DIM), lambda bi, h, qi: (bi, 0, h)),
           pl.BlockSpec((V_DIM, s), lambda bi, h, qi: (h, bi))],
        out_specs=pl.BlockSpec((None, tstep, V_DIM), lambda bi, h, qi: (bi, qi, h)),
        out_shape=jax.ShapeDtypeStruct((b, s, ATTN_WIDTH), BF16),
        scratch_shapes=[pltpu.VMEM((s, LANES), BF16), pltpu.VMEM((tq, tq), F32)]
        + [pltpu.VMEM((s, 2 * tq), F32)] * nsub + [pltpu.VMEM((s, 2 * tq), BF16)] * nsub
        + [pltpu.VMEM((4 * SUBLANES, 2 * tq), F32)] * nsub,
        compiler_params=pltpu.CompilerParams(
            dimension_semantics=("arbitrary",) * 3, vmem_limit_bytes=VMEM_LIMIT),
        name="attn",
    )(lq1, lk1, lq2, lk2, subln_g, q, k, vt)


def _local_scan(a, u, reverse):
    row = lax.broadcasted_iota(jnp.int32, a.shape, 0)
    for d in (1, 2, 4):
        shift = SUBLANES - d if reverse else d
        valid = (row < SUBLANES - d) if reverse else (row >= d)
        a_s = jnp.where(valid, pltpu.roll(a, shift, 0), 1.0)
        u_s = jnp.where(valid, pltpu.roll(u, shift, 0), 0.0)
        u = a * u_s + u
        a = a * a_s
    return a, u


def _lru_kernel(xr_ref, gr_ref, cw_ref, cb_ref, wg_ref, bg_ref, lam_ref, o_ref,
                xp_ref, xc_ref, h_ref, cum_ref, *, seq, rows):
    width = xr_ref.shape[-1]
    ntile = width // LANES
    grp = seq // SUBLANES
    left = LRU_CONV_LEFT
    right = LRU_CONV_WIDTH - 1 - LRU_CONV_LEFT
    top = left * SUBLANES
    sub = lax.broadcasted_iota(jnp.int32, (SUBLANES, LANES), 0)

    for s in range(SUBLANES):
        blk = xr_ref[s * grp:(s + 1) * grp, :]
        for p in range(ntile):
            xp_ref[p, pl.ds(top + s, grp, stride=SUBLANES), :] = blk[:, p * LANES:(p + 1) * LANES]
    for p in range(ntile):
        for k in range(left):
            src = xp_ref[p, top + (grp - 1 - k) * SUBLANES:top + (grp - k) * SUBLANES, :]
            xp_ref[p, top - (k + 1) * SUBLANES:top - k * SUBLANES, :] = jnp.where(
                sub == 0, 0.0, pltpu.roll(src, 1, 0))
        for k in range(right):
            src = xp_ref[p, top + k * SUBLANES:top + (k + 1) * SUBLANES, :]
            xp_ref[p, top + (grp + k) * SUBLANES:top + (grp + k + 1) * SUBLANES, :] = jnp.where(
                sub == SUBLANES - 1, 0.0, pltpu.roll(src, SUBLANES - 1, 0))

    neg_lam = -lam_ref[...]
    softplus = jnp.maximum(neg_lam, 0.0) + jnp.log1p(jnp.exp(-jnp.abs(neg_lam)))
    rate = LRU_C * softplus
    rate_log2 = -rate * math.log2(math.e)

    nchunk = seq // rows
    for c in range(nchunk):
        r0 = c * rows
        for p in range(ntile):
            cols = slice(p * LANES, (p + 1) * LANES)
            xc = cb_ref[:, cols]
            for tap in range(LRU_CONV_WIDTH):
                start = top + r0 + (tap - left) * SUBLANES
                xc = xc + xp_ref[p, start:start + rows, :] * cw_ref[tap:tap + 1, cols]
            xc_ref[p, r0:r0 + rows, :] = xc

    zero = jnp.zeros((SUBLANES, LANES), F32)
    one = jnp.ones((SUBLANES, LANES), F32)
    ends = [[(zero, one), (zero, one)] for _ in range(ntile)]
    for c in range(nchunk):
        for p in range(ntile):
            cols = slice(p * LANES, (p + 1) * LANES)
            for d in range(2):
                r0 = (c if d == 0 else nchunk - 1 - c) * rows
                gcols = slice(2 * d * LANES, (2 * d + 2) * LANES)
                xc = xc_ref[p, r0:r0 + rows, :]
                gates = _dot(xc.astype(BF16), wg_ref[p, :, gcols]) + bg_ref[p:p + 1, gcols]
                r = jax.nn.sigmoid(gates[:, 0:LANES])
                i = jax.nn.sigmoid(gates[:, LANES:2 * LANES])
                t = jnp.tanh(r * rate[d:d + 1, cols])
                tt = t + t
                prod = tt * (1.0 + t)
                mult = jnp.where(prod > 0.0, tt * lax.rsqrt(prod), 0.0)
                a = jnp.exp2(r * rate_log2[d:d + 1, cols])
                u = mult * (i * xc)
                h, cum = ends[p][d]
                vrows = range(rows // SUBLANES)
                for j in (vrows if d == 0 else reversed(vrows)):
                    rs = slice(j * SUBLANES, (j + 1) * SUBLANES)
                    h = a[rs] * h + u[rs]
                    cum = a[rs] * cum
                    h_ref[d, p, r0 + j * SUBLANES:r0 + (j + 1) * SUBLANES, :] = h
                    cum_ref[d, p, r0 + j * SUBLANES:r0 + (j + 1) * SUBLANES, :] = cum
                ends[p][d] = (h, cum)

    enter = []
    for p in range(ntile):
        for d in range(2):
            h_end, cum_end = ends[p][d]
            _, chained = _local_scan(cum_end, h_end, reverse=(d == 1))
            if d == 0:
                enter.append(jnp.where(sub == 0, 0.0, pltpu.roll(chained, 1, 0)))
            else:
                enter.append(jnp.where(sub == SUBLANES - 1, 0.0,
                                       pltpu.roll(chained, SUBLANES - 1, 0)))

    for c in range(seq // rows):
        rs = slice(c * rows, (c + 1) * rows)
        for p in range(ntile):
            y = None
            for d in range(2):
                init = jnp.tile(enter[2 * p + d], (rows // SUBLANES, 1))
                part = h_ref[d, p, rs, :] + cum_ref[d, p, rs, :] * init
                y = part if y is None else y + part
            xc_ref[p, rs, :] = y

    for s in range(SUBLANES):
        rs = slice(s * grp, (s + 1) * grp)
        for p in range(ntile):
            cols = slice(p * LANES, (p + 1) * LANES)
            y = xc_ref[p, pl.ds(s, grp, stride=SUBLANES), :]
            o_ref[rs, cols] = (jax.nn.gelu(gr_ref[rs, cols], approximate=True) * y).astype(o_ref.dtype)


def _lru(xr, gr, conv_w, conv_b, wg, bg, lru_lambda, rows):
    b, s, width = xr.shape
    ntile = width // LANES
    seqblk = pl.BlockSpec((None, s, width), lambda bi: (bi, 0, 0))
    full = lambda a: pl.BlockSpec(a.shape, lambda bi: (0,) * a.ndim)
    kernel = functools.partial(_lru_kernel, seq=s, rows=rows)
    halo_rows = (LRU_CONV_WIDTH - 1) * SUBLANES
    return pl.pallas_call(
        kernel,
        grid=(b,),
        in_specs=[seqblk, seqblk, full(conv_w), full(conv_b), full(wg), full(bg), full(lru_lambda)],
        out_specs=seqblk,
        out_shape=jax.ShapeDtypeStruct((b, s, width), BF16),
        scratch_shapes=[pltpu.VMEM((ntile, s + halo_rows, LANES), F32),
                        pltpu.VMEM((ntile, s, LANES), F32),
                        pltpu.VMEM((2, ntile, s, LANES), F32),
                        pltpu.VMEM((2, ntile, s, LANES), F32)],
        compiler_params=pltpu.CompilerParams(
            dimension_semantics=("arbitrary",), vmem_limit_bytes=VMEM_LIMIT),
        name="lru",
    )(xr, gr, conv_w, conv_b, wg, bg, lru_lambda)


def _outproj_kernel(x_ref, a_ref, l_ref, w_ref, g_ref, xm_ref, h_ref, wb_ref):
    @pl.when(pl.program_id(0) == 0)
    def _cast_weights():
        wb_ref[...] = w_ref[...].astype(BF16)

    aw = a_ref.shape[-1]
    xm = x_ref[...] + _dot(a_ref[...], wb_ref[0:aw, :]) + _dot(l_ref[...], wb_ref[aw:, :])
    xm_ref[...] = xm
    h_ref[...] = _rms(xm, g_ref[...]).astype(h_ref.dtype)


def _outproj(x2, attn2, lru2, w, g, tm):
    n, d = x2.shape
    row = lambda i: (i, 0)
    const = lambda i: (0, 0)
    return pl.pallas_call(
        _outproj_kernel,
        grid=(n // tm,),
        in_specs=[pl.BlockSpec((tm, d), row),
                  pl.BlockSpec((tm, attn2.shape[1]), row),
                  pl.BlockSpec((tm, lru2.shape[1]), row),
                  pl.BlockSpec(w.shape, const, pipeline_mode=pl.Buffered(1)),
                  pl.BlockSpec((1, d), const)],
        out_specs=[pl.BlockSpec((tm, d), row)] * 2,
        out_shape=[jax.ShapeDtypeStruct((n, d), F32), jax.ShapeDtypeStruct((n, d), BF16)],
        scratch_shapes=[pltpu.VMEM(w.shape, BF16)],
        compiler_params=pltpu.CompilerParams(
            dimension_semantics=("arbitrary",), vmem_limit_bytes=VMEM_LIMIT),
        name="outproj",
    )(x2, attn2, lru2, w, g)


def _ffn_kernel(hp_ref, hm_ref, hn_ref, xm_ref, wup_ref, cw_ref, cb_ref, wdn_ref, fg_ref, o_ref,
                perm_ref, hext_ref, ua_ref, ub_ref, acta_ref, actb_ref, acc_ref,
                *, tile, chunk, d_ff):
    i = pl.program_id(1)
    nchunks = d_ff // chunk
    ngrp = tile // SUBLANES
    nslab = perm_ref.shape[0]
    d = nslab * LANES

    for s in range(SUBLANES):
        rows = hm_ref[s * ngrp:(s + 1) * ngrp, :].astype(F32)
        for n in range(nslab):
            perm_ref[n, pl.ds(s, ngrp, stride=SUBLANES), :] = rows[:, n * LANES:(n + 1) * LANES]
    for n in range(nslab):
        hext_ref[0:tile, n * LANES:(n + 1) * LANES] = perm_ref[n].astype(BF16)
    prev = jnp.where(i > 0, hp_ref[BF16_ROWS - 1:BF16_ROWS, :].astype(F32), 0.0)
    nxt = jnp.where(i < pl.num_programs(1) - 1, hn_ref[0:1, :].astype(F32), 0.0)
    hrow = lax.broadcasted_iota(jnp.int32, (BF16_ROWS, d), 0)
    halo = jnp.where(hrow == 0, prev, jnp.where(hrow == 1, nxt, 0.0))
    hext_ref[tile:tile + BF16_ROWS, :] = halo.astype(BF16)
    acc_ref[...] = jnp.zeros_like(acc_ref)

    def offsets(c):
        og, ov = c * chunk, d_ff + c * chunk
        if isinstance(c, int):
            return og, ov
        return pl.multiple_of(og, chunk), pl.multiple_of(ov, chunk)

    def up(c, u_ref):
        og, ov = offsets(c)
        hext = hext_ref[...]
        u_ref[:, 0:chunk] = _dot(hext, wup_ref[:, pl.ds(og, chunk)])
        u_ref[:, chunk:2 * chunk] = _dot(hext, wup_ref[:, pl.ds(ov, chunk)])

    def glu(c, u_ref, act_ref):
        sub = lax.broadcasted_iota(jnp.int32, (SUBLANES, chunk), 0)
        blk = BF16_ROWS

        def conv(col0, off, r0):
            cols = slice(col0, col0 + chunk)
            reps = (blk // SUBLANES, 1)
            w = [jnp.tile(cw_ref[k * SUBLANES:(k + 1) * SUBLANES, pl.ds(off, chunk)], reps)
                 for k in range(3)]
            bias = jnp.tile(cb_ref[:, pl.ds(off, chunk)], reps)
            cur = u_ref[r0:r0 + blk, cols]
            if r0 == 0:
                first = jnp.where(sub == 0, u_ref[tile:tile + 1, cols],
                                  pltpu.roll(u_ref[tile - SUBLANES:tile, cols], 1, 0))
                um1 = jnp.concatenate([first, cur[0:blk - SUBLANES]], axis=0)
            else:
                um1 = u_ref[r0 - SUBLANES:r0 + blk - SUBLANES, cols]
            if r0 + blk == tile:
                last = jnp.where(sub == SUBLANES - 1, u_ref[tile + 1:tile + 2, cols],
                                 pltpu.roll(u_ref[0:SUBLANES, cols], SUBLANES - 1, 0))
                up1 = jnp.concatenate([cur[SUBLANES:blk], last], axis=0)
            else:
                up1 = u_ref[r0 + SUBLANES:r0 + blk + SUBLANES, cols]
            return bias + um1 * w[0] + cur * w[1] + up1 * w[2]

        og, ov = offsets(c)
        for r0 in range(0, tile, blk):
            gate = conv(0, og, r0)
            val = conv(chunk, ov, r0)
            act_ref[r0:r0 + blk, :] = (jax.nn.gelu(gate, approximate=True) * val).astype(BF16)

    def down(c, act_ref):
        og, _ = offsets(c)
        y = _dot(act_ref[...], wdn_ref[pl.ds(og, chunk), :])
        for n in range(nslab):
            acc_ref[n, 0:tile, :] += y[:, n * LANES:(n + 1) * LANES]

    assert nchunks % 2 == 1 and nchunks >= 3
    up(0, ua_ref)

    def pair(j, carry):
        c = 2 * j
        up(c + 1, ub_ref)
        glu(c, ua_ref, acta_ref)
        down(c, acta_ref)
        up(c + 2, ua_ref)
        glu(c + 1, ub_ref, actb_ref)
        down(c + 1, actb_ref)
        return carry

    for j in range(nchunks // 2):
        pair(j, 0)
    glu(nchunks - 1, ua_ref, acta_ref)
    down(nchunks - 1, acta_ref)
    for s in range(SUBLANES):
        rs = slice(s * ngrp, (s + 1) * ngrp)
        y = jnp.concatenate([acc_ref[n, pl.ds(s, ngrp, stride=SUBLANES), :] for n in range(nslab)],
                            axis=1)
        o_ref[rs, :] = _rms(xm_ref[rs, :] + y, fg_ref[...])


def _ffn(h2, xm, w_up, conv_w, conv_b, w_down, final_g, tile, chunk):
    b, s, d = h2.shape
    d_ff = w_down.shape[0]
    nh = tile // BF16_ROWS
    last_halo = s // BF16_ROWS - 1
    main = lambda bi, i: (bi, i, 0)
    const = lambda bi, i: (0, 0)
    single = dict(pipeline_mode=pl.Buffered(1))
    kernel = functools.partial(_ffn_kernel, tile=tile, chunk=chunk, d_ff=d_ff)
    return pl.pallas_call(
        kernel,
        grid=(b, s // tile),
        in_specs=[pl.BlockSpec((None, BF16_ROWS, d), lambda bi, i: (bi, jnp.maximum(i * nh - 1, 0), 0)),
                  pl.BlockSpec((None, tile, d), main),
                  pl.BlockSpec((None, BF16_ROWS, d),
                               lambda bi, i: (bi, jnp.minimum((i + 1) * nh, last_halo), 0)),
                  pl.BlockSpec((None, tile, d), main),
                  pl.BlockSpec(w_up.shape, const, **single),
                  pl.BlockSpec(conv_w.shape, const),
                  pl.BlockSpec(conv_b.shape, const),
                  pl.BlockSpec(w_down.shape, const, **single),
                  pl.BlockSpec((1, d), const)],
        out_specs=pl.BlockSpec((None, tile, d), main),
        out_shape=jax.ShapeDtypeStruct((b, s, d), F32),
        scratch_shapes=[pltpu.VMEM((d // LANES, tile, LANES), F32),
                        pltpu.VMEM((tile + BF16_ROWS, d), BF16)]
        + [pltpu.VMEM((tile + BF16_ROWS, 2 * chunk), F32)] * 2
        + [pltpu.VMEM((tile, chunk), BF16)] * 2
        + [pltpu.VMEM((d // LANES, tile + SUBLANES, LANES), F32)],
        compiler_params=pltpu.CompilerParams(
            dimension_semantics=("arbitrary",) * 2, vmem_limit_bytes=VMEM_LIMIT),
        name="ffn",
    )(h2, h2, h2, xm, w_up, conv_w, conv_b, w_down, final_g)


def _gate_weights(w_a, b_a, w_x, b_x):
    ndir, nblk, bd, _ = w_a.shape
    per = LANES // bd
    npair = nblk // per

    def blockdiag(w):
        w = w.reshape(npair, per, bd, bd)
        eye = jnp.eye(per, dtype=w.dtype)
        return jnp.einsum('pbij,bc->pbicj', w, eye).reshape(npair, LANES, LANES)

    ws, bs = [], []
    for d in range(ndir):
        for w, bias in ((w_a, b_a), (w_x, b_x)):
            ws.append(blockdiag(w[d]))
            bs.append(bias[d].reshape(npair, LANES))
    return jnp.concatenate(ws, axis=-1).astype(BF16), jnp.concatenate(bs, axis=-1).astype(F32)


def kernel(x, attn_norm_g, w_in, lambda_q1, lambda_k1, lambda_q2, lambda_k2, subln_g,
           lru_conv_w, lru_conv_b, lru_w_a, lru_b_a, lru_w_x, lru_b_x, lru_lambda,
           w_out, ffn_norm_g, w_up, ffn_conv_w, ffn_conv_b, w_down, final_norm_g):
    b, s, d = x.shape
    depth = w_in.shape[0]
    x2 = x.reshape(b * s, d)
    for l in range(depth):
        lambda_init = 0.8 - 0.6 * math.exp(-0.3 * l)
        q, k, vt, xr, gr = _inproj(x2, attn_norm_g[l][None], w_in[l], tm=512)
        lw = xr.shape[-1]
        attn = _attention(q.reshape(b, s, -1), k.reshape(b, s, -1), vt,
                          lambda_q1[l][None], lambda_k1[l][None], lambda_q2[l][None],
                          lambda_k2[l][None], subln_g[l][:, None], lambda_init, tq=256, nsub=4)
        wg, bg = _gate_weights(lru_w_a[l], lru_b_a[l], lru_w_x[l], lru_b_x[l])
        lru = _lru(xr.reshape(b, s, lw), gr.reshape(b, s, lw), lru_conv_w[l], lru_conv_b[l][None],
                   wg, bg, lru_lambda[l], rows=256)
        xm, h2 = _outproj(x2, attn.reshape(b * s, -1), lru.reshape(b * s, -1),
                          w_out[l], ffn_norm_g[l][None], tm=1024)
        assert depth == 1
        x2 = _ffn(h2.reshape(b, s, d), xm.reshape(b, s, d), w_up[l].astype(BF16),
                  jnp.repeat(ffn_conv_w[l], SUBLANES, axis=0),
                  jnp.broadcast_to(ffn_conv_b[l][None], (SUBLANES, ffn_conv_b.shape[-1])),
                  w_down[l].astype(BF16), final_norm_g[None],
                  tile=512, chunk=256).reshape(b * s, d)
    return x2.reshape(b, s, d)
```

```python
import functools
import math

import jax
import jax.numpy as jnp
from jax import lax
from jax.experimental import pallas as pl
from jax.experimental.pallas import tpu as pltpu

F32 = jnp.float32
BF16 = jnp.bfloat16

N_HEADS = 4
HEAD_DIM = 64
V_DIM = 2 * HEAD_DIM
ATTN_WIDTH = N_HEADS * V_DIM
LRU_BLOCK = 64
LRU_CONV_WIDTH = 4
LRU_CONV_LEFT = 2
LRU_C = 8.0
FFN_CONV_LEFT = 1
NORM_EPS = 1e-6
LANES = 128
SUBLANES = 8
BF16_ROWS = 16
VMEM_LIMIT = 56 * 1024 * 1024


def _rms(x, g):
    return (x * lax.rsqrt(jnp.mean(x * x, axis=-1, keepdims=True) + NORM_EPS)) * g


def _dot(a, b):
    return jnp.dot(a, b, preferred_element_type=F32)


_NT = (((1,), (1,)), ((), ()))


def _inproj_kernel(x_ref, g_ref, w_ref, q_ref, k_ref, vt_ref, xr_ref, gr_ref, wb_ref, wvt_ref):
    aw = ATTN_WIDTH
    d = x_ref.shape[-1]

    @pl.when(pl.program_id(0) == 0)
    def _cast_weights():
        rows = 2 * LANES
        for r in range(0, d, rows):
            w = w_ref[r:r + rows, :]
            wb_ref[r:r + rows, 0:aw] = (w[:, 0:aw] * (HEAD_DIM ** -0.5)).astype(BF16)
            wb_ref[r:r + rows, aw:] = w[:, aw:].astype(BF16)
        for c in range(aw // LANES):
            cols = slice(2 * aw + c * LANES, 2 * aw + (c + 1) * LANES)
            wvt_ref[c * LANES:(c + 1) * LANES, :] = w_ref[:, cols].T.astype(BF16)

    hb = _rms(x_ref[...], g_ref[...]).astype(BF16)
    q_ref[...] = _dot(hb, wb_ref[:, 0:aw]).astype(BF16)
    k_ref[...] = _dot(hb, wb_ref[:, aw:2 * aw]).astype(BF16)
    vt_ref[...] = lax.dot_general(wvt_ref[...], hb, _NT, preferred_element_type=F32).astype(BF16)
    lw = xr_ref.shape[-1]
    xr_ref[...] = _dot(hb, wb_ref[:, 3 * aw:3 * aw + lw])
    gr_ref[...] = _dot(hb, wb_ref[:, 3 * aw + lw:3 * aw + 2 * lw])


def _inproj(x2, g, w, tm):
    n, d = x2.shape
    lw = (w.shape[1] - 3 * ATTN_WIDTH) // 2
    row = lambda i: (i, 0)
    const = lambda i: (0, 0)
    return pl.pallas_call(
        _inproj_kernel,
        grid=(n // tm,),
        in_specs=[pl.BlockSpec((tm, d), row),
                  pl.BlockSpec((1, d), const),
                  pl.BlockSpec(w.shape, const, pipeline_mode=pl.Buffered(1))],
        out_specs=[pl.BlockSpec((tm, ATTN_WIDTH), row)] * 2
        + [pl.BlockSpec((ATTN_WIDTH, tm), lambda i: (0, i))]
        + [pl.BlockSpec((tm, lw), row)] * 2,
        out_shape=[jax.ShapeDtypeStruct((n, ATTN_WIDTH), BF16)] * 2
        + [jax.ShapeDtypeStruct((ATTN_WIDTH, n), BF16)]
        + [jax.ShapeDtypeStruct((n, lw), F32)] * 2,
        scratch_shapes=[pltpu.VMEM(w.shape, BF16), pltpu.VMEM((ATTN_WIDTH, d), BF16)],
        compiler_params=pltpu.CompilerParams(
            dimension_semantics=("arbitrary",), vmem_limit_bytes=VMEM_LIMIT),
        name="inproj",
    )(x2, g, w)


def _attn_kernel(lq1_ref, lk1_ref, lq2_ref, lk2_ref, sg_ref, q_ref, k_ref, vt_ref, o_ref,
                 kf_ref, dist_ref, *se_refs, tq, nsub, seq, lambda_init):
    h = pl.program_id(1)
    qi = pl.program_id(2)
    s_refs, e_refs, m_refs = se_refs[:nsub], se_refs[nsub:2 * nsub], se_refs[2 * nsub:]
    nblk = seq // tq
    assert nblk & (nblk - 1) == 0 and 3 * nblk <= LANES and tq <= 256
    shift = nblk.bit_length() - 1

    @pl.when((pl.program_id(0) == 0) & (h == 0) & (qi == 0))
    def _init():
        lane = lax.broadcasted_iota(jnp.int32, (seq, LANES), 1)
        row = lax.broadcasted_iota(jnp.int32, (seq, LANES), 0)
        grp = lane >> shift
        hit = (row // tq) == (lane & (nblk - 1))
        dj = (row % tq).astype(F32)
        kf_ref[...] = jnp.where(hit & (grp < 2), 1.0,
                                jnp.where(hit & (grp == 2), dj, 0.0)).astype(BF16)
        r = lax.broadcasted_iota(jnp.int32, (tq, tq), 0)
        c = lax.broadcasted_iota(jnp.int32, (tq, tq), 1)
        dist_ref[...] = jnp.abs(r - c).astype(F32)

    lam = (jnp.exp(jnp.sum(lq1_ref[...] * lk1_ref[...], axis=-1, keepdims=True))
           - jnp.exp(jnp.sum(lq2_ref[...] * lk2_ref[...], axis=-1, keepdims=True))
           + lambda_init)
    slope = jnp.where(h == 0, 2.0 ** -2, jnp.where(h == 1, 2.0 ** -4,
                      jnp.where(h == 2, 2.0 ** -6, 2.0 ** -8))).astype(F32)

    kaug = jnp.concatenate([k_ref[...], kf_ref[...]], axis=1)
    vt_ones = jnp.concatenate([vt_ref[...], jnp.ones((BF16_ROWS, seq), BF16)], axis=0)
    lane = lax.broadcasted_iota(jnp.int32, (tq, LANES), 1)
    di = lax.broadcasted_iota(jnp.int32, (tq, LANES), 0).astype(F32)
    grp = lane >> shift
    strip = 4 * SUBLANES

    def scores(sb):
        blk = qi * nsub + sb
        diff = blk - (lane & (nblk - 1))
        sign = jnp.where(diff > 0, 1.0, jnp.where(diff < 0, -1.0, 0.0))
        qf = jnp.where(grp == 0, -slope * sign * di,
                       jnp.where(grp == 1, -slope * tq * jnp.abs(diff).astype(F32),
                                 jnp.where(grp == 2, slope * sign, 0.0))).astype(BF16)
        q = q_ref[sb * tq:(sb + 1) * tq, :]
        zero = jnp.zeros_like(q)
        qaug = jnp.concatenate(
            [jnp.concatenate([jnp.where(lane < HEAD_DIM, q, zero), qf], axis=1),
             jnp.concatenate([jnp.where(lane >= HEAD_DIM, q, zero), qf], axis=1)], axis=0)
        s_ref = s_refs[sb]
        s = lax.dot_general(kaug, qaug, _NT, preferred_element_type=F32)
        s_ref[...] = s
        mx = s[0:strip, :]
        for r in range(strip, seq, strip):
            mx = jnp.maximum(mx, s[r:r + strip, :])
        m_refs[sb][...] = mx
        diag = pl.ds(pl.multiple_of(blk * tq, tq), tq)
        diag_bias = slope * dist_ref[...]
        s_ref[diag, 0:tq] = s_ref[diag, 0:tq] - diag_bias
        s_ref[diag, tq:2 * tq] = s_ref[diag, tq:2 * tq] - diag_bias

    def finish(sb):
        s_ref, e_ref = s_refs[sb], e_refs[sb]
        mx = jnp.max(m_refs[sb][...], axis=0, keepdims=True)
        for r in range(0, seq, strip):
            e_ref[r:r + strip, :] = jnp.exp(s_ref[r:r + strip, :] - mx).astype(BF16)
        o12 = _dot(vt_ones, e_ref[...])
        norm = o12[V_DIM:V_DIM + 1, :]
        o12 = o12[0:V_DIM, :]
        o = o12[:, 0:tq] * (1.0 / norm[:, 0:tq]) - o12[:, tq:2 * tq] * (lam / norm[:, tq:2 * tq])
        o = o * lax.rsqrt(jnp.mean(o * o, axis=0, keepdims=True) + NORM_EPS)
        o = o * sg_ref[...] * (1.0 - lambda_init)
        o_ref[sb * tq:(sb + 1) * tq, :] = o.T.astype(o_ref.dtype)

    scores(0)
    for sb in range(nsub):
        if sb + 1 < nsub:
            scores(sb + 1)
        finish(sb)


def _attention(q, k, vt, lq1, lk1, lq2, lk2, subln_g, lambda_init, tq, nsub):
    b, s, _ = q.shape
    vec = lambda bi, h, qi: (0, 0)
    tstep = tq * nsub
    kernel = functools.partial(_attn_kernel, tq=tq, nsub=nsub, seq=s, lambda_init=lambda_init)
    return pl.pallas_call(
        kernel,
        grid=(b, N_HEADS, s // tstep),
        in_specs=[pl.BlockSpec((1, HEAD_DIM), vec)] * 4
        + [pl.BlockSpec((V_DIM, 1), vec),
           pl.BlockSpec((None, tstep, V_DIM), lambda bi, h, qi: (bi, qi, h)),
           pl.BlockSpec((None, s, V_DIM), lambda bi, h, qi: (bi, 0, h)),
           pl.BlockSpec((V_DIM, s), lambda bi, h, qi: (h, bi))],
        out_specs=pl.BlockSpec((None, tstep, V_DIM), lambda bi, h, qi: (bi, qi, h)),
        out_shape=jax.ShapeDtypeStruct((b, s, ATTN_WIDTH), BF16),
        scratch_shapes=[pltpu.VMEM((s, LANES), BF16), pltpu.VMEM((tq, tq), F32)]
        + [pltpu.VMEM((s, 2 * tq), F32)] * nsub + [pltpu.VMEM((s, 2 * tq), BF16)] * nsub
        + [pltpu.VMEM((4 * SUBLANES, 2 * tq), F32)] * nsub,
        compiler_params=pltpu.CompilerParams(
            dimension_semantics=("arbitrary",) * 3, vmem_limit_bytes=VMEM_LIMIT),
        name="attn",
    )(lq1, lk1, lq2, lk2, subln_g, q, k, vt)


def _local_scan(a, u, reverse):
    row = lax.broadcasted_iota(jnp.int32, a.shape, 0)
    for d in (1, 2, 4):
        shift = SUBLANES - d if reverse else d
        valid = (row < SUBLANES - d) if reverse else (row >= d)
        a_s = jnp.where(valid, pltpu.roll(a, shift, 0), 1.0)
        u_s = jnp.where(valid, pltpu.roll(u, shift, 0), 0.0)
        u = a * u_s + u
        a = a * a_s
    return a, u


def _lru_kernel(xr_ref, gr_ref, cw_ref, cb_ref, wg_ref, bg_ref, lam_ref, o_ref,
                xp_ref, xc_ref, h_ref, cum_ref, *, seq, rows):
    width = xr_ref.shape[-1]
    ntile = width // LANES
    grp = seq // SUBLANES
    left = LRU_CONV_LEFT
    right = LRU_CONV_WIDTH - 1 - LRU_CONV_LEFT
    top = left * SUBLANES
    sub = lax.broadcasted_iota(jnp.int32, (SUBLANES, LANES), 0)

    for s in range(SUBLANES):
        blk = xr_ref[s * grp:(s + 1) * grp, :]
        for p in range(ntile):
            xp_ref[p, pl.ds(top + s, grp, stride=SUBLANES), :] = blk[:, p * LANES:(p + 1) * LANES]
    for p in range(ntile):
        for k in range(left):
            src = xp_ref[p, top + (grp - 1 - k) * SUBLANES:top + (grp - k) * SUBLANES, :]
            xp_ref[p, top - (k + 1) * SUBLANES:top - k * SUBLANES, :] = jnp.where(
                sub == 0, 0.0, pltpu.roll(src, 1, 0))
        for k in range(right):
            src = xp_ref[p, top + k * SUBLANES:top + (k + 1) * SUBLANES, :]
            xp_ref[p, top + (grp + k) * SUBLANES:top + (grp + k + 1) * SUBLANES, :] = jnp.where(
                sub == SUBLANES - 1, 0.0, pltpu.roll(src, SUBLANES - 1, 0))

    neg_lam = -lam_ref[...]
    softplus = jnp.maximum(neg_lam, 0.0) + jnp.log1p(jnp.exp(-jnp.abs(neg_lam)))
    rate = LRU_C * softplus
    rate_log2 = -rate * math.log2(math.e)

    nchunk = seq // rows
    for c in range(nchunk):
        r0 = c * rows
        for p in range(ntile):
            cols = slice(p * LANES, (p + 1) * LANES)
            reps = (rows // SUBLANES, 1)
            xc = jnp.tile(cb_ref[:, cols], reps)
            for tap in range(LRU_CONV_WIDTH):
                start = top + r0 + (tap - left) * SUBLANES
                w = jnp.tile(cw_ref[tap * SUBLANES:(tap + 1) * SUBLANES, cols], reps)
                xc = xc + xp_ref[p, start:start + rows, :] * w
            xc_ref[p, r0:r0 + rows, :] = xc

    zero = jnp.zeros((SUBLANES, LANES), F32)
    one = jnp.ones((SUBLANES, LANES), F32)
    ends = [[(zero, one), (zero, one)] for _ in range(ntile)]
    nvr = rows // SUBLANES
    for c in range(nchunk):
        decay, drive = {}, {}
        for p in range(ntile):
            cols = slice(p * LANES, (p + 1) * LANES)
            for d in range(2):
                r0 = (c if d == 0 else nchunk - 1 - c) * rows
                gcols = slice(2 * d * LANES, (2 * d + 2) * LANES)
                xc = xc_ref[p, r0:r0 + rows, :]
                th = jnp.tanh(_dot(xc.astype(BF16), wg_ref[p, :, gcols]) + bg_ref[p:p + 1, gcols])
                r = 0.5 + 0.5 * th[:, 0:LANES]
                i = 0.5 + 0.5 * th[:, LANES:2 * LANES]
                t = jnp.tanh(r * rate[d:d + 1, cols])
                mult = jnp.exp2((jnp.log(t) - jnp.log(1.0 + t)) * (0.5 * math.log2(math.e)) + 0.5)
                decay[p, d] = jnp.exp2(r * rate_log2[d:d + 1, cols])
                drive[p, d] = mult * (i * xc)
        for step in range(nvr):
            for p in range(ntile):
                for d in range(2):
                    j = step if d == 0 else nvr - 1 - step
                    r0 = (c if d == 0 else nchunk - 1 - c) * rows + j * SUBLANES
                    rs = slice(j * SUBLANES, (j + 1) * SUBLANES)
                    h, cum = ends[p][d]
                    h = decay[p, d][rs] * h + drive[p, d][rs]
                    cum = decay[p, d][rs] * cum
                    h_ref[d, p, r0:r0 + SUBLANES, :] = h
                    cum_ref[d, p, r0:r0 + SUBLANES, :] = cum
                    ends[p][d] = (h, cum)

    enter = []
    for p in range(ntile):
        for d in range(2):
            h_end, cum_end = ends[p][d]
            _, chained = _local_scan(cum_end, h_end, reverse=(d == 1))
            if d == 0:
                enter.append(jnp.where(sub == 0, 0.0, pltpu.roll(chained, 1, 0)))
            else:
                enter.append(jnp.where(sub == SUBLANES - 1, 0.0,
                                       pltpu.roll(chained, SUBLANES - 1, 0)))

    for c in range(seq // rows):
        rs = slice(c * rows, (c + 1) * rows)
        for p in range(ntile):
            y = None
            for d in range(2):
                init = jnp.tile(enter[2 * p + d], (rows // SUBLANES, 1))
                part = h_ref[d, p, rs, :] + cum_ref[d, p, rs, :] * init
                y = part if y is None else y + part
            xc_ref[p, rs, :] = y

    for s in range(SUBLANES):
        rs = slice(s * grp, (s + 1) * grp)
        for p in range(ntile):
            cols = slice(p * LANES, (p + 1) * LANES)
            y = xc_ref[p, pl.ds(s, grp, stride=SUBLANES), :]
            o_ref[rs, cols] = (jax.nn.gelu(gr_ref[rs, cols], approximate=True) * y).astype(o_ref.dtype)


def _lru(xr, gr, conv_w, conv_b, wg, bg, lru_lambda, rows):
    b, s, width = xr.shape
    ntile = width // LANES
    seqblk = pl.BlockSpec((None, s, width), lambda bi: (bi, 0, 0))
    full = lambda a: pl.BlockSpec(a.shape, lambda bi: (0,) * a.ndim)
    kernel = functools.partial(_lru_kernel, seq=s, rows=rows)
    halo_rows = (LRU_CONV_WIDTH - 1) * SUBLANES
    return pl.pallas_call(
        kernel,
        grid=(b,),
        in_specs=[seqblk, seqblk, full(conv_w), full(conv_b), full(wg), full(bg), full(lru_lambda)],
        out_specs=seqblk,
        out_shape=jax.ShapeDtypeStruct((b, s, width), BF16),
        scratch_shapes=[pltpu.VMEM((ntile, s + halo_rows, LANES), F32),
                        pltpu.VMEM((ntile, s, LANES), F32),
                        pltpu.VMEM((2, ntile, s, LANES), F32),
                        pltpu.VMEM((2, ntile, s, LANES), F32)],
        compiler_params=pltpu.CompilerParams(
            dimension_semantics=("arbitrary",), vmem_limit_bytes=VMEM_LIMIT),
        name="lru",
    )(xr, gr, conv_w, conv_b, wg, bg, lru_lambda)


def _outproj_kernel(x_ref, a_ref, l_ref, w_ref, g_ref, xm_ref, h_ref, wb_ref):
    @pl.when(pl.program_id(0) == 0)
    def _cast_weights():
        wb_ref[...] = w_ref[...].astype(BF16)

    aw = a_ref.shape[-1]
    xm = x_ref[...] + _dot(a_ref[...], wb_ref[0:aw, :]) + _dot(l_ref[...], wb_ref[aw:, :])
    xm_ref[...] = xm
    h_ref[...] = _rms(xm, g_ref[...]).astype(h_ref.dtype)


def _outproj(x2, attn2, lru2, w, g, tm):
    n, d = x2.shape
    row = lambda i: (i, 0)
    const = lambda i: (0, 0)
    return pl.pallas_call(
        _outproj_kernel,
        grid=(n // tm,),
        in_specs=[pl.BlockSpec((tm, d), row),
                  pl.BlockSpec((tm, attn2.shape[1]), row),
                  pl.BlockSpec((tm, lru2.shape[1]), row),
                  pl.BlockSpec(w.shape, const, pipeline_mode=pl.Buffered(1)),
                  pl.BlockSpec((1, d), const)],
        out_specs=[pl.BlockSpec((tm, d), row)] * 2,
        out_shape=[jax.ShapeDtypeStruct((n, d), F32), jax.ShapeDtypeStruct((n, d), BF16)],
        scratch_shapes=[pltpu.VMEM(w.shape, BF16)],
        compiler_params=pltpu.CompilerParams(
            dimension_semantics=("arbitrary",), vmem_limit_bytes=VMEM_LIMIT),
        name="outproj",
    )(x2, attn2, lru2, w, g)


def _ffn_kernel(hp_ref, hm_ref, hn_ref, xm_ref, wup_ref, cw_ref, cb_ref, wdn_ref, fg_ref, o_ref,
                perm_ref, hext_ref, ua_ref, ub_ref, acta_ref, actb_ref, acc_ref,
                *, tile, chunk, d_ff):
    i = pl.program_id(1)
    nchunks = d_ff // chunk
    ngrp = tile // SUBLANES
    nslab = perm_ref.shape[0]
    d = nslab * LANES

    for s in range(SUBLANES):
        rows = hm_ref[s * ngrp:(s + 1) * ngrp, :].astype(F32)
        for n in range(nslab):
            perm_ref[n, pl.ds(s, ngrp, stride=SUBLANES), :] = rows[:, n * LANES:(n + 1) * LANES]
    for n in range(nslab):
        hext_ref[0:tile, n * LANES:(n + 1) * LANES] = perm_ref[n].astype(BF16)
    prev = jnp.where(i > 0, hp_ref[BF16_ROWS - 1:BF16_ROWS, :].astype(F32), 0.0)
    nxt = jnp.where(i < pl.num_programs(1) - 1, hn_ref[0:1, :].astype(F32), 0.0)
    hrow = lax.broadcasted_iota(jnp.int32, (BF16_ROWS, d), 0)
    halo = jnp.where(hrow == 0, prev, jnp.where(hrow == 1, nxt, 0.0))
    hext_ref[tile:tile + BF16_ROWS, :] = halo.astype(BF16)
    acc_ref[...] = jnp.zeros_like(acc_ref)

    def offsets(c):
        og, ov = c * chunk, d_ff + c * chunk
        if isinstance(c, int):
            return og, ov
        return pl.multiple_of(og, chunk), pl.multiple_of(ov, chunk)

    def up(c, u_ref):
        og, ov = offsets(c)
        hext = hext_ref[...]
        u_ref[:, 0:chunk] = _dot(hext, wup_ref[:, pl.ds(og, chunk)])
        u_ref[:, chunk:2 * chunk] = _dot(hext, wup_ref[:, pl.ds(ov, chunk)])

    def glu(c, u_ref, act_ref):
        sub = lax.broadcasted_iota(jnp.int32, (SUBLANES, chunk), 0)
        blk = BF16_ROWS

        def conv(col0, off, r0):
            cols = slice(col0, col0 + chunk)
            reps = (blk // SUBLANES, 1)
            w = [jnp.tile(cw_ref[k * SUBLANES:(k + 1) * SUBLANES, pl.ds(off, chunk)], reps)
                 for k in range(3)]
            bias = jnp.tile(cb_ref[:, pl.ds(off, chunk)], reps)
            cur = u_ref[r0:r0 + blk, cols]
            if r0 == 0:
                first = jnp.where(sub == 0, u_ref[tile:tile + 1, cols],
                                  pltpu.roll(u_ref[tile - SUBLANES:tile, cols], 1, 0))
                um1 = jnp.concatenate([first, cur[0:blk - SUBLANES]], axis=0)
            else:
                um1 = u_ref[r0 - SUBLANES:r0 + blk - SUBLANES, cols]
            if r0 + blk == tile:
                last = jnp.where(sub == SUBLANES - 1, u_ref[tile + 1:tile + 2, cols],
                                 pltpu.roll(u_ref[0:SUBLANES, cols], SUBLANES - 1, 0))
                up1 = jnp.concatenate([cur[SUBLANES:blk], last], axis=0)
            else:
                up1 = u_ref[r0 + SUBLANES:r0 + blk + SUBLANES, cols]
            return bias + um1 * w[0] + cur * w[1] + up1 * w[2]

        og, ov = offsets(c)
        for r0 in range(0, tile, blk):
            gate = conv(0, og, r0)
            val = conv(chunk, ov, r0)
            act_ref[r0:r0 + blk, :] = (jax.nn.gelu(gate, approximate=True) * val).astype(BF16)

    def down(c, act_ref):
        og, _ = offsets(c)
        y = _dot(act_ref[...], wdn_ref[pl.ds(og, chunk), :])
        for n in range(nslab):
            acc_ref[n, 0:tile, :] += y[:, n * LANES:(n + 1) * LANES]

    assert nchunks % 2 == 1 and nchunks >= 3
    up(0, ua_ref)

    def pair(j, carry):
        c = 2 * j
        up(c + 1, ub_ref)
        glu(c, ua_ref, acta_ref)
        down(c, acta_ref)
        up(c + 2, ua_ref)
        glu(c + 1, ub_ref, actb_ref)
        down(c + 1, actb_ref)
        return carry

    for j in range(nchunks // 2):
        pair(j, 0)
    glu(nchunks - 1, ua_ref, acta_ref)
    down(nchunks - 1, acta_ref)
    for s in range(SUBLANES):
        rs = slice(s * ngrp, (s + 1) * ngrp)
        y = jnp.concatenate([acc_ref[n, pl.ds(s, ngrp, stride=SUBLANES), :] for n in range(nslab)],
                            axis=1)
        o_ref[rs, :] = _rms(xm_ref[rs, :] + y, fg_ref[...])


def _ffn(h2, xm, w_up, conv_w, conv_b, w_down, final_g, tile, chunk):
    b, s, d = h2.shape
    d_ff = w_down.shape[0]
    nh = tile // BF16_ROWS
    last_halo = s // BF16_ROWS - 1
    main = lambda bi, i: (bi, i, 0)
    const = lambda bi, i: (0, 0)
    single = dict(pipeline_mode=pl.Buffered(1))
    kernel = functools.partial(_ffn_kernel, tile=tile, chunk=chunk, d_ff=d_ff)
    return pl.pallas_call(
        kernel,
        grid=(b, s // tile),
        in_specs=[pl.BlockSpec((None, BF16_ROWS, d), lambda bi, i: (bi, jnp.maximum(i * nh - 1, 0), 0)),
                  pl.BlockSpec((None, tile, d), main),
                  pl.BlockSpec((None, BF16_ROWS, d),
                               lambda bi, i: (bi, jnp.minimum((i + 1) * nh, last_halo), 0)),
                  pl.BlockSpec((None, tile, d), main),
                  pl.BlockSpec(w_up.shape, const, **single),
                  pl.BlockSpec(conv_w.shape, const),
                  pl.BlockSpec(conv_b.shape, const),
                  pl.BlockSpec(w_down.shape, const, **single),
                  pl.BlockSpec((1, d), const)],
        out_specs=pl.BlockSpec((None, tile, d), main),
        out_shape=jax.ShapeDtypeStruct((b, s, d), F32),
        scratch_shapes=[pltpu.VMEM((d // LANES, tile, LANES), F32),
                        pltpu.VMEM((tile + BF16_ROWS, d), BF16)]
        + [pltpu.VMEM((tile + BF16_ROWS, 2 * chunk), F32)] * 2
        + [pltpu.VMEM((tile, chunk), BF16)] * 2
        + [pltpu.VMEM((d // LANES, tile + SUBLANES, LANES), F32)],
        compiler_params=pltpu.CompilerParams(
            dimension_semantics=("arbitrary",) * 2, vmem_limit_bytes=VMEM_LIMIT),
        name="ffn",
    )(h2, h2, h2, xm, w_up, conv_w, conv_b, w_down, final_g)


def _gate_weights(w_a, b_a, w_x, b_x):
    ndir, nblk, bd, _ = w_a.shape
    per = LANES // bd
    npair = nblk // per

    def blockdiag(w):
        w = w.reshape(npair, per, bd, bd)
        eye = jnp.eye(per, dtype=w.dtype)
        return jnp.einsum('pbij,bc->pbicj', w, eye).reshape(npair, LANES, LANES)

    ws, bs = [], []
    for d in range(ndir):
        for w, bias in ((w_a, b_a), (w_x, b_x)):
            ws.append(blockdiag(w[d]))
            bs.append(bias[d].reshape(npair, LANES))
    return ((0.5 * jnp.concatenate(ws, axis=-1)).astype(BF16),
            (0.5 * jnp.concatenate(bs, axis=-1)).astype(F32))


def kernel(x, attn_norm_g, w_in, lambda_q1, lambda_k1, lambda_q2, lambda_k2, subln_g,
           lru_conv_w, lru_conv_b, lru_w_a, lru_b_a, lru_w_x, lru_b_x, lru_lambda,
           w_out, ffn_norm_g, w_up, ffn_conv_w, ffn_conv_b, w_down, final_norm_g):
    b, s, d = x.shape
    depth = w_in.shape[0]
    x2 = x.reshape(b * s, d)
    for l in range(depth):
        lambda_init = 0.8 - 0.6 * math.exp(-0.3 * l)
        q, k, vt, xr, gr = _inproj(x2, attn_norm_g[l][None], w_in[l], tm=512)
        lw = xr.shape[-1]
        attn = _attention(q.reshape(b, s, -1), k.reshape(b, s, -1), vt,
                          lambda_q1[l][None], lambda_k1[l][None], lambda_q2[l][None],
                          lambda_k2[l][None], subln_g[l][:, None], lambda_init, tq=256, nsub=4)
        wg, bg = _gate_weights(lru_w_a[l], lru_b_a[l], lru_w_x[l], lru_b_x[l])
        lru = _lru(xr.reshape(b, s, lw), gr.reshape(b, s, lw),
                   jnp.repeat(lru_conv_w[l], SUBLANES, axis=0),
                   jnp.broadcast_to(lru_conv_b[l][None], (SUBLANES, lw)),
                   wg, bg, lru_lambda[l], rows=256)
        xm, h2 = _outproj(x2, attn.reshape(b * s, -1), lru.reshape(b * s, -1),
                          w_out[l], ffn_norm_g[l][None], tm=1024)
        assert depth == 1
        x2 = _ffn(h2.reshape(b, s, d), xm.reshape(b, s, d), w_up[l].astype(BF16),
                  jnp.repeat(ffn_conv_w[l], SUBLANES, axis=0),
                  jnp.broadcast_to(ffn_conv_b[l][None], (SUBLANES, ffn_conv_b.shape[-1])),
                  w_down[l].astype(BF16), final_norm_g[None],
                  tile=512, chunk=256).reshape(b * s, d)
    return x2.reshape(b, s, d)
```

```python
import functools
import math

import jax
import jax.numpy as jnp
from jax import lax
from jax.experimental import pallas as pl
from jax.experimental.pallas import tpu as pltpu

F32 = jnp.float32
BF16 = jnp.bfloat16

N_HEADS = 4
HEAD_DIM = 64
V_DIM = 2 * HEAD_DIM
ATTN_WIDTH = N_HEADS * V_DIM
LRU_BLOCK = 64
LRU_CONV_WIDTH = 4
LRU_CONV_LEFT = 2
LRU_C = 8.0
FFN_CONV_LEFT = 1
NORM_EPS = 1e-6
LANES = 128
SUBLANES = 8
BF16_ROWS = 16
VMEM_LIMIT = 56 * 1024 * 1024


def _rms(x, g):
    return (x * lax.rsqrt(jnp.mean(x * x, axis=-1, keepdims=True) + NORM_EPS)) * g


def _dot(a, b):
    return jnp.dot(a, b, preferred_element_type=F32)


_NT = (((1,), (1,)), ((), ()))


def _inproj_kernel(x_ref, g_ref, w_ref, q_ref, k_ref, vt_ref, xr_ref, gr_ref, wb_ref, wvt_ref):
    aw = ATTN_WIDTH
    d = x_ref.shape[-1]

    @pl.when(pl.program_id(0) == 0)
    def _cast_weights():
        rows = 2 * LANES
        for r in range(0, d, rows):
            w = w_ref[r:r + rows, :]
            wb_ref[r:r + rows, 0:aw] = (w[:, 0:aw] * (HEAD_DIM ** -0.5)).astype(BF16)
            wb_ref[r:r + rows, aw:] = w[:, aw:].astype(BF16)
        for c in range(aw // LANES):
            cols = slice(2 * aw + c * LANES, 2 * aw + (c + 1) * LANES)
            wvt_ref[c * LANES:(c + 1) * LANES, :] = w_ref[:, cols].T.astype(BF16)

    hb = _rms(x_ref[...], g_ref[...]).astype(BF16)
    q_ref[...] = _dot(hb, wb_ref[:, 0:aw]).astype(BF16)
    k_ref[...] = _dot(hb, wb_ref[:, aw:2 * aw]).astype(BF16)
    vt_ref[...] = lax.dot_general(wvt_ref[...], hb, _NT, preferred_element_type=F32).astype(BF16)
    lw = xr_ref.shape[-1]
    xr_ref[...] = _dot(hb, wb_ref[:, 3 * aw:3 * aw + lw])
    gr_ref[...] = _dot(hb, wb_ref[:, 3 * aw + lw:3 * aw + 2 * lw])


def _inproj(x2, g, w, tm):
    n, d = x2.shape
    lw = (w.shape[1] - 3 * ATTN_WIDTH) // 2
    row = lambda i: (i, 0)
    const = lambda i: (0, 0)
    return pl.pallas_call(
        _inproj_kernel,
        grid=(n // tm,),
        in_specs=[pl.BlockSpec((tm, d), row),
                  pl.BlockSpec((1, d), const),
                  pl.BlockSpec(w.shape, const, pipeline_mode=pl.Buffered(1))],
        out_specs=[pl.BlockSpec((tm, ATTN_WIDTH), row)] * 2
        + [pl.BlockSpec((ATTN_WIDTH, tm), lambda i: (0, i))]
        + [pl.BlockSpec((tm, lw), row)] * 2,
        out_shape=[jax.ShapeDtypeStruct((n, ATTN_WIDTH), BF16)] * 2
        + [jax.ShapeDtypeStruct((ATTN_WIDTH, n), BF16)]
        + [jax.ShapeDtypeStruct((n, lw), F32)] * 2,
        scratch_shapes=[pltpu.VMEM(w.shape, BF16), pltpu.VMEM((ATTN_WIDTH, d), BF16)],
        compiler_params=pltpu.CompilerParams(
            dimension_semantics=("arbitrary",), vmem_limit_bytes=VMEM_LIMIT),
        name="inproj",
    )(x2, g, w)


def _attn_kernel(lq1_ref, lk1_ref, lq2_ref, lk2_ref, sg_ref, q_ref, k_ref, vt_ref, *rest,
                 tq, nsub, seq, lambda_init):
    nside = (len(rest) - 3 - 3 * nsub) // 2
    side_in, o_ref, side_out = rest[:nside], rest[nside], rest[nside + 1:2 * nside + 1]
    kf_ref, dist_ref, *se_refs = rest[2 * nside + 1:]
    for src, dst in zip(side_in, side_out):
        dst[...] = src[...].astype(dst.dtype)

    h = pl.program_id(1)
    qi = pl.program_id(2)
    s_refs, e_refs, m_refs = se_refs[:nsub], se_refs[nsub:2 * nsub], se_refs[2 * nsub:]
    nblk = seq // tq
    assert nblk & (nblk - 1) == 0 and 3 * nblk <= LANES and tq <= 256
    shift = nblk.bit_length() - 1

    @pl.when((pl.program_id(0) == 0) & (h == 0) & (qi == 0))
    def _init():
        lane = lax.broadcasted_iota(jnp.int32, (seq, LANES), 1)
        row = lax.broadcasted_iota(jnp.int32, (seq, LANES), 0)
        grp = lane >> shift
        hit = (row // tq) == (lane & (nblk - 1))
        dj = (row % tq).astype(F32)
        kf_ref[...] = jnp.where(hit & (grp < 2), 1.0,
                                jnp.where(hit & (grp == 2), dj, 0.0)).astype(BF16)
        r = lax.broadcasted_iota(jnp.int32, (tq, tq), 0)
        c = lax.broadcasted_iota(jnp.int32, (tq, tq), 1)
        dist_ref[...] = jnp.abs(r - c).astype(F32)

    lam = (jnp.exp(jnp.sum(lq1_ref[...] * lk1_ref[...], axis=-1, keepdims=True))
           - jnp.exp(jnp.sum(lq2_ref[...] * lk2_ref[...], axis=-1, keepdims=True))
           + lambda_init)
    slope = jnp.where(h == 0, 2.0 ** -2, jnp.where(h == 1, 2.0 ** -4,
                      jnp.where(h == 2, 2.0 ** -6, 2.0 ** -8))).astype(F32)

    kaug = jnp.concatenate([k_ref[...], kf_ref[...]], axis=1)
    vt_ones = jnp.concatenate([vt_ref[...], jnp.ones((BF16_ROWS, seq), BF16)], axis=0)
    lane = lax.broadcasted_iota(jnp.int32, (tq, LANES), 1)
    di = lax.broadcasted_iota(jnp.int32, (tq, LANES), 0).astype(F32)
    grp = lane >> shift
    strip = 4 * SUBLANES

    def scores(sb):
        blk = qi * nsub + sb
        diff = blk - (lane & (nblk - 1))
        sign = jnp.where(diff > 0, 1.0, jnp.where(diff < 0, -1.0, 0.0))
        qf = jnp.where(grp == 0, -slope * sign * di,
                       jnp.where(grp == 1, -slope * tq * jnp.abs(diff).astype(F32),
                                 jnp.where(grp == 2, slope * sign, 0.0))).astype(BF16)
        q = q_ref[sb * tq:(sb + 1) * tq, :]
        zero = jnp.zeros_like(q)
        qaug = jnp.concatenate(
            [jnp.concatenate([jnp.where(lane < HEAD_DIM, q, zero), qf], axis=1),
             jnp.concatenate([jnp.where(lane >= HEAD_DIM, q, zero), qf], axis=1)], axis=0)
        s_ref = s_refs[sb]
        s = lax.dot_general(kaug, qaug, _NT, preferred_element_type=F32)
        s_ref[...] = s
        mx = s[0:strip, :]
        for r in range(strip, seq, strip):
            mx = jnp.maximum(mx, s[r:r + strip, :])
        m_refs[sb][...] = mx
        diag = pl.ds(pl.multiple_of(blk * tq, tq), tq)
        diag_bias = slope * dist_ref[...]
        s_ref[diag, 0:tq] = s_ref[diag, 0:tq] - diag_bias
        s_ref[diag, tq:2 * tq] = s_ref[diag, tq:2 * tq] - diag_bias

    def finish(sb):
        s_ref, e_ref = s_refs[sb], e_refs[sb]
        mx = jnp.max(m_refs[sb][...], axis=0, keepdims=True)
        for r in range(0, seq, strip):
            e_ref[r:r + strip, :] = jnp.exp(s_ref[r:r + strip, :] - mx).astype(BF16)
        o12 = _dot(vt_ones, e_ref[...])
        norm = o12[V_DIM:V_DIM + 1, :]
        o12 = o12[0:V_DIM, :]
        o = o12[:, 0:tq] * (1.0 / norm[:, 0:tq]) - o12[:, tq:2 * tq] * (lam / norm[:, tq:2 * tq])
        o = o * lax.rsqrt(jnp.mean(o * o, axis=0, keepdims=True) + NORM_EPS)
        o = o * sg_ref[...] * (1.0 - lambda_init)
        o_ref[sb * tq:(sb + 1) * tq, :] = o.T.astype(o_ref.dtype)

    scores(0)
    for sb in range(nsub):
        if sb + 1 < nsub:
            scores(sb + 1)
        finish(sb)


def _attention(q, k, vt, lq1, lk1, lq2, lk2, subln_g, lambda_init, tq, nsub, cast_along=()):
    b, s, _ = q.shape
    vec = lambda bi, h, qi: (0, 0)
    tstep = tq * nsub
    nq = s // tstep
    nsteps = b * N_HEADS * nq

    def row_block(last):
        return lambda bi, h, qi: (jnp.minimum((bi * N_HEADS + h) * nq + qi, last), 0)

    side_in, side_out, side_shapes = [], [], []
    for w in cast_along:
        rows = -(-w.shape[0] // (nsteps * BF16_ROWS)) * BF16_ROWS
        while w.shape[0] % rows:
            rows += BF16_ROWS
        for specs in (side_in, side_out):
            specs.append(pl.BlockSpec((rows, w.shape[1]), row_block(w.shape[0] // rows - 1)))
        side_shapes.append(jax.ShapeDtypeStruct(w.shape, BF16))
    kernel = functools.partial(_attn_kernel, tq=tq, nsub=nsub, seq=s, lambda_init=lambda_init)
    return pl.pallas_call(
        kernel,
        grid=(b, N_HEADS, s // tstep),
        in_specs=[pl.BlockSpec((1, HEAD_DIM), vec)] * 4
        + [pl.BlockSpec((V_DIM, 1), vec),
           pl.BlockSpec((None, tstep, V_DIM), lambda bi, h, qi: (bi, qi, h)),
           pl.BlockSpec((None, s, V_DIM), lambda bi, h, qi: (bi, 0, h)),
           pl.BlockSpec((V_DIM, s), lambda bi, h, qi: (h, bi))] + side_in,
        out_specs=[pl.BlockSpec((None, tstep, V_DIM), lambda bi, h, qi: (bi, qi, h))] + side_out,
        out_shape=[jax.ShapeDtypeStruct((b, s, ATTN_WIDTH), BF16)] + side_shapes,
        scratch_shapes=[pltpu.VMEM((s, LANES), BF16), pltpu.VMEM((tq, tq), F32)]
        + [pltpu.VMEM((s, 2 * tq), F32)] * nsub + [pltpu.VMEM((s, 2 * tq), BF16)] * nsub
        + [pltpu.VMEM((4 * SUBLANES, 2 * tq), F32)] * nsub,
        compiler_params=pltpu.CompilerParams(
            dimension_semantics=("arbitrary",) * 3, vmem_limit_bytes=VMEM_LIMIT),
        name="attn",
    )(lq1, lk1, lq2, lk2, subln_g, q, k, vt, *cast_along)


def _local_scan(a, u, reverse):
    row = lax.broadcasted_iota(jnp.int32, a.shape, 0)
    for d in (1, 2, 4):
        shift = SUBLANES - d if reverse else d
        valid = (row < SUBLANES - d) if reverse else (row >= d)
        a_s = jnp.where(valid, pltpu.roll(a, shift, 0), 1.0)
        u_s = jnp.where(valid, pltpu.roll(u, shift, 0), 0.0)
        u = a * u_s + u
        a = a * a_s
    return a, u


def _lru_kernel(xr_ref, gr_ref, cw_ref, cb_ref, wg_ref, bg_ref, lam_ref, o_ref,
                xp_ref, xc_ref, h_ref, cum_ref, *, seq, rows):
    width = xr_ref.shape[-1]
    ntile = width // LANES
    grp = seq // SUBLANES
    left = LRU_CONV_LEFT
    right = LRU_CONV_WIDTH - 1 - LRU_CONV_LEFT
    top = left * SUBLANES
    sub = lax.broadcasted_iota(jnp.int32, (SUBLANES, LANES), 0)

    for s in range(SUBLANES):
        blk = xr_ref[s * grp:(s + 1) * grp, :]
        for p in range(ntile):
            xp_ref[p, pl.ds(top + s, grp, stride=SUBLANES), :] = blk[:, p * LANES:(p + 1) * LANES]
    for p in range(ntile):
        for k in range(left):
            src = xp_ref[p, top + (grp - 1 - k) * SUBLANES:top + (grp - k) * SUBLANES, :]
            xp_ref[p, top - (k + 1) * SUBLANES:top - k * SUBLANES, :] = jnp.where(
                sub == 0, 0.0, pltpu.roll(src, 1, 0))
        for k in range(right):
            src = xp_ref[p, top + k * SUBLANES:top + (k + 1) * SUBLANES, :]
            xp_ref[p, top + (grp + k) * SUBLANES:top + (grp + k + 1) * SUBLANES, :] = jnp.where(
                sub == SUBLANES - 1, 0.0, pltpu.roll(src, SUBLANES - 1, 0))

    neg_lam = -lam_ref[...]
    softplus = jnp.maximum(neg_lam, 0.0) + jnp.log1p(jnp.exp(-jnp.abs(neg_lam)))
    rate = LRU_C * softplus
    rate_log2 = -rate * math.log2(math.e)

    nchunk = seq // rows
    for c in range(nchunk):
        r0 = c * rows
        for p in range(ntile):
            cols = slice(p * LANES, (p + 1) * LANES)
            xc = cb_ref[:, cols]
            for tap in range(LRU_CONV_WIDTH):
                start = top + r0 + (tap - left) * SUBLANES
                xc = xc + xp_ref[p, start:start + rows, :] * cw_ref[tap:tap + 1, cols]
            xc_ref[p, r0:r0 + rows, :] = xc

    zero = jnp.zeros((SUBLANES, LANES), F32)
    one = jnp.ones((SUBLANES, LANES), F32)
    ends = [[(zero, one), (zero, one)] for _ in range(ntile)]
    for c in range(nchunk):
        for p in range(ntile):
            cols = slice(p * LANES, (p + 1) * LANES)
            for d in range(2):
                r0 = (c if d == 0 else nchunk - 1 - c) * rows
                gcols = slice(2 * d * LANES, (2 * d + 2) * LANES)
                xc = xc_ref[p, r0:r0 + rows, :]
                gates = _dot(xc.astype(BF16), wg_ref[p, :, gcols]) + bg_ref[p:p + 1, gcols]
                r = jax.nn.sigmoid(gates[:, 0:LANES])
                i = jax.nn.sigmoid(gates[:, LANES:2 * LANES])
                t = jnp.tanh(r * rate[d:d + 1, cols])
                tt = t + t
                prod = tt * (1.0 + t)
                mult = jnp.where(prod > 0.0, tt * lax.rsqrt(prod), 0.0)
                a = jnp.exp2(r * rate_log2[d:d + 1, cols])
                u = mult * (i * xc)
                h, cum = ends[p][d]
                vrows = range(rows // SUBLANES)
                for j in (vrows if d == 0 else reversed(vrows)):
                    rs = slice(j * SUBLANES, (j + 1) * SUBLANES)
                    h = a[rs] * h + u[rs]
                    cum = a[rs] * cum
                    h_ref[d, p, r0 + j * SUBLANES:r0 + (j + 1) * SUBLANES, :] = h
                    cum_ref[d, p, r0 + j * SUBLANES:r0 + (j + 1) * SUBLANES, :] = cum
                ends[p][d] = (h, cum)

    enter = []
    for p in range(ntile):
        for d in range(2):
            h_end, cum_end = ends[p][d]
            _, chained = _local_scan(cum_end, h_end, reverse=(d == 1))
            if d == 0:
                enter.append(jnp.where(sub == 0, 0.0, pltpu.roll(chained, 1, 0)))
            else:
                enter.append(jnp.where(sub == SUBLANES - 1, 0.0,
                                       pltpu.roll(chained, SUBLANES - 1, 0)))

    for c in range(seq // rows):
        rs = slice(c * rows, (c + 1) * rows)
        for p in range(ntile):
            y = None
            for d in range(2):
                init = jnp.tile(enter[2 * p + d], (rows // SUBLANES, 1))
                part = h_ref[d, p, rs, :] + cum_ref[d, p, rs, :] * init
                y = part if y is None else y + part
            xc_ref[p, rs, :] = y

    for s in range(SUBLANES):
        rs = slice(s * grp, (s + 1) * grp)
        for p in range(ntile):
            cols = slice(p * LANES, (p + 1) * LANES)
            y = xc_ref[p, pl.ds(s, grp, stride=SUBLANES), :]
            o_ref[rs, cols] = (jax.nn.gelu(gr_ref[rs, cols], approximate=True) * y).astype(o_ref.dtype)


def _lru(xr, gr, conv_w, conv_b, wg, bg, lru_lambda, rows):
    b, s, width = xr.shape
    ntile = width // LANES
    seqblk = pl.BlockSpec((None, s, width), lambda bi: (bi, 0, 0))
    full = lambda a: pl.BlockSpec(a.shape, lambda bi: (0,) * a.ndim)
    kernel = functools.partial(_lru_kernel, seq=s, rows=rows)
    halo_rows = (LRU_CONV_WIDTH - 1) * SUBLANES
    return pl.pallas_call(
        kernel,
        grid=(b,),
        in_specs=[seqblk, seqblk, full(conv_w), full(conv_b), full(wg), full(bg), full(lru_lambda)],
        out_specs=seqblk,
        out_shape=jax.ShapeDtypeStruct((b, s, width), BF16),
        scratch_shapes=[pltpu.VMEM((ntile, s + halo_rows, LANES), F32),
                        pltpu.VMEM((ntile, s, LANES), F32),
                        pltpu.VMEM((2, ntile, s, LANES), F32),
                        pltpu.VMEM((2, ntile, s, LANES), F32)],
        compiler_params=pltpu.CompilerParams(
            dimension_semantics=("arbitrary",), vmem_limit_bytes=VMEM_LIMIT),
        name="lru",
    )(xr, gr, conv_w, conv_b, wg, bg, lru_lambda)


def _outproj_kernel(x_ref, a_ref, l_ref, w_ref, g_ref, xm_ref, h_ref, wb_ref):
    @pl.when(pl.program_id(0) == 0)
    def _cast_weights():
        wb_ref[...] = w_ref[...].astype(BF16)

    aw = a_ref.shape[-1]
    xm = x_ref[...] + _dot(a_ref[...], wb_ref[0:aw, :]) + _dot(l_ref[...], wb_ref[aw:, :])
    xm_ref[...] = xm
    h_ref[...] = _rms(xm, g_ref[...]).astype(h_ref.dtype)


def _outproj(x2, attn2, lru2, w, g, tm):
    n, d = x2.shape
    row = lambda i: (i, 0)
    const = lambda i: (0, 0)
    return pl.pallas_call(
        _outproj_kernel,
        grid=(n // tm,),
        in_specs=[pl.BlockSpec((tm, d), row),
                  pl.BlockSpec((tm, attn2.shape[1]), row),
                  pl.BlockSpec((tm, lru2.shape[1]), row),
                  pl.BlockSpec(w.shape, const, pipeline_mode=pl.Buffered(1)),
                  pl.BlockSpec((1, d), const)],
        out_specs=[pl.BlockSpec((tm, d), row)] * 2,
        out_shape=[jax.ShapeDtypeStruct((n, d), F32), jax.ShapeDtypeStruct((n, d), BF16)],
        scratch_shapes=[pltpu.VMEM(w.shape, BF16)],
        compiler_params=pltpu.CompilerParams(
            dimension_semantics=("arbitrary",), vmem_limit_bytes=VMEM_LIMIT),
        name="outproj",
    )(x2, attn2, lru2, w, g)


def _ffn_kernel(hp_ref, hm_ref, hn_ref, xm_ref, wup_ref, cw_ref, cb_ref, wdn_ref, fg_ref, o_ref,
                perm_ref, hext_ref, ua_ref, ub_ref, acta_ref, actb_ref, acc_ref,
                *, tile, chunk, d_ff):
    i = pl.program_id(1)
    nchunks = d_ff // chunk
    ngrp = tile // SUBLANES
    nslab = perm_ref.shape[0]
    d = nslab * LANES

    for s in range(SUBLANES):
        rows = hm_ref[s * ngrp:(s + 1) * ngrp, :].astype(F32)
        for n in range(nslab):
            perm_ref[n, pl.ds(s, ngrp, stride=SUBLANES), :] = rows[:, n * LANES:(n + 1) * LANES]
    for n in range(nslab):
        hext_ref[0:tile, n * LANES:(n + 1) * LANES] = perm_ref[n].astype(BF16)
    prev = jnp.where(i > 0, hp_ref[BF16_ROWS - 1:BF16_ROWS, :].astype(F32), 0.0)
    nxt = jnp.where(i < pl.num_programs(1) - 1, hn_ref[0:1, :].astype(F32), 0.0)
    hrow = lax.broadcasted_iota(jnp.int32, (BF16_ROWS, d), 0)
    halo = jnp.where(hrow == 0, prev, jnp.where(hrow == 1, nxt, 0.0))
    hext_ref[tile:tile + BF16_ROWS, :] = halo.astype(BF16)
    acc_ref[...] = jnp.zeros_like(acc_ref)

    def offsets(c):
        og, ov = c * chunk, d_ff + c * chunk
        if isinstance(c, int):
            return og, ov
        return pl.multiple_of(og, chunk), pl.multiple_of(ov, chunk)

    def up(c, u_ref):
        og, ov = offsets(c)
        hext = hext_ref[...]
        u_ref[:, 0:chunk] = _dot(hext, wup_ref[:, pl.ds(og, chunk)])
        u_ref[:, chunk:2 * chunk] = _dot(hext, wup_ref[:, pl.ds(ov, chunk)])

    def glu(c, u_ref, act_ref):
        sub = lax.broadcasted_iota(jnp.int32, (SUBLANES, chunk), 0)
        blk = BF16_ROWS

        def conv(col0, off, r0):
            cols = slice(col0, col0 + chunk)
            reps = (blk // SUBLANES, 1)
            w = [jnp.tile(cw_ref[k * SUBLANES:(k + 1) * SUBLANES, pl.ds(off, chunk)], reps)
                 for k in range(3)]
            bias = jnp.tile(cb_ref[:, pl.ds(off, chunk)], reps)
            cur = u_ref[r0:r0 + blk, cols]
            if r0 == 0:
                first = jnp.where(sub == 0, u_ref[tile:tile + 1, cols],
                                  pltpu.roll(u_ref[tile - SUBLANES:tile, cols], 1, 0))
                um1 = jnp.concatenate([first, cur[0:blk - SUBLANES]], axis=0)
            else:
                um1 = u_ref[r0 - SUBLANES:r0 + blk - SUBLANES, cols]
            if r0 + blk == tile:
                last = jnp.where(sub == SUBLANES - 1, u_ref[tile + 1:tile + 2, cols],
                                 pltpu.roll(u_ref[0:SUBLANES, cols], SUBLANES - 1, 0))
                up1 = jnp.concatenate([cur[SUBLANES:blk], last], axis=0)
            else:
                up1 = u_ref[r0 + SUBLANES:r0 + blk + SUBLANES, cols]
            return bias + um1 * w[0] + cur * w[1] + up1 * w[2]

        og, ov = offsets(c)
        for r0 in range(0, tile, blk):
            gate = conv(0, og, r0)
            val = conv(chunk, ov, r0)
            act_ref[r0:r0 + blk, :] = (jax.nn.gelu(gate, approximate=True) * val).astype(BF16)

    def down(c, act_ref):
        og, _ = offsets(c)
        y = _dot(act_ref[...], wdn_ref[pl.ds(og, chunk), :])
        for n in range(nslab):
            acc_ref[n, 0:tile, :] += y[:, n * LANES:(n + 1) * LANES]

    assert nchunks % 2 == 1 and nchunks >= 3
    up(0, ua_ref)

    def pair(j, carry):
        c = 2 * j
        up(c + 1, ub_ref)
        glu(c, ua_ref, acta_ref)
        down(c, acta_ref)
        up(c + 2, ua_ref)
        glu(c + 1, ub_ref, actb_ref)
        down(c + 1, actb_ref)
        return carry

    for j in range(nchunks // 2):
        pair(j, 0)
    glu(nchunks - 1, ua_ref, acta_ref)
    down(nchunks - 1, acta_ref)
    for s in range(SUBLANES):
        rs = slice(s * ngrp, (s + 1) * ngrp)
        y = jnp.concatenate([acc_ref[n, pl.ds(s, ngrp, stride=SUBLANES), :] for n in range(nslab)],
                            axis=1)
        o_ref[rs, :] = _rms(xm_ref[rs, :] + y, fg_ref[...])


def _ffn(h2, xm, w_up, conv_w, conv_b, w_down, final_g, tile, chunk):
    b, s, d = h2.shape
    d_ff = w_down.shape[0]
    nh = tile // BF16_ROWS
    last_halo = s // BF16_ROWS - 1
    main = lambda bi, i: (bi, i, 0)
    const = lambda bi, i: (0, 0)
    single = dict(pipeline_mode=pl.Buffered(1))
    kernel = functools.partial(_ffn_kernel, tile=tile, chunk=chunk, d_ff=d_ff)
    return pl.pallas_call(
        kernel,
        grid=(b, s // tile),
        in_specs=[pl.BlockSpec((None, BF16_ROWS, d), lambda bi, i: (bi, jnp.maximum(i * nh - 1, 0), 0)),
                  pl.BlockSpec((None, tile, d), main),
                  pl.BlockSpec((None, BF16_ROWS, d),
                               lambda bi, i: (bi, jnp.minimum((i + 1) * nh, last_halo), 0)),
                  pl.BlockSpec((None, tile, d), main),
                  pl.BlockSpec(w_up.shape, const, **single),
                  pl.BlockSpec(conv_w.shape, const),
                  pl.BlockSpec(conv_b.shape, const),
                  pl.BlockSpec(w_down.shape, const, **single),
                  pl.BlockSpec((1, d), const)],
        out_specs=pl.BlockSpec((None, tile, d), main),
        out_shape=jax.ShapeDtypeStruct((b, s, d), F32),
        scratch_shapes=[pltpu.VMEM((d // LANES, tile, LANES), F32),
                        pltpu.VMEM((tile + BF16_ROWS, d), BF16)]
        + [pltpu.VMEM((tile + BF16_ROWS, 2 * chunk), F32)] * 2
        + [pltpu.VMEM((tile, chunk), BF16)] * 2
        + [pltpu.VMEM((d // LANES, tile + SUBLANES, LANES), F32)],
        compiler_params=pltpu.CompilerParams(
            dimension_semantics=("arbitrary",) * 2, vmem_limit_bytes=VMEM_LIMIT),
        name="ffn",
    )(h2, h2, h2, xm, w_up, conv_w, conv_b, w_down, final_g)


def _gate_weights(w_a, b_a, w_x, b_x):
    ndir, nblk, bd, _ = w_a.shape
    per = LANES // bd
    npair = nblk // per

    def blockdiag(w):
        w = w.reshape(npair, per, bd, bd)
        eye = jnp.eye(per, dtype=w.dtype)
        return jnp.einsum('pbij,bc->pbicj', w, eye).reshape(npair, LANES, LANES)

    ws, bs = [], []
    for d in range(ndir):
        for w, bias in ((w_a, b_a), (w_x, b_x)):
            ws.append(blockdiag(w[d]))
            bs.append(bias[d].reshape(npair, LANES))
    return jnp.concatenate(ws, axis=-1).astype(BF16), jnp.concatenate(bs, axis=-1).astype(F32)


def kernel(x, attn_norm_g, w_in, lambda_q1, lambda_k1, lambda_q2, lambda_k2, subln_g,
           lru_conv_w, lru_conv_b, lru_w_a, lru_b_a, lru_w_x, lru_b_x, lru_lambda,
           w_out, ffn_norm_g, w_up, ffn_conv_w, ffn_conv_b, w_down, final_norm_g):
    b, s, d = x.shape
    depth = w_in.shape[0]
    x2 = x.reshape(b * s, d)
    for l in range(depth):
        lambda_init = 0.8 - 0.6 * math.exp(-0.3 * l)
        q, k, vt, xr, gr = _inproj(x2, attn_norm_g[l][None], w_in[l], tm=512)
        lw = xr.shape[-1]
        attn, w_up_bf, w_down_bf = _attention(
            q.reshape(b, s, -1), k.reshape(b, s, -1), vt,
            lambda_q1[l][None], lambda_k1[l][None], lambda_q2[l][None], lambda_k2[l][None],
            subln_g[l][:, None], lambda_init, tq=256, nsub=4, cast_along=(w_up[l], w_down[l]))
        wg, bg = _gate_weights(lru_w_a[l], lru_b_a[l], lru_w_x[l], lru_b_x[l])
        lru = _lru(xr.reshape(b, s, lw), gr.reshape(b, s, lw), lru_conv_w[l], lru_conv_b[l][None],
                   wg, bg, lru_lambda[l], rows=256)
        xm, h2 = _outproj(x2, attn.reshape(b * s, -1), lru.reshape(b * s, -1),
                          w_out[l], ffn_norm_g[l][None], tm=1024)
        assert depth == 1
        x2 = _ffn(h2.reshape(b, s, d), xm.reshape(b, s, d), w_up_bf,
                  jnp.repeat(ffn_conv_w[l], SUBLANES, axis=0),
                  jnp.broadcast_to(ffn_conv_b[l][None], (SUBLANES, ffn_conv_b.shape[-1])),
                  w_down_bf, final_norm_g[None],
                  tile=512, chunk=256).reshape(b * s, d)
    return x2.reshape(b, s, d)
```

```python
import functools
import math

import jax
import jax.numpy as jnp
from jax import lax
from jax.experimental import pallas as pl
from jax.experimental.pallas import tpu as pltpu

F32 = jnp.float32
BF16 = jnp.bfloat16

N_HEADS = 4
HEAD_DIM = 64
V_DIM = 2 * HEAD_DIM
ATTN_WIDTH = N_HEADS * V_DIM
LRU_BLOCK = 64
LRU_CONV_WIDTH = 4
LRU_CONV_LEFT = 2
LRU_C = 8.0
FFN_CONV_LEFT = 1
NORM_EPS = 1e-6
LANES = 128
SUBLANES = 8
BF16_ROWS = 16
VMEM_LIMIT = 56 * 1024 * 1024


def _rms(x, g):
    return (x * lax.rsqrt(jnp.mean(x * x, axis=-1, keepdims=True) + NORM_EPS)) * g


def _dot(a, b):
    return jnp.dot(a, b, preferred_element_type=F32)


_NT = (((1,), (1,)), ((), ()))


def _inproj_kernel(x_ref, g_ref, w_ref, q_ref, k_ref, vt_ref, xr_ref, gr_ref, wb_ref, wvt_ref):
    aw = ATTN_WIDTH
    d = x_ref.shape[-1]

    @pl.when(pl.program_id(0) == 0)
    def _cast_weights():
        rows = 2 * LANES
        for r in range(0, d, rows):
            w = w_ref[r:r + rows, :]
            wb_ref[r:r + rows, 0:aw] = (w[:, 0:aw] * (HEAD_DIM ** -0.5)).astype(BF16)
            wb_ref[r:r + rows, aw:] = w[:, aw:].astype(BF16)
        for c in range(aw // LANES):
            cols = slice(2 * aw + c * LANES, 2 * aw + (c + 1) * LANES)
            wvt_ref[c * LANES:(c + 1) * LANES, :] = w_ref[:, cols].T.astype(BF16)

    hb = _rms(x_ref[...], g_ref[...]).astype(BF16)
    q_ref[...] = _dot(hb, wb_ref[:, 0:aw]).astype(BF16)
    k_ref[...] = _dot(hb, wb_ref[:, aw:2 * aw]).astype(BF16)
    vt_ref[...] = lax.dot_general(wvt_ref[...], hb, _NT, preferred_element_type=F32).astype(BF16)
    lw = xr_ref.shape[-1]
    xr_ref[...] = _dot(hb, wb_ref[:, 3 * aw:3 * aw + lw])
    gr_ref[...] = _dot(hb, wb_ref[:, 3 * aw + lw:3 * aw + 2 * lw])


def _inproj(x2, g, w, tm):
    n, d = x2.shape
    lw = (w.shape[1] - 3 * ATTN_WIDTH) // 2
    row = lambda i: (i, 0)
    const = lambda i: (0, 0)
    return pl.pallas_call(
        _inproj_kernel,
        grid=(n // tm,),
        in_specs=[pl.BlockSpec((tm, d), row),
                  pl.BlockSpec((1, d), const),
                  pl.BlockSpec(w.shape, const, pipeline_mode=pl.Buffered(1))],
        out_specs=[pl.BlockSpec((tm, ATTN_WIDTH), row)] * 2
        + [pl.BlockSpec((ATTN_WIDTH, tm), lambda i: (0, i))]
        + [pl.BlockSpec((tm, lw), row)] * 2,
        out_shape=[jax.ShapeDtypeStruct((n, ATTN_WIDTH), BF16)] * 2
        + [jax.ShapeDtypeStruct((ATTN_WIDTH, n), BF16)]
        + [jax.ShapeDtypeStruct((n, lw), F32)] * 2,
        scratch_shapes=[pltpu.VMEM(w.shape, BF16), pltpu.VMEM((ATTN_WIDTH, d), BF16)],
        compiler_params=pltpu.CompilerParams(
            dimension_semantics=("arbitrary",), vmem_limit_bytes=VMEM_LIMIT),
        name="inproj",
    )(x2, g, w)


def _attn_kernel(lq1_ref, lk1_ref, lq2_ref, lk2_ref, sg_ref, q_ref, k_ref, vt_ref, *rest,
                 tq, nsub, seq, lambda_init):
    nside = (len(rest) - 3 - 3 * nsub) // 2
    side_in, o_ref, side_out = rest[:nside], rest[nside], rest[nside + 1:2 * nside + 1]
    kf_ref, dist_ref, *se_refs = rest[2 * nside + 1:]
    for src, dst in zip(side_in, side_out):
        dst[...] = src[...].astype(dst.dtype)

    h = pl.program_id(1)
    qi = pl.program_id(2)
    s_refs, e_refs, m_refs = se_refs[:nsub], se_refs[nsub:2 * nsub], se_refs[2 * nsub:]
    nblk = seq // tq
    assert nblk & (nblk - 1) == 0 and 3 * nblk <= LANES and tq <= 256
    shift = nblk.bit_length() - 1

    @pl.when((pl.program_id(0) == 0) & (h == 0) & (qi == 0))
    def _init():
        lane = lax.broadcasted_iota(jnp.int32, (seq, LANES), 1)
        row = lax.broadcasted_iota(jnp.int32, (seq, LANES), 0)
        grp = lane >> shift
        hit = (row // tq) == (lane & (nblk - 1))
        dj = (row % tq).astype(F32)
        kf_ref[...] = jnp.where(hit & (grp < 2), 1.0,
                                jnp.where(hit & (grp == 2), dj, 0.0)).astype(BF16)
        r = lax.broadcasted_iota(jnp.int32, (tq, tq), 0)
        c = lax.broadcasted_iota(jnp.int32, (tq, tq), 1)
        dist_ref[...] = jnp.abs(r - c).astype(F32)

    lam = (jnp.exp(jnp.sum(lq1_ref[...] * lk1_ref[...], axis=-1, keepdims=True))
           - jnp.exp(jnp.sum(lq2_ref[...] * lk2_ref[...], axis=-1, keepdims=True))
           + lambda_init)
    slope = jnp.where(h == 0, 2.0 ** -2, jnp.where(h == 1, 2.0 ** -4,
                      jnp.where(h == 2, 2.0 ** -6, 2.0 ** -8))).astype(F32)

    kaug = jnp.concatenate([k_ref[...], kf_ref[...]], axis=1)
    vt_ones = jnp.concatenate([vt_ref[...], jnp.ones((BF16_ROWS, seq), BF16)], axis=0)
    lane = lax.broadcasted_iota(jnp.int32, (tq, LANES), 1)
    di = lax.broadcasted_iota(jnp.int32, (tq, LANES), 0).astype(F32)
    grp = lane >> shift
    strip = 4 * SUBLANES

    def scores(sb):
        blk = qi * nsub + sb
        diff = blk - (lane & (nblk - 1))
        sign = jnp.where(diff > 0, 1.0, jnp.where(diff < 0, -1.0, 0.0))
        qf = jnp.where(grp == 0, -slope * sign * di,
                       jnp.where(grp == 1, -slope * tq * jnp.abs(diff).astype(F32),
                                 jnp.where(grp == 2, slope * sign, 0.0))).astype(BF16)
        q = q_ref[sb * tq:(sb + 1) * tq, :]
        zero = jnp.zeros_like(q)
        qaug = jnp.concatenate(
            [jnp.concatenate([jnp.where(lane < HEAD_DIM, q, zero), qf], axis=1),
             jnp.concatenate([jnp.where(lane >= HEAD_DIM, q, zero), qf], axis=1)], axis=0)
        s_ref = s_refs[sb]
        s = lax.dot_general(kaug, qaug, _NT, preferred_element_type=F32)
        s_ref[...] = s
        mx = s[0:strip, :]
        for r in range(strip, seq, strip):
            mx = jnp.maximum(mx, s[r:r + strip, :])
        m_refs[sb][...] = mx
        diag = pl.ds(pl.multiple_of(blk * tq, tq), tq)
        diag_bias = slope * dist_ref[...]
        s_ref[diag, 0:tq] = s_ref[diag, 0:tq] - diag_bias
        s_ref[diag, tq:2 * tq] = s_ref[diag, tq:2 * tq] - diag_bias

    def finish(sb):
        s_ref, e_ref = s_refs[sb], e_refs[sb]
        mx = jnp.max(m_refs[sb][...], axis=0, keepdims=True)
        for r in range(0, seq, strip):
            e_ref[r:r + strip, :] = jnp.exp(s_ref[r:r + strip, :] - mx).astype(BF16)
        o12 = _dot(vt_ones, e_ref[...])
        norm = o12[V_DIM:V_DIM + 1, :]
        o12 = o12[0:V_DIM, :]
        o = o12[:, 0:tq] * (1.0 / norm[:, 0:tq]) - o12[:, tq:2 * tq] * (lam / norm[:, tq:2 * tq])
        o = o * lax.rsqrt(jnp.mean(o * o, axis=0, keepdims=True) + NORM_EPS)
        o = o * sg_ref[...] * (1.0 - lambda_init)
        o_ref[sb * tq:(sb + 1) * tq, :] = o.T.astype(o_ref.dtype)

    scores(0)
    for sb in range(nsub):
        if sb + 1 < nsub:
            scores(sb + 1)
        finish(sb)


def _attention(q, k, vt, lq1, lk1, lq2, lk2, subln_g, lambda_init, tq, nsub, cast_along=()):
    b, s, _ = q.shape
    vec = lambda bi, h, qi: (0, 0)
    tstep = tq * nsub
    nq = s // tstep
    nsteps = b * N_HEADS * nq

    def row_block(last):
        return lambda bi, h, qi: (jnp.minimum((bi * N_HEADS + h) * nq + qi, last), 0)

    side_in, side_out, side_shapes = [], [], []
    for w in cast_along:
        rows = -(-w.shape[0] // (nsteps * BF16_ROWS)) * BF16_ROWS
        while w.shape[0] % rows:
            rows += BF16_ROWS
        for specs in (side_in, side_out):
            specs.append(pl.BlockSpec((rows, w.shape[1]), row_block(w.shape[0] // rows - 1)))
        side_shapes.append(jax.ShapeDtypeStruct(w.shape, BF16))
    kernel = functools.partial(_attn_kernel, tq=tq, nsub=nsub, seq=s, lambda_init=lambda_init)
    return pl.pallas_call(
        kernel,
        grid=(b, N_HEADS, s // tstep),
        in_specs=[pl.BlockSpec((1, HEAD_DIM), vec)] * 4
        + [pl.BlockSpec((V_DIM, 1), vec),
           pl.BlockSpec((None, tstep, V_DIM), lambda bi, h, qi: (bi, qi, h)),
           pl.BlockSpec((None, s, V_DIM), lambda bi, h, qi: (bi, 0, h)),
           pl.BlockSpec((V_DIM, s), lambda bi, h, qi: (h, bi))] + side_in,
        out_specs=[pl.BlockSpec((None, tstep, V_DIM), lambda bi, h, qi: (bi, qi, h))] + side_out,
        out_shape=[jax.ShapeDtypeStruct((b, s, ATTN_WIDTH), BF16)] + side_shapes,
        scratch_shapes=[pltpu.VMEM((s, LANES), BF16), pltpu.VMEM((tq, tq), F32)]
        + [pltpu.VMEM((s, 2 * tq), F32)] * nsub + [pltpu.VMEM((s, 2 * tq), BF16)] * nsub
        + [pltpu.VMEM((4 * SUBLANES, 2 * tq), F32)] * nsub,
        compiler_params=pltpu.CompilerParams(
            dimension_semantics=("arbitrary",) * 3, vmem_limit_bytes=VMEM_LIMIT),
        name="attn",
    )(lq1, lk1, lq2, lk2, subln_g, q, k, vt, *cast_along)


def _local_scan(a, u, reverse):
    row = lax.broadcasted_iota(jnp.int32, a.shape, 0)
    for d in (1, 2, 4):
        shift = SUBLANES - d if reverse else d
        valid = (row < SUBLANES - d) if reverse else (row >= d)
        a_s = jnp.where(valid, pltpu.roll(a, shift, 0), 1.0)
        u_s = jnp.where(valid, pltpu.roll(u, shift, 0), 0.0)
        u = a * u_s + u
        a = a * a_s
    return a, u


def _lru_kernel(xr_ref, gr_ref, cw_ref, cb_ref, wg_ref, bg_ref, lam_ref, o_ref,
                xp_ref, xc_ref, gates_ref, h_ref, cum_ref, *, seq, rows):
    width = xr_ref.shape[-1]
    ntile = width // LANES
    grp = seq // SUBLANES
    left = LRU_CONV_LEFT
    right = LRU_CONV_WIDTH - 1 - LRU_CONV_LEFT
    top = left * SUBLANES
    sub = lax.broadcasted_iota(jnp.int32, (SUBLANES, LANES), 0)

    for s in range(SUBLANES):
        blk = xr_ref[s * grp:(s + 1) * grp, :]
        for p in range(ntile):
            xp_ref[p, pl.ds(top + s, grp, stride=SUBLANES), :] = blk[:, p * LANES:(p + 1) * LANES]
    for p in range(ntile):
        for k in range(left):
            src = xp_ref[p, top + (grp - 1 - k) * SUBLANES:top + (grp - k) * SUBLANES, :]
            xp_ref[p, top - (k + 1) * SUBLANES:top - k * SUBLANES, :] = jnp.where(
                sub == 0, 0.0, pltpu.roll(src, 1, 0))
        for k in range(right):
            src = xp_ref[p, top + k * SUBLANES:top + (k + 1) * SUBLANES, :]
            xp_ref[p, top + (grp + k) * SUBLANES:top + (grp + k + 1) * SUBLANES, :] = jnp.where(
                sub == SUBLANES - 1, 0.0, pltpu.roll(src, SUBLANES - 1, 0))

    neg_lam = -lam_ref[...]
    softplus = jnp.maximum(neg_lam, 0.0) + jnp.log1p(jnp.exp(-jnp.abs(neg_lam)))
    rate = LRU_C * softplus
    rate_log2 = -rate * math.log2(math.e)

    nchunk = seq // rows
    for c in range(nchunk):
        r0 = c * rows
        for p in range(ntile):
            cols = slice(p * LANES, (p + 1) * LANES)
            xc = cb_ref[:, cols]
            for tap in range(LRU_CONV_WIDTH):
                start = top + r0 + (tap - left) * SUBLANES
                xc = xc + xp_ref[p, start:start + rows, :] * cw_ref[tap:tap + 1, cols]
            xc_ref[p, r0:r0 + rows, :] = xc

    zero = jnp.zeros((SUBLANES, LANES), F32)
    one = jnp.ones((SUBLANES, LANES), F32)
    nvr = rows // SUBLANES
    tile8 = lambda row: jnp.broadcast_to(row, (SUBLANES, LANES))

    def chunk(c, ends):
        ends = [list(ends[2 * k:2 * k + 2]) for k in range(2 * ntile)]
        base = (pl.multiple_of(c * rows, rows), pl.multiple_of((nchunk - 1 - c) * rows, rows))
        for p in range(ntile):
            for d in range(2):
                gcols = slice(2 * d * LANES, (2 * d + 2) * LANES)
                xcb = xc_ref[p, pl.ds(base[d], rows), :].astype(BF16)
                gates_ref[2 * p + d] = _dot(xcb, wg_ref[p, :, gcols]) + bg_ref[p:p + 1, gcols]
        for step in range(nvr):
            for p in range(ntile):
                cols = slice(p * LANES, (p + 1) * LANES)
                for d in range(2):
                    j = step if d == 0 else nvr - 1 - step
                    row = pl.ds(base[d] + j * SUBLANES, SUBLANES)
                    g = gates_ref[2 * p + d, j * SUBLANES:(j + 1) * SUBLANES, :]
                    th = jnp.tanh(g)
                    r = 0.5 + 0.5 * th[:, 0:LANES]
                    i = 0.5 + 0.5 * th[:, LANES:2 * LANES]
                    t = jnp.tanh(r * tile8(rate[d:d + 1, cols]))
                    tt = t + t
                    prod = tt * (1.0 + t)
                    mult = jnp.where(prod > 0.0, tt * lax.rsqrt(prod), 0.0)
                    a = jnp.exp2(r * tile8(rate_log2[d:d + 1, cols]))
                    u = mult * (i * xc_ref[p, row, :])
                    h, cum = ends[2 * p + d]
                    h = a * h + u
                    cum = a * cum
                    h_ref[d, p, row, :] = h
                    cum_ref[d, p, row, :] = cum
                    ends[2 * p + d] = [h, cum]
        return tuple(x for pair in ends for x in pair)

    flat = lax.fori_loop(0, nchunk, chunk, (zero, one) * (2 * ntile))
    ends = [[(flat[2 * (2 * p + d)], flat[2 * (2 * p + d) + 1]) for d in range(2)]
            for p in range(ntile)]

    enter = []
    for p in range(ntile):
        for d in range(2):
            h_end, cum_end = ends[p][d]
            _, chained = _local_scan(cum_end, h_end, reverse=(d == 1))
            if d == 0:
                enter.append(jnp.where(sub == 0, 0.0, pltpu.roll(chained, 1, 0)))
            else:
                enter.append(jnp.where(sub == SUBLANES - 1, 0.0,
                                       pltpu.roll(chained, SUBLANES - 1, 0)))

    for p in range(ntile):
        for j in range(grp):
            rs = slice(j * SUBLANES, (j + 1) * SUBLANES)
            xc_ref[p, rs, :] = ((h_ref[0, p, rs, :] + cum_ref[0, p, rs, :] * enter[2 * p])
                                + (h_ref[1, p, rs, :] + cum_ref[1, p, rs, :] * enter[2 * p + 1]))

    blk = BF16_ROWS

    def segment(s, carry):
        for p in range(ntile):
            cols = slice(p * LANES, (p + 1) * LANES)
            for j0 in range(0, grp, blk):
                rs = pl.ds(pl.multiple_of(s * grp + j0, blk), blk)
                y = xc_ref[p, pl.ds(s + j0 * SUBLANES, blk, stride=SUBLANES), :]
                o_ref[rs, cols] = (jax.nn.gelu(gr_ref[rs, cols], approximate=True)
                                   * y).astype(o_ref.dtype)
        return carry

    lax.fori_loop(0, SUBLANES, segment, 0)


def _lru(xr, gr, conv_w, conv_b, wg, bg, lru_lambda, rows):
    b, s, width = xr.shape
    ntile = width // LANES
    seqblk = pl.BlockSpec((None, s, width), lambda bi: (bi, 0, 0))
    full = lambda a: pl.BlockSpec(a.shape, lambda bi: (0,) * a.ndim)
    kernel = functools.partial(_lru_kernel, seq=s, rows=rows)
    halo_rows = (LRU_CONV_WIDTH - 1) * SUBLANES
    return pl.pallas_call(
        kernel,
        grid=(b,),
        in_specs=[seqblk, seqblk, full(conv_w), full(conv_b), full(wg), full(bg), full(lru_lambda)],
        out_specs=seqblk,
        out_shape=jax.ShapeDtypeStruct((b, s, width), BF16),
        scratch_shapes=[pltpu.VMEM((ntile, s + halo_rows, LANES), F32),
                        pltpu.VMEM((ntile, s, LANES), F32),
                        pltpu.VMEM((2 * ntile, rows, 2 * LANES), F32),
                        pltpu.VMEM((2, ntile, s, LANES), F32),
                        pltpu.VMEM((2, ntile, s, LANES), F32)],
        compiler_params=pltpu.CompilerParams(
            dimension_semantics=("arbitrary",), vmem_limit_bytes=VMEM_LIMIT),
        name="lru",
    )(xr, gr, conv_w, conv_b, wg, bg, lru_lambda)


def _outproj_kernel(x_ref, a_ref, l_ref, w_ref, g_ref, xm_ref, h_ref, wb_ref):
    @pl.when(pl.program_id(0) == 0)
    def _cast_weights():
        wb_ref[...] = w_ref[...].astype(BF16)

    aw = a_ref.shape[-1]
    xm = x_ref[...] + _dot(a_ref[...], wb_ref[0:aw, :]) + _dot(l_ref[...], wb_ref[aw:, :])
    xm_ref[...] = xm
    h_ref[...] = _rms(xm, g_ref[...]).astype(h_ref.dtype)


def _outproj(x2, attn2, lru2, w, g, tm):
    n, d = x2.shape
    row = lambda i: (i, 0)
    const = lambda i: (0, 0)
    return pl.pallas_call(
        _outproj_kernel,
        grid=(n // tm,),
        in_specs=[pl.BlockSpec((tm, d), row),
                  pl.BlockSpec((tm, attn2.shape[1]), row),
                  pl.BlockSpec((tm, lru2.shape[1]), row),
                  pl.BlockSpec(w.shape, const, pipeline_mode=pl.Buffered(1)),
                  pl.BlockSpec((1, d), const)],
        out_specs=[pl.BlockSpec((tm, d), row)] * 2,
        out_shape=[jax.ShapeDtypeStruct((n, d), F32), jax.ShapeDtypeStruct((n, d), BF16)],
        scratch_shapes=[pltpu.VMEM(w.shape, BF16)],
        compiler_params=pltpu.CompilerParams(
            dimension_semantics=("arbitrary",), vmem_limit_bytes=VMEM_LIMIT),
        name="outproj",
    )(x2, attn2, lru2, w, g)


def _ffn_kernel(hp_ref, hm_ref, hn_ref, xm_ref, wup_ref, cw_ref, cb_ref, wdn_ref, fg_ref, o_ref,
                perm_ref, hext_ref, ua_ref, ub_ref, acta_ref, actb_ref, acc_ref,
                *, tile, chunk, d_ff):
    i = pl.program_id(1)
    nchunks = d_ff // chunk
    ngrp = tile // SUBLANES
    nslab = perm_ref.shape[0]
    d = nslab * LANES

    for s in range(SUBLANES):
        rows = hm_ref[s * ngrp:(s + 1) * ngrp, :].astype(F32)
        for n in range(nslab):
            perm_ref[n, pl.ds(s, ngrp, stride=SUBLANES), :] = rows[:, n * LANES:(n + 1) * LANES]
    for n in range(nslab):
        hext_ref[0:tile, n * LANES:(n + 1) * LANES] = perm_ref[n].astype(BF16)
    prev = jnp.where(i > 0, hp_ref[BF16_ROWS - 1:BF16_ROWS, :].astype(F32), 0.0)
    nxt = jnp.where(i < pl.num_programs(1) - 1, hn_ref[0:1, :].astype(F32), 0.0)
    hrow = lax.broadcasted_iota(jnp.int32, (BF16_ROWS, d), 0)
    halo = jnp.where(hrow == 0, prev, jnp.where(hrow == 1, nxt, 0.0))
    hext_ref[tile:tile + BF16_ROWS, :] = halo.astype(BF16)
    acc_ref[...] = jnp.zeros_like(acc_ref)

    def offsets(c):
        og, ov = c * chunk, d_ff + c * chunk
        if isinstance(c, int):
            return og, ov
        return pl.multiple_of(og, chunk), pl.multiple_of(ov, chunk)

    def up(c, u_ref):
        og, ov = offsets(c)
        hext = hext_ref[...]
        u_ref[:, 0:chunk] = _dot(hext, wup_ref[:, pl.ds(og, chunk)])
        u_ref[:, chunk:2 * chunk] = _dot(hext, wup_ref[:, pl.ds(ov, chunk)])

    def glu(c, u_ref, act_ref):
        sub = lax.broadcasted_iota(jnp.int32, (SUBLANES, chunk), 0)
        blk = BF16_ROWS

        def conv(col0, off, r0):
            cols = slice(col0, col0 + chunk)
            reps = (blk // SUBLANES, 1)
            w = [jnp.tile(cw_ref[k * SUBLANES:(k + 1) * SUBLANES, pl.ds(off, chunk)], reps)
                 for k in range(3)]
            bias = jnp.tile(cb_ref[:, pl.ds(off, chunk)], reps)
            cur = u_ref[r0:r0 + blk, cols]
            if r0 == 0:
                first = jnp.where(sub == 0, u_ref[tile:tile + 1, cols],
                                  pltpu.roll(u_ref[tile - SUBLANES:tile, cols], 1, 0))
                um1 = jnp.concatenate([first, cur[0:blk - SUBLANES]], axis=0)
            else:
                um1 = u_ref[r0 - SUBLANES:r0 + blk - SUBLANES, cols]
            if r0 + blk == tile:
                last = jnp.where(sub == SUBLANES - 1, u_ref[tile + 1:tile + 2, cols],
                                 pltpu.roll(u_ref[0:SUBLANES, cols], SUBLANES - 1, 0))
                up1 = jnp.concatenate([cur[SUBLANES:blk], last], axis=0)
            else:
                up1 = u_ref[r0 + SUBLANES:r0 + blk + SUBLANES, cols]
            return bias + um1 * w[0] + cur * w[1] + up1 * w[2]

        og, ov = offsets(c)
        for r0 in range(0, tile, blk):
            gate = conv(0, og, r0)
            val = conv(chunk, ov, r0)
            act_ref[r0:r0 + blk, :] = (jax.nn.gelu(gate, approximate=True) * val).astype(BF16)

    def down(c, act_ref):
        og, _ = offsets(c)
        y = _dot(act_ref[...], wdn_ref[pl.ds(og, chunk), :])
        for n in range(nslab):
            acc_ref[n, 0:tile, :] += y[:, n * LANES:(n + 1) * LANES]

    assert nchunks % 2 == 1 and nchunks >= 3
    up(0, ua_ref)

    def pair(j, carry):
        c = 2 * j
        up(c + 1, ub_ref)
        glu(c, ua_ref, acta_ref)
        down(c, acta_ref)
        up(c + 2, ua_ref)
        glu(c + 1, ub_ref, actb_ref)
        down(c + 1, actb_ref)
        return carry

    for j in range(nchunks // 2):
        pair(j, 0)
    glu(nchunks - 1, ua_ref, acta_ref)
    down(nchunks - 1, acta_ref)
    for s in range(SUBLANES):
        rs = slice(s * ngrp, (s + 1) * ngrp)
        y = jnp.concatenate([acc_ref[n, pl.ds(s, ngrp, stride=SUBLANES), :] for n in range(nslab)],
                            axis=1)
        o_ref[rs, :] = _rms(xm_ref[rs, :] + y, fg_ref[...])


def _ffn(h2, xm, w_up, conv_w, conv_b, w_down, final_g, tile, chunk):
    b, s, d = h2.shape
    d_ff = w_down.shape[0]
    nh = tile // BF16_ROWS
    last_halo = s // BF16_ROWS - 1
    main = lambda bi, i: (bi, i, 0)
    const = lambda bi, i: (0, 0)
    single = dict(pipeline_mode=pl.Buffered(1))
    kernel = functools.partial(_ffn_kernel, tile=tile, chunk=chunk, d_ff=d_ff)
    return pl.pallas_call(
        kernel,
        grid=(b, s // tile),
        in_specs=[pl.BlockSpec((None, BF16_ROWS, d), lambda bi, i: (bi, jnp.maximum(i * nh - 1, 0), 0)),
                  pl.BlockSpec((None, tile, d), main),
                  pl.BlockSpec((None, BF16_ROWS, d),
                               lambda bi, i: (bi, jnp.minimum((i + 1) * nh, last_halo), 0)),
                  pl.BlockSpec((None, tile, d), main),
                  pl.BlockSpec(w_up.shape, const, **single),
                  pl.BlockSpec(conv_w.shape, const),
                  pl.BlockSpec(conv_b.shape, const),
                  pl.BlockSpec(w_down.shape, const, **single),
                  pl.BlockSpec((1, d), const)],
        out_specs=pl.BlockSpec((None, tile, d), main),
        out_shape=jax.ShapeDtypeStruct((b, s, d), F32),
        scratch_shapes=[pltpu.VMEM((d // LANES, tile, LANES), F32),
                        pltpu.VMEM((tile + BF16_ROWS, d), BF16)]
        + [pltpu.VMEM((tile + BF16_ROWS, 2 * chunk), F32)] * 2
        + [pltpu.VMEM((tile, chunk), BF16)] * 2
        + [pltpu.VMEM((d // LANES, tile + SUBLANES, LANES), F32)],
        compiler_params=pltpu.CompilerParams(
            dimension_semantics=("arbitrary",) * 2, vmem_limit_bytes=VMEM_LIMIT),
        name="ffn",
    )(h2, h2, h2, xm, w_up, conv_w, conv_b, w_down, final_g)


def _gate_weights(w_a, b_a, w_x, b_x):
    ndir, nblk, bd, _ = w_a.shape
    per = LANES // bd
    npair = nblk // per

    def blockdiag(w):
        w = w.reshape(npair, per, bd, bd)
        eye = jnp.eye(per, dtype=w.dtype)
        return jnp.einsum('pbij,bc->pbicj', w, eye).reshape(npair, LANES, LANES)

    ws, bs = [], []
    for d in range(ndir):
        for w, bias in ((w_a, b_a), (w_x, b_x)):
            ws.append(blockdiag(w[d]))
            bs.append(bias[d].reshape(npair, LANES))
    return ((0.5 * jnp.concatenate(ws, axis=-1)).astype(BF16),
            (0.5 * jnp.concatenate(bs, axis=-1)).astype(F32))


def kernel(x, attn_norm_g, w_in, lambda_q1, lambda_k1, lambda_q2, lambda_k2, subln_g,
           lru_conv_w, lru_conv_b, lru_w_a, lru_b_a, lru_w_x, lru_b_x, lru_lambda,
           w_out, ffn_norm_g, w_up, ffn_conv_w, ffn_conv_b, w_down, final_norm_g):
    b, s, d = x.shape
    depth = w_in.shape[0]
    x2 = x.reshape(b * s, d)
    for l in range(depth):
        lambda_init = 0.8 - 0.6 * math.exp(-0.3 * l)
        q, k, vt, xr, gr = _inproj(x2, attn_norm_g[l][None], w_in[l], tm=512)
        lw = xr.shape[-1]
        attn, w_up_bf, w_down_bf = _attention(
            q.reshape(b, s, -1), k.reshape(b, s, -1), vt,
            lambda_q1[l][None], lambda_k1[l][None], lambda_q2[l][None], lambda_k2[l][None],
            subln_g[l][:, None], lambda_init, tq=256, nsub=4, cast_along=(w_up[l], w_down[l]))
        wg, bg = _gate_weights(lru_w_a[l], lru_b_a[l], lru_w_x[l], lru_b_x[l])
        lru = _lru(xr.reshape(b, s, lw), gr.reshape(b, s, lw), lru_conv_w[l], lru_conv_b[l][None],
                   wg, bg, lru_lambda[l], rows=256)
        xm, h2 = _outproj(x2, attn.reshape(b * s, -1), lru.reshape(b * s, -1),
                          w_out[l], ffn_norm_g[l][None], tm=1024)
        assert depth == 1
        x2 = _ffn(h2.reshape(b, s, d), xm.reshape(b, s, d), w_up_bf,
                  jnp.repeat(ffn_conv_w[l], SUBLANES, axis=0),
                  jnp.broadcast_to(ffn_conv_b[l][None], (SUBLANES, ffn_conv_b.shape[-1])),
                  w_down_bf, final_norm_g[None],
                  tile=512, chunk=256).reshape(b * s, d)
    return x2.reshape(b, s, d)
```

```python
import functools
import math

import jax
import jax.numpy as jnp
from jax import lax
from jax.experimental import pallas as pl
from jax.experimental.pallas import tpu as pltpu

F32 = jnp.float32
BF16 = jnp.bfloat16

N_HEADS = 4
HEAD_DIM = 64
V_DIM = 2 * HEAD_DIM
ATTN_WIDTH = N_HEADS * V_DIM
LRU_BLOCK = 64
LRU_CONV_WIDTH = 4
LRU_CONV_LEFT = 2
LRU_C = 8.0
FFN_CONV_LEFT = 1
NORM_EPS = 1e-6
LANES = 128
SUBLANES = 8
BF16_ROWS = 16
VMEM_LIMIT = 56 * 1024 * 1024
FFN_TILE = 512


def _rms(x, g):
    return (x * lax.rsqrt(jnp.mean(x * x, axis=-1, keepdims=True) + NORM_EPS)) * g


def _dot(a, b):
    return jnp.dot(a, b, preferred_element_type=F32)


_NT = (((1,), (1,)), ((), ()))


def _inproj_kernel(x_ref, g_ref, w_ref, q_ref, k_ref, vt_ref, xr_ref, gr_ref, wb_ref, wvt_ref):
    aw = ATTN_WIDTH
    d = x_ref.shape[-1]

    @pl.when(pl.program_id(0) == 0)
    def _cast_weights():
        rows = 2 * LANES
        for r in range(0, d, rows):
            w = w_ref[r:r + rows, :]
            wb_ref[r:r + rows, 0:aw] = (w[:, 0:aw] * (HEAD_DIM ** -0.5)).astype(BF16)
            wb_ref[r:r + rows, aw:] = w[:, aw:].astype(BF16)
        for c in range(aw // LANES):
            cols = slice(2 * aw + c * LANES, 2 * aw + (c + 1) * LANES)
            wvt_ref[c * LANES:(c + 1) * LANES, :] = w_ref[:, cols].T.astype(BF16)

    hb = _rms(x_ref[...], g_ref[...]).astype(BF16)
    q_ref[...] = _dot(hb, wb_ref[:, 0:aw]).astype(BF16)
    k_ref[...] = _dot(hb, wb_ref[:, aw:2 * aw]).astype(BF16)
    vt_ref[...] = lax.dot_general(wvt_ref[...], hb, _NT, preferred_element_type=F32).astype(BF16)
    lw = xr_ref.shape[-1]
    xr_ref[...] = _dot(hb, wb_ref[:, 3 * aw:3 * aw + lw])
    gr_ref[...] = _dot(hb, wb_ref[:, 3 * aw + lw:3 * aw + 2 * lw])


def _inproj(x2, g, w, tm):
    n, d = x2.shape
    lw = (w.shape[1] - 3 * ATTN_WIDTH) // 2
    row = lambda i: (i, 0)
    const = lambda i: (0, 0)
    return pl.pallas_call(
        _inproj_kernel,
        grid=(n // tm,),
        in_specs=[pl.BlockSpec((tm, d), row),
                  pl.BlockSpec((1, d), const),
                  pl.BlockSpec(w.shape, const, pipeline_mode=pl.Buffered(1))],
        out_specs=[pl.BlockSpec((tm, ATTN_WIDTH), row)] * 2
        + [pl.BlockSpec((ATTN_WIDTH, tm), lambda i: (0, i))]
        + [pl.BlockSpec((tm, lw), row)] * 2,
        out_shape=[jax.ShapeDtypeStruct((n, ATTN_WIDTH), BF16)] * 2
        + [jax.ShapeDtypeStruct((ATTN_WIDTH, n), BF16)]
        + [jax.ShapeDtypeStruct((n, lw), F32)] * 2,
        scratch_shapes=[pltpu.VMEM(w.shape, BF16), pltpu.VMEM((ATTN_WIDTH, d), BF16)],
        compiler_params=pltpu.CompilerParams(
            dimension_semantics=("arbitrary",), vmem_limit_bytes=VMEM_LIMIT),
        name="inproj",
    )(x2, g, w)


def _attn_kernel(lq1_ref, lk1_ref, lq2_ref, lk2_ref, sg_ref, q_ref, k_ref, vt_ref, *rest,
                 tq, nsub, seq, lambda_init):
    nside = (len(rest) - 3 - 3 * nsub) // 2
    side_in, o_ref, side_out = rest[:nside], rest[nside], rest[nside + 1:2 * nside + 1]
    kf_ref, dist_ref, *se_refs = rest[2 * nside + 1:]
    for src, dst in zip(side_in, side_out):
        dst[...] = src[...].astype(dst.dtype)

    h = pl.program_id(1)
    qi = pl.program_id(2)
    s_refs, e_refs, m_refs = se_refs[:nsub], se_refs[nsub:2 * nsub], se_refs[2 * nsub:]
    nblk = seq // tq
    assert nblk & (nblk - 1) == 0 and 3 * nblk <= LANES and tq <= 256
    shift = nblk.bit_length() - 1

    @pl.when((pl.program_id(0) == 0) & (h == 0) & (qi == 0))
    def _init():
        lane = lax.broadcasted_iota(jnp.int32, (seq, LANES), 1)
        row = lax.broadcasted_iota(jnp.int32, (seq, LANES), 0)
        grp = lane >> shift
        hit = (row // tq) == (lane & (nblk - 1))
        dj = (row % tq).astype(F32)
        kf_ref[...] = jnp.where(hit & (grp < 2), 1.0,
                                jnp.where(hit & (grp == 2), dj, 0.0)).astype(BF16)
        r = lax.broadcasted_iota(jnp.int32, (tq, tq), 0)
        c = lax.broadcasted_iota(jnp.int32, (tq, tq), 1)
        dist_ref[...] = jnp.abs(r - c).astype(F32)

    lam = (jnp.exp(jnp.sum(lq1_ref[...] * lk1_ref[...], axis=-1, keepdims=True))
           - jnp.exp(jnp.sum(lq2_ref[...] * lk2_ref[...], axis=-1, keepdims=True))
           + lambda_init)
    slope = jnp.where(h == 0, 2.0 ** -2, jnp.where(h == 1, 2.0 ** -4,
                      jnp.where(h == 2, 2.0 ** -6, 2.0 ** -8))).astype(F32)

    kaug = jnp.concatenate([k_ref[...], kf_ref[...]], axis=1)
    vt_ones = jnp.concatenate([vt_ref[...], jnp.ones((BF16_ROWS, seq), BF16)], axis=0)
    lane = lax.broadcasted_iota(jnp.int32, (tq, LANES), 1)
    di = lax.broadcasted_iota(jnp.int32, (tq, LANES), 0).astype(F32)
    grp = lane >> shift
    strip = 4 * SUBLANES

    def scores(sb):
        blk = qi * nsub + sb
        diff = blk - (lane & (nblk - 1))
        sign = jnp.where(diff > 0, 1.0, jnp.where(diff < 0, -1.0, 0.0))
        qf = jnp.where(grp == 0, -slope * sign * di,
                       jnp.where(grp == 1, -slope * tq * jnp.abs(diff).astype(F32),
                                 jnp.where(grp == 2, slope * sign, 0.0))).astype(BF16)
        q = q_ref[sb * tq:(sb + 1) * tq, :]
        zero = jnp.zeros_like(q)
        qaug = jnp.concatenate(
            [jnp.concatenate([jnp.where(lane < HEAD_DIM, q, zero), qf], axis=1),
             jnp.concatenate([jnp.where(lane >= HEAD_DIM, q, zero), qf], axis=1)], axis=0)
        s_ref = s_refs[sb]
        s = lax.dot_general(kaug, qaug, _NT, preferred_element_type=F32)
        s_ref[...] = s
        mx = s[0:strip, :]
        for r in range(strip, seq, strip):
            mx = jnp.maximum(mx, s[r:r + strip, :])
        m_refs[sb][...] = mx
        diag = pl.ds(pl.multiple_of(blk * tq, tq), tq)
        diag_bias = slope * dist_ref[...]
        s_ref[diag, 0:tq] = s_ref[diag, 0:tq] - diag_bias
        s_ref[diag, tq:2 * tq] = s_ref[diag, tq:2 * tq] - diag_bias

    def finish(sb):
        s_ref, e_ref = s_refs[sb], e_refs[sb]
        mx = jnp.max(m_refs[sb][...], axis=0, keepdims=True)
        for r in range(0, seq, strip):
            e_ref[r:r + strip, :] = jnp.exp(s_ref[r:r + strip, :] - mx).astype(BF16)
        o12 = _dot(vt_ones, e_ref[...])
        norm = o12[V_DIM:V_DIM + 1, :]
        o12 = o12[0:V_DIM, :]
        o = o12[:, 0:tq] * (1.0 / norm[:, 0:tq]) - o12[:, tq:2 * tq] * (lam / norm[:, tq:2 * tq])
        o = o * lax.rsqrt(jnp.mean(o * o, axis=0, keepdims=True) + NORM_EPS)
        o = o * sg_ref[...] * (1.0 - lambda_init)
        o_ref[sb * tq:(sb + 1) * tq, :] = o.T.astype(o_ref.dtype)

    scores(0)
    for sb in range(nsub):
        if sb + 1 < nsub:
            scores(sb + 1)
        finish(sb)


def _attention(q, k, vt, lq1, lk1, lq2, lk2, subln_g, lambda_init, tq, nsub, cast_along=()):
    b, s, _ = q.shape
    vec = lambda bi, h, qi: (0, 0)
    tstep = tq * nsub
    nq = s // tstep
    nsteps = b * N_HEADS * nq

    def row_block(last):
        return lambda bi, h, qi: (jnp.minimum((bi * N_HEADS + h) * nq + qi, last), 0)

    side_in, side_out, side_shapes = [], [], []
    for w in cast_along:
        rows = -(-w.shape[0] // (nsteps * BF16_ROWS)) * BF16_ROWS
        while w.shape[0] % rows:
            rows += BF16_ROWS
        for specs in (side_in, side_out):
            specs.append(pl.BlockSpec((rows, w.shape[1]), row_block(w.shape[0] // rows - 1)))
        side_shapes.append(jax.ShapeDtypeStruct(w.shape, BF16))
    kernel = functools.partial(_attn_kernel, tq=tq, nsub=nsub, seq=s, lambda_init=lambda_init)
    return pl.pallas_call(
        kernel,
        grid=(b, N_HEADS, s // tstep),
        in_specs=[pl.BlockSpec((1, HEAD_DIM), vec)] * 4
        + [pl.BlockSpec((V_DIM, 1), vec),
           pl.BlockSpec((None, tstep, V_DIM), lambda bi, h, qi: (bi, qi, h)),
           pl.BlockSpec((None, s, V_DIM), lambda bi, h, qi: (bi, 0, h)),
           pl.BlockSpec((V_DIM, s), lambda bi, h, qi: (h, bi))] + side_in,
        out_specs=[pl.BlockSpec((None, tstep, V_DIM), lambda bi, h, qi: (bi, qi, h))] + side_out,
        out_shape=[jax.ShapeDtypeStruct((b, s, ATTN_WIDTH), BF16)] + side_shapes,
        scratch_shapes=[pltpu.VMEM((s, LANES), BF16), pltpu.VMEM((tq, tq), F32)]
        + [pltpu.VMEM((s, 2 * tq), F32)] * nsub + [pltpu.VMEM((s, 2 * tq), BF16)] * nsub
        + [pltpu.VMEM((4 * SUBLANES, 2 * tq), F32)] * nsub,
        compiler_params=pltpu.CompilerParams(
            dimension_semantics=("arbitrary",) * 3, vmem_limit_bytes=VMEM_LIMIT),
        name="attn",
    )(lq1, lk1, lq2, lk2, subln_g, q, k, vt, *cast_along)


def _local_scan(a, u, reverse):
    row = lax.broadcasted_iota(jnp.int32, a.shape, 0)
    for d in (1, 2, 4):
        shift = SUBLANES - d if reverse else d
        valid = (row < SUBLANES - d) if reverse else (row >= d)
        a_s = jnp.where(valid, pltpu.roll(a, shift, 0), 1.0)
        u_s = jnp.where(valid, pltpu.roll(u, shift, 0), 0.0)
        u = a * u_s + u
        a = a * a_s
    return a, u


def _lru_kernel(xr_ref, gr_ref, cw_ref, cb_ref, wg_ref, bg_ref, lam_ref, o_ref,
                xp_ref, xc_ref, gates_ref, h_ref, cum_ref, *, seq, rows):
    width = xr_ref.shape[-1]
    ntile = width // LANES
    grp = seq // SUBLANES
    left = LRU_CONV_LEFT
    right = LRU_CONV_WIDTH - 1 - LRU_CONV_LEFT
    top = left * SUBLANES
    sub = lax.broadcasted_iota(jnp.int32, (SUBLANES, LANES), 0)

    for s in range(SUBLANES):
        blk = xr_ref[s * grp:(s + 1) * grp, :]
        for p in range(ntile):
            xp_ref[p, pl.ds(top + s, grp, stride=SUBLANES), :] = blk[:, p * LANES:(p + 1) * LANES]
    for p in range(ntile):
        for k in range(left):
            src = xp_ref[p, top + (grp - 1 - k) * SUBLANES:top + (grp - k) * SUBLANES, :]
            xp_ref[p, top - (k + 1) * SUBLANES:top - k * SUBLANES, :] = jnp.where(
                sub == 0, 0.0, pltpu.roll(src, 1, 0))
        for k in range(right):
            src = xp_ref[p, top + k * SUBLANES:top + (k + 1) * SUBLANES, :]
            xp_ref[p, top + (grp + k) * SUBLANES:top + (grp + k + 1) * SUBLANES, :] = jnp.where(
                sub == SUBLANES - 1, 0.0, pltpu.roll(src, SUBLANES - 1, 0))

    neg_lam = -lam_ref[...]
    softplus = jnp.maximum(neg_lam, 0.0) + jnp.log1p(jnp.exp(-jnp.abs(neg_lam)))
    rate = LRU_C * softplus
    rate_log2 = -rate * math.log2(math.e)

    nchunk = seq // rows
    for c in range(nchunk):
        r0 = c * rows
        for p in range(ntile):
            cols = slice(p * LANES, (p + 1) * LANES)
            xc = cb_ref[:, cols]
            for tap in range(LRU_CONV_WIDTH):
                start = top + r0 + (tap - left) * SUBLANES
                xc = xc + xp_ref[p, start:start + rows, :] * cw_ref[tap:tap + 1, cols]
            xc_ref[p, r0:r0 + rows, :] = xc

    zero = jnp.zeros((SUBLANES, LANES), F32)
    one = jnp.ones((SUBLANES, LANES), F32)
    nvr = rows // SUBLANES
    tile8 = lambda row: jnp.broadcast_to(row, (SUBLANES, LANES))

    def chunk(c, ends):
        ends = [list(ends[2 * k:2 * k + 2]) for k in range(2 * ntile)]
        base = (pl.multiple_of(c * rows, rows), pl.multiple_of((nchunk - 1 - c) * rows, rows))
        for p in range(ntile):
            for d in range(2):
                gcols = slice(2 * d * LANES, (2 * d + 2) * LANES)
                xcb = xc_ref[p, pl.ds(base[d], rows), :].astype(BF16)
                gates_ref[2 * p + d] = _dot(xcb, wg_ref[p, :, gcols]) + bg_ref[p:p + 1, gcols]
        for step in range(nvr):
            for p in range(ntile):
                cols = slice(p * LANES, (p + 1) * LANES)
                for d in range(2):
                    j = step if d == 0 else nvr - 1 - step
                    row = pl.ds(base[d] + j * SUBLANES, SUBLANES)
                    g = gates_ref[2 * p + d, j * SUBLANES:(j + 1) * SUBLANES, :]
                    th = jnp.tanh(g)
                    r = 0.5 + 0.5 * th[:, 0:LANES]
                    i = 0.5 + 0.5 * th[:, LANES:2 * LANES]
                    t = jnp.tanh(r * tile8(rate[d:d + 1, cols]))
                    tt = t + t
                    prod = tt * (1.0 + t)
                    mult = jnp.where(prod > 0.0, tt * lax.rsqrt(prod), 0.0)
                    a = jnp.exp2(r * tile8(rate_log2[d:d + 1, cols]))
                    u = mult * (i * xc_ref[p, row, :])
                    h, cum = ends[2 * p + d]
                    h = a * h + u
                    cum = a * cum
                    h_ref[d, p, row, :] = h
                    cum_ref[d, p, row, :] = cum
                    ends[2 * p + d] = [h, cum]
        return tuple(x for pair in ends for x in pair)

    flat = lax.fori_loop(0, nchunk, chunk, (zero, one) * (2 * ntile))
    ends = [[(flat[2 * (2 * p + d)], flat[2 * (2 * p + d) + 1]) for d in range(2)]
            for p in range(ntile)]

    enter = []
    for p in range(ntile):
        for d in range(2):
            h_end, cum_end = ends[p][d]
            _, chained = _local_scan(cum_end, h_end, reverse=(d == 1))
            if d == 0:
                enter.append(jnp.where(sub == 0, 0.0, pltpu.roll(chained, 1, 0)))
            else:
                enter.append(jnp.where(sub == SUBLANES - 1, 0.0,
                                       pltpu.roll(chained, SUBLANES - 1, 0)))

    for p in range(ntile):
        for j in range(grp):
            rs = slice(j * SUBLANES, (j + 1) * SUBLANES)
            xc_ref[p, rs, :] = ((h_ref[0, p, rs, :] + cum_ref[0, p, rs, :] * enter[2 * p])
                                + (h_ref[1, p, rs, :] + cum_ref[1, p, rs, :] * enter[2 * p + 1]))

    blk = BF16_ROWS

    def segment(s, carry):
        for p in range(ntile):
            cols = slice(p * LANES, (p + 1) * LANES)
            for j0 in range(0, grp, blk):
                rs = pl.ds(pl.multiple_of(s * grp + j0, blk), blk)
                y = xc_ref[p, pl.ds(s + j0 * SUBLANES, blk, stride=SUBLANES), :]
                o_ref[rs, cols] = (jax.nn.gelu(gr_ref[rs, cols], approximate=True)
                                   * y).astype(o_ref.dtype)
        return carry

    lax.fori_loop(0, SUBLANES, segment, 0)


def _lru(xr, gr, conv_w, conv_b, wg, bg, lru_lambda, rows):
    b, s, width = xr.shape
    ntile = width // LANES
    seqblk = pl.BlockSpec((None, s, width), lambda bi: (bi, 0, 0))
    full = lambda a: pl.BlockSpec(a.shape, lambda bi: (0,) * a.ndim)
    kernel = functools.partial(_lru_kernel, seq=s, rows=rows)
    halo_rows = (LRU_CONV_WIDTH - 1) * SUBLANES
    return pl.pallas_call(
        kernel,
        grid=(b,),
        in_specs=[seqblk, seqblk, full(conv_w), full(conv_b), full(wg), full(bg), full(lru_lambda)],
        out_specs=seqblk,
        out_shape=jax.ShapeDtypeStruct((b, s, width), BF16),
        scratch_shapes=[pltpu.VMEM((ntile, s + halo_rows, LANES), F32),
                        pltpu.VMEM((ntile, s, LANES), F32),
                        pltpu.VMEM((2 * ntile, rows, 2 * LANES), F32),
                        pltpu.VMEM((2, ntile, s, LANES), F32),
                        pltpu.VMEM((2, ntile, s, LANES), F32)],
        compiler_params=pltpu.CompilerParams(
            dimension_semantics=("arbitrary",), vmem_limit_bytes=VMEM_LIMIT),
        name="lru",
    )(xr, gr, conv_w, conv_b, wg, bg, lru_lambda)


def _outproj_kernel(x_ref, a_ref, l_ref, w_ref, g_ref, xm_ref, h_ref, wb_ref, perm_ref, *, ptile):
    @pl.when(pl.program_id(0) == 0)
    def _cast_weights():
        wb_ref[...] = w_ref[...].astype(BF16)

    aw = a_ref.shape[-1]
    xm = x_ref[...] + _dot(a_ref[...], wb_ref[0:aw, :]) + _dot(l_ref[...], wb_ref[aw:, :])
    xm_ref[...] = xm
    ngrp = ptile // SUBLANES
    nslab = perm_ref.shape[0]
    for t in range(x_ref.shape[0] // ptile):
        for s in range(SUBLANES):
            r0 = t * ptile + s * ngrp
            h = _rms(xm[r0:r0 + ngrp, :], g_ref[...])
            for n in range(nslab):
                perm_ref[n, pl.ds(t * ptile + s, ngrp, stride=SUBLANES), :] = (
                    h[:, n * LANES:(n + 1) * LANES])
    for n in range(nslab):
        h_ref[:, n * LANES:(n + 1) * LANES] = perm_ref[n].astype(h_ref.dtype)


def _outproj(x2, attn2, lru2, w, g, tm, ptile):
    n, d = x2.shape
    row = lambda i: (i, 0)
    const = lambda i: (0, 0)
    assert tm % ptile == 0
    return pl.pallas_call(
        functools.partial(_outproj_kernel, ptile=ptile),
        grid=(n // tm,),
        in_specs=[pl.BlockSpec((tm, d), row),
                  pl.BlockSpec((tm, attn2.shape[1]), row),
                  pl.BlockSpec((tm, lru2.shape[1]), row),
                  pl.BlockSpec(w.shape, const, pipeline_mode=pl.Buffered(1)),
                  pl.BlockSpec((1, d), const)],
        out_specs=[pl.BlockSpec((tm, d), row)] * 2,
        out_shape=[jax.ShapeDtypeStruct((n, d), F32), jax.ShapeDtypeStruct((n, d), BF16)],
        scratch_shapes=[pltpu.VMEM(w.shape, BF16), pltpu.VMEM((d // LANES, tm, LANES), F32)],
        compiler_params=pltpu.CompilerParams(
            dimension_semantics=("arbitrary",), vmem_limit_bytes=VMEM_LIMIT),
        name="outproj",
    )(x2, attn2, lru2, w, g)


def _ffn_kernel(hp_ref, hm_ref, hn_ref, xm_ref, wup_ref, cw_ref, cb_ref, wdn_ref, fg_ref, o_ref,
                hext_ref, ua_ref, ub_ref, acta_ref, actb_ref, acc_ref,
                *, tile, chunk, d_ff):
    i = pl.program_id(1)
    nchunks = d_ff // chunk
    ngrp = tile // SUBLANES
    nslab = acc_ref.shape[0]
    d = nslab * LANES

    hext_ref[0:tile, :] = hm_ref[...]
    prev = jnp.where(i > 0, hp_ref[BF16_ROWS - 1:BF16_ROWS, :].astype(F32), 0.0)
    nxt = jnp.where(i < pl.num_programs(1) - 1, hn_ref[0:1, :].astype(F32), 0.0)
    hrow = lax.broadcasted_iota(jnp.int32, (BF16_ROWS, d), 0)
    halo = jnp.where(hrow == 0, prev, jnp.where(hrow == 1, nxt, 0.0))
    hext_ref[tile:tile + BF16_ROWS, :] = halo.astype(BF16)
    acc_ref[...] = jnp.zeros_like(acc_ref)

    def offsets(c):
        og, ov = c * chunk, d_ff + c * chunk
        if isinstance(c, int):
            return og, ov
        return pl.multiple_of(og, chunk), pl.multiple_of(ov, chunk)

    def up(c, u_ref):
        og, ov = offsets(c)
        hext = hext_ref[...]
        u_ref[:, 0:chunk] = _dot(hext, wup_ref[:, pl.ds(og, chunk)])
        u_ref[:, chunk:2 * chunk] = _dot(hext, wup_ref[:, pl.ds(ov, chunk)])

    def glu(c, u_ref, act_ref):
        sub = lax.broadcasted_iota(jnp.int32, (SUBLANES, chunk), 0)
        blk = BF16_ROWS

        def conv(col0, off, r0):
            cols = slice(col0, col0 + chunk)
            reps = (blk // SUBLANES, 1)
            w = [jnp.tile(cw_ref[k * SUBLANES:(k + 1) * SUBLANES, pl.ds(off, chunk)], reps)
                 for k in range(3)]
            bias = jnp.tile(cb_ref[:, pl.ds(off, chunk)], reps)
            cur = u_ref[r0:r0 + blk, cols]
            if r0 == 0:
                first = jnp.where(sub == 0, u_ref[tile:tile + 1, cols],
                                  pltpu.roll(u_ref[tile - SUBLANES:tile, cols], 1, 0))
                um1 = jnp.concatenate([first, cur[0:blk - SUBLANES]], axis=0)
            else:
                um1 = u_ref[r0 - SUBLANES:r0 + blk - SUBLANES, cols]
            if r0 + blk == tile:
                last = jnp.where(sub == SUBLANES - 1, u_ref[tile + 1:tile + 2, cols],
                                 pltpu.roll(u_ref[0:SUBLANES, cols], SUBLANES - 1, 0))
                up1 = jnp.concatenate([cur[SUBLANES:blk], last], axis=0)
            else:
                up1 = u_ref[r0 + SUBLANES:r0 + blk + SUBLANES, cols]
            return bias + um1 * w[0] + cur * w[1] + up1 * w[2]

        og, ov = offsets(c)
        for r0 in range(0, tile, blk):
            gate = conv(0, og, r0)
            val = conv(chunk, ov, r0)
            act_ref[r0:r0 + blk, :] = (jax.nn.gelu(gate, approximate=True) * val).astype(BF16)

    def down(c, act_ref):
        og, _ = offsets(c)
        y = _dot(act_ref[...], wdn_ref[pl.ds(og, chunk), :])
        for n in range(nslab):
            acc_ref[n, 0:tile, :] += y[:, n * LANES:(n + 1) * LANES]

    assert nchunks % 2 == 1 and nchunks >= 3
    up(0, ua_ref)

    def pair(j, carry):
        c = 2 * j
        up(c + 1, ub_ref)
        glu(c, ua_ref, acta_ref)
        down(c, acta_ref)
        up(c + 2, ua_ref)
        glu(c + 1, ub_ref, actb_ref)
        down(c + 1, actb_ref)
        return carry

    for j in range(nchunks // 2):
        pair(j, 0)
    glu(nchunks - 1, ua_ref, acta_ref)
    down(nchunks - 1, acta_ref)
    for s in range(SUBLANES):
        rs = slice(s * ngrp, (s + 1) * ngrp)
        y = jnp.concatenate([acc_ref[n, pl.ds(s, ngrp, stride=SUBLANES), :] for n in range(nslab)],
                            axis=1)
        o_ref[rs, :] = _rms(xm_ref[rs, :] + y, fg_ref[...])


def _ffn(h2, xm, w_up, conv_w, conv_b, w_down, final_g, tile, chunk):
    b, s, d = h2.shape
    d_ff = w_down.shape[0]
    nh = tile // BF16_ROWS
    last_halo = s // BF16_ROWS - 1
    main = lambda bi, i: (bi, i, 0)
    const = lambda bi, i: (0, 0)
    single = dict(pipeline_mode=pl.Buffered(1))
    kernel = functools.partial(_ffn_kernel, tile=tile, chunk=chunk, d_ff=d_ff)
    return pl.pallas_call(
        kernel,
        grid=(b, s // tile),
        in_specs=[pl.BlockSpec((None, BF16_ROWS, d), lambda bi, i: (bi, jnp.maximum(i * nh - 1, 0), 0)),
                  pl.BlockSpec((None, tile, d), main),
                  pl.BlockSpec((None, BF16_ROWS, d),
                               lambda bi, i: (bi, jnp.minimum((i + 1) * nh, last_halo), 0)),
                  pl.BlockSpec((None, tile, d), main),
                  pl.BlockSpec(w_up.shape, const, **single),
                  pl.BlockSpec(conv_w.shape, const),
                  pl.BlockSpec(conv_b.shape, const),
                  pl.BlockSpec(w_down.shape, const, **single),
                  pl.BlockSpec((1, d), const)],
        out_specs=pl.BlockSpec((None, tile, d), main),
        out_shape=jax.ShapeDtypeStruct((b, s, d), F32),
        scratch_shapes=[pltpu.VMEM((tile + BF16_ROWS, d), BF16)]
        + [pltpu.VMEM((tile + BF16_ROWS, 2 * chunk), F32)] * 2
        + [pltpu.VMEM((tile, chunk), BF16)] * 2
        + [pltpu.VMEM((d // LANES, tile + SUBLANES, LANES), F32)],
        compiler_params=pltpu.CompilerParams(
            dimension_semantics=("arbitrary",) * 2, vmem_limit_bytes=VMEM_LIMIT),
        name="ffn",
    )(h2, h2, h2, xm, w_up, conv_w, conv_b, w_down, final_g)


def _gate_weights(w_a, b_a, w_x, b_x):
    ndir, nblk, bd, _ = w_a.shape
    per = LANES // bd
    npair = nblk // per

    def blockdiag(w):
        w = w.reshape(npair, per, bd, bd)
        eye = jnp.eye(per, dtype=w.dtype)
        return jnp.einsum('pbij,bc->pbicj', w, eye).reshape(npair, LANES, LANES)

    ws, bs = [], []
    for d in range(ndir):
        for w, bias in ((w_a, b_a), (w_x, b_x)):
            ws.append(blockdiag(w[d]))
            bs.append(bias[d].reshape(npair, LANES))
    return ((0.5 * jnp.concatenate(ws, axis=-1)).astype(BF16),
            (0.5 * jnp.concatenate(bs, axis=-1)).astype(F32))


def kernel(x, attn_norm_g, w_in, lambda_q1, lambda_k1, lambda_q2, lambda_k2, subln_g,
           lru_conv_w, lru_conv_b, lru_w_a, lru_b_a, lru_w_x, lru_b_x, lru_lambda,
           w_out, ffn_norm_g, w_up, ffn_conv_w, ffn_conv_b, w_down, final_norm_g):
    b, s, d = x.shape
    depth = w_in.shape[0]
    x2 = x.reshape(b * s, d)
    for l in range(depth):
        lambda_init = 0.8 - 0.6 * math.exp(-0.3 * l)
        q, k, vt, xr, gr = _inproj(x2, attn_norm_g[l][None], w_in[l], tm=512)
        lw = xr.shape[-1]
        attn, w_up_bf, w_down_bf = _attention(
            q.reshape(b, s, -1), k.reshape(b, s, -1), vt,
            lambda_q1[l][None], lambda_k1[l][None], lambda_q2[l][None], lambda_k2[l][None],
            subln_g[l][:, None], lambda_init, tq=256, nsub=4, cast_along=(w_up[l], w_down[l]))
        wg, bg = _gate_weights(lru_w_a[l], lru_b_a[l], lru_w_x[l], lru_b_x[l])
        lru = _lru(xr.reshape(b, s, lw), gr.reshape(b, s, lw), lru_conv_w[l], lru_conv_b[l][None],
                   wg, bg, lru_lambda[l], rows=256)
        xm, h2 = _outproj(x2, attn.reshape(b * s, -1), lru.reshape(b * s, -1),
                          w_out[l], ffn_norm_g[l][None], tm=1024, ptile=FFN_TILE)
        assert depth == 1
        x2 = _ffn(h2.reshape(b, s, d), xm.reshape(b, s, d), w_up_bf,
                  jnp.repeat(ffn_conv_w[l], SUBLANES, axis=0),
                  jnp.broadcast_to(ffn_conv_b[l][None], (SUBLANES, ffn_conv_b.shape[-1])),
                  w_down_bf, final_norm_g[None],
                  tile=FFN_TILE, chunk=256).reshape(b * s, d)
    return x2.reshape(b, s, d)
```

```python
import functools
import math

import jax
import jax.numpy as jnp
from jax import lax
from jax.experimental import pallas as pl
from jax.experimental.pallas import tpu as pltpu

F32 = jnp.float32
BF16 = jnp.bfloat16

N_HEADS = 4
HEAD_DIM = 64
V_DIM = 2 * HEAD_DIM
ATTN_WIDTH = N_HEADS * V_DIM
LRU_CONV_WIDTH = 4
LRU_CONV_LEFT = 2
LRU_C = 8.0
NORM_EPS = 1e-6
LANES = 128
SUBLANES = 8
BF16_ROWS = 16
VMEM_LIMIT = 56 * 1024 * 1024


def _rms(x, g):
    return (x * lax.rsqrt(jnp.mean(x * x, axis=-1, keepdims=True) + NORM_EPS)) * g


def _dot(a, b):
    return jnp.dot(a, b, preferred_element_type=F32)


_NT = (((1,), (1,)), ((), ()))


def _inproj_kernel(x_ref, g_ref, w_ref, q_ref, k_ref, vt_ref, xr_ref, gr_ref, wb_ref, wvt_ref):
    aw = ATTN_WIDTH
    d = x_ref.shape[-1]

    @pl.when(pl.program_id(0) == 0)
    def _cast_weights():
        rows = 2 * LANES
        for r in range(0, d, rows):
            w = w_ref[r:r + rows, :]
            wb_ref[r:r + rows, 0:aw] = (w[:, 0:aw] * (HEAD_DIM ** -0.5)).astype(BF16)
            wb_ref[r:r + rows, aw:] = w[:, aw:].astype(BF16)
        for c in range(aw // LANES):
            cols = slice(2 * aw + c * LANES, 2 * aw + (c + 1) * LANES)
            wvt_ref[c * LANES:(c + 1) * LANES, :] = w_ref[:, cols].T.astype(BF16)

    hb = _rms(x_ref[...], g_ref[...]).astype(BF16)
    q_ref[...] = _dot(hb, wb_ref[:, 0:aw]).astype(BF16)
    k_ref[...] = _dot(hb, wb_ref[:, aw:2 * aw]).astype(BF16)
    vt_ref[...] = lax.dot_general(wvt_ref[...], hb, _NT, preferred_element_type=F32).astype(BF16)
    lw = xr_ref.shape[-1]
    xr_ref[...] = _dot(hb, wb_ref[:, 3 * aw:3 * aw + lw])
    gr_ref[...] = _dot(hb, wb_ref[:, 3 * aw + lw:3 * aw + 2 * lw])


def _inproj(x2, g, w, tm):
    n, d = x2.shape
    lw = (w.shape[1] - 3 * ATTN_WIDTH) // 2
    row = lambda i: (i, 0)
    const = lambda i: (0, 0)
    return pl.pallas_call(
        _inproj_kernel,
        grid=(n // tm,),
        in_specs=[pl.BlockSpec((tm, d), row),
                  pl.BlockSpec((1, d), const),
                  pl.BlockSpec(w.shape, const, pipeline_mode=pl.Buffered(1))],
        out_specs=[pl.BlockSpec((tm, ATTN_WIDTH), row)] * 2
        + [pl.BlockSpec((ATTN_WIDTH, tm), lambda i: (0, i))]
        + [pl.BlockSpec((tm, lw), row)] * 2,
        out_shape=[jax.ShapeDtypeStruct((n, ATTN_WIDTH), BF16)] * 2
        + [jax.ShapeDtypeStruct((ATTN_WIDTH, n), BF16)]
        + [jax.ShapeDtypeStruct((n, lw), F32)] * 2,
        scratch_shapes=[pltpu.VMEM(w.shape, BF16), pltpu.VMEM((ATTN_WIDTH, d), BF16)],
        compiler_params=pltpu.CompilerParams(
            dimension_semantics=("arbitrary",), vmem_limit_bytes=VMEM_LIMIT),
        name="inproj",
    )(x2, g, w)


def _attn_kernel(lq1_ref, lk1_ref, lq2_ref, lk2_ref, sg_ref, q_ref, k_ref, vt_ref, *rest,
                 tq, nsub, seq, lambda_init):
    nside = (len(rest) - 3 - 3 * nsub) // 2
    side_in, o_ref, side_out = rest[:nside], rest[nside], rest[nside + 1:2 * nside + 1]
    kf_ref, dist_ref, *se_refs = rest[2 * nside + 1:]
    for src, dst in zip(side_in, side_out):
        dst[...] = src[...].astype(dst.dtype)

    h = pl.program_id(1)
    qi = pl.program_id(2)
    s_refs, e_refs, m_refs = se_refs[:nsub], se_refs[nsub:2 * nsub], se_refs[2 * nsub:]
    nblk = seq // tq
    assert nblk & (nblk - 1) == 0 and 3 * nblk <= LANES and tq <= 256
    shift = nblk.bit_length() - 1

    @pl.when((pl.program_id(0) == 0) & (h == 0) & (qi == 0))
    def _init():
        lane = lax.broadcasted_iota(jnp.int32, (seq, LANES), 1)
        row = lax.broadcasted_iota(jnp.int32, (seq, LANES), 0)
        grp = lane >> shift
        hit = (row // tq) == (lane & (nblk - 1))
        dj = (row % tq).astype(F32)
        kf_ref[...] = jnp.where(hit & (grp < 2), 1.0,
                                jnp.where(hit & (grp == 2), dj, 0.0)).astype(BF16)
        r = lax.broadcasted_iota(jnp.int32, (tq, tq), 0)
        c = lax.broadcasted_iota(jnp.int32, (tq, tq), 1)
        dist_ref[...] = jnp.abs(r - c).astype(F32)

    lam = (jnp.exp(jnp.sum(lq1_ref[...] * lk1_ref[...], axis=-1, keepdims=True))
           - jnp.exp(jnp.sum(lq2_ref[...] * lk2_ref[...], axis=-1, keepdims=True))
           + lambda_init)
    slope = jnp.where(h == 0, 2.0 ** -2, jnp.where(h == 1, 2.0 ** -4,
                      jnp.where(h == 2, 2.0 ** -6, 2.0 ** -8))).astype(F32)

    kaug = jnp.concatenate([k_ref[...], kf_ref[...]], axis=1)
    vt_ones = jnp.concatenate([vt_ref[...], jnp.ones((BF16_ROWS, seq), BF16)], axis=0)
    lane = lax.broadcasted_iota(jnp.int32, (tq, LANES), 1)
    di = lax.broadcasted_iota(jnp.int32, (tq, LANES), 0).astype(F32)
    grp = lane >> shift
    strip = 4 * SUBLANES

    def scores(sb):
        blk = qi * nsub + sb
        diff = blk - (lane & (nblk - 1))
        sign = jnp.where(diff > 0, 1.0, jnp.where(diff < 0, -1.0, 0.0))
        qf = jnp.where(grp == 0, -slope * sign * di,
                       jnp.where(grp == 1, -slope * tq * jnp.abs(diff).astype(F32),
                                 jnp.where(grp == 2, slope * sign, 0.0))).astype(BF16)
        q = q_ref[sb * tq:(sb + 1) * tq, :]
        zero = jnp.zeros_like(q)
        qaug = jnp.concatenate(
            [jnp.concatenate([jnp.where(lane < HEAD_DIM, q, zero), qf], axis=1),
             jnp.concatenate([jnp.where(lane >= HEAD_DIM, q, zero), qf], axis=1)], axis=0)
        s_ref = s_refs[sb]
        s = lax.dot_general(kaug, qaug, _NT, preferred_element_type=F32)
        s_ref[...] = s
        mx = s[0:strip, :]
        for r in range(strip, seq, strip):
            mx = jnp.maximum(mx, s[r:r + strip, :])
        m_refs[sb][...] = mx
        diag = pl.ds(pl.multiple_of(blk * tq, tq), tq)
        diag_bias = slope * dist_ref[...]
        s_ref[diag, 0:tq] = s_ref[diag, 0:tq] - diag_bias
        s_ref[diag, tq:2 * tq] = s_ref[diag, tq:2 * tq] - diag_bias

    def finish(sb):
        s_ref, e_ref = s_refs[sb], e_refs[sb]
        mx = jnp.max(m_refs[sb][...], axis=0, keepdims=True)
        for r in range(0, seq, strip):
            e_ref[r:r + strip, :] = jnp.exp(s_ref[r:r + strip, :] - mx).astype(BF16)
        o12 = _dot(vt_ones, e_ref[...])
        norm = o12[V_DIM:V_DIM + 1, :]
        o12 = o12[0:V_DIM, :]
        o = o12[:, 0:tq] * (1.0 / norm[:, 0:tq]) - o12[:, tq:2 * tq] * (lam / norm[:, tq:2 * tq])
        o = o * lax.rsqrt(jnp.mean(o * o, axis=0, keepdims=True) + NORM_EPS)
        o = o * sg_ref[...] * (1.0 - lambda_init)
        o_ref[sb * tq:(sb + 1) * tq, :] = o.T.astype(o_ref.dtype)

    scores(0)
    for sb in range(nsub):
        if sb + 1 < nsub:
            scores(sb + 1)
        finish(sb)


def _attention(q, k, vt, lq1, lk1, lq2, lk2, subln_g, lambda_init, tq, nsub, cast_along=()):
    b, s, _ = q.shape
    vec = lambda bi, h, qi: (0, 0)
    tstep = tq * nsub
    nq = s // tstep
    nsteps = b * N_HEADS * nq

    def row_block(last):
        return lambda bi, h, qi: (jnp.minimum((bi * N_HEADS + h) * nq + qi, last), 0)

    side_in, side_out, side_shapes = [], [], []
    for w in cast_along:
        rows = -(-w.shape[0] // (nsteps * BF16_ROWS)) * BF16_ROWS
        while w.shape[0] % rows:
            rows += BF16_ROWS
        for specs in (side_in, side_out):
            specs.append(pl.BlockSpec((rows, w.shape[1]), row_block(w.shape[0] // rows - 1)))
        side_shapes.append(jax.ShapeDtypeStruct(w.shape, BF16))
    kernel = functools.partial(_attn_kernel, tq=tq, nsub=nsub, seq=s, lambda_init=lambda_init)
    return pl.pallas_call(
        kernel,
        grid=(b, N_HEADS, s // tstep),
        in_specs=[pl.BlockSpec((1, HEAD_DIM), vec)] * 4
        + [pl.BlockSpec((V_DIM, 1), vec),
           pl.BlockSpec((None, tstep, V_DIM), lambda bi, h, qi: (bi, qi, h)),
           pl.BlockSpec((None, s, V_DIM), lambda bi, h, qi: (bi, 0, h)),
           pl.BlockSpec((V_DIM, s), lambda bi, h, qi: (h, bi))] + side_in,
        out_specs=[pl.BlockSpec((None, tstep, V_DIM), lambda bi, h, qi: (bi, qi, h))] + side_out,
        out_shape=[jax.ShapeDtypeStruct((b, s, ATTN_WIDTH), BF16)] + side_shapes,
        scratch_shapes=[pltpu.VMEM((s, LANES), BF16), pltpu.VMEM((tq, tq), F32)]
        + [pltpu.VMEM((s, 2 * tq), F32)] * nsub + [pltpu.VMEM((s, 2 * tq), BF16)] * nsub
        + [pltpu.VMEM((4 * SUBLANES, 2 * tq), F32)] * nsub,
        compiler_params=pltpu.CompilerParams(
            dimension_semantics=("arbitrary",) * 3, vmem_limit_bytes=VMEM_LIMIT),
        name="attn",
    )(lq1, lk1, lq2, lk2, subln_g, q, k, vt, *cast_along)


def _local_scan(a, u, reverse):
    row = lax.broadcasted_iota(jnp.int32, a.shape, 0)
    for d in (1, 2, 4):
        shift = SUBLANES - d if reverse else d
        valid = (row < SUBLANES - d) if reverse else (row >= d)
        a_s = jnp.where(valid, pltpu.roll(a, shift, 0), 1.0)
        u_s = jnp.where(valid, pltpu.roll(u, shift, 0), 0.0)
        u = a * u_s + u
        a = a * a_s
    return a, u


def _lru_kernel(xr_ref, gr_ref, cw_ref, cb_ref, wg_ref, bg_ref, lam_ref, o_ref,
                xp_ref, xc_ref, gates_ref, h_ref, cum_ref, *, seq, rows):
    width = xr_ref.shape[-1]
    ntile = width // LANES
    grp = seq // SUBLANES
    left = LRU_CONV_LEFT
    right = LRU_CONV_WIDTH - 1 - LRU_CONV_LEFT
    top = left * SUBLANES
    sub = lax.broadcasted_iota(jnp.int32, (SUBLANES, LANES), 0)

    for s in range(SUBLANES):
        blk = xr_ref[s * grp:(s + 1) * grp, :]
        for p in range(ntile):
            xp_ref[p, pl.ds(top + s, grp, stride=SUBLANES), :] = blk[:, p * LANES:(p + 1) * LANES]
    for p in range(ntile):
        for k in range(left):
            src = xp_ref[p, top + (grp - 1 - k) * SUBLANES:top + (grp - k) * SUBLANES, :]
            xp_ref[p, top - (k + 1) * SUBLANES:top - k * SUBLANES, :] = jnp.where(
                sub == 0, 0.0, pltpu.roll(src, 1, 0))
        for k in range(right):
            src = xp_ref[p, top + k * SUBLANES:top + (k + 1) * SUBLANES, :]
            xp_ref[p, top + (grp + k) * SUBLANES:top + (grp + k + 1) * SUBLANES, :] = jnp.where(
                sub == SUBLANES - 1, 0.0, pltpu.roll(src, SUBLANES - 1, 0))

    neg_lam = -lam_ref[...]
    softplus = jnp.maximum(neg_lam, 0.0) + jnp.log1p(jnp.exp(-jnp.abs(neg_lam)))
    rate = LRU_C * softplus
    rate_log2 = -rate * math.log2(math.e)

    nchunk = seq // rows
    for c in range(nchunk):
        r0 = c * rows
        for p in range(ntile):
            cols = slice(p * LANES, (p + 1) * LANES)
            reps = (rows // SUBLANES, 1)
            xc = jnp.tile(cb_ref[:, cols], reps)
            for tap in range(LRU_CONV_WIDTH):
                start = top + r0 + (tap - left) * SUBLANES
                w = jnp.tile(cw_ref[tap * SUBLANES:(tap + 1) * SUBLANES, cols], reps)
                xc = xc + xp_ref[p, start:start + rows, :] * w
            xc_ref[p, r0:r0 + rows, :] = xc

    zero = jnp.zeros((SUBLANES, LANES), F32)
    one = jnp.ones((SUBLANES, LANES), F32)
    nvr = rows // SUBLANES
    tile8 = lambda row: jnp.broadcast_to(row, (SUBLANES, LANES))

    def chunk(c, ends):
        ends = [list(ends[2 * k:2 * k + 2]) for k in range(2 * ntile)]
        base = (pl.multiple_of(c * rows, rows), pl.multiple_of((nchunk - 1 - c) * rows, rows))
        for p in range(ntile):
            for d in range(2):
                gcols = slice(2 * d * LANES, (2 * d + 2) * LANES)
                xcb = xc_ref[p, pl.ds(base[d], rows), :].astype(BF16)
                gates_ref[2 * p + d] = _dot(xcb, wg_ref[p, :, gcols]) + bg_ref[p:p + 1, gcols]
        for step in range(nvr):
            for p in range(ntile):
                cols = slice(p * LANES, (p + 1) * LANES)
                for d in range(2):
                    j = step if d == 0 else nvr - 1 - step
                    row = pl.ds(base[d] + j * SUBLANES, SUBLANES)
                    g = gates_ref[2 * p + d, j * SUBLANES:(j + 1) * SUBLANES, :]
                    th = jnp.tanh(g)
                    r = 0.5 + 0.5 * th[:, 0:LANES]
                    i = 0.5 + 0.5 * th[:, LANES:2 * LANES]
                    t = jnp.tanh(r * tile8(rate[d:d + 1, cols]))
                    tt = t + t
                    prod = tt * (1.0 + t)
                    mult = jnp.where(prod > 0.0, tt * lax.rsqrt(prod), 0.0)
                    a = jnp.exp2(r * tile8(rate_log2[d:d + 1, cols]))
                    u = mult * (i * xc_ref[p, row, :])
                    h, cum = ends[2 * p + d]
                    h = a * h + u
                    cum = a * cum
                    h_ref[d, p, row, :] = h
                    cum_ref[d, p, row, :] = cum
                    ends[2 * p + d] = [h, cum]
        return tuple(x for pair in ends for x in pair)

    flat = lax.fori_loop(0, nchunk, chunk, (zero, one) * (2 * ntile))
    ends = [[(flat[2 * (2 * p + d)], flat[2 * (2 * p + d) + 1]) for d in range(2)]
            for p in range(ntile)]

    enter = []
    for p in range(ntile):
        for d in range(2):
            h_end, cum_end = ends[p][d]
            _, chained = _local_scan(cum_end, h_end, reverse=(d == 1))
            if d == 0:
                enter.append(jnp.where(sub == 0, 0.0, pltpu.roll(chained, 1, 0)))
            else:
                enter.append(jnp.where(sub == SUBLANES - 1, 0.0,
                                       pltpu.roll(chained, SUBLANES - 1, 0)))

    for p in range(ntile):
        for j in range(grp):
            rs = slice(j * SUBLANES, (j + 1) * SUBLANES)
            xc_ref[p, rs, :] = ((h_ref[0, p, rs, :] + cum_ref[0, p, rs, :] * enter[2 * p])
                                + (h_ref[1, p, rs, :] + cum_ref[1, p, rs, :] * enter[2 * p + 1]))

    blk = BF16_ROWS

    def segment(s, carry):
        for p in range(ntile):
            cols = slice(p * LANES, (p + 1) * LANES)
            for j0 in range(0, grp, blk):
                rs = pl.ds(pl.multiple_of(s * grp + j0, blk), blk)
                y = xc_ref[p, pl.ds(s + j0 * SUBLANES, blk, stride=SUBLANES), :]
                o_ref[rs, cols] = (jax.nn.gelu(gr_ref[rs, cols], approximate=True)
                                   * y).astype(o_ref.dtype)
        return carry

    lax.fori_loop(0, SUBLANES, segment, 0)


def _lru(xr, gr, conv_w, conv_b, wg, bg, lru_lambda, rows):
    b, s, width = xr.shape
    ntile = width // LANES
    seqblk = pl.BlockSpec((None, s, width), lambda bi: (bi, 0, 0))
    full = lambda a: pl.BlockSpec(a.shape, lambda bi: (0,) * a.ndim)
    kernel = functools.partial(_lru_kernel, seq=s, rows=rows)
    halo_rows = (LRU_CONV_WIDTH - 1) * SUBLANES
    return pl.pallas_call(
        kernel,
        grid=(b,),
        in_specs=[seqblk, seqblk, full(conv_w), full(conv_b), full(wg), full(bg), full(lru_lambda)],
        out_specs=seqblk,
        out_shape=jax.ShapeDtypeStruct((b, s, width), BF16),
        scratch_shapes=[pltpu.VMEM((ntile, s + halo_rows, LANES), F32),
                        pltpu.VMEM((ntile, s, LANES), F32),
                        pltpu.VMEM((2 * ntile, rows, 2 * LANES), F32),
                        pltpu.VMEM((2, ntile, s, LANES), F32),
                        pltpu.VMEM((2, ntile, s, LANES), F32)],
        compiler_params=pltpu.CompilerParams(
            dimension_semantics=("arbitrary",), vmem_limit_bytes=VMEM_LIMIT),
        name="lru",
    )(xr, gr, conv_w, conv_b, wg, bg, lru_lambda)


def _outproj_kernel(x_ref, a_ref, l_ref, w_ref, g_ref, xm_ref, h_ref, wb_ref):
    @pl.when(pl.program_id(0) == 0)
    def _cast_weights():
        wb_ref[...] = w_ref[...].astype(BF16)

    aw = a_ref.shape[-1]
    xm = x_ref[...] + _dot(a_ref[...], wb_ref[0:aw, :]) + _dot(l_ref[...], wb_ref[aw:, :])
    xm_ref[...] = xm
    h_ref[...] = _rms(xm, g_ref[...]).astype(h_ref.dtype)


def _outproj(x2, attn2, lru2, w, g, tm):
    n, d = x2.shape
    row = lambda i: (i, 0)
    const = lambda i: (0, 0)
    return pl.pallas_call(
        _outproj_kernel,
        grid=(n // tm,),
        in_specs=[pl.BlockSpec((tm, d), row),
                  pl.BlockSpec((tm, attn2.shape[1]), row),
                  pl.BlockSpec((tm, lru2.shape[1]), row),
                  pl.BlockSpec(w.shape, const, pipeline_mode=pl.Buffered(1)),
                  pl.BlockSpec((1, d), const)],
        out_specs=[pl.BlockSpec((tm, d), row)] * 2,
        out_shape=[jax.ShapeDtypeStruct((n, d), F32), jax.ShapeDtypeStruct((n, d), BF16)],
        scratch_shapes=[pltpu.VMEM(w.shape, BF16)],
        compiler_params=pltpu.CompilerParams(
            dimension_semantics=("arbitrary",), vmem_limit_bytes=VMEM_LIMIT),
        name="outproj",
    )(x2, attn2, lru2, w, g)


def _ffn_kernel(hp_ref, hm_ref, hn_ref, xm_ref, wup_ref, cw_ref, cb_ref, wdn_ref, fg_ref, o_ref,
                perm_ref, hext_ref, ua_ref, ub_ref, acta_ref, actb_ref, acc_ref,
                *, tile, chunk, d_ff):
    i = pl.program_id(1)
    nchunks = d_ff // chunk
    ngrp = tile // SUBLANES
    nslab = perm_ref.shape[0]
    d = nslab * LANES

    for s in range(SUBLANES):
        rows = hm_ref[s * ngrp:(s + 1) * ngrp, :].astype(F32)
        for n in range(nslab):
            perm_ref[n, pl.ds(s, ngrp, stride=SUBLANES), :] = rows[:, n * LANES:(n + 1) * LANES]
    for n in range(nslab):
        hext_ref[0:tile, n * LANES:(n + 1) * LANES] = perm_ref[n].astype(BF16)
    prev = jnp.where(i > 0, hp_ref[BF16_ROWS - 1:BF16_ROWS, :].astype(F32), 0.0)
    nxt = jnp.where(i < pl.num_programs(1) - 1, hn_ref[0:1, :].astype(F32), 0.0)
    hrow = lax.broadcasted_iota(jnp.int32, (BF16_ROWS, d), 0)
    halo = jnp.where(hrow == 0, prev, jnp.where(hrow == 1, nxt, 0.0))
    hext_ref[tile:tile + BF16_ROWS, :] = halo.astype(BF16)
    acc_ref[...] = jnp.zeros_like(acc_ref)

    def offsets(c):
        og, ov = c * chunk, d_ff + c * chunk
        if isinstance(c, int):
            return og, ov
        return pl.multiple_of(og, chunk), pl.multiple_of(ov, chunk)

    def up(c, u_ref):
        og, ov = offsets(c)
        hext = hext_ref[...]
        u_ref[:, 0:chunk] = _dot(hext, wup_ref[:, pl.ds(og, chunk)])
        u_ref[:, chunk:2 * chunk] = _dot(hext, wup_ref[:, pl.ds(ov, chunk)])

    def glu(c, u_ref, act_ref):
        sub = lax.broadcasted_iota(jnp.int32, (SUBLANES, chunk), 0)
        blk = BF16_ROWS

        def conv(col0, off, r0):
            cols = slice(col0, col0 + chunk)
            reps = (blk // SUBLANES, 1)
            w = [jnp.tile(cw_ref[k * SUBLANES:(k + 1) * SUBLANES, pl.ds(off, chunk)], reps)
                 for k in range(3)]
            bias = jnp.tile(cb_ref[:, pl.ds(off, chunk)], reps)
            cur = u_ref[r0:r0 + blk, cols]
            if r0 == 0:
                first = jnp.where(sub == 0, u_ref[tile:tile + 1, cols],
                                  pltpu.roll(u_ref[tile - SUBLANES:tile, cols], 1, 0))
                um1 = jnp.concatenate([first, cur[0:blk - SUBLANES]], axis=0)
            else:
                um1 = u_ref[r0 - SUBLANES:r0 + blk - SUBLANES, cols]
            if r0 + blk == tile:
                last = jnp.where(sub == SUBLANES - 1, u_ref[tile + 1:tile + 2, cols],
                                 pltpu.roll(u_ref[0:SUBLANES, cols], SUBLANES - 1, 0))
                up1 = jnp.concatenate([cur[SUBLANES:blk], last], axis=0)
            else:
                up1 = u_ref[r0 + SUBLANES:r0 + blk + SUBLANES, cols]
            return bias + um1 * w[0] + cur * w[1] + up1 * w[2]

        og, ov = offsets(c)
        for r0 in range(0, tile, blk):
            gate = conv(0, og, r0)
            val = conv(chunk, ov, r0)
            act_ref[r0:r0 + blk, :] = (jax.nn.gelu(gate, approximate=True) * val).astype(BF16)

    def down(c, act_ref):
        og, _ = offsets(c)
        y = _dot(act_ref[...], wdn_ref[pl.ds(og, chunk), :])
        for n in range(nslab):
            acc_ref[n, 0:tile, :] += y[:, n * LANES:(n + 1) * LANES]

    assert nchunks % 2 == 1 and nchunks >= 3
    up(0, ua_ref)

    def pair(j, carry):
        c = 2 * j
        up(c + 1, ub_ref)
        glu(c, ua_ref, acta_ref)
        down(c, acta_ref)
        up(c + 2, ua_ref)
        glu(c + 1, ub_ref, actb_ref)
        down(c + 1, actb_ref)
        return carry

    for j in range(nchunks // 2):
        pair(j, 0)
    glu(nchunks - 1, ua_ref, acta_ref)
    down(nchunks - 1, acta_ref)
    for s in range(SUBLANES):
        rs = slice(s * ngrp, (s + 1) * ngrp)
        y = jnp.concatenate([acc_ref[n, pl.ds(s, ngrp, stride=SUBLANES), :] for n in range(nslab)],
                            axis=1)
        o_ref[rs, :] = _rms(xm_ref[rs, :] + y, fg_ref[...])


def _ffn(h2, xm, w_up, conv_w, conv_b, w_down, final_g, tile, chunk):
    b, s, d = h2.shape
    d_ff = w_down.shape[0]
    nh = tile // BF16_ROWS
    last_halo = s // BF16_ROWS - 1
    main = lambda bi, i: (bi, i, 0)
    const = lambda bi, i: (0, 0)
    single = dict(pipeline_mode=pl.Buffered(1))
    kernel = functools.partial(_ffn_kernel, tile=tile, chunk=chunk, d_ff=d_ff)
    return pl.pallas_call(
        kernel,
        grid=(b, s // tile),
        in_specs=[pl.BlockSpec((None, BF16_ROWS, d), lambda bi, i: (bi, jnp.maximum(i * nh - 1, 0), 0)),
                  pl.BlockSpec((None, tile, d), main),
                  pl.BlockSpec((None, BF16_ROWS, d),
                               lambda bi, i: (bi, jnp.minimum((i + 1) * nh, last_halo), 0)),
                  pl.BlockSpec((None, tile, d), main),
                  pl.BlockSpec(w_up.shape, const, **single),
                  pl.BlockSpec(conv_w.shape, const),
                  pl.BlockSpec(conv_b.shape, const),
                  pl.BlockSpec(w_down.shape, const, **single),
                  pl.BlockSpec((1, d), const)],
        out_specs=pl.BlockSpec((None, tile, d), main),
        out_shape=jax.ShapeDtypeStruct((b, s, d), F32),
        scratch_shapes=[pltpu.VMEM((d // LANES, tile, LANES), F32),
                        pltpu.VMEM((tile + BF16_ROWS, d), BF16)]
        + [pltpu.VMEM((tile + BF16_ROWS, 2 * chunk), F32)] * 2
        + [pltpu.VMEM((tile, chunk), BF16)] * 2
        + [pltpu.VMEM((d // LANES, tile + SUBLANES, LANES), F32)],
        compiler_params=pltpu.CompilerParams(
            dimension_semantics=("arbitrary",) * 2, vmem_limit_bytes=VMEM_LIMIT),
        name="ffn",
    )(h2, h2, h2, xm, w_up, conv_w, conv_b, w_down, final_g)


def _gate_weights(w_a, b_a, w_x, b_x):
    ndir, nblk, bd, _ = w_a.shape
    per = LANES // bd
    npair = nblk // per

    def blockdiag(w):
        w = w.reshape(npair, per, bd, bd)
        eye = jnp.eye(per, dtype=w.dtype)
        return jnp.einsum('pbij,bc->pbicj', w, eye).reshape(npair, LANES, LANES)

    ws, bs = [], []
    for d in range(ndir):
        for w, bias in ((w_a, b_a), (w_x, b_x)):
            ws.append(blockdiag(w[d]))
            bs.append(bias[d].reshape(npair, LANES))
    return ((0.5 * jnp.concatenate(ws, axis=-1)).astype(BF16),
            (0.5 * jnp.concatenate(bs, axis=-1)).astype(F32))


def kernel(x, attn_norm_g, w_in, lambda_q1, lambda_k1, lambda_q2, lambda_k2, subln_g,
           lru_conv_w, lru_conv_b, lru_w_a, lru_b_a, lru_w_x, lru_b_x, lru_lambda,
           w_out, ffn_norm_g, w_up, ffn_conv_w, ffn_conv_b, w_down, final_norm_g):
    b, s, d = x.shape
    depth = w_in.shape[0]
    x2 = x.reshape(b * s, d)
    assert depth == 1
    for l in range(depth):
        lambda_init = 0.8 - 0.6 * math.exp(-0.3 * l)
        q, k, vt, xr, gr = _inproj(x2, attn_norm_g[l][None], w_in[l], tm=512)
        lw = xr.shape[-1]
        attn, w_up_bf, w_down_bf = _attention(
            q.reshape(b, s, -1), k.reshape(b, s, -1), vt,
            lambda_q1[l][None], lambda_k1[l][None], lambda_q2[l][None], lambda_k2[l][None],
            subln_g[l][:, None], lambda_init, tq=256, nsub=4, cast_along=(w_up[l], w_down[l]))
        wg, bg = _gate_weights(lru_w_a[l], lru_b_a[l], lru_w_x[l], lru_b_x[l])
        lru = _lru(xr.reshape(b, s, lw), gr.reshape(b, s, lw),
                   jnp.repeat(lru_conv_w[l], SUBLANES, axis=0),
                   jnp.broadcast_to(lru_conv_b[l][None], (SUBLANES, lw)),
                   wg, bg, lru_lambda[l], rows=512)
        xm, h2 = _outproj(x2, attn.reshape(b * s, -1), lru.reshape(b * s, -1),
                          w_out[l], ffn_norm_g[l][None], tm=1024)
        x2 = _ffn(h2.reshape(b, s, d), xm.reshape(b, s, d), w_up_bf,
                  jnp.repeat(ffn_conv_w[l], SUBLANES, axis=0),
                  jnp.broadcast_to(ffn_conv_b[l][None], (SUBLANES, ffn_conv_b.shape[-1])),
                  w_down_bf, final_norm_g[None],
                  tile=512, chunk=256).reshape(b * s, d)
    return x2.reshape(b, s, d)
```

```python
import functools
import math

import jax
import jax.numpy as jnp
from jax import lax
from jax.experimental import pallas as pl
from jax.experimental.pallas import tpu as pltpu

F32 = jnp.float32
BF16 = jnp.bfloat16

N_HEADS = 4
HEAD_DIM = 64
V_DIM = 2 * HEAD_DIM
ATTN_WIDTH = N_HEADS * V_DIM
LRU_CONV_WIDTH = 4
LRU_CONV_LEFT = 2
LRU_C = 8.0
NORM_EPS = 1e-6
LANES = 128
SUBLANES = 8
BF16_ROWS = 16
VMEM_LIMIT = 56 * 1024 * 1024


def _rms(x, g):
    return (x * lax.rsqrt(jnp.mean(x * x, axis=-1, keepdims=True) + NORM_EPS)) * g


def _dot(a, b):
    return jnp.dot(a, b, preferred_element_type=F32)


_NT = (((1,), (1,)), ((), ()))


def _inproj_kernel(x_ref, g_ref, w_ref, q_ref, k_ref, vt_ref, xr_ref, gr_ref, wb_ref, wvt_ref):
    aw = ATTN_WIDTH
    d = x_ref.shape[-1]

    @pl.when(pl.program_id(0) == 0)
    def _cast_weights():
        rows = 2 * LANES
        for r in range(0, d, rows):
            w = w_ref[r:r + rows, :]
            wb_ref[r:r + rows, 0:aw] = (w[:, 0:aw] * (HEAD_DIM ** -0.5)).astype(BF16)
            wb_ref[r:r + rows, aw:] = w[:, aw:].astype(BF16)
        for c in range(aw // LANES):
            cols = slice(2 * aw + c * LANES, 2 * aw + (c + 1) * LANES)
            wvt_ref[c * LANES:(c + 1) * LANES, :] = w_ref[:, cols].T.astype(BF16)

    hb = _rms(x_ref[...], g_ref[...]).astype(BF16)
    q_ref[...] = _dot(hb, wb_ref[:, 0:aw]).astype(BF16)
    k_ref[...] = _dot(hb, wb_ref[:, aw:2 * aw]).astype(BF16)
    vt_ref[...] = lax.dot_general(wvt_ref[...], hb, _NT, preferred_element_type=F32).astype(BF16)
    lw = xr_ref.shape[-1]
    xr_ref[...] = _dot(hb, wb_ref[:, 3 * aw:3 * aw + lw])
    gr_ref[...] = _dot(hb, wb_ref[:, 3 * aw + lw:3 * aw + 2 * lw])


def _inproj(x2, g, w, tm):
    n, d = x2.shape
    lw = (w.shape[1] - 3 * ATTN_WIDTH) // 2
    row = lambda i: (i, 0)
    const = lambda i: (0, 0)
    return pl.pallas_call(
        _inproj_kernel,
        grid=(n // tm,),
        in_specs=[pl.BlockSpec((tm, d), row),
                  pl.BlockSpec((1, d), const),
                  pl.BlockSpec(w.shape, const, pipeline_mode=pl.Buffered(1))],
        out_specs=[pl.BlockSpec((tm, ATTN_WIDTH), row)] * 2
        + [pl.BlockSpec((ATTN_WIDTH, tm), lambda i: (0, i))]
        + [pl.BlockSpec((tm, lw), row)] * 2,
        out_shape=[jax.ShapeDtypeStruct((n, ATTN_WIDTH), BF16)] * 2
        + [jax.ShapeDtypeStruct((ATTN_WIDTH, n), BF16)]
        + [jax.ShapeDtypeStruct((n, lw), F32)] * 2,
        scratch_shapes=[pltpu.VMEM(w.shape, BF16), pltpu.VMEM((ATTN_WIDTH, d), BF16)],
        compiler_params=pltpu.CompilerParams(
            dimension_semantics=("arbitrary",), vmem_limit_bytes=VMEM_LIMIT),
        name="inproj",
    )(x2, g, w)


def _attn_kernel(lq1_ref, lk1_ref, lq2_ref, lk2_ref, sg_ref, q_ref, k_ref, vt_ref, *rest,
                 tq, nsub, seq, lambda_init):
    nside = (len(rest) - 3 - 3 * nsub) // 2
    side_in, o_ref, side_out = rest[:nside], rest[nside], rest[nside + 1:2 * nside + 1]
    kf_ref, dist_ref, *se_refs = rest[2 * nside + 1:]
    for src, dst in zip(side_in, side_out):
        dst[...] = src[...].astype(dst.dtype)

    h = pl.program_id(1)
    qi = pl.program_id(2)
    s_refs, e_refs, m_refs = se_refs[:nsub], se_refs[nsub:2 * nsub], se_refs[2 * nsub:]
    nblk = seq // tq
    assert nblk & (nblk - 1) == 0 and 3 * nblk <= LANES and tq <= 256
    shift = nblk.bit_length() - 1

    @pl.when((pl.program_id(0) == 0) & (h == 0) & (qi == 0))
    def _init():
        lane = lax.broadcasted_iota(jnp.int32, (seq, LANES), 1)
        row = lax.broadcasted_iota(jnp.int32, (seq, LANES), 0)
        grp = lane >> shift
        hit = (row // tq) == (lane & (nblk - 1))
        dj = (row % tq).astype(F32)
        kf_ref[...] = jnp.where(hit & (grp < 2), 1.0,
                                jnp.where(hit & (grp == 2), dj, 0.0)).astype(BF16)
        r = lax.broadcasted_iota(jnp.int32, (tq, tq), 0)
        c = lax.broadcasted_iota(jnp.int32, (tq, tq), 1)
        dist_ref[...] = jnp.abs(r - c).astype(F32)

    lam = (jnp.exp(jnp.sum(lq1_ref[...] * lk1_ref[...], axis=-1, keepdims=True))
           - jnp.exp(jnp.sum(lq2_ref[...] * lk2_ref[...], axis=-1, keepdims=True))
           + lambda_init)
    slope = jnp.where(h == 0, 2.0 ** -2, jnp.where(h == 1, 2.0 ** -4,
                      jnp.where(h == 2, 2.0 ** -6, 2.0 ** -8))).astype(F32)

    kaug = jnp.concatenate([k_ref[...], kf_ref[...]], axis=1)
    vt_ones = jnp.concatenate([vt_ref[...], jnp.ones((BF16_ROWS, seq), BF16)], axis=0)
    lane = lax.broadcasted_iota(jnp.int32, (tq, LANES), 1)
    di = lax.broadcasted_iota(jnp.int32, (tq, LANES), 0).astype(F32)
    grp = lane >> shift
    strip = 4 * SUBLANES

    def scores(sb):
        blk = qi * nsub + sb
        diff = blk - (lane & (nblk - 1))
        sign = jnp.where(diff > 0, 1.0, jnp.where(diff < 0, -1.0, 0.0))
        qf = jnp.where(grp == 0, -slope * sign * di,
                       jnp.where(grp == 1, -slope * tq * jnp.abs(diff).astype(F32),
                                 jnp.where(grp == 2, slope * sign, 0.0))).astype(BF16)
        q = q_ref[sb * tq:(sb + 1) * tq, :]
        zero = jnp.zeros_like(q)
        qaug = jnp.concatenate(
            [jnp.concatenate([jnp.where(lane < HEAD_DIM, q, zero), qf], axis=1),
             jnp.concatenate([jnp.where(lane >= HEAD_DIM, q, zero), qf], axis=1)], axis=0)
        s_ref = s_refs[sb]
        s = lax.dot_general(kaug, qaug, _NT, preferred_element_type=F32)
        s_ref[...] = s
        mx = s[0:strip, :]
        for r in range(strip, seq, strip):
            mx = jnp.maximum(mx, s[r:r + strip, :])
        m_refs[sb][...] = mx
        diag = pl.ds(pl.multiple_of(blk * tq, tq), tq)
        diag_bias = slope * dist_ref[...]
        s_ref[diag, 0:tq] = s_ref[diag, 0:tq] - diag_bias
        s_ref[diag, tq:2 * tq] = s_ref[diag, tq:2 * tq] - diag_bias

    def finish(sb):
        s_ref, e_ref = s_refs[sb], e_refs[sb]
        mx = jnp.max(m_refs[sb][...], axis=0, keepdims=True)
        for r in range(0, seq, strip):
            e_ref[r:r + strip, :] = jnp.exp(s_ref[r:r + strip, :] - mx).astype(BF16)
        o12 = _dot(vt_ones, e_ref[...])
        norm = o12[V_DIM:V_DIM + 1, :]
        o12 = o12[0:V_DIM, :]
        o = o12[:, 0:tq] * (1.0 / norm[:, 0:tq]) - o12[:, tq:2 * tq] * (lam / norm[:, tq:2 * tq])
        o = o * lax.rsqrt(jnp.mean(o * o, axis=0, keepdims=True) + NORM_EPS)
        o = o * sg_ref[...] * (1.0 - lambda_init)
        o_ref[sb * tq:(sb + 1) * tq, :] = o.T.astype(o_ref.dtype)

    scores(0)
    for sb in range(nsub):
        if sb + 1 < nsub:
            scores(sb + 1)
        finish(sb)


def _attention(q, k, vt, lq1, lk1, lq2, lk2, subln_g, lambda_init, tq, nsub, cast_along=()):
    b, s, _ = q.shape
    vec = lambda bi, h, qi: (0, 0)
    tstep = tq * nsub
    nq = s // tstep
    nsteps = b * N_HEADS * nq

    def row_block(last):
        return lambda bi, h, qi: (jnp.minimum((bi * N_HEADS + h) * nq + qi, last), 0)

    side_in, side_out, side_shapes = [], [], []
    for w in cast_along:
        rows = -(-w.shape[0] // (nsteps * BF16_ROWS)) * BF16_ROWS
        while w.shape[0] % rows:
            rows += BF16_ROWS
        for specs in (side_in, side_out):
            specs.append(pl.BlockSpec((rows, w.shape[1]), row_block(w.shape[0] // rows - 1)))
        side_shapes.append(jax.ShapeDtypeStruct(w.shape, BF16))
    kernel = functools.partial(_attn_kernel, tq=tq, nsub=nsub, seq=s, lambda_init=lambda_init)
    return pl.pallas_call(
        kernel,
        grid=(b, N_HEADS, s // tstep),
        in_specs=[pl.BlockSpec((1, HEAD_DIM), vec)] * 4
        + [pl.BlockSpec((V_DIM, 1), vec),
           pl.BlockSpec((None, tstep, V_DIM), lambda bi, h, qi: (bi, qi, h)),
           pl.BlockSpec((None, s, V_DIM), lambda bi, h, qi: (bi, 0, h)),
           pl.BlockSpec((V_DIM, s), lambda bi, h, qi: (h, bi))] + side_in,
        out_specs=[pl.BlockSpec((None, tstep, V_DIM), lambda bi, h, qi: (bi, qi, h))] + side_out,
        out_shape=[jax.ShapeDtypeStruct((b, s, ATTN_WIDTH), BF16)] + side_shapes,
        scratch_shapes=[pltpu.VMEM((s, LANES), BF16), pltpu.VMEM((tq, tq), F32)]
        + [pltpu.VMEM((s, 2 * tq), F32)] * nsub + [pltpu.VMEM((s, 2 * tq), BF16)] * nsub
        + [pltpu.VMEM((4 * SUBLANES, 2 * tq), F32)] * nsub,
        compiler_params=pltpu.CompilerParams(
            dimension_semantics=("arbitrary",) * 3, vmem_limit_bytes=VMEM_LIMIT),
        name="attn",
    )(lq1, lk1, lq2, lk2, subln_g, q, k, vt, *cast_along)


def _local_scan(a, u, reverse):
    row = lax.broadcasted_iota(jnp.int32, a.shape, 0)
    for d in (1, 2, 4):
        shift = SUBLANES - d if reverse else d
        valid = (row < SUBLANES - d) if reverse else (row >= d)
        a_s = jnp.where(valid, pltpu.roll(a, shift, 0), 1.0)
        u_s = jnp.where(valid, pltpu.roll(u, shift, 0), 0.0)
        u = a * u_s + u
        a = a * a_s
    return a, u


def _lru_kernel(xr_ref, gr_ref, cw_ref, cb_ref, wg_ref, bg_ref, lam_ref, o_ref,
                xp_ref, xc_ref, gates_ref, h_ref, cum_ref, *, seq, rows):
    width = xr_ref.shape[-1]
    ntile = width // LANES
    grp = seq // SUBLANES
    left = LRU_CONV_LEFT
    right = LRU_CONV_WIDTH - 1 - LRU_CONV_LEFT
    top = left * SUBLANES
    sub = lax.broadcasted_iota(jnp.int32, (SUBLANES, LANES), 0)

    for s in range(SUBLANES):
        blk = xr_ref[s * grp:(s + 1) * grp, :]
        for p in range(ntile):
            xp_ref[p, pl.ds(top + s, grp, stride=SUBLANES), :] = blk[:, p * LANES:(p + 1) * LANES]
    for p in range(ntile):
        for k in range(left):
            src = xp_ref[p, top + (grp - 1 - k) * SUBLANES:top + (grp - k) * SUBLANES, :]
            xp_ref[p, top - (k + 1) * SUBLANES:top - k * SUBLANES, :] = jnp.where(
                sub == 0, 0.0, pltpu.roll(src, 1, 0))
        for k in range(right):
            src = xp_ref[p, top + k * SUBLANES:top + (k + 1) * SUBLANES, :]
            xp_ref[p, top + (grp + k) * SUBLANES:top + (grp + k + 1) * SUBLANES, :] = jnp.where(
                sub == SUBLANES - 1, 0.0, pltpu.roll(src, SUBLANES - 1, 0))

    neg_lam = -lam_ref[...]
    softplus = jnp.maximum(neg_lam, 0.0) + jnp.log1p(jnp.exp(-jnp.abs(neg_lam)))
    rate = LRU_C * softplus
    rate_log2 = -rate * math.log2(math.e)

    nchunk = seq // rows
    for c in range(nchunk):
        r0 = c * rows
        for p in range(ntile):
            cols = slice(p * LANES, (p + 1) * LANES)
            reps = (rows // SUBLANES, 1)
            xc = jnp.tile(cb_ref[:, cols], reps)
            for tap in range(LRU_CONV_WIDTH):
                start = top + r0 + (tap - left) * SUBLANES
                w = jnp.tile(cw_ref[tap * SUBLANES:(tap + 1) * SUBLANES, cols], reps)
                xc = xc + xp_ref[p, start:start + rows, :] * w
            xc_ref[p, r0:r0 + rows, :] = xc

    zero = jnp.zeros((SUBLANES, LANES), F32)
    one = jnp.ones((SUBLANES, LANES), F32)
    nvr = rows // SUBLANES
    tile8 = lambda row: jnp.broadcast_to(row, (SUBLANES, LANES))

    def chunk(c, ends):
        ends = [list(ends[2 * k:2 * k + 2]) for k in range(2 * ntile)]
        base = (pl.multiple_of(c * rows, rows), pl.multiple_of((nchunk - 1 - c) * rows, rows))
        for p in range(ntile):
            for d in range(2):
                gcols = slice(2 * d * LANES, (2 * d + 2) * LANES)
                xcb = xc_ref[p, pl.ds(base[d], rows), :].astype(BF16)
                gates_ref[2 * p + d] = _dot(xcb, wg_ref[p, :, gcols]) + bg_ref[p:p + 1, gcols]
        for step in range(nvr):
            for p in range(ntile):
                cols = slice(p * LANES, (p + 1) * LANES)
                for d in range(2):
                    j = step if d == 0 else nvr - 1 - step
                    row = pl.ds(base[d] + j * SUBLANES, SUBLANES)
                    g = gates_ref[2 * p + d, j * SUBLANES:(j + 1) * SUBLANES, :]
                    th = jnp.tanh(g)
                    r = 0.5 + 0.5 * th[:, 0:LANES]
                    i = 0.5 + 0.5 * th[:, LANES:2 * LANES]
                    t = jnp.tanh(r * tile8(rate[d:d + 1, cols]))
                    tt = t + t
                    prod = tt * (1.0 + t)
                    mult = jnp.where(prod > 0.0, tt * lax.rsqrt(prod), 0.0)
                    a = jnp.exp2(r * tile8(rate_log2[d:d + 1, cols]))
                    u = mult * (i * xc_ref[p, row, :])
                    h, cum = ends[2 * p + d]
                    h = a * h + u
                    cum = a * cum
                    h_ref[d, p, row, :] = h
                    cum_ref[d, p, row, :] = cum
                    ends[2 * p + d] = [h, cum]
        return tuple(x for pair in ends for x in pair)

    flat = lax.fori_loop(0, nchunk, chunk, (zero, one) * (2 * ntile))
    ends = [[(flat[2 * (2 * p + d)], flat[2 * (2 * p + d) + 1]) for d in range(2)]
            for p in range(ntile)]

    enter = []
    for p in range(ntile):
        for d in range(2):
            h_end, cum_end = ends[p][d]
            _, chained = _local_scan(cum_end, h_end, reverse=(d == 1))
            if d == 0:
                enter.append(jnp.where(sub == 0, 0.0, pltpu.roll(chained, 1, 0)))
            else:
                enter.append(jnp.where(sub == SUBLANES - 1, 0.0,
                                       pltpu.roll(chained, SUBLANES - 1, 0)))

    for p in range(ntile):
        for j in range(grp):
            rs = slice(j * SUBLANES, (j + 1) * SUBLANES)
            xc_ref[p, rs, :] = ((h_ref[0, p, rs, :] + cum_ref[0, p, rs, :] * enter[2 * p])
                                + (h_ref[1, p, rs, :] + cum_ref[1, p, rs, :] * enter[2 * p + 1]))

    blk = BF16_ROWS

    def segment(s, carry):
        for p in range(ntile):
            cols = slice(p * LANES, (p + 1) * LANES)
            for j0 in range(0, grp, blk):
                rs = pl.ds(pl.multiple_of(s * grp + j0, blk), blk)
                y = xc_ref[p, pl.ds(s + j0 * SUBLANES, blk, stride=SUBLANES), :]
                o_ref[rs, cols] = (jax.nn.gelu(gr_ref[rs, cols], approximate=True)
                                   * y).astype(o_ref.dtype)
        return carry

    lax.fori_loop(0, SUBLANES, segment, 0)


def _lru(xr, gr, conv_w, conv_b, wg, bg, lru_lambda, rows):
    b, s, width = xr.shape
    ntile = width // LANES
    seqblk = pl.BlockSpec((None, s, width), lambda bi: (bi, 0, 0))
    full = lambda a: pl.BlockSpec(a.shape, lambda bi: (0,) * a.ndim)
    kernel = functools.partial(_lru_kernel, seq=s, rows=rows)
    halo_rows = (LRU_CONV_WIDTH - 1) * SUBLANES
    return pl.pallas_call(
        kernel,
        grid=(b,),
        in_specs=[seqblk, seqblk, full(conv_w), full(conv_b), full(wg), full(bg), full(lru_lambda)],
        out_specs=seqblk,
        out_shape=jax.ShapeDtypeStruct((b, s, width), BF16),
        scratch_shapes=[pltpu.VMEM((ntile, s + halo_rows, LANES), F32),
                        pltpu.VMEM((ntile, s, LANES), F32),
                        pltpu.VMEM((2 * ntile, rows, 2 * LANES), F32),
                        pltpu.VMEM((2, ntile, s, LANES), F32),
                        pltpu.VMEM((2, ntile, s, LANES), F32)],
        compiler_params=pltpu.CompilerParams(
            dimension_semantics=("arbitrary",), vmem_limit_bytes=VMEM_LIMIT),
        name="lru",
    )(xr, gr, conv_w, conv_b, wg, bg, lru_lambda)


def _outproj_kernel(x_ref, a_ref, l_ref, w_ref, g_ref, xm_ref, h_ref, wb_ref):
    @pl.when(pl.program_id(0) == 0)
    def _cast_weights():
        wb_ref[...] = w_ref[...].astype(BF16)

    aw = a_ref.shape[-1]
    xm = x_ref[...] + _dot(a_ref[...], wb_ref[0:aw, :]) + _dot(l_ref[...], wb_ref[aw:, :])
    xm_ref[...] = xm
    h_ref[...] = _rms(xm, g_ref[...]).astype(h_ref.dtype)


def _outproj(x2, attn2, lru2, w, g, tm):
    n, d = x2.shape
    row = lambda i: (i, 0)
    const = lambda i: (0, 0)
    return pl.pallas_call(
        _outproj_kernel,
        grid=(n // tm,),
        in_specs=[pl.BlockSpec((tm, d), row),
                  pl.BlockSpec((tm, attn2.shape[1]), row),
                  pl.BlockSpec((tm, lru2.shape[1]), row),
                  pl.BlockSpec(w.shape, const, pipeline_mode=pl.Buffered(1)),
                  pl.BlockSpec((1, d), const)],
        out_specs=[pl.BlockSpec((tm, d), row)] * 2,
        out_shape=[jax.ShapeDtypeStruct((n, d), F32), jax.ShapeDtypeStruct((n, d), BF16)],
        scratch_shapes=[pltpu.VMEM(w.shape, BF16)],
        compiler_params=pltpu.CompilerParams(
            dimension_semantics=("arbitrary",), vmem_limit_bytes=VMEM_LIMIT),
        name="outproj",
    )(x2, attn2, lru2, w, g)


def _ffn_kernel(hp_ref, hm_ref, hn_ref, xm_ref, wup_ref, cw_ref, cb_ref, wdn_ref, fg_ref, o_ref,
                perm_ref, hext_ref, ua_ref, ub_ref, acta_ref, actb_ref, acc_ref,
                *, tile, chunk, d_ff):
    i = pl.program_id(1)
    nchunks = d_ff // chunk
    ngrp = tile // SUBLANES
    nslab = perm_ref.shape[0]
    d = nslab * LANES

    for s in range(SUBLANES):
        rows = hm_ref[s * ngrp:(s + 1) * ngrp, :].astype(F32)
        for n in range(nslab):
            perm_ref[n, pl.ds(s, ngrp, stride=SUBLANES), :] = rows[:, n * LANES:(n + 1) * LANES]
    for n in range(nslab):
        hext_ref[0:tile, n * LANES:(n + 1) * LANES] = perm_ref[n].astype(BF16)
    prev = jnp.where(i > 0, hp_ref[BF16_ROWS - 1:BF16_ROWS, :].astype(F32), 0.0)
    nxt = jnp.where(i < pl.num_programs(1) - 1, hn_ref[0:1, :].astype(F32), 0.0)
    hrow = lax.broadcasted_iota(jnp.int32, (BF16_ROWS, d), 0)
    halo = jnp.where(hrow == 0, prev, jnp.where(hrow == 1, nxt, 0.0))
    hext_ref[tile:tile + BF16_ROWS, :] = halo.astype(BF16)
    acc_ref[...] = jnp.zeros_like(acc_ref)

    def offsets(c):
        og, ov = c * chunk, d_ff + c * chunk
        if isinstance(c, int):
            return og, ov
        return pl.multiple_of(og, chunk), pl.multiple_of(ov, chunk)

    def up(c, u_ref):
        og, ov = offsets(c)
        hext = hext_ref[...]
        u_ref[:, 0:chunk] = _dot(hext, wup_ref[:, pl.ds(og, chunk)])
        u_ref[:, chunk:2 * chunk] = _dot(hext, wup_ref[:, pl.ds(ov, chunk)])

    def glu(c, u_ref, act_ref):
        sub = lax.broadcasted_iota(jnp.int32, (SUBLANES, chunk), 0)
        blk = BF16_ROWS

        def conv(col0, off, r0):
            cols = slice(col0, col0 + chunk)
            reps = (blk // SUBLANES, 1)
            w = [jnp.tile(cw_ref[k * SUBLANES:(k + 1) * SUBLANES, pl.ds(off, chunk)], reps)
                 for k in range(3)]
            bias = jnp.tile(cb_ref[:, pl.ds(off, chunk)], reps)
            cur = u_ref[r0:r0 + blk, cols]
            if r0 == 0:
                first = jnp.where(sub == 0, u_ref[tile:tile + 1, cols],
                                  pltpu.roll(u_ref[tile - SUBLANES:tile, cols], 1, 0))
                um1 = jnp.concatenate([first, cur[0:blk - SUBLANES]], axis=0)
            else:
                um1 = u_ref[r0 - SUBLANES:r0 + blk - SUBLANES, cols]
            if r0 + blk == tile:
                last = jnp.where(sub == SUBLANES - 1, u_ref[tile + 1:tile + 2, cols],
                                 pltpu.roll(u_ref[0:SUBLANES, cols], SUBLANES - 1, 0))
                up1 = jnp.concatenate([cur[SUBLANES:blk], last], axis=0)
            else:
                up1 = u_ref[r0 + SUBLANES:r0 + blk + SUBLANES, cols]
            return bias + um1 * w[0] + cur * w[1] + up1 * w[2]

        og, ov = offsets(c)
        for r0 in range(0, tile, blk):
            gate = conv(0, og, r0)
            val = conv(chunk, ov, r0)
            act_ref[r0:r0 + blk, :] = (jax.nn.gelu(gate, approximate=True) * val).astype(BF16)

    def down(c, act_ref, nc):
        og, _ = offsets(c)
        y = _dot(act_ref[:, 0:nc * chunk], wdn_ref[pl.ds(og, nc * chunk), :])
        for n in range(nslab):
            acc_ref[n, 0:tile, :] += y[:, n * LANES:(n + 1) * LANES]

    assert nchunks % 2 == 1 and nchunks >= 3
    u_refs, act_refs = (ua_ref, ub_ref), (acta_ref, actb_ref)
    up(0, ua_ref)
    for c in range(nchunks):
        if c + 1 < nchunks:
            up(c + 1, u_refs[(c + 1) % 2])
        act = act_refs[(c // 2) % 2]
        glu(c, u_refs[c % 2], act.at[:, (c % 2) * chunk:(c % 2 + 1) * chunk])
        if c % 2 == 1:
            down(c - 1, act, 2)
    down(nchunks - 1, act_refs[((nchunks - 1) // 2) % 2], 1)
    for s in range(SUBLANES):
        rs = slice(s * ngrp, (s + 1) * ngrp)
        y = jnp.concatenate([acc_ref[n, pl.ds(s, ngrp, stride=SUBLANES), :] for n in range(nslab)],
                            axis=1)
        o_ref[rs, :] = _rms(xm_ref[rs, :] + y, fg_ref[...])


def _ffn(h2, xm, w_up, conv_w, conv_b, w_down, final_g, tile, chunk):
    b, s, d = h2.shape
    d_ff = w_down.shape[0]
    nh = tile // BF16_ROWS
    last_halo = s // BF16_ROWS - 1
    main = lambda bi, i: (bi, i, 0)
    const = lambda bi, i: (0, 0)
    single = dict(pipeline_mode=pl.Buffered(1))
    kernel = functools.partial(_ffn_kernel, tile=tile, chunk=chunk, d_ff=d_ff)
    return pl.pallas_call(
        kernel,
        grid=(b, s // tile),
        in_specs=[pl.BlockSpec((None, BF16_ROWS, d), lambda bi, i: (bi, jnp.maximum(i * nh - 1, 0), 0)),
                  pl.BlockSpec((None, tile, d), main),
                  pl.BlockSpec((None, BF16_ROWS, d),
                               lambda bi, i: (bi, jnp.minimum((i + 1) * nh, last_halo), 0)),
                  pl.BlockSpec((None, tile, d), main),
                  pl.BlockSpec(w_up.shape, const, **single),
                  pl.BlockSpec(conv_w.shape, const),
                  pl.BlockSpec(conv_b.shape, const),
                  pl.BlockSpec(w_down.shape, const, **single),
                  pl.BlockSpec((1, d), const)],
        out_specs=pl.BlockSpec((None, tile, d), main),
        out_shape=jax.ShapeDtypeStruct((b, s, d), F32),
        scratch_shapes=[pltpu.VMEM((d // LANES, tile, LANES), F32),
                        pltpu.VMEM((tile + BF16_ROWS, d), BF16)]
        + [pltpu.VMEM((tile + BF16_ROWS, 2 * chunk), F32)] * 2
        + [pltpu.VMEM((tile, 2 * chunk), BF16)] * 2
        + [pltpu.VMEM((d // LANES, tile + SUBLANES, LANES), F32)],
        compiler_params=pltpu.CompilerParams(
            dimension_semantics=("arbitrary",) * 2, vmem_limit_bytes=VMEM_LIMIT),
        name="ffn",
    )(h2, h2, h2, xm, w_up, conv_w, conv_b, w_down, final_g)


def _gate_weights(w_a, b_a, w_x, b_x):
    ndir, nblk, bd, _ = w_a.shape
    per = LANES // bd
    npair = nblk // per

    def blockdiag(w):
        w = w.reshape(npair, per, bd, bd)
        eye = jnp.eye(per, dtype=w.dtype)
        return jnp.einsum('pbij,bc->pbicj', w, eye).reshape(npair, LANES, LANES)

    ws, bs = [], []
    for d in range(ndir):
        for w, bias in ((w_a, b_a), (w_x, b_x)):
            ws.append(blockdiag(w[d]))
            bs.append(bias[d].reshape(npair, LANES))
    return ((0.5 * jnp.concatenate(ws, axis=-1)).astype(BF16),
            (0.5 * jnp.concatenate(bs, axis=-1)).astype(F32))


def kernel(x, attn_norm_g, w_in, lambda_q1, lambda_k1, lambda_q2, lambda_k2, subln_g,
           lru_conv_w, lru_conv_b, lru_w_a, lru_b_a, lru_w_x, lru_b_x, lru_lambda,
           w_out, ffn_norm_g, w_up, ffn_conv_w, ffn_conv_b, w_down, final_norm_g):
    b, s, d = x.shape
    depth = w_in.shape[0]
    x2 = x.reshape(b * s, d)
    assert depth == 1
    for l in range(depth):
        lambda_init = 0.8 - 0.6 * math.exp(-0.3 * l)
        q, k, vt, xr, gr = _inproj(x2, attn_norm_g[l][None], w_in[l], tm=512)
        lw = xr.shape[-1]
        attn, w_up_bf, w_down_bf = _attention(
            q.reshape(b, s, -1), k.reshape(b, s, -1), vt,
            lambda_q1[l][None], lambda_k1[l][None], lambda_q2[l][None], lambda_k2[l][None],
            subln_g[l][:, None], lambda_init, tq=256, nsub=4, cast_along=(w_up[l], w_down[l]))
        wg, bg = _gate_weights(lru_w_a[l], lru_b_a[l], lru_w_x[l], lru_b_x[l])
        lru = _lru(xr.reshape(b, s, lw), gr.reshape(b, s, lw),
                   jnp.repeat(lru_conv_w[l], SUBLANES, axis=0),
                   jnp.broadcast_to(lru_conv_b[l][None], (SUBLANES, lw)),
                   wg, bg, lru_lambda[l], rows=512)
        xm, h2 = _outproj(x2, attn.reshape(b * s, -1), lru.reshape(b * s, -1),
                          w_out[l], ffn_norm_g[l][None], tm=1024)
        x2 = _ffn(h2.reshape(b, s, d), xm.reshape(b, s, d), w_up_bf,
                  jnp.repeat(ffn_conv_w[l], SUBLANES, axis=0),
                  jnp.broadcast_to(ffn_conv_b[l][None], (SUBLANES, ffn_conv_b.shape[-1])),
                  w_down_bf, final_norm_g[None],
                  tile=512, chunk=256).reshape(b * s, d)
    return x2.reshape(b, s, d)
```

```python
import functools
import math

import jax
import jax.numpy as jnp
from jax import lax
from jax.experimental import pallas as pl
from jax.experimental.pallas import tpu as pltpu

F32 = jnp.float32
BF16 = jnp.bfloat16

N_HEADS = 4
HEAD_DIM = 64
V_DIM = 2 * HEAD_DIM
ATTN_WIDTH = N_HEADS * V_DIM
LRU_CONV_WIDTH = 4
LRU_CONV_LEFT = 2
LRU_C = 8.0
NORM_EPS = 1e-6
LANES = 128
SUBLANES = 8
BF16_ROWS = 16
VMEM_LIMIT = 56 * 1024 * 1024


def _rms(x, g):
    return (x * lax.rsqrt(jnp.mean(x * x, axis=-1, keepdims=True) + NORM_EPS)) * g


def _dot(a, b):
    return jnp.dot(a, b, preferred_element_type=F32)


_NT = (((1,), (1,)), ((), ()))


def _inproj_kernel(x_ref, g_ref, w_ref, q_ref, k_ref, vt_ref, xr_ref, gr_ref, wb_ref, wvt_ref):
    aw = ATTN_WIDTH
    d = x_ref.shape[-1]

    @pl.when(pl.program_id(0) == 0)
    def _cast_weights():
        rows = 2 * LANES
        for r in range(0, d, rows):
            w = w_ref[r:r + rows, :]
            wb_ref[r:r + rows, 0:aw] = (w[:, 0:aw] * (HEAD_DIM ** -0.5)).astype(BF16)
            wb_ref[r:r + rows, aw:] = w[:, aw:].astype(BF16)
        for c in range(aw // LANES):
            cols = slice(2 * aw + c * LANES, 2 * aw + (c + 1) * LANES)
            wvt_ref[c * LANES:(c + 1) * LANES, :] = w_ref[:, cols].T.astype(BF16)

    hb = _rms(x_ref[...], g_ref[...]).astype(BF16)
    q_ref[...] = _dot(hb, wb_ref[:, 0:aw]).astype(BF16)
    k_ref[...] = _dot(hb, wb_ref[:, aw:2 * aw]).astype(BF16)
    vt_ref[...] = lax.dot_general(wvt_ref[...], hb, _NT, preferred_element_type=F32).astype(BF16)
    lw = xr_ref.shape[-1]
    xr_ref[...] = _dot(hb, wb_ref[:, 3 * aw:3 * aw + lw])
    gr_ref[...] = _dot(hb, wb_ref[:, 3 * aw + lw:3 * aw + 2 * lw])


def _inproj(x2, g, w, tm):
    n, d = x2.shape
    lw = (w.shape[1] - 3 * ATTN_WIDTH) // 2
    row = lambda i: (i, 0)
    const = lambda i: (0, 0)
    return pl.pallas_call(
        _inproj_kernel,
        grid=(n // tm,),
        in_specs=[pl.BlockSpec((tm, d), row),
                  pl.BlockSpec((1, d), const),
                  pl.BlockSpec(w.shape, const, pipeline_mode=pl.Buffered(1))],
        out_specs=[pl.BlockSpec((tm, ATTN_WIDTH), row)] * 2
        + [pl.BlockSpec((ATTN_WIDTH, tm), lambda i: (0, i))]
        + [pl.BlockSpec((tm, lw), row)] * 2,
        out_shape=[jax.ShapeDtypeStruct((n, ATTN_WIDTH), BF16)] * 2
        + [jax.ShapeDtypeStruct((ATTN_WIDTH, n), BF16)]
        + [jax.ShapeDtypeStruct((n, lw), F32)] * 2,
        scratch_shapes=[pltpu.VMEM(w.shape, BF16), pltpu.VMEM((ATTN_WIDTH, d), BF16)],
        compiler_params=pltpu.CompilerParams(
            dimension_semantics=("arbitrary",), vmem_limit_bytes=VMEM_LIMIT),
        name="inproj",
    )(x2, g, w)


def _attn_kernel(lq1_ref, lk1_ref, lq2_ref, lk2_ref, sg_ref, q_ref, k_ref, vt_ref, *rest,
                 tq, nsub, seq, lambda_init):
    nside = (len(rest) - 3 - 3 * nsub) // 2
    side_in, o_ref, side_out = rest[:nside], rest[nside], rest[nside + 1:2 * nside + 1]
    kf_ref, dist_ref, *se_refs = rest[2 * nside + 1:]
    for src, dst in zip(side_in, side_out):
        dst[...] = src[...].astype(dst.dtype)

    h = pl.program_id(1)
    qi = pl.program_id(2)
    s_refs, e_refs, m_refs = se_refs[:nsub], se_refs[nsub:2 * nsub], se_refs[2 * nsub:]
    nblk = seq // tq
    assert nblk & (nblk - 1) == 0 and 3 * nblk <= LANES and tq <= 256
    shift = nblk.bit_length() - 1

    @pl.when((pl.program_id(0) == 0) & (h == 0) & (qi == 0))
    def _init():
        lane = lax.broadcasted_iota(jnp.int32, (seq, LANES), 1)
        row = lax.broadcasted_iota(jnp.int32, (seq, LANES), 0)
        grp = lane >> shift
        hit = (row // tq) == (lane & (nblk - 1))
        dj = (row % tq).astype(F32)
        kf_ref[...] = jnp.where(hit & (grp < 2), 1.0,
                                jnp.where(hit & (grp == 2), dj, 0.0)).astype(BF16)
        r = lax.broadcasted_iota(jnp.int32, (tq, tq), 0)
        c = lax.broadcasted_iota(jnp.int32, (tq, tq), 1)
        dist_ref[...] = jnp.abs(r - c).astype(F32)

    lam = (jnp.exp(jnp.sum(lq1_ref[...] * lk1_ref[...], axis=-1, keepdims=True))
           - jnp.exp(jnp.sum(lq2_ref[...] * lk2_ref[...], axis=-1, keepdims=True))
           + lambda_init)
    slope = jnp.where(h == 0, 2.0 ** -2, jnp.where(h == 1, 2.0 ** -4,
                      jnp.where(h == 2, 2.0 ** -6, 2.0 ** -8))).astype(F32)

    kaug = jnp.concatenate([k_ref[...], kf_ref[...]], axis=1)
    vt_ones = jnp.concatenate([vt_ref[...], jnp.ones((BF16_ROWS, seq), BF16)], axis=0)
    lane = lax.broadcasted_iota(jnp.int32, (tq, LANES), 1)
    di = lax.broadcasted_iota(jnp.int32, (tq, LANES), 0).astype(F32)
    grp = lane >> shift
    strip = 4 * SUBLANES

    def scores(sb):
        blk = qi * nsub + sb
        diff = blk - (lane & (nblk - 1))
        sign = jnp.where(diff > 0, 1.0, jnp.where(diff < 0, -1.0, 0.0))
        qf = jnp.where(grp == 0, -slope * sign * di,
                       jnp.where(grp == 1, -slope * tq * jnp.abs(diff).astype(F32),
                                 jnp.where(grp == 2, slope * sign, 0.0))).astype(BF16)
        q = q_ref[sb * tq:(sb + 1) * tq, :]
        zero = jnp.zeros_like(q)
        qaug = jnp.concatenate(
            [jnp.concatenate([jnp.where(lane < HEAD_DIM, q, zero), qf], axis=1),
             jnp.concatenate([jnp.where(lane >= HEAD_DIM, q, zero), qf], axis=1)], axis=0)
        s_ref = s_refs[sb]
        s = lax.dot_general(kaug, qaug, _NT, preferred_element_type=F32)
        s_ref[...] = s
        mx = s[0:strip, :]
        for r in range(strip, seq, strip):
            mx = jnp.maximum(mx, s[r:r + strip, :])
        m_refs[sb][...] = mx
        diag = pl.ds(pl.multiple_of(blk * tq, tq), tq)
        diag_bias = slope * dist_ref[...]
        s_ref[diag, 0:tq] = s_ref[diag, 0:tq] - diag_bias
        s_ref[diag, tq:2 * tq] = s_ref[diag, tq:2 * tq] - diag_bias

    def finish(sb):
        s_ref, e_ref = s_refs[sb], e_refs[sb]
        mx = jnp.max(m_refs[sb][...], axis=0, keepdims=True)
        for r in range(0, seq, strip):
            e_ref[r:r + strip, :] = jnp.exp(s_ref[r:r + strip, :] - mx).astype(BF16)
        o12 = _dot(vt_ones, e_ref[...])
        norm = o12[V_DIM:V_DIM + 1, :]
        o12 = o12[0:V_DIM, :]
        o = o12[:, 0:tq] * (1.0 / norm[:, 0:tq]) - o12[:, tq:2 * tq] * (lam / norm[:, tq:2 * tq])
        o = o * lax.rsqrt(jnp.mean(o * o, axis=0, keepdims=True) + NORM_EPS)
        o = o * sg_ref[...] * (1.0 - lambda_init)
        o_ref[sb * tq:(sb + 1) * tq, :] = o.T.astype(o_ref.dtype)

    scores(0)
    for sb in range(nsub):
        if sb + 1 < nsub:
            scores(sb + 1)
        finish(sb)


def _attention(q, k, vt, lq1, lk1, lq2, lk2, subln_g, lambda_init, tq, nsub, cast_along=()):
    b, s, _ = q.shape
    vec = lambda bi, h, qi: (0, 0)
    tstep = tq * nsub
    nq = s // tstep
    nsteps = b * N_HEADS * nq

    def row_block(last):
        return lambda bi, h, qi: (jnp.minimum((bi * N_HEADS + h) * nq + qi, last), 0)

    side_in, side_out, side_shapes = [], [], []
    for w in cast_along:
        rows = -(-w.shape[0] // (nsteps * BF16_ROWS)) * BF16_ROWS
        while w.shape[0] % rows:
            rows += BF16_ROWS
        for specs in (side_in, side_out):
            specs.append(pl.BlockSpec((rows, w.shape[1]), row_block(w.shape[0] // rows - 1)))
        side_shapes.append(jax.ShapeDtypeStruct(w.shape, BF16))
    kernel = functools.partial(_attn_kernel, tq=tq, nsub=nsub, seq=s, lambda_init=lambda_init)
    return pl.pallas_call(
        kernel,
        grid=(b, N_HEADS, s // tstep),
        in_specs=[pl.BlockSpec((1, HEAD_DIM), vec)] * 4
        + [pl.BlockSpec((V_DIM, 1), vec),
           pl.BlockSpec((None, tstep, V_DIM), lambda bi, h, qi: (bi, qi, h)),
           pl.BlockSpec((None, s, V_DIM), lambda bi, h, qi: (bi, 0, h)),
           pl.BlockSpec((V_DIM, s), lambda bi, h, qi: (h, bi))] + side_in,
        out_specs=[pl.BlockSpec((None, tstep, V_DIM), lambda bi, h, qi: (bi, qi, h))] + side_out,
        out_shape=[jax.ShapeDtypeStruct((b, s, ATTN_WIDTH), BF16)] + side_shapes,
        scratch_shapes=[pltpu.VMEM((s, LANES), BF16), pltpu.VMEM((tq, tq), F32)]
        + [pltpu.VMEM((s, 2 * tq), F32)] * nsub + [pltpu.VMEM((s, 2 * tq), BF16)] * nsub
        + [pltpu.VMEM((4 * SUBLANES, 2 * tq), F32)] * nsub,
        compiler_params=pltpu.CompilerParams(
            dimension_semantics=("arbitrary",) * 3, vmem_limit_bytes=VMEM_LIMIT),
        name="attn",
    )(lq1, lk1, lq2, lk2, subln_g, q, k, vt, *cast_along)


def _local_scan(a, u, reverse):
    row = lax.broadcasted_iota(jnp.int32, a.shape, 0)
    for d in (1, 2, 4):
        shift = SUBLANES - d if reverse else d
        valid = (row < SUBLANES - d) if reverse else (row >= d)
        a_s = jnp.where(valid, pltpu.roll(a, shift, 0), 1.0)
        u_s = jnp.where(valid, pltpu.roll(u, shift, 0), 0.0)
        u = a * u_s + u
        a = a * a_s
    return a, u


def _lru_kernel(xr_ref, gr_ref, cw_ref, cb_ref, wg_ref, bg_ref, lam_ref, o_ref,
                xp_ref, xc_ref, gates_ref, h_ref, cum_ref, *, seq, rows):
    width = xr_ref.shape[-1]
    ntile = width // LANES
    grp = seq // SUBLANES
    left = LRU_CONV_LEFT
    right = LRU_CONV_WIDTH - 1 - LRU_CONV_LEFT
    top = left * SUBLANES
    sub = lax.broadcasted_iota(jnp.int32, (SUBLANES, LANES), 0)

    for s in range(SUBLANES):
        blk = xr_ref[s * grp:(s + 1) * grp, :]
        for p in range(ntile):
            xp_ref[p, pl.ds(top + s, grp, stride=SUBLANES), :] = blk[:, p * LANES:(p + 1) * LANES]
    for p in range(ntile):
        for k in range(left):
            src = xp_ref[p, top + (grp - 1 - k) * SUBLANES:top + (grp - k) * SUBLANES, :]
            xp_ref[p, top - (k + 1) * SUBLANES:top - k * SUBLANES, :] = jnp.where(
                sub == 0, 0.0, pltpu.roll(src, 1, 0))
        for k in range(right):
            src = xp_ref[p, top + k * SUBLANES:top + (k + 1) * SUBLANES, :]
            xp_ref[p, top + (grp + k) * SUBLANES:top + (grp + k + 1) * SUBLANES, :] = jnp.where(
                sub == SUBLANES - 1, 0.0, pltpu.roll(src, SUBLANES - 1, 0))

    neg_lam = -lam_ref[...]
    softplus = jnp.maximum(neg_lam, 0.0) + jnp.log1p(jnp.exp(-jnp.abs(neg_lam)))
    rate = LRU_C * softplus
    rate_log2 = -rate * math.log2(math.e)

    nchunk = seq // rows
    for c in range(nchunk):
        r0 = c * rows
        for p in range(ntile):
            cols = slice(p * LANES, (p + 1) * LANES)
            reps = (rows // SUBLANES, 1)
            xc = jnp.tile(cb_ref[:, cols], reps)
            for tap in range(LRU_CONV_WIDTH):
                start = top + r0 + (tap - left) * SUBLANES
                w = jnp.tile(cw_ref[tap * SUBLANES:(tap + 1) * SUBLANES, cols], reps)
                xc = xc + xp_ref[p, start:start + rows, :] * w
            xc_ref[p, r0:r0 + rows, :] = xc

    zero = jnp.zeros((SUBLANES, LANES), F32)
    one = jnp.ones((SUBLANES, LANES), F32)
    nvr = rows // SUBLANES
    tile8 = lambda row: jnp.broadcast_to(row, (SUBLANES, LANES))

    def chunk(c, ends):
        ends = [list(ends[2 * k:2 * k + 2]) for k in range(2 * ntile)]
        base = (pl.multiple_of(c * rows, rows), pl.multiple_of((nchunk - 1 - c) * rows, rows))
        for p in range(ntile):
            for d in range(2):
                gcols = slice(2 * d * LANES, (2 * d + 2) * LANES)
                xcb = xc_ref[p, pl.ds(base[d], rows), :].astype(BF16)
                gates_ref[2 * p + d] = _dot(xcb, wg_ref[p, :, gcols]) + bg_ref[p:p + 1, gcols]
        for step in range(nvr):
            for p in range(ntile):
                cols = slice(p * LANES, (p + 1) * LANES)
                for d in range(2):
                    j = step if d == 0 else nvr - 1 - step
                    row = pl.ds(base[d] + j * SUBLANES, SUBLANES)
                    g = gates_ref[2 * p + d, j * SUBLANES:(j + 1) * SUBLANES, :]
                    th = jnp.tanh(g)
                    r = 0.5 + 0.5 * th[:, 0:LANES]
                    i = 0.5 + 0.5 * th[:, LANES:2 * LANES]
                    t = jnp.tanh(r * tile8(rate[d:d + 1, cols]))
                    tt = t + t
                    prod = tt * (1.0 + t)
                    mult = jnp.where(prod > 0.0, tt * lax.rsqrt(prod), 0.0)
                    a = jnp.exp2(r * tile8(rate_log2[d:d + 1, cols]))
                    u = mult * (i * xc_ref[p, row, :])
                    h, cum = ends[2 * p + d]
                    h = a * h + u
                    cum = a * cum
                    h_ref[d, p, row, :] = h
                    cum_ref[d, p, row, :] = cum
                    ends[2 * p + d] = [h, cum]
        return tuple(x for pair in ends for x in pair)

    flat = lax.fori_loop(0, nchunk, chunk, (zero, one) * (2 * ntile))
    ends = [[(flat[2 * (2 * p + d)], flat[2 * (2 * p + d) + 1]) for d in range(2)]
            for p in range(ntile)]

    enter = []
    for p in range(ntile):
        for d in range(2):
            h_end, cum_end = ends[p][d]
            _, chained = _local_scan(cum_end, h_end, reverse=(d == 1))
            if d == 0:
                enter.append(jnp.where(sub == 0, 0.0, pltpu.roll(chained, 1, 0)))
            else:
                enter.append(jnp.where(sub == SUBLANES - 1, 0.0,
                                       pltpu.roll(chained, SUBLANES - 1, 0)))

    for p in range(ntile):
        for j in range(grp):
            rs = slice(j * SUBLANES, (j + 1) * SUBLANES)
            xc_ref[p, rs, :] = ((h_ref[0, p, rs, :] + cum_ref[0, p, rs, :] * enter[2 * p])
                                + (h_ref[1, p, rs, :] + cum_ref[1, p, rs, :] * enter[2 * p + 1]))

    blk = BF16_ROWS

    def segment(s, carry):
        for p in range(ntile):
            cols = slice(p * LANES, (p + 1) * LANES)
            for j0 in range(0, grp, blk):
                rs = pl.ds(pl.multiple_of(s * grp + j0, blk), blk)
                y = xc_ref[p, pl.ds(s + j0 * SUBLANES, blk, stride=SUBLANES), :]
                o_ref[rs, cols] = (jax.nn.gelu(gr_ref[rs, cols], approximate=True)
                                   * y).astype(o_ref.dtype)
        return carry

    lax.fori_loop(0, SUBLANES, segment, 0)


def _lru(xr, gr, conv_w, conv_b, wg, bg, lru_lambda, rows):
    b, s, width = xr.shape
    ntile = width // LANES
    seqblk = pl.BlockSpec((None, s, width), lambda bi: (bi, 0, 0))
    full = lambda a: pl.BlockSpec(a.shape, lambda bi: (0,) * a.ndim)
    kernel = functools.partial(_lru_kernel, seq=s, rows=rows)
    halo_rows = (LRU_CONV_WIDTH - 1) * SUBLANES
    return pl.pallas_call(
        kernel,
        grid=(b,),
        in_specs=[seqblk, seqblk, full(conv_w), full(conv_b), full(wg), full(bg), full(lru_lambda)],
        out_specs=seqblk,
        out_shape=jax.ShapeDtypeStruct((b, s, width), BF16),
        scratch_shapes=[pltpu.VMEM((ntile, s + halo_rows, LANES), F32),
                        pltpu.VMEM((ntile, s, LANES), F32),
                        pltpu.VMEM((2 * ntile, rows, 2 * LANES), F32),
                        pltpu.VMEM((2, ntile, s, LANES), F32),
                        pltpu.VMEM((2, ntile, s, LANES), F32)],
        compiler_params=pltpu.CompilerParams(
            dimension_semantics=("arbitrary",), vmem_limit_bytes=VMEM_LIMIT),
        name="lru",
    )(xr, gr, conv_w, conv_b, wg, bg, lru_lambda)


def _outproj_kernel(x_ref, a_ref, l_ref, w_ref, g_ref, xm_ref, h_ref, wb_ref):
    @pl.when(pl.program_id(0) == 0)
    def _cast_weights():
        wb_ref[...] = w_ref[...].astype(BF16)

    aw = a_ref.shape[-1]
    xm = x_ref[...] + _dot(a_ref[...], wb_ref[0:aw, :]) + _dot(l_ref[...], wb_ref[aw:, :])
    xm_ref[...] = xm
    h_ref[...] = _rms(xm, g_ref[...]).astype(h_ref.dtype)


def _outproj(x2, attn2, lru2, w, g, tm):
    n, d = x2.shape
    row = lambda i: (i, 0)
    const = lambda i: (0, 0)
    return pl.pallas_call(
        _outproj_kernel,
        grid=(n // tm,),
        in_specs=[pl.BlockSpec((tm, d), row),
                  pl.BlockSpec((tm, attn2.shape[1]), row),
                  pl.BlockSpec((tm, lru2.shape[1]), row),
                  pl.BlockSpec(w.shape, const, pipeline_mode=pl.Buffered(1)),
                  pl.BlockSpec((1, d), const)],
        out_specs=[pl.BlockSpec((tm, d), row)] * 2,
        out_shape=[jax.ShapeDtypeStruct((n, d), F32), jax.ShapeDtypeStruct((n, d), BF16)],
        scratch_shapes=[pltpu.VMEM(w.shape, BF16)],
        compiler_params=pltpu.CompilerParams(
            dimension_semantics=("arbitrary",), vmem_limit_bytes=VMEM_LIMIT),
        name="outproj",
    )(x2, attn2, lru2, w, g)


def _ffn_kernel(hp_ref, hm_ref, hn_ref, xm_ref, wup_ref, cw_ref, cb_ref, wdn_ref, fg_ref, o_ref,
                perm_ref, hext_ref, ua_ref, ub_ref, act_ref, y_ref,
                *, tile, chunk, d_ff):
    i = pl.program_id(1)
    nchunks = d_ff // chunk
    ngrp = tile // SUBLANES
    nslab = perm_ref.shape[0]
    d = nslab * LANES

    for s in range(SUBLANES):
        rows = hm_ref[s * ngrp:(s + 1) * ngrp, :].astype(F32)
        for n in range(nslab):
            perm_ref[n, pl.ds(s, ngrp, stride=SUBLANES), :] = rows[:, n * LANES:(n + 1) * LANES]
    for n in range(nslab):
        hext_ref[0:tile, n * LANES:(n + 1) * LANES] = perm_ref[n].astype(BF16)
    prev = jnp.where(i > 0, hp_ref[BF16_ROWS - 1:BF16_ROWS, :].astype(F32), 0.0)
    nxt = jnp.where(i < pl.num_programs(1) - 1, hn_ref[0:1, :].astype(F32), 0.0)
    hrow = lax.broadcasted_iota(jnp.int32, (BF16_ROWS, d), 0)
    halo = jnp.where(hrow == 0, prev, jnp.where(hrow == 1, nxt, 0.0))
    hext_ref[tile:tile + BF16_ROWS, :] = halo.astype(BF16)

    def offsets(c):
        og, ov = c * chunk, d_ff + c * chunk
        if isinstance(c, int):
            return og, ov
        return pl.multiple_of(og, chunk), pl.multiple_of(ov, chunk)

    def up(c, u_ref):
        og, ov = offsets(c)
        hext = hext_ref[...]
        u_ref[:, 0:chunk] = _dot(hext, wup_ref[:, pl.ds(og, chunk)])
        u_ref[:, chunk:2 * chunk] = _dot(hext, wup_ref[:, pl.ds(ov, chunk)])

    def glu(c, u_ref, act_ref):
        sub = lax.broadcasted_iota(jnp.int32, (SUBLANES, chunk), 0)
        blk = BF16_ROWS

        def conv(col0, off, r0):
            cols = slice(col0, col0 + chunk)
            reps = (blk // SUBLANES, 1)
            w = [jnp.tile(cw_ref[k * SUBLANES:(k + 1) * SUBLANES, pl.ds(off, chunk)], reps)
                 for k in range(3)]
            bias = jnp.tile(cb_ref[:, pl.ds(off, chunk)], reps)
            cur = u_ref[r0:r0 + blk, cols]
            if r0 == 0:
                first = jnp.where(sub == 0, u_ref[tile:tile + 1, cols],
                                  pltpu.roll(u_ref[tile - SUBLANES:tile, cols], 1, 0))
                um1 = jnp.concatenate([first, cur[0:blk - SUBLANES]], axis=0)
            else:
                um1 = u_ref[r0 - SUBLANES:r0 + blk - SUBLANES, cols]
            if r0 + blk == tile:
                last = jnp.where(sub == SUBLANES - 1, u_ref[tile + 1:tile + 2, cols],
                                 pltpu.roll(u_ref[0:SUBLANES, cols], SUBLANES - 1, 0))
                up1 = jnp.concatenate([cur[SUBLANES:blk], last], axis=0)
            else:
                up1 = u_ref[r0 + SUBLANES:r0 + blk + SUBLANES, cols]
            return bias + um1 * w[0] + cur * w[1] + up1 * w[2]

        og, ov = offsets(c)
        for r0 in range(0, tile, blk):
            gate = conv(0, og, r0)
            val = conv(chunk, ov, r0)
            act_ref[r0:r0 + blk, :] = (jax.nn.gelu(gate, approximate=True) * val).astype(BF16)

    u_refs = (ua_ref, ub_ref)
    up(0, ua_ref)
    for c in range(nchunks):
        if c + 1 < nchunks:
            up(c + 1, u_refs[(c + 1) % 2])
        glu(c, u_refs[c % 2], act_ref.at[:, c * chunk:(c + 1) * chunk])
    y = _dot(act_ref[...], wdn_ref[...])
    for n in range(nslab):
        y_ref[n, 0:tile, :] = y[:, n * LANES:(n + 1) * LANES]
    for s in range(SUBLANES):
        rs = slice(s * ngrp, (s + 1) * ngrp)
        y = jnp.concatenate([y_ref[n, pl.ds(s, ngrp, stride=SUBLANES), :] for n in range(nslab)],
                            axis=1)
        o_ref[rs, :] = _rms(xm_ref[rs, :] + y, fg_ref[...])


def _ffn(h2, xm, w_up, conv_w, conv_b, w_down, final_g, tile, chunk):
    b, s, d = h2.shape
    d_ff = w_down.shape[0]
    nh = tile // BF16_ROWS
    last_halo = s // BF16_ROWS - 1
    main = lambda bi, i: (bi, i, 0)
    const = lambda bi, i: (0, 0)
    single = dict(pipeline_mode=pl.Buffered(1))
    kernel = functools.partial(_ffn_kernel, tile=tile, chunk=chunk, d_ff=d_ff)
    return pl.pallas_call(
        kernel,
        grid=(b, s // tile),
        in_specs=[pl.BlockSpec((None, BF16_ROWS, d), lambda bi, i: (bi, jnp.maximum(i * nh - 1, 0), 0)),
                  pl.BlockSpec((None, tile, d), main),
                  pl.BlockSpec((None, BF16_ROWS, d),
                               lambda bi, i: (bi, jnp.minimum((i + 1) * nh, last_halo), 0)),
                  pl.BlockSpec((None, tile, d), main),
                  pl.BlockSpec(w_up.shape, const, **single),
                  pl.BlockSpec(conv_w.shape, const),
                  pl.BlockSpec(conv_b.shape, const),
                  pl.BlockSpec(w_down.shape, const, **single),
                  pl.BlockSpec((1, d), const)],
        out_specs=pl.BlockSpec((None, tile, d), main),
        out_shape=jax.ShapeDtypeStruct((b, s, d), F32),
        scratch_shapes=[pltpu.VMEM((d // LANES, tile, LANES), F32),
                        pltpu.VMEM((tile + BF16_ROWS, d), BF16)]
        + [pltpu.VMEM((tile + BF16_ROWS, 2 * chunk), F32)] * 2
        + [pltpu.VMEM((tile, d_ff), BF16),
           pltpu.VMEM((d // LANES, tile + SUBLANES, LANES), F32)],
        compiler_params=pltpu.CompilerParams(
            dimension_semantics=("arbitrary",) * 2, vmem_limit_bytes=VMEM_LIMIT),
        name="ffn",
    )(h2, h2, h2, xm, w_up, conv_w, conv_b, w_down, final_g)


def _gate_weights(w_a, b_a, w_x, b_x):
    ndir, nblk, bd, _ = w_a.shape
    per = LANES // bd
    npair = nblk // per

    def blockdiag(w):
        w = w.reshape(npair, per, bd, bd)
        eye = jnp.eye(per, dtype=w.dtype)
        return jnp.einsum('pbij,bc->pbicj', w, eye).reshape(npair, LANES, LANES)

    ws, bs = [], []
    for d in range(ndir):
        for w, bias in ((w_a, b_a), (w_x, b_x)):
            ws.append(blockdiag(w[d]))
            bs.append(bias[d].reshape(npair, LANES))
    return ((0.5 * jnp.concatenate(ws, axis=-1)).astype(BF16),
            (0.5 * jnp.concatenate(bs, axis=-1)).astype(F32))


def kernel(x, attn_norm_g, w_in, lambda_q1, lambda_k1, lambda_q2, lambda_k2, subln_g,
           lru_conv_w, lru_conv_b, lru_w_a, lru_b_a, lru_w_x, lru_b_x, lru_lambda,
           w_out, ffn_norm_g, w_up, ffn_conv_w, ffn_conv_b, w_down, final_norm_g):
    b, s, d = x.shape
    depth = w_in.shape[0]
    x2 = x.reshape(b * s, d)
    assert depth == 1
    for l in range(depth):
        lambda_init = 0.8 - 0.6 * math.exp(-0.3 * l)
        q, k, vt, xr, gr = _inproj(x2, attn_norm_g[l][None], w_in[l], tm=512)
        lw = xr.shape[-1]
        attn, w_up_bf, w_down_bf = _attention(
            q.reshape(b, s, -1), k.reshape(b, s, -1), vt,
            lambda_q1[l][None], lambda_k1[l][None], lambda_q2[l][None], lambda_k2[l][None],
            subln_g[l][:, None], lambda_init, tq=256, nsub=4, cast_along=(w_up[l], w_down[l]))
        wg, bg = _gate_weights(lru_w_a[l], lru_b_a[l], lru_w_x[l], lru_b_x[l])
        lru = _lru(xr.reshape(b, s, lw), gr.reshape(b, s, lw),
                   jnp.repeat(lru_conv_w[l], SUBLANES, axis=0),
                   jnp.broadcast_to(lru_conv_b[l][None], (SUBLANES, lw)),
                   wg, bg, lru_lambda[l], rows=512)
        xm, h2 = _outproj(x2, attn.reshape(b * s, -1), lru.reshape(b * s, -1),
                          w_out[l], ffn_norm_g[l][None], tm=1024)
        x2 = _ffn(h2.reshape(b, s, d), xm.reshape(b, s, d), w_up_bf,
                  jnp.repeat(ffn_conv_w[l], SUBLANES, axis=0),
                  jnp.broadcast_to(ffn_conv_b[l][None], (SUBLANES, ffn_conv_b.shape[-1])),
                  w_down_bf, final_norm_g[None],
                  tile=512, chunk=256).reshape(b * s, d)
    return x2.reshape(b, s, d)
```

```python
import functools
import math

import jax
import jax.numpy as jnp
from jax import lax
from jax.experimental import pallas as pl
from jax.experimental.pallas import tpu as pltpu

F32 = jnp.float32
BF16 = jnp.bfloat16

N_HEADS = 4
HEAD_DIM = 64
V_DIM = 2 * HEAD_DIM
ATTN_WIDTH = N_HEADS * V_DIM
LRU_CONV_WIDTH = 4
LRU_CONV_LEFT = 2
LRU_C = 8.0
NORM_EPS = 1e-6
LOG2_E = math.log2(math.e)
LANES = 128
SUBLANES = 8
BF16_ROWS = 16
VMEM_LIMIT = 56 * 1024 * 1024


def _rms(x, g):
    return (x * lax.rsqrt(jnp.mean(x * x, axis=-1, keepdims=True) + NORM_EPS)) * g


def _dot(a, b):
    return jnp.dot(a, b, preferred_element_type=F32)


_NT = (((1,), (1,)), ((), ()))


def _inproj_kernel(x_ref, g_ref, w_ref, q_ref, k_ref, vt_ref, xr_ref, gr_ref, wb_ref, wvt_ref):
    aw = ATTN_WIDTH
    d = x_ref.shape[-1]

    @pl.when(pl.program_id(0) == 0)
    def _cast_weights():
        rows = 2 * LANES
        for r in range(0, d, rows):
            w = w_ref[r:r + rows, :]
            wb_ref[r:r + rows, 0:aw] = (w[:, 0:aw] * (HEAD_DIM ** -0.5 * LOG2_E)).astype(BF16)
            wb_ref[r:r + rows, aw:] = w[:, aw:].astype(BF16)
        for c in range(aw // LANES):
            cols = slice(2 * aw + c * LANES, 2 * aw + (c + 1) * LANES)
            wvt_ref[c * LANES:(c + 1) * LANES, :] = w_ref[:, cols].T.astype(BF16)

    hb = _rms(x_ref[...], g_ref[...]).astype(BF16)
    q_ref[...] = _dot(hb, wb_ref[:, 0:aw]).astype(BF16)
    k_ref[...] = _dot(hb, wb_ref[:, aw:2 * aw]).astype(BF16)
    vt_ref[...] = lax.dot_general(wvt_ref[...], hb, _NT, preferred_element_type=F32).astype(BF16)
    lw = xr_ref.shape[-1]
    xr_ref[...] = _dot(hb, wb_ref[:, 3 * aw:3 * aw + lw])
    gr_ref[...] = _dot(hb, wb_ref[:, 3 * aw + lw:3 * aw + 2 * lw])


def _inproj(x2, g, w, tm):
    n, d = x2.shape
    lw = (w.shape[1] - 3 * ATTN_WIDTH) // 2
    row = lambda i: (i, 0)
    const = lambda i: (0, 0)
    return pl.pallas_call(
        _inproj_kernel,
        grid=(n // tm,),
        in_specs=[pl.BlockSpec((tm, d), row),
                  pl.BlockSpec((1, d), const),
                  pl.BlockSpec(w.shape, const, pipeline_mode=pl.Buffered(1))],
        out_specs=[pl.BlockSpec((tm, ATTN_WIDTH), row)] * 2
        + [pl.BlockSpec((ATTN_WIDTH, tm), lambda i: (0, i))]
        + [pl.BlockSpec((tm, lw), row)] * 2,
        out_shape=[jax.ShapeDtypeStruct((n, ATTN_WIDTH), BF16)] * 2
        + [jax.ShapeDtypeStruct((ATTN_WIDTH, n), BF16)]
        + [jax.ShapeDtypeStruct((n, lw), F32)] * 2,
        scratch_shapes=[pltpu.VMEM(w.shape, BF16), pltpu.VMEM((ATTN_WIDTH, d), BF16)],
        compiler_params=pltpu.CompilerParams(
            dimension_semantics=("arbitrary",), vmem_limit_bytes=VMEM_LIMIT),
        name="inproj",
    )(x2, g, w)


def _attn_kernel(lq1_ref, lk1_ref, lq2_ref, lk2_ref, sg_ref, q_ref, k_ref, vt_ref, *rest,
                 tq, nsub, seq, lambda_init):
    nside = (len(rest) - 3 - 3 * nsub) // 2
    side_in, o_ref, side_out = rest[:nside], rest[nside], rest[nside + 1:2 * nside + 1]
    kf_ref, dist_ref, *se_refs = rest[2 * nside + 1:]
    for src, dst in zip(side_in, side_out):
        dst[...] = src[...].astype(dst.dtype)

    h = pl.program_id(1)
    qi = pl.program_id(2)
    s_refs, e_refs, m_refs = se_refs[:nsub], se_refs[nsub:2 * nsub], se_refs[2 * nsub:]
    nblk = seq // tq
    assert nblk & (nblk - 1) == 0 and 9 * nblk <= LANES and tq <= 256
    shift = nblk.bit_length() - 1

    @pl.when((pl.program_id(0) == 0) & (h == 0) & (qi == 0))
    def _init():
        lane = lax.broadcasted_iota(jnp.int32, (seq, LANES), 1)
        row = lax.broadcasted_iota(jnp.int32, (seq, LANES), 0)
        grp = lane >> shift
        hit = (row // tq) == (lane & (nblk - 1))
        dj = (row % tq).astype(F32)
        kf_ref[...] = jnp.where(hit & (grp < 6), 1.0,
                                jnp.where(hit & (grp < 9), dj, 0.0)).astype(BF16)
        r = lax.broadcasted_iota(jnp.int32, (tq, tq), 0)
        c = lax.broadcasted_iota(jnp.int32, (tq, tq), 1)
        dist_ref[...] = jnp.abs(r - c).astype(F32)

    lam = (jnp.exp(jnp.sum(lq1_ref[...] * lk1_ref[...], axis=-1, keepdims=True))
           - jnp.exp(jnp.sum(lq2_ref[...] * lk2_ref[...], axis=-1, keepdims=True))
           + lambda_init)
    slope = LOG2_E * jnp.where(h == 0, 2.0 ** -2, jnp.where(h == 1, 2.0 ** -4,
                               jnp.where(h == 2, 2.0 ** -6, 2.0 ** -8))).astype(F32)

    kaug = jnp.concatenate([k_ref[...], kf_ref[...]], axis=1)
    vt_ones = jnp.concatenate([vt_ref[...], jnp.ones((BF16_ROWS, seq), BF16)], axis=0)
    lane = lax.broadcasted_iota(jnp.int32, (tq, LANES), 1)
    di = lax.broadcasted_iota(jnp.int32, (tq, LANES), 0).astype(F32)
    grp = lane >> shift
    term = (grp >= 3).astype(jnp.int32) + (grp >= 6).astype(jnp.int32)
    piece = grp - 3 * term
    strip = 4 * SUBLANES

    def scores(sb):
        blk = qi * nsub + sb
        diff = blk - (lane & (nblk - 1))
        sign = jnp.where(diff > 0, 1.0, jnp.where(diff < 0, -1.0, 0.0))
        fac = jnp.where(term == 0, -slope * sign * di,
                        jnp.where(term == 1, -slope * tq * jnp.abs(diff).astype(F32), slope * sign))
        hi = fac.astype(BF16)
        rest1 = fac - hi.astype(F32)
        mid = rest1.astype(BF16)
        lo = (rest1 - mid.astype(F32)).astype(BF16)
        qf = jnp.where(grp >= 9, jnp.zeros_like(hi),
                       jnp.where(piece == 0, hi, jnp.where(piece == 1, mid, lo)))
        q = q_ref[sb * tq:(sb + 1) * tq, :]
        zero = jnp.zeros_like(q)
        qaug = jnp.concatenate(
            [jnp.concatenate([jnp.where(lane < HEAD_DIM, q, zero), qf], axis=1),
             jnp.concatenate([jnp.where(lane >= HEAD_DIM, q, zero), qf], axis=1)], axis=0)
        s_ref = s_refs[sb]
        s = lax.dot_general(kaug, qaug, _NT, preferred_element_type=F32)
        s_ref[...] = s
        mx = s[0:strip, :]
        for r in range(strip, seq, strip):
            mx = jnp.maximum(mx, s[r:r + strip, :])
        m_refs[sb][...] = mx
        diag = pl.ds(pl.multiple_of(blk * tq, tq), tq)
        diag_bias = slope * dist_ref[...]
        s_ref[diag, 0:tq] = s_ref[diag, 0:tq] - diag_bias
        s_ref[diag, tq:2 * tq] = s_ref[diag, tq:2 * tq] - diag_bias

    def finish(sb):
        s_ref, e_ref = s_refs[sb], e_refs[sb]
        mx = jnp.max(m_refs[sb][...], axis=0, keepdims=True)
        for r in range(0, seq, strip):
            e_ref[r:r + strip, :] = jnp.exp2(s_ref[r:r + strip, :] - mx).astype(BF16)
        o12 = _dot(vt_ones, e_ref[...])
        norm = o12[V_DIM:V_DIM + 1, :]
        o12 = o12[0:V_DIM, :]
        o = o12[:, 0:tq] * (1.0 / norm[:, 0:tq]) - o12[:, tq:2 * tq] * (lam / norm[:, tq:2 * tq])
        o = o * lax.rsqrt(jnp.mean(o * o, axis=0, keepdims=True) + NORM_EPS)
        o = o * sg_ref[...] * (1.0 - lambda_init)
        o_ref[sb * tq:(sb + 1) * tq, :] = o.T.astype(o_ref.dtype)

    scores(0)
    for sb in range(nsub):
        if sb + 1 < nsub:
            scores(sb + 1)
        finish(sb)


def _attention(q, k, vt, lq1, lk1, lq2, lk2, subln_g, lambda_init, tq, nsub, cast_along=()):
    b, s, _ = q.shape
    vec = lambda bi, h, qi: (0, 0)
    tstep = tq * nsub
    nq = s // tstep
    nsteps = b * N_HEADS * nq

    def row_block(last):
        return lambda bi, h, qi: (jnp.minimum((bi * N_HEADS + h) * nq + qi, last), 0)

    side_in, side_out, side_shapes = [], [], []
    for w in cast_along:
        rows = -(-w.shape[0] // (nsteps * BF16_ROWS)) * BF16_ROWS
        while w.shape[0] % rows:
            rows += BF16_ROWS
        for specs in (side_in, side_out):
            specs.append(pl.BlockSpec((rows, w.shape[1]), row_block(w.shape[0] // rows - 1)))
        side_shapes.append(jax.ShapeDtypeStruct(w.shape, BF16))
    kernel = functools.partial(_attn_kernel, tq=tq, nsub=nsub, seq=s, lambda_init=lambda_init)
    return pl.pallas_call(
        kernel,
        grid=(b, N_HEADS, s // tstep),
        in_specs=[pl.BlockSpec((1, HEAD_DIM), vec)] * 4
        + [pl.BlockSpec((V_DIM, 1), vec),
           pl.BlockSpec((None, tstep, V_DIM), lambda bi, h, qi: (bi, qi, h)),
           pl.BlockSpec((None, s, V_DIM), lambda bi, h, qi: (bi, 0, h)),
           pl.BlockSpec((V_DIM, s), lambda bi, h, qi: (h, bi))] + side_in,
        out_specs=[pl.BlockSpec((None, tstep, V_DIM), lambda bi, h, qi: (bi, qi, h))] + side_out,
        out_shape=[jax.ShapeDtypeStruct((b, s, ATTN_WIDTH), BF16)] + side_shapes,
        scratch_shapes=[pltpu.VMEM((s, LANES), BF16), pltpu.VMEM((tq, tq), F32)]
        + [pltpu.VMEM((s, 2 * tq), F32)] * nsub + [pltpu.VMEM((s, 2 * tq), BF16)] * nsub
        + [pltpu.VMEM((4 * SUBLANES, 2 * tq), F32)] * nsub,
        compiler_params=pltpu.CompilerParams(
            dimension_semantics=("arbitrary",) * 3, vmem_limit_bytes=VMEM_LIMIT),
        name="attn",
    )(lq1, lk1, lq2, lk2, subln_g, q, k, vt, *cast_along)


def _local_scan(a, u, reverse):
    row = lax.broadcasted_iota(jnp.int32, a.shape, 0)
    for d in (1, 2, 4):
        shift = SUBLANES - d if reverse else d
        valid = (row < SUBLANES - d) if reverse else (row >= d)
        a_s = jnp.where(valid, pltpu.roll(a, shift, 0), 1.0)
        u_s = jnp.where(valid, pltpu.roll(u, shift, 0), 0.0)
        u = a * u_s + u
        a = a * a_s
    return a, u


def _lru_kernel(xr_ref, gr_ref, cw_ref, cb_ref, wg_ref, bg_ref, lam_ref, o_ref,
                xp_ref, xc_ref, gates_ref, h_ref, cum_ref, *, seq, rows):
    width = xr_ref.shape[-1]
    ntile = width // LANES
    grp = seq // SUBLANES
    left = LRU_CONV_LEFT
    right = LRU_CONV_WIDTH - 1 - LRU_CONV_LEFT
    top = left * SUBLANES
    sub = lax.broadcasted_iota(jnp.int32, (SUBLANES, LANES), 0)

    for s in range(SUBLANES):
        blk = xr_ref[s * grp:(s + 1) * grp, :]
        for p in range(ntile):
            xp_ref[p, pl.ds(top + s, grp, stride=SUBLANES), :] = blk[:, p * LANES:(p + 1) * LANES]
    for p in range(ntile):
        for k in range(left):
            src = xp_ref[p, top + (grp - 1 - k) * SUBLANES:top + (grp - k) * SUBLANES, :]
            xp_ref[p, top - (k + 1) * SUBLANES:top - k * SUBLANES, :] = jnp.where(
                sub == 0, 0.0, pltpu.roll(src, 1, 0))
        for k in range(right):
            src = xp_ref[p, top + k * SUBLANES:top + (k + 1) * SUBLANES, :]
            xp_ref[p, top + (grp + k) * SUBLANES:top + (grp + k + 1) * SUBLANES, :] = jnp.where(
                sub == SUBLANES - 1, 0.0, pltpu.roll(src, SUBLANES - 1, 0))

    neg_lam = -lam_ref[...]
    softplus = jnp.maximum(neg_lam, 0.0) + jnp.log1p(jnp.exp(-jnp.abs(neg_lam)))
    rate = LRU_C * softplus
    rate_log2 = -rate * math.log2(math.e)

    nchunk = seq // rows
    for c in range(nchunk):
        r0 = c * rows
        for p in range(ntile):
            cols = slice(p * LANES, (p + 1) * LANES)
            reps = (rows // SUBLANES, 1)
            xc = jnp.tile(cb_ref[:, cols], reps)
            for tap in range(LRU_CONV_WIDTH):
                start = top + r0 + (tap - left) * SUBLANES
                w = jnp.tile(cw_ref[tap * SUBLANES:(tap + 1) * SUBLANES, cols], reps)
                xc = xc + xp_ref[p, start:start + rows, :] * w
            xc_ref[p, r0:r0 + rows, :] = xc

    zero = jnp.zeros((SUBLANES, LANES), F32)
    one = jnp.ones((SUBLANES, LANES), F32)
    nvr = rows // SUBLANES
    tile8 = lambda row: jnp.broadcast_to(row, (SUBLANES, LANES))

    def chunk(c, ends):
        ends = [list(ends[2 * k:2 * k + 2]) for k in range(2 * ntile)]
        base = (pl.multiple_of(c * rows, rows), pl.multiple_of((nchunk - 1 - c) * rows, rows))
        for p in range(ntile):
            for d in range(2):
                gcols = slice(2 * d * LANES, (2 * d + 2) * LANES)
                xcb = xc_ref[p, pl.ds(base[d], rows), :].astype(BF16)
                gates_ref[2 * p + d] = _dot(xcb, wg_ref[p, :, gcols]) + bg_ref[p:p + 1, gcols]
        for step in range(nvr):
            for p in range(ntile):
                cols = slice(p * LANES, (p + 1) * LANES)
                for d in range(2):
                    j = step if d == 0 else nvr - 1 - step
                    row = pl.ds(base[d] + j * SUBLANES, SUBLANES)
                    g = gates_ref[2 * p + d, j * SUBLANES:(j + 1) * SUBLANES, :]
                    th = jnp.tanh(g)
                    r = 0.5 + 0.5 * th[:, 0:LANES]
                    i = 0.5 + 0.5 * th[:, LANES:2 * LANES]
                    t = jnp.tanh(r * tile8(rate[d:d + 1, cols]))
                    tt = t + t
                    prod = tt * (1.0 + t)
                    mult = jnp.where(prod > 0.0, tt * lax.rsqrt(prod), 0.0)
                    a = jnp.exp2(r * tile8(rate_log2[d:d + 1, cols]))
                    u = mult * (i * xc_ref[p, row, :])
                    h, cum = ends[2 * p + d]
                    h = a * h + u
                    cum = a * cum
                    h_ref[d, p, row, :] = h
                    cum_ref[d, p, row, :] = cum
                    ends[2 * p + d] = [h, cum]
        return tuple(x for pair in ends for x in pair)

    flat = lax.fori_loop(0, nchunk, chunk, (zero, one) * (2 * ntile))
    ends = [[(flat[2 * (2 * p + d)], flat[2 * (2 * p + d) + 1]) for d in range(2)]
            for p in range(ntile)]

    enter = []
    for p in range(ntile):
        for d in range(2):
            h_end, cum_end = ends[p][d]
            _, chained = _local_scan(cum_end, h_end, reverse=(d == 1))
            if d == 0:
                enter.append(jnp.where(sub == 0, 0.0, pltpu.roll(chained, 1, 0)))
            else:
                enter.append(jnp.where(sub == SUBLANES - 1, 0.0,
                                       pltpu.roll(chained, SUBLANES - 1, 0)))

    for p in range(ntile):
        for j in range(grp):
            rs = slice(j * SUBLANES, (j + 1) * SUBLANES)
            xc_ref[p, rs, :] = ((h_ref[0, p, rs, :] + cum_ref[0, p, rs, :] * enter[2 * p])
                                + (h_ref[1, p, rs, :] + cum_ref[1, p, rs, :] * enter[2 * p + 1]))

    blk = BF16_ROWS

    def segment(s, carry):
        for p in range(ntile):
            cols = slice(p * LANES, (p + 1) * LANES)
            for j0 in range(0, grp, blk):
                rs = pl.ds(pl.multiple_of(s * grp + j0, blk), blk)
                y = xc_ref[p, pl.ds(s + j0 * SUBLANES, blk, stride=SUBLANES), :]
                o_ref[rs, cols] = (jax.nn.gelu(gr_ref[rs, cols], approximate=True)
                                   * y).astype(o_ref.dtype)
        return carry

    lax.fori_loop(0, SUBLANES, segment, 0)


def _lru(xr, gr, conv_w, conv_b, wg, bg, lru_lambda, rows):
    b, s, width = xr.shape
    ntile = width // LANES
    seqblk = pl.BlockSpec((None, s, width), lambda bi: (bi, 0, 0))
    full = lambda a: pl.BlockSpec(a.shape, lambda bi: (0,) * a.ndim)
    kernel = functools.partial(_lru_kernel, seq=s, rows=rows)
    halo_rows = (LRU_CONV_WIDTH - 1) * SUBLANES
    return pl.pallas_call(
        kernel,
        grid=(b,),
        in_specs=[seqblk, seqblk, full(conv_w), full(conv_b), full(wg), full(bg), full(lru_lambda)],
        out_specs=seqblk,
        out_shape=jax.ShapeDtypeStruct((b, s, width), BF16),
        scratch_shapes=[pltpu.VMEM((ntile, s + halo_rows, LANES), F32),
                        pltpu.VMEM((ntile, s, LANES), F32),
                        pltpu.VMEM((2 * ntile, rows, 2 * LANES), F32),
                        pltpu.VMEM((2, ntile, s, LANES), F32),
                        pltpu.VMEM((2, ntile, s, LANES), F32)],
        compiler_params=pltpu.CompilerParams(
            dimension_semantics=("arbitrary",), vmem_limit_bytes=VMEM_LIMIT),
        name="lru",
    )(xr, gr, conv_w, conv_b, wg, bg, lru_lambda)


def _outproj_kernel(x_ref, a_ref, l_ref, w_ref, g_ref, xm_ref, h_ref, wb_ref):
    @pl.when(pl.program_id(0) == 0)
    def _cast_weights():
        wb_ref[...] = w_ref[...].astype(BF16)

    aw = a_ref.shape[-1]
    xm = x_ref[...] + _dot(a_ref[...], wb_ref[0:aw, :]) + _dot(l_ref[...], wb_ref[aw:, :])
    xm_ref[...] = xm
    h_ref[...] = _rms(xm, g_ref[...]).astype(h_ref.dtype)


def _outproj(x2, attn2, lru2, w, g, tm):
    n, d = x2.shape
    row = lambda i: (i, 0)
    const = lambda i: (0, 0)
    return pl.pallas_call(
        _outproj_kernel,
        grid=(n // tm,),
        in_specs=[pl.BlockSpec((tm, d), row),
                  pl.BlockSpec((tm, attn2.shape[1]), row),
                  pl.BlockSpec((tm, lru2.shape[1]), row),
                  pl.BlockSpec(w.shape, const, pipeline_mode=pl.Buffered(1)),
                  pl.BlockSpec((1, d), const)],
        out_specs=[pl.BlockSpec((tm, d), row)] * 2,
        out_shape=[jax.ShapeDtypeStruct((n, d), F32), jax.ShapeDtypeStruct((n, d), BF16)],
        scratch_shapes=[pltpu.VMEM(w.shape, BF16)],
        compiler_params=pltpu.CompilerParams(
            dimension_semantics=("arbitrary",), vmem_limit_bytes=VMEM_LIMIT),
        name="outproj",
    )(x2, attn2, lru2, w, g)


def _ffn_kernel(hp_ref, hm_ref, hn_ref, xm_ref, wup_ref, cw_ref, cb_ref, wdn_ref, fg_ref, o_ref,
                perm_ref, hext_ref, ua_ref, ub_ref, act_ref, y_ref,
                *, tile, chunk, d_ff):
    i = pl.program_id(1)
    nchunks = d_ff // chunk
    ngrp = tile // SUBLANES
    nslab = perm_ref.shape[0]
    d = nslab * LANES

    for s in range(SUBLANES):
        rows = hm_ref[s * ngrp:(s + 1) * ngrp, :].astype(F32)
        for n in range(nslab):
            perm_ref[n, pl.ds(s, ngrp, stride=SUBLANES), :] = rows[:, n * LANES:(n + 1) * LANES]
    for n in range(nslab):
        hext_ref[0:tile, n * LANES:(n + 1) * LANES] = perm_ref[n].astype(BF16)
    prev = jnp.where(i > 0, hp_ref[BF16_ROWS - 1:BF16_ROWS, :].astype(F32), 0.0)
    nxt = jnp.where(i < pl.num_programs(1) - 1, hn_ref[0:1, :].astype(F32), 0.0)
    hrow = lax.broadcasted_iota(jnp.int32, (BF16_ROWS, d), 0)
    halo = jnp.where(hrow == 0, prev, jnp.where(hrow == 1, nxt, 0.0))
    hext_ref[tile:tile + BF16_ROWS, :] = halo.astype(BF16)

    def offsets(c):
        og, ov = c * chunk, d_ff + c * chunk
        if isinstance(c, int):
            return og, ov
        return pl.multiple_of(og, chunk), pl.multiple_of(ov, chunk)

    def up(c, u_ref):
        og, ov = offsets(c)
        hext = hext_ref[...]
        u_ref[:, 0:chunk] = _dot(hext, wup_ref[:, pl.ds(og, chunk)])
        u_ref[:, chunk:2 * chunk] = _dot(hext, wup_ref[:, pl.ds(ov, chunk)])

    def glu(c, u_ref, act_ref):
        sub = lax.broadcasted_iota(jnp.int32, (SUBLANES, chunk), 0)
        blk = BF16_ROWS

        def conv(col0, off, r0):
            cols = slice(col0, col0 + chunk)
            reps = (blk // SUBLANES, 1)
            w = [jnp.tile(cw_ref[k * SUBLANES:(k + 1) * SUBLANES, pl.ds(off, chunk)], reps)
                 for k in range(3)]
            bias = jnp.tile(cb_ref[:, pl.ds(off, chunk)], reps)
            cur = u_ref[r0:r0 + blk, cols]
            if r0 == 0:
                first = jnp.where(sub == 0, u_ref[tile:tile + 1, cols],
                                  pltpu.roll(u_ref[tile - SUBLANES:tile, cols], 1, 0))
                um1 = jnp.concatenate([first, cur[0:blk - SUBLANES]], axis=0)
            else:
                um1 = u_ref[r0 - SUBLANES:r0 + blk - SUBLANES, cols]
            if r0 + blk == tile:
                last = jnp.where(sub == SUBLANES - 1, u_ref[tile + 1:tile + 2, cols],
                                 pltpu.roll(u_ref[0:SUBLANES, cols], SUBLANES - 1, 0))
                up1 = jnp.concatenate([cur[SUBLANES:blk], last], axis=0)
            else:
                up1 = u_ref[r0 + SUBLANES:r0 + blk + SUBLANES, cols]
            return bias + um1 * w[0] + cur * w[1] + up1 * w[2]

        og, ov = offsets(c)
        for r0 in range(0, tile, blk):
            gate = conv(0, og, r0)
            val = conv(chunk, ov, r0)
            act_ref[r0:r0 + blk, :] = (jax.nn.gelu(gate, approximate=True) * val).astype(BF16)

    u_refs = (ua_ref, ub_ref)
    up(0, ua_ref)
    for c in range(nchunks):
        if c + 1 < nchunks:
            up(c + 1, u_refs[(c + 1) % 2])
        glu(c, u_refs[c % 2], act_ref.at[:, c * chunk:(c + 1) * chunk])
    y = _dot(act_ref[...], wdn_ref[...])
    for n in range(nslab):
        y_ref[n, 0:tile, :] = y[:, n * LANES:(n + 1) * LANES]
    for s in range(SUBLANES):
        rs = slice(s * ngrp, (s + 1) * ngrp)
        y = jnp.concatenate([y_ref[n, pl.ds(s, ngrp, stride=SUBLANES), :] for n in range(nslab)],
                            axis=1)
        o_ref[rs, :] = _rms(xm_ref[rs, :] + y, fg_ref[...])


def _ffn(h2, xm, w_up, conv_w, conv_b, w_down, final_g, tile, chunk):
    b, s, d = h2.shape
    d_ff = w_down.shape[0]
    nh = tile // BF16_ROWS
    last_halo = s // BF16_ROWS - 1
    main = lambda bi, i: (bi, i, 0)
    const = lambda bi, i: (0, 0)
    single = dict(pipeline_mode=pl.Buffered(1))
    kernel = functools.partial(_ffn_kernel, tile=tile, chunk=chunk, d_ff=d_ff)
    return pl.pallas_call(
        kernel,
        grid=(b, s // tile),
        in_specs=[pl.BlockSpec((None, BF16_ROWS, d), lambda bi, i: (bi, jnp.maximum(i * nh - 1, 0), 0)),
                  pl.BlockSpec((None, tile, d), main),
                  pl.BlockSpec((None, BF16_ROWS, d),
                               lambda bi, i: (bi, jnp.minimum((i + 1) * nh, last_halo), 0)),
                  pl.BlockSpec((None, tile, d), main),
                  pl.BlockSpec(w_up.shape, const, **single),
                  pl.BlockSpec(conv_w.shape, const),
                  pl.BlockSpec(conv_b.shape, const),
                  pl.BlockSpec(w_down.shape, const, **single),
                  pl.BlockSpec((1, d), const)],
        out_specs=pl.BlockSpec((None, tile, d), main),
        out_shape=jax.ShapeDtypeStruct((b, s, d), F32),
        scratch_shapes=[pltpu.VMEM((d // LANES, tile, LANES), F32),
                        pltpu.VMEM((tile + BF16_ROWS, d), BF16)]
        + [pltpu.VMEM((tile + BF16_ROWS, 2 * chunk), F32)] * 2
        + [pltpu.VMEM((tile, d_ff), BF16),
           pltpu.VMEM((d // LANES, tile + SUBLANES, LANES), F32)],
        compiler_params=pltpu.CompilerParams(
            dimension_semantics=("arbitrary",) * 2, vmem_limit_bytes=VMEM_LIMIT),
        name="ffn",
    )(h2, h2, h2, xm, w_up, conv_w, conv_b, w_down, final_g)


def _gate_weights(w_a, b_a, w_x, b_x):
    ndir, nblk, bd, _ = w_a.shape
    per = LANES // bd
    npair = nblk // per

    def blockdiag(w):
        w = w.reshape(npair, per, bd, bd)
        eye = jnp.eye(per, dtype=w.dtype)
        return jnp.einsum('pbij,bc->pbicj', w, eye).reshape(npair, LANES, LANES)

    ws, bs = [], []
    for d in range(ndir):
        for w, bias in ((w_a, b_a), (w_x, b_x)):
            ws.append(blockdiag(w[d]))
            bs.append(bias[d].reshape(npair, LANES))
    return ((0.5 * jnp.concatenate(ws, axis=-1)).astype(BF16),
            (0.5 * jnp.concatenate(bs, axis=-1)).astype(F32))


def kernel(x, attn_norm_g, w_in, lambda_q1, lambda_k1, lambda_q2, lambda_k2, subln_g,
           lru_conv_w, lru_conv_b, lru_w_a, lru_b_a, lru_w_x, lru_b_x, lru_lambda,
           w_out, ffn_norm_g, w_up, ffn_conv_w, ffn_conv_b, w_down, final_norm_g):
    b, s, d = x.shape
    depth = w_in.shape[0]
    x2 = x.reshape(b * s, d)
    assert depth == 1
    for l in range(depth):
        lambda_init = 0.8 - 0.6 * math.exp(-0.3 * l)
        q, k, vt, xr, gr = _inproj(x2, attn_norm_g[l][None], w_in[l], tm=512)
        lw = xr.shape[-1]
        attn, w_up_bf, w_down_bf = _attention(
            q.reshape(b, s, -1), k.reshape(b, s, -1), vt,
            lambda_q1[l][None], lambda_k1[l][None], lambda_q2[l][None], lambda_k2[l][None],
            subln_g[l][:, None], lambda_init, tq=256, nsub=4, cast_along=(w_up[l], w_down[l]))
        wg, bg = _gate_weights(lru_w_a[l], lru_b_a[l], lru_w_x[l], lru_b_x[l])
        lru = _lru(xr.reshape(b, s, lw), gr.reshape(b, s, lw),
                   jnp.repeat(lru_conv_w[l], SUBLANES, axis=0),
                   jnp.broadcast_to(lru_conv_b[l][None], (SUBLANES, lw)),
                   wg, bg, lru_lambda[l], rows=512)
        xm, h2 = _outproj(x2, attn.reshape(b * s, -1), lru.reshape(b * s, -1),
                          w_out[l], ffn_norm_g[l][None], tm=1024)
        x2 = _ffn(h2.reshape(b, s, d), xm.reshape(b, s, d), w_up_bf,
                  jnp.repeat(ffn_conv_w[l], SUBLANES, axis=0),
                  jnp.broadcast_to(ffn_conv_b[l][None], (SUBLANES, ffn_conv_b.shape[-1])),
                  w_down_bf, final_norm_g[None],
                  tile=512, chunk=256).reshape(b * s, d)
    return x2.reshape(b, s, d)
```

```python
import functools
import math

import jax
import jax.numpy as jnp
from jax import lax
from jax.experimental import pallas as pl
from jax.experimental.pallas import tpu as pltpu

F32 = jnp.float32
BF16 = jnp.bfloat16

N_HEADS = 4
HEAD_DIM = 64
V_DIM = 2 * HEAD_DIM
ATTN_WIDTH = N_HEADS * V_DIM
LRU_CONV_WIDTH = 4
LRU_CONV_LEFT = 2
LRU_C = 8.0
NORM_EPS = 1e-6
LOG2_E = math.log2(math.e)
LANES = 128
SUBLANES = 8
BF16_ROWS = 16
VMEM_LIMIT = 56 * 1024 * 1024


def _rms(x, g):
    return (x * lax.rsqrt(jnp.mean(x * x, axis=-1, keepdims=True) + NORM_EPS)) * g


def _dot(a, b):
    return jnp.dot(a, b, preferred_element_type=F32)


_NT = (((1,), (1,)), ((), ()))


def _inproj_kernel(x_ref, g_ref, w_ref, q_ref, k_ref, vt_ref, xr_ref, gr_ref, wb_ref, wvt_ref):
    aw = ATTN_WIDTH
    d = x_ref.shape[-1]

    @pl.when(pl.program_id(0) == 0)
    def _cast_weights():
        rows = 2 * LANES
        for r in range(0, d, rows):
            w = w_ref[r:r + rows, :]
            wb_ref[r:r + rows, 0:aw] = (w[:, 0:aw] * (HEAD_DIM ** -0.5 * LOG2_E)).astype(BF16)
            wb_ref[r:r + rows, aw:] = w[:, aw:].astype(BF16)
        for c in range(aw // LANES):
            cols = slice(2 * aw + c * LANES, 2 * aw + (c + 1) * LANES)
            wvt_ref[c * LANES:(c + 1) * LANES, :] = w_ref[:, cols].T.astype(BF16)

    hb = _rms(x_ref[...], g_ref[...]).astype(BF16)
    q_ref[...] = _dot(hb, wb_ref[:, 0:aw]).astype(BF16)
    k_ref[...] = _dot(hb, wb_ref[:, aw:2 * aw]).astype(BF16)
    vt_ref[...] = lax.dot_general(wvt_ref[...], hb, _NT, preferred_element_type=F32).astype(BF16)
    lw = xr_ref.shape[-1]
    xr_ref[...] = _dot(hb, wb_ref[:, 3 * aw:3 * aw + lw])
    gr_ref[...] = _dot(hb, wb_ref[:, 3 * aw + lw:3 * aw + 2 * lw])


def _inproj(x2, g, w, tm):
    n, d = x2.shape
    lw = (w.shape[1] - 3 * ATTN_WIDTH) // 2
    row = lambda i: (i, 0)
    const = lambda i: (0, 0)
    return pl.pallas_call(
        _inproj_kernel,
        grid=(n // tm,),
        in_specs=[pl.BlockSpec((tm, d), row),
                  pl.BlockSpec((1, d), const),
                  pl.BlockSpec(w.shape, const, pipeline_mode=pl.Buffered(1))],
        out_specs=[pl.BlockSpec((tm, ATTN_WIDTH), row)] * 2
        + [pl.BlockSpec((ATTN_WIDTH, tm), lambda i: (0, i))]
        + [pl.BlockSpec((tm, lw), row)] * 2,
        out_shape=[jax.ShapeDtypeStruct((n, ATTN_WIDTH), BF16)] * 2
        + [jax.ShapeDtypeStruct((ATTN_WIDTH, n), BF16)]
        + [jax.ShapeDtypeStruct((n, lw), F32)] * 2,
        scratch_shapes=[pltpu.VMEM(w.shape, BF16), pltpu.VMEM((ATTN_WIDTH, d), BF16)],
        compiler_params=pltpu.CompilerParams(
            dimension_semantics=("arbitrary",), vmem_limit_bytes=VMEM_LIMIT),
        name="inproj",
    )(x2, g, w)


def _attn_kernel(lq1_ref, lk1_ref, lq2_ref, lk2_ref, sg_ref, q_ref, k_ref, vt_ref, *rest,
                 tq, nsub, seq, lambda_init):
    nside = (len(rest) - 4 - 3 * nsub) // 2
    side_in, o_ref, side_out = rest[:nside], rest[nside], rest[nside + 1:2 * nside + 1]
    kf_ref, dist_ref, qf_ref, *se_refs = rest[2 * nside + 1:]
    for src, dst in zip(side_in, side_out):
        dst[...] = src[...].astype(dst.dtype)

    h = pl.program_id(1)
    qi = pl.program_id(2)
    s_refs, e_refs, m_refs = se_refs[:nsub], se_refs[nsub:2 * nsub], se_refs[2 * nsub:]
    nblk = seq // tq
    assert nblk & (nblk - 1) == 0 and 9 * nblk <= LANES and tq <= 256
    shift = nblk.bit_length() - 1

    def slope_of(head):
        return LOG2_E * jnp.where(head == 0, 2.0 ** -2, jnp.where(head == 1, 2.0 ** -4,
                                  jnp.where(head == 2, 2.0 ** -6, 2.0 ** -8))).astype(F32)

    @pl.when((pl.program_id(0) == 0) & (h == 0) & (qi == 0))
    def _init():
        lane = lax.broadcasted_iota(jnp.int32, (seq, LANES), 1)
        row = lax.broadcasted_iota(jnp.int32, (seq, LANES), 0)
        grp = lane >> shift
        hit = (row // tq) == (lane & (nblk - 1))
        dj = (row % tq).astype(F32)
        kf_ref[...] = jnp.where(hit & (grp < 6), 1.0,
                                jnp.where(hit & (grp < 9), dj, 0.0)).astype(BF16)
        r = lax.broadcasted_iota(jnp.int32, (tq, tq), 0)
        c = lax.broadcasted_iota(jnp.int32, (tq, tq), 1)
        dist_ref[...] = jnp.abs(r - c).astype(F32)

        lane = lax.broadcasted_iota(jnp.int32, (tq, LANES), 1)
        di = lax.broadcasted_iota(jnp.int32, (tq, LANES), 0).astype(F32)
        grp = lane >> shift
        term = (grp >= 3).astype(jnp.int32) + (grp >= 6).astype(jnp.int32)
        piece = grp - 3 * term

        def factors(idx, carry):
            slope = slope_of(idx // nblk)
            diff = idx % nblk - (lane & (nblk - 1))
            sign = jnp.where(diff > 0, 1.0, jnp.where(diff < 0, -1.0, 0.0))
            fac = jnp.where(term == 0, -slope * sign * di,
                            jnp.where(term == 1, -slope * tq * jnp.abs(diff).astype(F32),
                                      slope * sign))
            hi = fac.astype(BF16)
            rest1 = fac - hi.astype(F32)
            mid = rest1.astype(BF16)
            lo = (rest1 - mid.astype(F32)).astype(BF16)
            qf_ref[idx] = jnp.where(grp >= 9, jnp.zeros_like(hi),
                                    jnp.where(piece == 0, hi, jnp.where(piece == 1, mid, lo)))
            return carry

        lax.fori_loop(0, N_HEADS * nblk, factors, 0)

    lam =(jnp.exp(jnp.sum(lq1_ref[...] * lk1_ref[...], axis=-1, keepdims=True))
           - jnp.exp(jnp.sum(lq2_ref[...] * lk2_ref[...], axis=-1, keepdims=True))
           + lambda_init)
    slope = slope_of(h)
    kaug = jnp.concatenate([k_ref[...], kf_ref[...]], axis=1)
    vt_ones = jnp.concatenate([vt_ref[...], jnp.ones((BF16_ROWS, seq), BF16)], axis=0)
    lane = lax.broadcasted_iota(jnp.int32, (tq, LANES), 1)
    strip = 4 * SUBLANES

    def scores(sb):
        blk = qi * nsub + sb
        qf = qf_ref[h * nblk + blk]
        q = q_ref[sb * tq:(sb + 1) * tq, :]
        zero = jnp.zeros_like(q)
        qaug = jnp.concatenate(
            [jnp.concatenate([jnp.where(lane < HEAD_DIM, q, zero), qf], axis=1),
             jnp.concatenate([jnp.where(lane >= HEAD_DIM, q, zero), qf], axis=1)], axis=0)
        s_ref = s_refs[sb]
        s = lax.dot_general(kaug, qaug, _NT, preferred_element_type=F32)
        s_ref[...] = s
        mx = s[0:strip, :]
        for r in range(strip, seq, strip):
            mx = jnp.maximum(mx, s[r:r + strip, :])
        m_refs[sb][...] = mx
        diag = pl.ds(pl.multiple_of(blk * tq, tq), tq)
        diag_bias = slope * dist_ref[...]
        s_ref[diag, 0:tq] = s_ref[diag, 0:tq] - diag_bias
        s_ref[diag, tq:2 * tq] = s_ref[diag, tq:2 * tq] - diag_bias

    def finish(sb):
        s_ref, e_ref = s_refs[sb], e_refs[sb]
        mx = jnp.max(m_refs[sb][...], axis=0, keepdims=True)
        for r in range(0, seq, strip):
            e_ref[r:r + strip, :] = jnp.exp2(s_ref[r:r + strip, :] - mx).astype(BF16)
        o12 = _dot(vt_ones, e_ref[...])
        norm = o12[V_DIM:V_DIM + 1, :]
        o12 = o12[0:V_DIM, :]
        o = o12[:, 0:tq] * (1.0 / norm[:, 0:tq]) - o12[:, tq:2 * tq] * (lam / norm[:, tq:2 * tq])
        o = o * lax.rsqrt(jnp.mean(o * o, axis=0, keepdims=True) + NORM_EPS)
        o = o * sg_ref[...] * (1.0 - lambda_init)
        o_ref[sb * tq:(sb + 1) * tq, :] = o.T.astype(o_ref.dtype)

    scores(0)
    for sb in range(nsub):
        if sb + 1 < nsub:
            scores(sb + 1)
        finish(sb)


def _attention(q, k, vt, lq1, lk1, lq2, lk2, subln_g, lambda_init, tq, nsub, cast_along=()):
    b, s, _ = q.shape
    vec = lambda bi, h, qi: (0, 0)
    tstep = tq * nsub
    nq = s // tstep
    nsteps = b * N_HEADS * nq

    def row_block(last):
        return lambda bi, h, qi: (jnp.minimum((bi * N_HEADS + h) * nq + qi, last), 0)

    side_in, side_out, side_shapes = [], [], []
    for w in cast_along:
        rows = -(-w.shape[0] // (nsteps * BF16_ROWS)) * BF16_ROWS
        while w.shape[0] % rows:
            rows += BF16_ROWS
        for specs in (side_in, side_out):
            specs.append(pl.BlockSpec((rows, w.shape[1]), row_block(w.shape[0] // rows - 1)))
        side_shapes.append(jax.ShapeDtypeStruct(w.shape, BF16))
    kernel = functools.partial(_attn_kernel, tq=tq, nsub=nsub, seq=s, lambda_init=lambda_init)
    return pl.pallas_call(
        kernel,
        grid=(b, N_HEADS, s // tstep),
        in_specs=[pl.BlockSpec((1, HEAD_DIM), vec)] * 4
        + [pl.BlockSpec((V_DIM, 1), vec),
           pl.BlockSpec((None, tstep, V_DIM), lambda bi, h, qi: (bi, qi, h)),
           pl.BlockSpec((None, s, V_DIM), lambda bi, h, qi: (bi, 0, h)),
           pl.BlockSpec((V_DIM, s), lambda bi, h, qi: (h, bi))] + side_in,
        out_specs=[pl.BlockSpec((None, tstep, V_DIM), lambda bi, h, qi: (bi, qi, h))] + side_out,
        out_shape=[jax.ShapeDtypeStruct((b, s, ATTN_WIDTH), BF16)] + side_shapes,
        scratch_shapes=[pltpu.VMEM((s, LANES), BF16), pltpu.VMEM((tq, tq), F32),
                        pltpu.VMEM((N_HEADS * (s // tq), tq, LANES), BF16)]
        + [pltpu.VMEM((s, 2 * tq), F32)] * nsub + [pltpu.VMEM((s, 2 * tq), BF16)] * nsub
        + [pltpu.VMEM((4 * SUBLANES, 2 * tq), F32)] * nsub,
        compiler_params=pltpu.CompilerParams(
            dimension_semantics=("arbitrary",) * 3, vmem_limit_bytes=VMEM_LIMIT),
        name="attn",
    )(lq1, lk1, lq2, lk2, subln_g, q, k, vt, *cast_along)


def _local_scan(a, u, reverse):
    row = lax.broadcasted_iota(jnp.int32, a.shape, 0)
    for d in (1, 2, 4):
        shift = SUBLANES - d if reverse else d
        valid = (row < SUBLANES - d) if reverse else (row >= d)
        a_s = jnp.where(valid, pltpu.roll(a, shift, 0), 1.0)
        u_s = jnp.where(valid, pltpu.roll(u, shift, 0), 0.0)
        u = a * u_s + u
        a = a * a_s
    return a, u


def _lru_kernel(xr_ref, gr_ref, cw_ref, cb_ref, wg_ref, bg_ref, lam_ref, o_ref,
                xp_ref, xc_ref, gates_ref, h_ref, cum_ref, *, seq, rows):
    width = xr_ref.shape[-1]
    ntile = width // LANES
    grp = seq // SUBLANES
    left = LRU_CONV_LEFT
    right = LRU_CONV_WIDTH - 1 - LRU_CONV_LEFT
    top = left * SUBLANES
    sub = lax.broadcasted_iota(jnp.int32, (SUBLANES, LANES), 0)

    for s in range(SUBLANES):
        blk = xr_ref[s * grp:(s + 1) * grp, :]
        for p in range(ntile):
            xp_ref[p, pl.ds(top + s, grp, stride=SUBLANES), :] = blk[:, p * LANES:(p + 1) * LANES]
    for p in range(ntile):
        for k in range(left):
            src = xp_ref[p, top + (grp - 1 - k) * SUBLANES:top + (grp - k) * SUBLANES, :]
            xp_ref[p, top - (k + 1) * SUBLANES:top - k * SUBLANES, :] = jnp.where(
                sub == 0, 0.0, pltpu.roll(src, 1, 0))
        for k in range(right):
            src = xp_ref[p, top + k * SUBLANES:top + (k + 1) * SUBLANES, :]
            xp_ref[p, top + (grp + k) * SUBLANES:top + (grp + k + 1) * SUBLANES, :] = jnp.where(
                sub == SUBLANES - 1, 0.0, pltpu.roll(src, SUBLANES - 1, 0))

    neg_lam = -lam_ref[...]
    softplus = jnp.maximum(neg_lam, 0.0) + jnp.log1p(jnp.exp(-jnp.abs(neg_lam)))
    rate = LRU_C * softplus
    rate_log2 = -rate * math.log2(math.e)

    nchunk = seq // rows
    for c in range(nchunk):
        r0 = c * rows
        for p in range(ntile):
            cols = slice(p * LANES, (p + 1) * LANES)
            reps = (rows // SUBLANES, 1)
            xc = jnp.tile(cb_ref[:, cols], reps)
            for tap in range(LRU_CONV_WIDTH):
                start = top + r0 + (tap - left) * SUBLANES
                w = jnp.tile(cw_ref[tap * SUBLANES:(tap + 1) * SUBLANES, cols], reps)
                xc = xc + xp_ref[p, start:start + rows, :] * w
            xc_ref[p, r0:r0 + rows, :] = xc

    zero = jnp.zeros((SUBLANES, LANES), F32)
    one = jnp.ones((SUBLANES, LANES), F32)
    nvr = rows // SUBLANES
    tile8 = lambda row: jnp.broadcast_to(row, (SUBLANES, LANES))

    def chunk(c, ends):
        ends = [list(ends[2 * k:2 * k + 2]) for k in range(2 * ntile)]
        base = (pl.multiple_of(c * rows, rows), pl.multiple_of((nchunk - 1 - c) * rows, rows))
        for p in range(ntile):
            for d in range(2):
                gcols = slice(2 * d * LANES, (2 * d + 2) * LANES)
                xcb = xc_ref[p, pl.ds(base[d], rows), :].astype(BF16)
                gates_ref[2 * p + d] = _dot(xcb, wg_ref[p, :, gcols]) + bg_ref[p:p + 1, gcols]
        for step in range(nvr):
            for p in range(ntile):
                cols = slice(p * LANES, (p + 1) * LANES)
                for d in range(2):
                    j = step if d == 0 else nvr - 1 - step
                    row = pl.ds(base[d] + j * SUBLANES, SUBLANES)
                    g = gates_ref[2 * p + d, j * SUBLANES:(j + 1) * SUBLANES, :]
                    th = jnp.tanh(g)
                    r = 0.5 + 0.5 * th[:, 0:LANES]
                    i = 0.5 + 0.5 * th[:, LANES:2 * LANES]
                    t = jnp.tanh(r * tile8(rate[d:d + 1, cols]))
                    tt = t + t
                    prod = tt * (1.0 + t)
                    mult = jnp.where(prod > 0.0, tt * lax.rsqrt(prod), 0.0)
                    a = jnp.exp2(r * tile8(rate_log2[d:d + 1, cols]))
                    u = mult * (i * xc_ref[p, row, :])
                    h, cum = ends[2 * p + d]
                    h = a * h + u
                    cum = a * cum
                    h_ref[d, p, row, :] = h
                    cum_ref[d, p, row, :] = cum
                    ends[2 * p + d] = [h, cum]
        return tuple(x for pair in ends for x in pair)

    flat = lax.fori_loop(0, nchunk, chunk, (zero, one) * (2 * ntile))
    ends = [[(flat[2 * (2 * p + d)], flat[2 * (2 * p + d) + 1]) for d in range(2)]
            for p in range(ntile)]

    enter = []
    for p in range(ntile):
        for d in range(2):
            h_end, cum_end = ends[p][d]
            _, chained = _local_scan(cum_end, h_end, reverse=(d == 1))
            if d == 0:
                enter.append(jnp.where(sub == 0, 0.0, pltpu.roll(chained, 1, 0)))
            else:
                enter.append(jnp.where(sub == SUBLANES - 1, 0.0,
                                       pltpu.roll(chained, SUBLANES - 1, 0)))

    for p in range(ntile):
        for j in range(grp):
            rs = slice(j * SUBLANES, (j + 1) * SUBLANES)
            xc_ref[p, rs, :] = ((h_ref[0, p, rs, :] + cum_ref[0, p, rs, :] * enter[2 * p])
                                + (h_ref[1, p, rs, :] + cum_ref[1, p, rs, :] * enter[2 * p + 1]))

    blk = BF16_ROWS

    def segment(s, carry):
        for p in range(ntile):
            cols = slice(p * LANES, (p + 1) * LANES)
            for j0 in range(0, grp, blk):
                rs = pl.ds(pl.multiple_of(s * grp + j0, blk), blk)
                y = xc_ref[p, pl.ds(s + j0 * SUBLANES, blk, stride=SUBLANES), :]
                o_ref[rs, cols] = (jax.nn.gelu(gr_ref[rs, cols], approximate=True)
                                   * y).astype(o_ref.dtype)
        return carry

    lax.fori_loop(0, SUBLANES, segment, 0)


def _lru(xr, gr, conv_w, conv_b, wg, bg, lru_lambda, rows):
    b, s, width = xr.shape
    ntile = width // LANES
    seqblk = pl.BlockSpec((None, s, width), lambda bi: (bi, 0, 0))
    full = lambda a: pl.BlockSpec(a.shape, lambda bi: (0,) * a.ndim)
    kernel = functools.partial(_lru_kernel, seq=s, rows=rows)
    halo_rows = (LRU_CONV_WIDTH - 1) * SUBLANES
    return pl.pallas_call(
        kernel,
        grid=(b,),
        in_specs=[seqblk, seqblk, full(conv_w), full(conv_b), full(wg), full(bg), full(lru_lambda)],
        out_specs=seqblk,
        out_shape=jax.ShapeDtypeStruct((b, s, width), BF16),
        scratch_shapes=[pltpu.VMEM((ntile, s + halo_rows, LANES), F32),
                        pltpu.VMEM((ntile, s, LANES), F32),
                        pltpu.VMEM((2 * ntile, rows, 2 * LANES), F32),
                        pltpu.VMEM((2, ntile, s, LANES), F32),
                        pltpu.VMEM((2, ntile, s, LANES), F32)],
        compiler_params=pltpu.CompilerParams(
            dimension_semantics=("arbitrary",), vmem_limit_bytes=VMEM_LIMIT),
        name="lru",
    )(xr, gr, conv_w, conv_b, wg, bg, lru_lambda)


def _outproj_kernel(x_ref, a_ref, l_ref, w_ref, g_ref, xm_ref, h_ref, wb_ref):
    @pl.when(pl.program_id(0) == 0)
    def _cast_weights():
        wb_ref[...] = w_ref[...].astype(BF16)

    aw = a_ref.shape[-1]
    xm = x_ref[...] + _dot(a_ref[...], wb_ref[0:aw, :]) + _dot(l_ref[...], wb_ref[aw:, :])
    xm_ref[...] = xm
    h_ref[...] = _rms(xm, g_ref[...]).astype(h_ref.dtype)


def _outproj(x2, attn2, lru2, w, g, tm):
    n, d = x2.shape
    row = lambda i: (i, 0)
    const = lambda i: (0, 0)
    return pl.pallas_call(
        _outproj_kernel,
        grid=(n // tm,),
        in_specs=[pl.BlockSpec((tm, d), row),
                  pl.BlockSpec((tm, attn2.shape[1]), row),
                  pl.BlockSpec((tm, lru2.shape[1]), row),
                  pl.BlockSpec(w.shape, const, pipeline_mode=pl.Buffered(1)),
                  pl.BlockSpec((1, d), const)],
        out_specs=[pl.BlockSpec((tm, d), row)] * 2,
        out_shape=[jax.ShapeDtypeStruct((n, d), F32), jax.ShapeDtypeStruct((n, d), BF16)],
        scratch_shapes=[pltpu.VMEM(w.shape, BF16)],
        compiler_params=pltpu.CompilerParams(
            dimension_semantics=("arbitrary",), vmem_limit_bytes=VMEM_LIMIT),
        name="outproj",
    )(x2, attn2, lru2, w, g)


def _ffn_kernel(hp_ref, hm_ref, hn_ref, xm_ref, wup_ref, cw_ref, cb_ref, wdn_ref, fg_ref, o_ref,
                perm_ref, hext_ref, ua_ref, ub_ref, act_ref, y_ref,
                *, tile, chunk, d_ff):
    i = pl.program_id(1)
    nchunks = d_ff // chunk
    ngrp = tile // SUBLANES
    nslab = perm_ref.shape[0]
    d = nslab * LANES

    for s in range(SUBLANES):
        rows = hm_ref[s * ngrp:(s + 1) * ngrp, :].astype(F32)
        for n in range(nslab):
            perm_ref[n, pl.ds(s, ngrp, stride=SUBLANES), :] = rows[:, n * LANES:(n + 1) * LANES]
    for n in range(nslab):
        hext_ref[0:tile, n * LANES:(n + 1) * LANES] = perm_ref[n].astype(BF16)
    prev = jnp.where(i > 0, hp_ref[BF16_ROWS - 1:BF16_ROWS, :].astype(F32), 0.0)
    nxt = jnp.where(i < pl.num_programs(1) - 1, hn_ref[0:1, :].astype(F32), 0.0)
    hrow = lax.broadcasted_iota(jnp.int32, (BF16_ROWS, d), 0)
    halo = jnp.where(hrow == 0, prev, jnp.where(hrow == 1, nxt, 0.0))
    hext_ref[tile:tile + BF16_ROWS, :] = halo.astype(BF16)

    def offsets(c):
        og, ov = c * chunk, d_ff + c * chunk
        if isinstance(c, int):
            return og, ov
        return pl.multiple_of(og, chunk), pl.multiple_of(ov, chunk)

    def up(c, u_ref):
        og, ov = offsets(c)
        hext = hext_ref[...]
        u_ref[:, 0:chunk] = _dot(hext, wup_ref[:, pl.ds(og, chunk)])
        u_ref[:, chunk:2 * chunk] = _dot(hext, wup_ref[:, pl.ds(ov, chunk)])

    def glu(c, u_ref, act_ref):
        sub = lax.broadcasted_iota(jnp.int32, (SUBLANES, chunk), 0)
        blk = BF16_ROWS

        def conv(col0, off, r0):
            cols = slice(col0, col0 + chunk)
            reps = (blk // SUBLANES, 1)
            w = [jnp.tile(cw_ref[k * SUBLANES:(k + 1) * SUBLANES, pl.ds(off, chunk)], reps)
                 for k in range(3)]
            bias = jnp.tile(cb_ref[:, pl.ds(off, chunk)], reps)
            cur = u_ref[r0:r0 + blk, cols]
            if r0 == 0:
                first = jnp.where(sub == 0, u_ref[tile:tile + 1, cols],
                                  pltpu.roll(u_ref[tile - SUBLANES:tile, cols], 1, 0))
                um1 = jnp.concatenate([first, cur[0:blk - SUBLANES]], axis=0)
            else:
                um1 = u_ref[r0 - SUBLANES:r0 + blk - SUBLANES, cols]
            if r0 + blk == tile:
                last = jnp.where(sub == SUBLANES - 1, u_ref[tile + 1:tile + 2, cols],
                                 pltpu.roll(u_ref[0:SUBLANES, cols], SUBLANES - 1, 0))
                up1 = jnp.concatenate([cur[SUBLANES:blk], last], axis=0)
            else:
                up1 = u_ref[r0 + SUBLANES:r0 + blk + SUBLANES, cols]
            return bias + um1 * w[0] + cur * w[1] + up1 * w[2]

        og, ov = offsets(c)
        for r0 in range(0, tile, blk):
            gate = conv(0, og, r0)
            val = conv(chunk, ov, r0)
            act_ref[r0:r0 + blk, :] = (jax.nn.gelu(gate, approximate=True) * val).astype(BF16)

    u_refs = (ua_ref, ub_ref)
    up(0, ua_ref)
    for c in range(nchunks):
        if c + 1 < nchunks:
            up(c + 1, u_refs[(c + 1) % 2])
        glu(c, u_refs[c % 2], act_ref.at[:, c * chunk:(c + 1) * chunk])
    y = _dot(act_ref[...], wdn_ref[...])
    for n in range(nslab):
        y_ref[n, 0:tile, :] = y[:, n * LANES:(n + 1) * LANES]
    for s in range(SUBLANES):
        rs = slice(s * ngrp, (s + 1) * ngrp)
        y = jnp.concatenate([y_ref[n, pl.ds(s, ngrp, stride=SUBLANES), :] for n in range(nslab)],
                            axis=1)
        o_ref[rs, :] = _rms(xm_ref[rs, :] + y, fg_ref[...])


def _ffn(h2, xm, w_up, conv_w, conv_b, w_down, final_g, tile, chunk):
    b, s, d = h2.shape
    d_ff = w_down.shape[0]
    nh = tile // BF16_ROWS
    last_halo = s // BF16_ROWS - 1
    main = lambda bi, i: (bi, i, 0)
    const = lambda bi, i: (0, 0)
    single = dict(pipeline_mode=pl.Buffered(1))
    kernel = functools.partial(_ffn_kernel, tile=tile, chunk=chunk, d_ff=d_ff)
    return pl.pallas_call(
        kernel,
        grid=(b, s // tile),
        in_specs=[pl.BlockSpec((None, BF16_ROWS, d), lambda bi, i: (bi, jnp.maximum(i * nh - 1, 0), 0)),
                  pl.BlockSpec((None, tile, d), main),
                  pl.BlockSpec((None, BF16_ROWS, d),
                               lambda bi, i: (bi, jnp.minimum((i + 1) * nh, last_halo), 0)),
                  pl.BlockSpec((None, tile, d), main),
                  pl.BlockSpec(w_up.shape, const, **single),
                  pl.BlockSpec(conv_w.shape, const),
                  pl.BlockSpec(conv_b.shape, const),
                  pl.BlockSpec(w_down.shape, const, **single),
                  pl.BlockSpec((1, d), const)],
        out_specs=pl.BlockSpec((None, tile, d), main),
        out_shape=jax.ShapeDtypeStruct((b, s, d), F32),
        scratch_shapes=[pltpu.VMEM((d // LANES, tile, LANES), F32),
                        pltpu.VMEM((tile + BF16_ROWS, d), BF16)]
        + [pltpu.VMEM((tile + BF16_ROWS, 2 * chunk), F32)] * 2
        + [pltpu.VMEM((tile, d_ff), BF16),
           pltpu.VMEM((d // LANES, tile + SUBLANES, LANES), F32)],
        compiler_params=pltpu.CompilerParams(
            dimension_semantics=("arbitrary",) * 2, vmem_limit_bytes=VMEM_LIMIT),
        name="ffn",
    )(h2, h2, h2, xm, w_up, conv_w, conv_b, w_down, final_g)


def _gate_weights(w_a, b_a, w_x, b_x):
    ndir, nblk, bd, _ = w_a.shape
    per = LANES // bd
    npair = nblk // per

    def blockdiag(w):
        w = w.reshape(npair, per, bd, bd)
        eye = jnp.eye(per, dtype=w.dtype)
        return jnp.einsum('pbij,bc->pbicj', w, eye).reshape(npair, LANES, LANES)

    ws, bs = [], []
    for d in range(ndir):
        for w, bias in ((w_a, b_a), (w_x, b_x)):
            ws.append(blockdiag(w[d]))
            bs.append(bias[d].reshape(npair, LANES))
    return ((0.5 * jnp.concatenate(ws, axis=-1)).astype(BF16),
            (0.5 * jnp.concatenate(bs, axis=-1)).astype(F32))


def kernel(x, attn_norm_g, w_in, lambda_q1, lambda_k1, lambda_q2, lambda_k2, subln_g,
           lru_conv_w, lru_conv_b, lru_w_a, lru_b_a, lru_w_x, lru_b_x, lru_lambda,
           w_out, ffn_norm_g, w_up, ffn_conv_w, ffn_conv_b, w_down, final_norm_g):
    b, s, d = x.shape
    depth = w_in.shape[0]
    x2 = x.reshape(b * s, d)
    assert depth == 1
    for l in range(depth):
        lambda_init = 0.8 - 0.6 * math.exp(-0.3 * l)
        q, k, vt, xr, gr = _inproj(x2, attn_norm_g[l][None], w_in[l], tm=512)
        lw = xr.shape[-1]
        attn, w_up_bf, w_down_bf = _attention(
            q.reshape(b, s, -1), k.reshape(b, s, -1), vt,
            lambda_q1[l][None], lambda_k1[l][None], lambda_q2[l][None], lambda_k2[l][None],
            subln_g[l][:, None], lambda_init, tq=256, nsub=4, cast_along=(w_up[l], w_down[l]))
        wg, bg = _gate_weights(lru_w_a[l], lru_b_a[l], lru_w_x[l], lru_b_x[l])
        lru = _lru(xr.reshape(b, s, lw), gr.reshape(b, s, lw),
                   jnp.repeat(lru_conv_w[l], SUBLANES, axis=0),
                   jnp.broadcast_to(lru_conv_b[l][None], (SUBLANES, lw)),
                   wg, bg, lru_lambda[l], rows=512)
        xm, h2 = _outproj(x2, attn.reshape(b * s, -1), lru.reshape(b * s, -1),
                          w_out[l], ffn_norm_g[l][None], tm=1024)
        x2 = _ffn(h2.reshape(b, s, d), xm.reshape(b, s, d), w_up_bf,
                  jnp.repeat(ffn_conv_w[l], SUBLANES, axis=0),
                  jnp.broadcast_to(ffn_conv_b[l][None], (SUBLANES, ffn_conv_b.shape[-1])),
                  w_down_bf, final_norm_g[None],
                  tile=512, chunk=256).reshape(b * s, d)
    return x2.reshape(b, s, d)
```

```python
import functools
import math

import jax
import jax.numpy as jnp
from jax import lax
from jax.experimental import pallas as pl
from jax.experimental.pallas import tpu as pltpu

F32 = jnp.float32
BF16 = jnp.bfloat16

N_HEADS = 4
HEAD_DIM = 64
V_DIM = 2 * HEAD_DIM
ATTN_WIDTH = N_HEADS * V_DIM
LRU_CONV_WIDTH = 4
LRU_CONV_LEFT = 2
LRU_C = 8.0
NORM_EPS = 1e-6
LOG2_E = math.log2(math.e)
LANES = 128
SUBLANES = 8
BF16_ROWS = 16
VMEM_LIMIT = 56 * 1024 * 1024


def _rms(x, g):
    return (x * lax.rsqrt(jnp.mean(x * x, axis=-1, keepdims=True) + NORM_EPS)) * g


def _dot(a, b):
    return jnp.dot(a, b, preferred_element_type=F32)


_NT = (((1,), (1,)), ((), ()))


def _inproj_kernel(x_ref, g_ref, w_ref, q_ref, k_ref, vt_ref, xr_ref, gr_ref, wb_ref, wvt_ref):
    aw = ATTN_WIDTH
    d = x_ref.shape[-1]

    @pl.when(pl.program_id(0) == 0)
    def _cast_weights():
        rows = 2 * LANES
        for r in range(0, d, rows):
            w = w_ref[r:r + rows, :]
            wb_ref[r:r + rows, 0:aw] = (w[:, 0:aw] * (HEAD_DIM ** -0.5 * LOG2_E)).astype(BF16)
            wb_ref[r:r + rows, aw:] = w[:, aw:].astype(BF16)
        for c in range(aw // LANES):
            cols = slice(2 * aw + c * LANES, 2 * aw + (c + 1) * LANES)
            wvt_ref[c * LANES:(c + 1) * LANES, :] = w_ref[:, cols].T.astype(BF16)

    hb = _rms(x_ref[...], g_ref[...]).astype(BF16)
    q_ref[...] = _dot(hb, wb_ref[:, 0:aw]).astype(BF16)
    k_ref[...] = _dot(hb, wb_ref[:, aw:2 * aw]).astype(BF16)
    vt_ref[...] = lax.dot_general(wvt_ref[...], hb, _NT, preferred_element_type=F32).astype(BF16)
    lw = xr_ref.shape[-1]
    xr_ref[...] = _dot(hb, wb_ref[:, 3 * aw:3 * aw + lw])
    gr_ref[...] = _dot(hb, wb_ref[:, 3 * aw + lw:3 * aw + 2 * lw])


def _inproj(x2, g, w, tm):
    n, d = x2.shape
    lw = (w.shape[1] - 3 * ATTN_WIDTH) // 2
    row = lambda i: (i, 0)
    const = lambda i: (0, 0)
    return pl.pallas_call(
        _inproj_kernel,
        grid=(n // tm,),
        in_specs=[pl.BlockSpec((tm, d), row),
                  pl.BlockSpec((1, d), const),
                  pl.BlockSpec(w.shape, const, pipeline_mode=pl.Buffered(1))],
        out_specs=[pl.BlockSpec((tm, ATTN_WIDTH), row)] * 2
        + [pl.BlockSpec((ATTN_WIDTH, tm), lambda i: (0, i))]
        + [pl.BlockSpec((tm, lw), row)] * 2,
        out_shape=[jax.ShapeDtypeStruct((n, ATTN_WIDTH), BF16)] * 2
        + [jax.ShapeDtypeStruct((ATTN_WIDTH, n), BF16)]
        + [jax.ShapeDtypeStruct((n, lw), F32)] * 2,
        scratch_shapes=[pltpu.VMEM(w.shape, BF16), pltpu.VMEM((ATTN_WIDTH, d), BF16)],
        compiler_params=pltpu.CompilerParams(
            dimension_semantics=("arbitrary",), vmem_limit_bytes=VMEM_LIMIT),
        name="inproj",
    )(x2, g, w)


def _attn_kernel(lq1_ref, lk1_ref, lq2_ref, lk2_ref, sg_ref, q_ref, k_ref, vt_ref, *rest,
                 tq, nsub, seq, lambda_init):
    nside = (len(rest) - 4 - 3 * nsub) // 2
    side_in, o_ref, side_out = rest[:nside], rest[nside], rest[nside + 1:2 * nside + 1]
    kf_ref, dist_ref, qf_ref, *se_refs = rest[2 * nside + 1:]
    for src, dst in zip(side_in, side_out):
        dst[...] = src[...].astype(dst.dtype)

    h = pl.program_id(1)
    qi = pl.program_id(2)
    s_refs, e_refs, m_refs = se_refs[:nsub], se_refs[nsub:2 * nsub], se_refs[2 * nsub:]
    nblk = seq // tq
    assert nblk & (nblk - 1) == 0 and 9 * nblk <= LANES and tq <= 256
    shift = nblk.bit_length() - 1

    def slope_of(head):
        return LOG2_E * jnp.where(head == 0, 2.0 ** -2, jnp.where(head == 1, 2.0 ** -4,
                                  jnp.where(head == 2, 2.0 ** -6, 2.0 ** -8))).astype(F32)

    @pl.when((pl.program_id(0) == 0) & (h == 0) & (qi == 0))
    def _init():
        lane = lax.broadcasted_iota(jnp.int32, (seq, LANES), 1)
        row = lax.broadcasted_iota(jnp.int32, (seq, LANES), 0)
        grp = lane >> shift
        hit = (row // tq) == (lane & (nblk - 1))
        dj = (row % tq).astype(F32)
        kf_ref[...] = jnp.where(hit & (grp < 6), 1.0,
                                jnp.where(hit & (grp < 9), dj, 0.0)).astype(BF16)
        r = lax.broadcasted_iota(jnp.int32, (tq, tq), 0)
        c = lax.broadcasted_iota(jnp.int32, (tq, tq), 1)
        dist_ref[...] = jnp.abs(r - c).astype(F32)

        lane = lax.broadcasted_iota(jnp.int32, (tq, LANES), 1)
        di = lax.broadcasted_iota(jnp.int32, (tq, LANES), 0).astype(F32)
        grp = lane >> shift
        term = (grp >= 3).astype(jnp.int32) + (grp >= 6).astype(jnp.int32)
        piece = grp - 3 * term

        def factors(idx, carry):
            slope = slope_of(idx // nblk)
            diff = idx % nblk - (lane & (nblk - 1))
            sign = jnp.where(diff > 0, 1.0, jnp.where(diff < 0, -1.0, 0.0))
            fac = jnp.where(term == 0, -slope * sign * di,
                            jnp.where(term == 1, -slope * tq * jnp.abs(diff).astype(F32),
                                      slope * sign))
            hi = fac.astype(BF16)
            rest1 = fac - hi.astype(F32)
            mid = rest1.astype(BF16)
            lo = (rest1 - mid.astype(F32)).astype(BF16)
            qf_ref[idx] = jnp.where(grp >= 9, jnp.zeros_like(hi),
                                    jnp.where(piece == 0, hi, jnp.where(piece == 1, mid, lo)))
            return carry

        lax.fori_loop(0, N_HEADS * nblk, factors, 0)

    lam = (jnp.exp(jnp.sum(lq1_ref[...] * lk1_ref[...], axis=-1, keepdims=True))
           - jnp.exp(jnp.sum(lq2_ref[...] * lk2_ref[...], axis=-1, keepdims=True))
           + lambda_init)
    slope = slope_of(h)
    kaug = jnp.concatenate([k_ref[...], kf_ref[...]], axis=1)
    vt_ones = jnp.concatenate([vt_ref[...], jnp.ones((BF16_ROWS, seq), BF16)], axis=0)
    lane = lax.broadcasted_iota(jnp.int32, (tq, LANES), 1)
    strip = 4 * SUBLANES

    def scores(sb):
        blk = qi * nsub + sb
        qf = qf_ref[h * nblk + blk]
        q = q_ref[sb * tq:(sb + 1) * tq, :]
        zero = jnp.zeros_like(q)
        qaug = jnp.concatenate(
            [jnp.concatenate([jnp.where(lane < HEAD_DIM, q, zero), qf], axis=1),
             jnp.concatenate([jnp.where(lane >= HEAD_DIM, q, zero), qf], axis=1)], axis=0)
        s_ref = s_refs[sb]
        s = lax.dot_general(kaug, qaug, _NT, preferred_element_type=F32)
        s_ref[...] = s
        mx = s[0:strip, :]
        for r in range(strip, seq, strip):
            mx = jnp.maximum(mx, s[r:r + strip, :])
        m_refs[sb][...] = mx
        diag = pl.ds(pl.multiple_of(blk * tq, tq), tq)
        diag_bias = slope * dist_ref[...]
        s_ref[diag, 0:tq] = s_ref[diag, 0:tq] - diag_bias
        s_ref[diag, tq:2 * tq] = s_ref[diag, tq:2 * tq] - diag_bias

    def finish(sb):
        s_ref, e_ref = s_refs[sb], e_refs[sb]
        mx = jnp.max(m_refs[sb][...], axis=0, keepdims=True)
        for r in range(0, seq, strip):
            e_ref[r:r + strip, :] = jnp.exp2(s_ref[r:r + strip, :] - mx).astype(BF16)
        o12 = _dot(vt_ones, e_ref[...])
        norm = o12[V_DIM:V_DIM + 1, :]
        o12 = o12[0:V_DIM, :]
        o = o12[:, 0:tq] * (1.0 / norm[:, 0:tq]) - o12[:, tq:2 * tq] * (lam / norm[:, tq:2 * tq])
        o = o * lax.rsqrt(jnp.mean(o * o, axis=0, keepdims=True) + NORM_EPS)
        o = o * sg_ref[...] * (1.0 - lambda_init)
        o_ref[sb * tq:(sb + 1) * tq, :] = o.T.astype(o_ref.dtype)

    scores(0)
    for sb in range(nsub):
        if sb + 1 < nsub:
            scores(sb + 1)
        finish(sb)


def _attention(q, k, vt, lq1, lk1, lq2, lk2, subln_g, lambda_init, tq, nsub, cast_along=()):
    b, s, _ = q.shape
    vec = lambda bi, h, qi: (0, 0)
    tstep = tq * nsub
    nq = s // tstep
    nsteps = b * N_HEADS * nq

    def row_block(last):
        return lambda bi, h, qi: (jnp.minimum((bi * N_HEADS + h) * nq + qi, last), 0)

    side_in, side_out, side_shapes = [], [], []
    for w in cast_along:
        rows = -(-w.shape[0] // (nsteps * BF16_ROWS)) * BF16_ROWS
        while w.shape[0] % rows:
            rows += BF16_ROWS
        for specs in (side_in, side_out):
            specs.append(pl.BlockSpec((rows, w.shape[1]), row_block(w.shape[0] // rows - 1)))
        side_shapes.append(jax.ShapeDtypeStruct(w.shape, BF16))
    kernel = functools.partial(_attn_kernel, tq=tq, nsub=nsub, seq=s, lambda_init=lambda_init)
    return pl.pallas_call(
        kernel,
        grid=(b, N_HEADS, s // tstep),
        in_specs=[pl.BlockSpec((1, HEAD_DIM), vec)] * 4
        + [pl.BlockSpec((V_DIM, 1), vec),
           pl.BlockSpec((None, tstep, V_DIM), lambda bi, h, qi: (bi, qi, h)),
           pl.BlockSpec((None, s, V_DIM), lambda bi, h, qi: (bi, 0, h)),
           pl.BlockSpec((V_DIM, s), lambda bi, h, qi: (h, bi))] + side_in,
        out_specs=[pl.BlockSpec((None, tstep, V_DIM), lambda bi, h, qi: (bi, qi, h))] + side_out,
        out_shape=[jax.ShapeDtypeStruct((b, s, ATTN_WIDTH), BF16)] + side_shapes,
        scratch_shapes=[pltpu.VMEM((s, LANES), BF16), pltpu.VMEM((tq, tq), F32),
                        pltpu.VMEM((N_HEADS * (s // tq), tq, LANES), BF16)]
        + [pltpu.VMEM((s, 2 * tq), F32)] * nsub + [pltpu.VMEM((s, 2 * tq), BF16)] * nsub
        + [pltpu.VMEM((4 * SUBLANES, 2 * tq), F32)] * nsub,
        compiler_params=pltpu.CompilerParams(
            dimension_semantics=("arbitrary",) * 3, vmem_limit_bytes=VMEM_LIMIT),
        name="attn",
    )(lq1, lk1, lq2, lk2, subln_g, q, k, vt, *cast_along)


def _local_scan(a, u, reverse):
    row = lax.broadcasted_iota(jnp.int32, a.shape, 0)
    for d in (1, 2, 4):
        shift = SUBLANES - d if reverse else d
        valid = (row < SUBLANES - d) if reverse else (row >= d)
        a_s = jnp.where(valid, pltpu.roll(a, shift, 0), 1.0)
        u_s = jnp.where(valid, pltpu.roll(u, shift, 0), 0.0)
        u = a * u_s + u
        a = a * a_s
    return a, u


def _lru_kernel(xr_ref, gr_ref, cw_ref, cb_ref, wg_ref, bg_ref, lam_ref, o_ref,
                xp_ref, xc_ref, gates_ref, h_ref, cum_ref, *, seq, rows):
    width = xr_ref.shape[-1]
    ntile = width // LANES
    grp = seq // SUBLANES
    left = LRU_CONV_LEFT
    right = LRU_CONV_WIDTH - 1 - LRU_CONV_LEFT
    top = left * SUBLANES
    sub = lax.broadcasted_iota(jnp.int32, (SUBLANES, LANES), 0)

    for s in range(SUBLANES):
        blk = xr_ref[s * grp:(s + 1) * grp, :]
        for p in range(ntile):
            xp_ref[p, pl.ds(top + s, grp, stride=SUBLANES), :] = blk[:, p * LANES:(p + 1) * LANES]
    for p in range(ntile):
        for k in range(left):
            src = xp_ref[p, top + (grp - 1 - k) * SUBLANES:top + (grp - k) * SUBLANES, :]
            xp_ref[p, top - (k + 1) * SUBLANES:top - k * SUBLANES, :] = jnp.where(
                sub == 0, 0.0, pltpu.roll(src, 1, 0))
        for k in range(right):
            src = xp_ref[p, top + k * SUBLANES:top + (k + 1) * SUBLANES, :]
            xp_ref[p, top + (grp + k) * SUBLANES:top + (grp + k + 1) * SUBLANES, :] = jnp.where(
                sub == SUBLANES - 1, 0.0, pltpu.roll(src, SUBLANES - 1, 0))

    neg_lam = -lam_ref[...]
    softplus = jnp.maximum(neg_lam, 0.0) + jnp.log1p(jnp.exp(-jnp.abs(neg_lam)))
    rate = LRU_C * softplus
    rate_log2 = -rate * math.log2(math.e)

    nchunk = seq // rows
    for c in range(nchunk):
        r0 = c * rows
        for p in range(ntile):
            cols = slice(p * LANES, (p + 1) * LANES)
            reps = (rows // SUBLANES, 1)
            xc = jnp.tile(cb_ref[:, cols], reps)
            for tap in range(LRU_CONV_WIDTH):
                start = top + r0 + (tap - left) * SUBLANES
                w = jnp.tile(cw_ref[tap * SUBLANES:(tap + 1) * SUBLANES, cols], reps)
                xc = xc + xp_ref[p, start:start + rows, :] * w
            xc_ref[p, r0:r0 + rows, :] = xc

    zero = jnp.zeros((SUBLANES, LANES), F32)
    one = jnp.ones((SUBLANES, LANES), F32)
    nvr = rows // SUBLANES
    tile8 = lambda row: jnp.broadcast_to(row, (SUBLANES, LANES))

    def chunk(c, ends):
        ends = [list(ends[2 * k:2 * k + 2]) for k in range(2 * ntile)]
        base = (pl.multiple_of(c * rows, rows), pl.multiple_of((nchunk - 1 - c) * rows, rows))
        for p in range(ntile):
            for d in range(2):
                gcols = slice(2 * d * LANES, (2 * d + 2) * LANES)
                xcb = xc_ref[p, pl.ds(base[d], rows), :].astype(BF16)
                gates_ref[2 * p + d] = _dot(xcb, wg_ref[p, :, gcols]) + bg_ref[p:p + 1, gcols]
        for step in range(nvr):
            for p in range(ntile):
                cols = slice(p * LANES, (p + 1) * LANES)
                for d in range(2):
                    j = step if d == 0 else nvr - 1 - step
                    row = pl.ds(base[d] + j * SUBLANES, SUBLANES)
                    g = gates_ref[2 * p + d, j * SUBLANES:(j + 1) * SUBLANES, :]
                    th = jnp.tanh(g)
                    r = 0.5 + 0.5 * th[:, 0:LANES]
                    i = 0.5 + 0.5 * th[:, LANES:2 * LANES]
                    t = jnp.tanh(r * tile8(rate[d:d + 1, cols]))
                    tt = t + t
                    prod = tt * (1.0 + t)
                    mult = jnp.where(prod > 0.0, tt * lax.rsqrt(prod), 0.0)
                    a = jnp.exp2(r * tile8(rate_log2[d:d + 1, cols]))
                    u = mult * (i * xc_ref[p, row, :])
                    h, cum = ends[2 * p + d]
                    h = a * h + u
                    cum = a * cum
                    h_ref[d, p, row, :] = h
                    cum_ref[d, p, row, :] = cum
                    ends[2 * p + d] = [h, cum]
        return tuple(x for pair in ends for x in pair)

    flat = lax.fori_loop(0, nchunk, chunk, (zero, one) * (2 * ntile))
    ends = [[(flat[2 * (2 * p + d)], flat[2 * (2 * p + d) + 1]) for d in range(2)]
            for p in range(ntile)]

    enter = []
    for p in range(ntile):
        for d in range(2):
            h_end, cum_end = ends[p][d]
            _, chained = _local_scan(cum_end, h_end, reverse=(d == 1))
            if d == 0:
                enter.append(jnp.where(sub == 0, 0.0, pltpu.roll(chained, 1, 0)))
            else:
                enter.append(jnp.where(sub == SUBLANES - 1, 0.0,
                                       pltpu.roll(chained, SUBLANES - 1, 0)))

    for p in range(ntile):
        for j in range(grp):
            rs = slice(j * SUBLANES, (j + 1) * SUBLANES)
            xc_ref[p, rs, :] = ((h_ref[0, p, rs, :] + cum_ref[0, p, rs, :] * enter[2 * p])
                                + (h_ref[1, p, rs, :] + cum_ref[1, p, rs, :] * enter[2 * p + 1]))

    blk = BF16_ROWS

    def segment(s, carry):
        for p in range(ntile):
            cols = slice(p * LANES, (p + 1) * LANES)
            for j0 in range(0, grp, blk):
                rs = pl.ds(pl.multiple_of(s * grp + j0, blk), blk)
                y = xc_ref[p, pl.ds(s + j0 * SUBLANES, blk, stride=SUBLANES), :]
                o_ref[rs, cols] = (jax.nn.gelu(gr_ref[rs, cols], approximate=True)
                                   * y).astype(o_ref.dtype)
        return carry

    lax.fori_loop(0, SUBLANES, segment, 0)


def _lru(xr, gr, conv_w, conv_b, wg, bg, lru_lambda, rows):
    b, s, width = xr.shape
    ntile = width // LANES
    seqblk = pl.BlockSpec((None, s, width), lambda bi: (bi, 0, 0))
    full = lambda a: pl.BlockSpec(a.shape, lambda bi: (0,) * a.ndim)
    kernel = functools.partial(_lru_kernel, seq=s, rows=rows)
    halo_rows = (LRU_CONV_WIDTH - 1) * SUBLANES
    return pl.pallas_call(
        kernel,
        grid=(b,),
        in_specs=[seqblk, seqblk, full(conv_w), full(conv_b), full(wg), full(bg), full(lru_lambda)],
        out_specs=seqblk,
        out_shape=jax.ShapeDtypeStruct((b, s, width), BF16),
        scratch_shapes=[pltpu.VMEM((ntile, s + halo_rows, LANES), F32),
                        pltpu.VMEM((ntile, s, LANES), F32),
                        pltpu.VMEM((2 * ntile, rows, 2 * LANES), F32),
                        pltpu.VMEM((2, ntile, s, LANES), F32),
                        pltpu.VMEM((2, ntile, s, LANES), F32)],
        compiler_params=pltpu.CompilerParams(
            dimension_semantics=("arbitrary",), vmem_limit_bytes=VMEM_LIMIT),
        name="lru",
    )(xr, gr, conv_w, conv_b, wg, bg, lru_lambda)


def _outproj_kernel(x_ref, a_ref, l_ref, w_ref, g_ref, xm_ref, h_ref, wb_ref):
    @pl.when(pl.program_id(0) == 0)
    def _cast_weights():
        wb_ref[...] = w_ref[...].astype(BF16)

    aw = a_ref.shape[-1]
    xm = x_ref[...] + _dot(a_ref[...], wb_ref[0:aw, :]) + _dot(l_ref[...], wb_ref[aw:, :])
    xm_ref[...] = xm
    h_ref[...] = _rms(xm, g_ref[...]).astype(h_ref.dtype)


def _outproj(x2, attn2, lru2, w, g, tm):
    n, d = x2.shape
    row = lambda i: (i, 0)
    const = lambda i: (0, 0)
    return pl.pallas_call(
        _outproj_kernel,
        grid=(n // tm,),
        in_specs=[pl.BlockSpec((tm, d), row),
                  pl.BlockSpec((tm, attn2.shape[1]), row),
                  pl.BlockSpec((tm, lru2.shape[1]), row),
                  pl.BlockSpec(w.shape, const, pipeline_mode=pl.Buffered(1)),
                  pl.BlockSpec((1, d), const)],
        out_specs=[pl.BlockSpec((tm, d), row)] * 2,
        out_shape=[jax.ShapeDtypeStruct((n, d), F32), jax.ShapeDtypeStruct((n, d), BF16)],
        scratch_shapes=[pltpu.VMEM(w.shape, BF16)],
        compiler_params=pltpu.CompilerParams(
            dimension_semantics=("arbitrary",), vmem_limit_bytes=VMEM_LIMIT),
        name="outproj",
    )(x2, attn2, lru2, w, g)


def _ffn_kernel(hp_ref, hm_ref, hn_ref, xm_ref, wup_ref, cw_ref, cb_ref, wdn_ref, fg_ref, o_ref,
                perm_ref, hext_ref, ua_ref, ub_ref, act_ref, y_ref,
                *, tile, chunk, d_ff):
    i = pl.program_id(1)
    nchunks = d_ff // chunk
    ngrp = tile // SUBLANES
    nslab = perm_ref.shape[0]
    d = nslab * LANES

    for s in range(SUBLANES):
        rows = hm_ref[s * ngrp:(s + 1) * ngrp, :].astype(F32)
        for n in range(nslab):
            perm_ref[n, pl.ds(s, ngrp, stride=SUBLANES), :] = rows[:, n * LANES:(n + 1) * LANES]
    for n in range(nslab):
        hext_ref[0:tile, n * LANES:(n + 1) * LANES] = perm_ref[n].astype(BF16)
    prev = jnp.where(i > 0, hp_ref[BF16_ROWS - 1:BF16_ROWS, :].astype(F32), 0.0)
    nxt = jnp.where(i < pl.num_programs(1) - 1, hn_ref[0:1, :].astype(F32), 0.0)
    hrow = lax.broadcasted_iota(jnp.int32, (BF16_ROWS, d), 0)
    halo = jnp.where(hrow == 0, prev, jnp.where(hrow == 1, nxt, 0.0))
    hext_ref[tile:tile + BF16_ROWS, :] = halo.astype(BF16)

    def offsets(c):
        og, ov = c * chunk, d_ff + c * chunk
        if isinstance(c, int):
            return og, ov
        return pl.multiple_of(og, chunk), pl.multiple_of(ov, chunk)

    def up(c, u_ref):
        og, ov = offsets(c)
        hext = hext_ref[...]
        u_ref[:, 0:chunk] = _dot(hext, wup_ref[:, pl.ds(og, chunk)])
        u_ref[:, chunk:2 * chunk] = _dot(hext, wup_ref[:, pl.ds(ov, chunk)])

    def glu(c, u_ref, act_ref):
        sub = lax.broadcasted_iota(jnp.int32, (SUBLANES, chunk), 0)
        blk = BF16_ROWS

        def conv(col0, off, r0):
            cols = slice(col0, col0 + chunk)
            reps = (blk // SUBLANES, 1)
            w = [jnp.tile(cw_ref[k * SUBLANES:(k + 1) * SUBLANES, pl.ds(off, chunk)], reps)
                 for k in range(3)]
            bias = jnp.tile(cb_ref[:, pl.ds(off, chunk)], reps)
            cur = u_ref[r0:r0 + blk, cols]
            if r0 == 0:
                first = jnp.where(sub == 0, u_ref[tile:tile + 1, cols],
                                  pltpu.roll(u_ref[tile - SUBLANES:tile, cols], 1, 0))
                um1 = jnp.concatenate([first, cur[0:blk - SUBLANES]], axis=0)
            else:
                um1 = u_ref[r0 - SUBLANES:r0 + blk - SUBLANES, cols]
            if r0 + blk == tile:
                last = jnp.where(sub == SUBLANES - 1, u_ref[tile + 1:tile + 2, cols],
                                 pltpu.roll(u_ref[0:SUBLANES, cols], SUBLANES - 1, 0))
                up1 = jnp.concatenate([cur[SUBLANES:blk], last], axis=0)
            else:
                up1 = u_ref[r0 + SUBLANES:r0 + blk + SUBLANES, cols]
            return bias + um1 * w[0] + cur * w[1] + up1 * w[2]

        og, ov = offsets(c)
        for r0 in range(0, tile, blk):
            gate = conv(0, og, r0)
            val = conv(chunk, ov, r0)
            act_ref[r0:r0 + blk, :] = (jax.nn.gelu(gate, approximate=True) * val).astype(BF16)

    u_refs = (ua_ref, ub_ref)
    up(0, ua_ref)
    for c in range(nchunks):
        if c + 1 < nchunks:
            up(c + 1, u_refs[(c + 1) % 2])
        glu(c, u_refs[c % 2], act_ref.at[:, c * chunk:(c + 1) * chunk])
    y = _dot(act_ref[...], wdn_ref[...])
    for n in range(nslab):
        y_ref[n, 0:tile, :] = y[:, n * LANES:(n + 1) * LANES]
    for s in range(SUBLANES):
        rs = slice(s * ngrp, (s + 1) * ngrp)
        y = jnp.concatenate([y_ref[n, pl.ds(s, ngrp, stride=SUBLANES), :] for n in range(nslab)],
                            axis=1)
        o_ref[rs, :] = _rms(xm_ref[rs, :] + y, fg_ref[...])


def _ffn(h2, xm, w_up, conv_w, conv_b, w_down, final_g, tile, chunk):
    b, s, d = h2.shape
    d_ff = w_down.shape[0]
    nh = tile // BF16_ROWS
    last_halo = s // BF16_ROWS - 1
    main = lambda bi, i: (bi, i, 0)
    const = lambda bi, i: (0, 0)
    single = dict(pipeline_mode=pl.Buffered(1))
    kernel = functools.partial(_ffn_kernel, tile=tile, chunk=chunk, d_ff=d_ff)
    return pl.pallas_call(
        kernel,
        grid=(b, s // tile),
        in_specs=[pl.BlockSpec((None, BF16_ROWS, d), lambda bi, i: (bi, jnp.maximum(i * nh - 1, 0), 0)),
                  pl.BlockSpec((None, tile, d), main),
                  pl.BlockSpec((None, BF16_ROWS, d),
                               lambda bi, i: (bi, jnp.minimum((i + 1) * nh, last_halo), 0)),
                  pl.BlockSpec((None, tile, d), main),
                  pl.BlockSpec(w_up.shape, const, **single),
                  pl.BlockSpec(conv_w.shape, const),
                  pl.BlockSpec(conv_b.shape, const),
                  pl.BlockSpec(w_down.shape, const, **single),
                  pl.BlockSpec((1, d), const)],
        out_specs=pl.BlockSpec((None, tile, d), main),
        out_shape=jax.ShapeDtypeStruct((b, s, d), F32),
        scratch_shapes=[pltpu.VMEM((d // LANES, tile, LANES), F32),
                        pltpu.VMEM((tile + BF16_ROWS, d), BF16)]
        + [pltpu.VMEM((tile + BF16_ROWS, 2 * chunk), F32)] * 2
        + [pltpu.VMEM((tile, d_ff), BF16),
           pltpu.VMEM((d // LANES, tile + SUBLANES, LANES), F32)],
        compiler_params=pltpu.CompilerParams(
            dimension_semantics=("arbitrary",) * 2, vmem_limit_bytes=VMEM_LIMIT),
        name="ffn",
    )(h2, h2, h2, xm, w_up, conv_w, conv_b, w_down, final_g)


def _gate_weights(w_a, b_a, w_x, b_x):
    ndir, nblk, bd, _ = w_a.shape
    per = LANES // bd
    npair = nblk // per

    def blockdiag(w):
        w = w.reshape(npair, per, bd, bd)
        eye = jnp.eye(per, dtype=w.dtype)
        return jnp.einsum('pbij,bc->pbicj', w, eye).reshape(npair, LANES, LANES)

    ws, bs = [], []
    for d in range(ndir):
        for w, bias in ((w_a, b_a), (w_x, b_x)):
            ws.append(blockdiag(w[d]))
            bs.append(bias[d].reshape(npair, LANES))
    return ((0.5 * jnp.concatenate(ws, axis=-1)).astype(BF16),
            (0.5 * jnp.concatenate(bs, axis=-1)).astype(F32))


def kernel(x, attn_norm_g, w_in, lambda_q1, lambda_k1, lambda_q2, lambda_k2, subln_g,
           lru_conv_w, lru_conv_b, lru_w_a, lru_b_a, lru_w_x, lru_b_x, lru_lambda,
           w_out, ffn_norm_g, w_up, ffn_conv_w, ffn_conv_b, w_down, final_norm_g):
    b, s, d = x.shape
    depth = w_in.shape[0]
    x2 = x.reshape(b * s, d)
    assert depth == 1
    for l in range(depth):
        lambda_init = 0.8 - 0.6 * math.exp(-0.3 * l)
        q, k, vt, xr, gr = _inproj(x2, attn_norm_g[l][None], w_in[l], tm=1024)
        lw = xr.shape[-1]
        attn, w_up_bf, w_down_bf = _attention(
            q.reshape(b, s, -1), k.reshape(b, s, -1), vt,
            lambda_q1[l][None], lambda_k1[l][None], lambda_q2[l][None], lambda_k2[l][None],
            subln_g[l][:, None], lambda_init, tq=256, nsub=4, cast_along=(w_up[l], w_down[l]))
        wg, bg = _gate_weights(lru_w_a[l], lru_b_a[l], lru_w_x[l], lru_b_x[l])
        lru = _lru(xr.reshape(b, s, lw), gr.reshape(b, s, lw),
                   jnp.repeat(lru_conv_w[l], SUBLANES, axis=0),
                   jnp.broadcast_to(lru_conv_b[l][None], (SUBLANES, lw)),
                   wg, bg, lru_lambda[l], rows=512)
        xm, h2 = _outproj(x2, attn.reshape(b * s, -1), lru.reshape(b * s, -1),
                          w_out[l], ffn_norm_g[l][None], tm=1024)
        x2 = _ffn(h2.reshape(b, s, d), xm.reshape(b, s, d), w_up_bf,
                  jnp.repeat(ffn_conv_w[l], SUBLANES, axis=0),
                  jnp.broadcast_to(ffn_conv_b[l][None], (SUBLANES, ffn_conv_b.shape[-1])),
                  w_down_bf, final_norm_g[None],
                  tile=512, chunk=256).reshape(b * s, d)
    return x2.reshape(b, s, d)
```

```python
import functools
import math

import jax
import jax.numpy as jnp
from jax import lax
from jax.experimental import pallas as pl
from jax.experimental.pallas import tpu as pltpu

F32 = jnp.float32
BF16 = jnp.bfloat16

N_HEADS = 4
HEAD_DIM = 64
V_DIM = 2 * HEAD_DIM
ATTN_WIDTH = N_HEADS * V_DIM
LRU_CONV_WIDTH = 4
LRU_CONV_LEFT = 2
LRU_C = 8.0
NORM_EPS = 1e-6
LOG2_E = math.log2(math.e)
LANES = 128
SUBLANES = 8
BF16_ROWS = 16
VMEM_LIMIT = 56 * 1024 * 1024


def _rms(x, g):
    return (x * lax.rsqrt(jnp.mean(x * x, axis=-1, keepdims=True) + NORM_EPS)) * g


def _dot(a, b):
    return jnp.dot(a, b, preferred_element_type=F32)


_NT = (((1,), (1,)), ((), ()))


def _inproj_kernel(x_ref, g_ref, w_ref, q_ref, k_ref, vt_ref, xr_ref, gr_ref, wb_ref, wvt_ref):
    aw = ATTN_WIDTH
    d = x_ref.shape[-1]

    @pl.when(pl.program_id(0) == 0)
    def _cast_weights():
        rows = 2 * LANES
        for r in range(0, d, rows):
            w = w_ref[r:r + rows, :]
            wb_ref[r:r + rows, 0:aw] = (w[:, 0:aw] * (HEAD_DIM ** -0.5 * LOG2_E)).astype(BF16)
            wb_ref[r:r + rows, aw:] = w[:, aw:].astype(BF16)
        for c in range(aw // LANES):
            cols = slice(2 * aw + c * LANES, 2 * aw + (c + 1) * LANES)
            wvt_ref[c * LANES:(c + 1) * LANES, :] = w_ref[:, cols].T.astype(BF16)

    hb = _rms(x_ref[...], g_ref[...]).astype(BF16)
    q_ref[...] = _dot(hb, wb_ref[:, 0:aw]).astype(BF16)
    k_ref[...] = _dot(hb, wb_ref[:, aw:2 * aw]).astype(BF16)
    vt_ref[...] = lax.dot_general(wvt_ref[...], hb, _NT, preferred_element_type=F32).astype(BF16)
    lw = xr_ref.shape[-1]
    xr_ref[...] = _dot(hb, wb_ref[:, 3 * aw:3 * aw + lw])
    gr_ref[...] = _dot(hb, wb_ref[:, 3 * aw + lw:3 * aw + 2 * lw])


def _inproj(x2, g, w, tm):
    n, d = x2.shape
    lw = (w.shape[1] - 3 * ATTN_WIDTH) // 2
    row = lambda i: (i, 0)
    const = lambda i: (0, 0)
    return pl.pallas_call(
        _inproj_kernel,
        grid=(n // tm,),
        in_specs=[pl.BlockSpec((tm, d), row),
                  pl.BlockSpec((1, d), const),
                  pl.BlockSpec(w.shape, const, pipeline_mode=pl.Buffered(1))],
        out_specs=[pl.BlockSpec((tm, ATTN_WIDTH), row)] * 2
        + [pl.BlockSpec((ATTN_WIDTH, tm), lambda i: (0, i))]
        + [pl.BlockSpec((tm, lw), row)] * 2,
        out_shape=[jax.ShapeDtypeStruct((n, ATTN_WIDTH), BF16)] * 2
        + [jax.ShapeDtypeStruct((ATTN_WIDTH, n), BF16)]
        + [jax.ShapeDtypeStruct((n, lw), F32)] * 2,
        scratch_shapes=[pltpu.VMEM(w.shape, BF16), pltpu.VMEM((ATTN_WIDTH, d), BF16)],
        compiler_params=pltpu.CompilerParams(
            dimension_semantics=("arbitrary",), vmem_limit_bytes=VMEM_LIMIT),
        name="inproj",
    )(x2, g, w)


def _attn_kernel(lq1_ref, lk1_ref, lq2_ref, lk2_ref, sg_ref, q_ref, k_ref, vt_ref, *rest,
                 tq, nsub, seq, lambda_init):
    nside = (len(rest) - 4 - 3 * nsub) // 2
    side_in, o_ref, side_out = rest[:nside], rest[nside], rest[nside + 1:2 * nside + 1]
    kf_ref, dist_ref, qf_ref, *se_refs = rest[2 * nside + 1:]
    for src, dst in zip(side_in, side_out):
        dst[...] = src[...].astype(dst.dtype)

    h = pl.program_id(1)
    qi = pl.program_id(2)
    s_refs, e_refs, m_refs = se_refs[:nsub], se_refs[nsub:2 * nsub], se_refs[2 * nsub:]
    nblk = seq // tq
    assert nblk & (nblk - 1) == 0 and 9 * nblk <= LANES and tq <= 256
    shift = nblk.bit_length() - 1

    def slope_of(head):
        return LOG2_E * jnp.where(head == 0, 2.0 ** -2, jnp.where(head == 1, 2.0 ** -4,
                                  jnp.where(head == 2, 2.0 ** -6, 2.0 ** -8))).astype(F32)

    @pl.when((pl.program_id(0) == 0) & (h == 0) & (qi == 0))
    def _init():
        lane = lax.broadcasted_iota(jnp.int32, (seq, LANES), 1)
        row = lax.broadcasted_iota(jnp.int32, (seq, LANES), 0)
        grp = lane >> shift
        hit = (row // tq) == (lane & (nblk - 1))
        dj = (row % tq).astype(F32)
        kf_ref[...] = jnp.where(hit & (grp < 6), 1.0,
                                jnp.where(hit & (grp < 9), dj, 0.0)).astype(BF16)
        r = lax.broadcasted_iota(jnp.int32, (tq, tq), 0)
        c = lax.broadcasted_iota(jnp.int32, (tq, tq), 1)
        dist_ref[...] = jnp.abs(r - c).astype(F32)

        lane = lax.broadcasted_iota(jnp.int32, (tq, LANES), 1)
        di = lax.broadcasted_iota(jnp.int32, (tq, LANES), 0).astype(F32)
        grp = lane >> shift
        term = (grp >= 3).astype(jnp.int32) + (grp >= 6).astype(jnp.int32)
        piece = grp - 3 * term

        def factors(idx, carry):
            slope = slope_of(idx // nblk)
            diff = idx % nblk - (lane & (nblk - 1))
            sign = jnp.where(diff > 0, 1.0, jnp.where(diff < 0, -1.0, 0.0))
            fac = jnp.where(term == 0, -slope * sign * di,
                            jnp.where(term == 1, -slope * tq * jnp.abs(diff).astype(F32),
                                      slope * sign))
            hi = fac.astype(BF16)
            rest1 = fac - hi.astype(F32)
            mid = rest1.astype(BF16)
            lo = (rest1 - mid.astype(F32)).astype(BF16)
            qf_ref[idx] = jnp.where(grp >= 9, jnp.zeros_like(hi),
                                    jnp.where(piece == 0, hi, jnp.where(piece == 1, mid, lo)))
            return carry

        lax.fori_loop(0, N_HEADS * nblk, factors, 0)

    lam = (jnp.exp(jnp.sum(lq1_ref[...] * lk1_ref[...], axis=-1, keepdims=True))
           - jnp.exp(jnp.sum(lq2_ref[...] * lk2_ref[...], axis=-1, keepdims=True))
           + lambda_init)
    slope = slope_of(h)
    kaug = jnp.concatenate([k_ref[...], kf_ref[...]], axis=1)
    vt_ones = jnp.concatenate([vt_ref[...], jnp.ones((BF16_ROWS, seq), BF16)], axis=0)
    lane = lax.broadcasted_iota(jnp.int32, (tq, LANES), 1)
    strip = 4 * SUBLANES

    def scores(sb):
        blk = qi * nsub + sb
        qf = qf_ref[h * nblk + blk]
        q = q_ref[sb * tq:(sb + 1) * tq, :]
        zero = jnp.zeros_like(q)
        qaug = jnp.concatenate(
            [jnp.concatenate([jnp.where(lane < HEAD_DIM, q, zero), qf], axis=1),
             jnp.concatenate([jnp.where(lane >= HEAD_DIM, q, zero), qf], axis=1)], axis=0)
        s_ref = s_refs[sb]
        s = lax.dot_general(kaug, qaug, _NT, preferred_element_type=F32)
        s_ref[...] = s
        mx = s[0:strip, :]
        for r in range(strip, seq, strip):
            mx = jnp.maximum(mx, s[r:r + strip, :])
        m_refs[sb][...] = mx
        diag = pl.ds(pl.multiple_of(blk * tq, tq), tq)
        diag_bias = slope * dist_ref[...]
        s_ref[diag, 0:tq] = s_ref[diag, 0:tq] - diag_bias
        s_ref[diag, tq:2 * tq] = s_ref[diag, tq:2 * tq] - diag_bias

    def finish(sb):
        s_ref, e_ref = s_refs[sb], e_refs[sb]
        mx = jnp.max(m_refs[sb][...], axis=0, keepdims=True)
        for r in range(0, seq, strip):
            e_ref[r:r + strip, :] = jnp.exp2(s_ref[r:r + strip, :] - mx).astype(BF16)
        o12 = _dot(vt_ones, e_ref[...])
        norm = o12[V_DIM:V_DIM + 1, :]
        o12 = o12[0:V_DIM, :]
        o = o12[:, 0:tq] * (1.0 / norm[:, 0:tq]) - o12[:, tq:2 * tq] * (lam / norm[:, tq:2 * tq])
        o = o * lax.rsqrt(jnp.mean(o * o, axis=0, keepdims=True) + NORM_EPS)
        o = o * sg_ref[...] * (1.0 - lambda_init)
        o_ref[sb * tq:(sb + 1) * tq, :] = o.T.astype(o_ref.dtype)

    scores(0)
    for sb in range(nsub):
        if sb + 1 < nsub:
            scores(sb + 1)
        finish(sb)


def _attention(q, k, vt, lq1, lk1, lq2, lk2, subln_g, lambda_init, tq, nsub, cast_along=()):
    b, s, _ = q.shape
    vec = lambda bi, h, qi: (0, 0)
    tstep = tq * nsub
    nq = s // tstep
    nsteps = b * N_HEADS * nq

    def row_block(last):
        return lambda bi, h, qi: (jnp.minimum((bi * N_HEADS + h) * nq + qi, last), 0)

    side_in, side_out, side_shapes = [], [], []
    for w in cast_along:
        rows = -(-w.shape[0] // (nsteps * BF16_ROWS)) * BF16_ROWS
        while w.shape[0] % rows:
            rows += BF16_ROWS
        for specs in (side_in, side_out):
            specs.append(pl.BlockSpec((rows, w.shape[1]), row_block(w.shape[0] // rows - 1)))
        side_shapes.append(jax.ShapeDtypeStruct(w.shape, BF16))
    kernel = functools.partial(_attn_kernel, tq=tq, nsub=nsub, seq=s, lambda_init=lambda_init)
    return pl.pallas_call(
        kernel,
        grid=(b, N_HEADS, s // tstep),
        in_specs=[pl.BlockSpec((1, HEAD_DIM), vec)] * 4
        + [pl.BlockSpec((V_DIM, 1), vec),
           pl.BlockSpec((None, tstep, V_DIM), lambda bi, h, qi: (bi, qi, h)),
           pl.BlockSpec((None, s, V_DIM), lambda bi, h, qi: (bi, 0, h)),
           pl.BlockSpec((V_DIM, s), lambda bi, h, qi: (h, bi))] + side_in,
        out_specs=[pl.BlockSpec((None, tstep, V_DIM), lambda bi, h, qi: (bi, qi, h))] + side_out,
        out_shape=[jax.ShapeDtypeStruct((b, s, ATTN_WIDTH), BF16)] + side_shapes,
        scratch_shapes=[pltpu.VMEM((s, LANES), BF16), pltpu.VMEM((tq, tq), F32),
                        pltpu.VMEM((N_HEADS * (s // tq), tq, LANES), BF16)]
        + [pltpu.VMEM((s, 2 * tq), F32)] * nsub + [pltpu.VMEM((s, 2 * tq), BF16)] * nsub
        + [pltpu.VMEM((4 * SUBLANES, 2 * tq), F32)] * nsub,
        compiler_params=pltpu.CompilerParams(
            dimension_semantics=("arbitrary",) * 3, vmem_limit_bytes=VMEM_LIMIT),
        name="attn",
    )(lq1, lk1, lq2, lk2, subln_g, q, k, vt, *cast_along)


def _local_scan(a, u, reverse):
    row = lax.broadcasted_iota(jnp.int32, a.shape, 0)
    for d in (1, 2, 4):
        shift = SUBLANES - d if reverse else d
        valid = (row < SUBLANES - d) if reverse else (row >= d)
        a_s = jnp.where(valid, pltpu.roll(a, shift, 0), 1.0)
        u_s = jnp.where(valid, pltpu.roll(u, shift, 0), 0.0)
        u = a * u_s + u
        a = a * a_s
    return a, u


def _lru_kernel(xr_ref, gr_ref, cw_ref, cb_ref, wg_ref, bg_ref, lam_ref, o_ref,
                xp_ref, xc_ref, gates_ref, h_ref, cum_ref, *, seq, rows):
    width = xr_ref.shape[-1]
    ntile = width // LANES
    grp = seq // SUBLANES
    left = LRU_CONV_LEFT
    right = LRU_CONV_WIDTH - 1 - LRU_CONV_LEFT
    top = left * SUBLANES
    sub = lax.broadcasted_iota(jnp.int32, (SUBLANES, LANES), 0)

    for s in range(SUBLANES):
        blk = xr_ref[s * grp:(s + 1) * grp, :]
        for p in range(ntile):
            xp_ref[p, pl.ds(top + s, grp, stride=SUBLANES), :] = blk[:, p * LANES:(p + 1) * LANES]
    for p in range(ntile):
        for k in range(left):
            src = xp_ref[p, top + (grp - 1 - k) * SUBLANES:top + (grp - k) * SUBLANES, :]
            xp_ref[p, top - (k + 1) * SUBLANES:top - k * SUBLANES, :] = jnp.where(
                sub == 0, 0.0, pltpu.roll(src, 1, 0))
        for k in range(right):
            src = xp_ref[p, top + k * SUBLANES:top + (k + 1) * SUBLANES, :]
            xp_ref[p, top + (grp + k) * SUBLANES:top + (grp + k + 1) * SUBLANES, :] = jnp.where(
                sub == SUBLANES - 1, 0.0, pltpu.roll(src, SUBLANES - 1, 0))

    neg_lam = -lam_ref[...]
    softplus = jnp.maximum(neg_lam, 0.0) + jnp.log1p(jnp.exp(-jnp.abs(neg_lam)))
    rate = LRU_C * softplus
    rate_log2 = -rate * math.log2(math.e)

    nchunk = seq // rows
    for c in range(nchunk):
        r0 = c * rows
        for p in range(ntile):
            cols = slice(p * LANES, (p + 1) * LANES)
            reps = (rows // SUBLANES, 1)
            xc = jnp.tile(cb_ref[:, cols], reps)
            for tap in range(LRU_CONV_WIDTH):
                start = top + r0 + (tap - left) * SUBLANES
                w = jnp.tile(cw_ref[tap * SUBLANES:(tap + 1) * SUBLANES, cols], reps)
                xc = xc + xp_ref[p, start:start + rows, :] * w
            xc_ref[p, r0:r0 + rows, :] = xc

    zero = jnp.zeros((SUBLANES, LANES), F32)
    one = jnp.ones((SUBLANES, LANES), F32)
    nvr = rows // SUBLANES
    tile8 = lambda row: jnp.broadcast_to(row, (SUBLANES, LANES))

    def chunk(c, ends):
        ends = [list(ends[2 * k:2 * k + 2]) for k in range(2 * ntile)]
        base = (pl.multiple_of(c * rows, rows), pl.multiple_of((nchunk - 1 - c) * rows, rows))
        for p in range(ntile):
            for d in range(2):
                gcols = slice(2 * d * LANES, (2 * d + 2) * LANES)
                xcb = xc_ref[p, pl.ds(base[d], rows), :].astype(BF16)
                gates_ref[2 * p + d] = _dot(xcb, wg_ref[p, :, gcols]) + bg_ref[p:p + 1, gcols]
        for step in range(nvr):
            for p in range(ntile):
                cols = slice(p * LANES, (p + 1) * LANES)
                for d in range(2):
                    j = step if d == 0 else nvr - 1 - step
                    row = pl.ds(base[d] + j * SUBLANES, SUBLANES)
                    g = gates_ref[2 * p + d, j * SUBLANES:(j + 1) * SUBLANES, :]
                    th = jnp.tanh(g)
                    r = 0.5 + 0.5 * th[:, 0:LANES]
                    i = 0.5 + 0.5 * th[:, LANES:2 * LANES]
                    t = jnp.tanh(r * tile8(rate[d:d + 1, cols]))
                    tt = t + t
                    prod = tt * (1.0 + t)
                    mult = jnp.where(prod > 0.0, tt * lax.rsqrt(prod), 0.0)
                    a = jnp.exp2(r * tile8(rate_log2[d:d + 1, cols]))
                    u = mult * (i * xc_ref[p, row, :])
                    h, cum = ends[2 * p + d]
                    h = a * h + u
                    cum = a * cum
                    h_ref[d, p, row, :] = h
                    cum_ref[d, p, row, :] = cum
                    ends[2 * p + d] = [h, cum]
        return tuple(x for pair in ends for x in pair)

    flat = lax.fori_loop(0, nchunk, chunk, (zero, one) * (2 * ntile))
    ends = [[(flat[2 * (2 * p + d)], flat[2 * (2 * p + d) + 1]) for d in range(2)]
            for p in range(ntile)]

    enter = []
    for p in range(ntile):
        for d in range(2):
            h_end, cum_end = ends[p][d]
            _, chained = _local_scan(cum_end, h_end, reverse=(d == 1))
            if d == 0:
                enter.append(jnp.where(sub == 0, 0.0, pltpu.roll(chained, 1, 0)))
            else:
                enter.append(jnp.where(sub == SUBLANES - 1, 0.0,
                                       pltpu.roll(chained, SUBLANES - 1, 0)))

    for p in range(ntile):
        for j in range(grp):
            rs = slice(j * SUBLANES, (j + 1) * SUBLANES)
            xc_ref[p, rs, :] = ((h_ref[0, p, rs, :] + cum_ref[0, p, rs, :] * enter[2 * p])
                                + (h_ref[1, p, rs, :] + cum_ref[1, p, rs, :] * enter[2 * p + 1]))

    blk = BF16_ROWS

    def segment(s, carry):
        for p in range(ntile):
            cols = slice(p * LANES, (p + 1) * LANES)
            for j0 in range(0, grp, blk):
                rs = pl.ds(pl.multiple_of(s * grp + j0, blk), blk)
                y = xc_ref[p, pl.ds(s + j0 * SUBLANES, blk, stride=SUBLANES), :]
                o_ref[rs, cols] = (jax.nn.gelu(gr_ref[rs, cols], approximate=True)
                                   * y).astype(o_ref.dtype)
        return carry

    lax.fori_loop(0, SUBLANES, segment, 0)


def _lru(xr, gr, conv_w, conv_b, wg, bg, lru_lambda, rows):
    b, s, width = xr.shape
    ntile = width // LANES
    seqblk = pl.BlockSpec((None, s, width), lambda bi: (bi, 0, 0))
    full = lambda a: pl.BlockSpec(a.shape, lambda bi: (0,) * a.ndim)
    kernel = functools.partial(_lru_kernel, seq=s, rows=rows)
    halo_rows = (LRU_CONV_WIDTH - 1) * SUBLANES
    return pl.pallas_call(
        kernel,
        grid=(b,),
        in_specs=[seqblk, seqblk, full(conv_w), full(conv_b), full(wg), full(bg), full(lru_lambda)],
        out_specs=seqblk,
        out_shape=jax.ShapeDtypeStruct((b, s, width), BF16),
        scratch_shapes=[pltpu.VMEM((ntile, s + halo_rows, LANES), F32),
                        pltpu.VMEM((ntile, s, LANES), F32),
                        pltpu.VMEM((2 * ntile, rows, 2 * LANES), F32),
                        pltpu.VMEM((2, ntile, s, LANES), F32),
                        pltpu.VMEM((2, ntile, s, LANES), F32)],
        compiler_params=pltpu.CompilerParams(
            dimension_semantics=("arbitrary",), vmem_limit_bytes=VMEM_LIMIT),
        name="lru",
    )(xr, gr, conv_w, conv_b, wg, bg, lru_lambda)


def _ffn_kernel(xp_ref, xc_ref, xn_ref, ap_ref, ac_ref, an_ref, lp_ref, lc_ref, ln_ref,
                wout_ref, og_ref, wup_ref, cw_ref, cb_ref, wdn_ref, fg_ref, o_ref,
                perm_ref, hext_ref, ua_ref, ub_ref, act_ref, y_ref, xm_ref,
                *, tile, chunk, d_ff):
    i = pl.program_id(1)
    nchunks = d_ff // chunk
    ngrp = tile // SUBLANES
    nslab = perm_ref.shape[0]
    d = nslab * LANES
    aw = ac_ref.shape[-1]

    def x_mid(x_ref, a_ref, l_ref):
        return (x_ref[...] + _dot(a_ref[...], wout_ref[0:aw, :])
                + _dot(l_ref[...], wout_ref[aw:, :]))

    xm_ref[...] = x_mid(xc_ref, ac_ref, lc_ref)
    for s in range(SUBLANES):
        rows = _rms(xm_ref[s * ngrp:(s + 1) * ngrp, :], og_ref[...])
        for n in range(nslab):
            perm_ref[n, pl.ds(s, ngrp, stride=SUBLANES), :] = rows[:, n * LANES:(n + 1) * LANES]
    for n in range(nslab):
        hext_ref[0:tile, n * LANES:(n + 1) * LANES] = perm_ref[n].astype(BF16)
    h_prev = _rms(x_mid(xp_ref, ap_ref, lp_ref), og_ref[...])
    h_next = _rms(x_mid(xn_ref, an_ref, ln_ref), og_ref[...])
    prev = jnp.where(i > 0, h_prev[BF16_ROWS - 1:BF16_ROWS, :], 0.0)
    nxt = jnp.where(i < pl.num_programs(1) - 1, h_next[0:1, :], 0.0)
    hrow = lax.broadcasted_iota(jnp.int32, (BF16_ROWS, d), 0)
    halo = jnp.where(hrow == 0, prev, jnp.where(hrow == 1, nxt, 0.0))
    hext_ref[tile:tile + BF16_ROWS, :] = halo.astype(BF16)

    def offsets(c):
        og, ov = c * chunk, d_ff + c * chunk
        if isinstance(c, int):
            return og, ov
        return pl.multiple_of(og, chunk), pl.multiple_of(ov, chunk)

    def up(c, u_ref):
        og, ov = offsets(c)
        hext = hext_ref[...]
        u_ref[:, 0:chunk] = _dot(hext, wup_ref[:, pl.ds(og, chunk)])
        u_ref[:, chunk:2 * chunk] = _dot(hext, wup_ref[:, pl.ds(ov, chunk)])

    def glu(c, u_ref, act_ref):
        sub = lax.broadcasted_iota(jnp.int32, (SUBLANES, chunk), 0)
        blk = BF16_ROWS

        def conv(col0, off, r0):
            cols = slice(col0, col0 + chunk)
            reps = (blk // SUBLANES, 1)
            w = [jnp.tile(cw_ref[k * SUBLANES:(k + 1) * SUBLANES, pl.ds(off, chunk)], reps)
                 for k in range(3)]
            bias = jnp.tile(cb_ref[:, pl.ds(off, chunk)], reps)
            cur = u_ref[r0:r0 + blk, cols]
            if r0 == 0:
                first = jnp.where(sub == 0, u_ref[tile:tile + 1, cols],
                                  pltpu.roll(u_ref[tile - SUBLANES:tile, cols], 1, 0))
                um1 = jnp.concatenate([first, cur[0:blk - SUBLANES]], axis=0)
            else:
                um1 = u_ref[r0 - SUBLANES:r0 + blk - SUBLANES, cols]
            if r0 + blk == tile:
                last = jnp.where(sub == SUBLANES - 1, u_ref[tile + 1:tile + 2, cols],
                                 pltpu.roll(u_ref[0:SUBLANES, cols], SUBLANES - 1, 0))
                up1 = jnp.concatenate([cur[SUBLANES:blk], last], axis=0)
            else:
                up1 = u_ref[r0 + SUBLANES:r0 + blk + SUBLANES, cols]
            return bias + um1 * w[0] + cur * w[1] + up1 * w[2]

        og, ov = offsets(c)
        for r0 in range(0, tile, blk):
            gate = conv(0, og, r0)
            val = conv(chunk, ov, r0)
            act_ref[r0:r0 + blk, :] = (jax.nn.gelu(gate, approximate=True) * val).astype(BF16)

    u_refs = (ua_ref, ub_ref)
    up(0, ua_ref)
    for c in range(nchunks):
        if c + 1 < nchunks:
            up(c + 1, u_refs[(c + 1) % 2])
        glu(c, u_refs[c % 2], act_ref.at[:, c * chunk:(c + 1) * chunk])
    y = _dot(act_ref[...], wdn_ref[...])
    for n in range(nslab):
        y_ref[n, 0:tile, :] = y[:, n * LANES:(n + 1) * LANES]
    for s in range(SUBLANES):
        rs = slice(s * ngrp, (s + 1) * ngrp)
        y = jnp.concatenate([y_ref[n, pl.ds(s, ngrp, stride=SUBLANES), :] for n in range(nslab)],
                            axis=1)
        o_ref[rs, :] = _rms(xm_ref[rs, :] + y, fg_ref[...])


def _ffn(x, attn, lru, w_out, out_g, w_up, conv_w, conv_b, w_down, final_g, tile, chunk):
    b, s, d = x.shape
    aw, lw = attn.shape[-1], lru.shape[-1]
    d_ff = w_down.shape[0]
    nh = tile // BF16_ROWS
    last_halo = s // BF16_ROWS - 1
    main = lambda bi, i: (bi, i, 0)
    const = lambda bi, i: (0, 0)
    single = dict(pipeline_mode=pl.Buffered(1))
    kernel = functools.partial(_ffn_kernel, tile=tile, chunk=chunk, d_ff=d_ff)
    return pl.pallas_call(
        kernel,
        grid=(b, s // tile),
        in_specs=[spec for width in (d, aw, lw) for spec in (
                      pl.BlockSpec((None, BF16_ROWS, width),
                                   lambda bi, i: (bi, jnp.maximum(i * nh - 1, 0), 0)),
                      pl.BlockSpec((None, tile, width), main),
                      pl.BlockSpec((None, BF16_ROWS, width),
                                   lambda bi, i: (bi, jnp.minimum((i + 1) * nh, last_halo), 0)))]
        + [pl.BlockSpec(w_out.shape, const, **single),
           pl.BlockSpec((1, d), const),
           pl.BlockSpec(w_up.shape, const, **single),
           pl.BlockSpec(conv_w.shape, const),
           pl.BlockSpec(conv_b.shape, const),
           pl.BlockSpec(w_down.shape, const, **single),
           pl.BlockSpec((1, d), const)],
        out_specs=pl.BlockSpec((None, tile, d), main),
        out_shape=jax.ShapeDtypeStruct((b, s, d), F32),
        scratch_shapes=[pltpu.VMEM((d // LANES, tile, LANES), F32),
                        pltpu.VMEM((tile + BF16_ROWS, d), BF16)]
        + [pltpu.VMEM((tile + BF16_ROWS, 2 * chunk), F32)] * 2
        + [pltpu.VMEM((tile, d_ff), BF16),
           pltpu.VMEM((d // LANES, tile + SUBLANES, LANES), F32),
           pltpu.VMEM((tile, d), F32)],
        compiler_params=pltpu.CompilerParams(
            dimension_semantics=("arbitrary",) * 2, vmem_limit_bytes=VMEM_LIMIT),
        name="ffn",
    )(x, x, x, attn, attn, attn, lru, lru, lru, w_out, out_g, w_up, conv_w, conv_b, w_down, final_g)


def _gate_weights(w_a, b_a, w_x, b_x):
    ndir, nblk, bd, _ = w_a.shape
    per = LANES // bd
    npair = nblk // per

    def blockdiag(w):
        w = w.reshape(npair, per, bd, bd)
        eye = jnp.eye(per, dtype=w.dtype)
        return jnp.einsum('pbij,bc->pbicj', w, eye).reshape(npair, LANES, LANES)

    ws, bs = [], []
    for d in range(ndir):
        for w, bias in ((w_a, b_a), (w_x, b_x)):
            ws.append(blockdiag(w[d]))
            bs.append(bias[d].reshape(npair, LANES))
    return ((0.5 * jnp.concatenate(ws, axis=-1)).astype(BF16),
            (0.5 * jnp.concatenate(bs, axis=-1)).astype(F32))


def kernel(x, attn_norm_g, w_in, lambda_q1, lambda_k1, lambda_q2, lambda_k2, subln_g,
           lru_conv_w, lru_conv_b, lru_w_a, lru_b_a, lru_w_x, lru_b_x, lru_lambda,
           w_out, ffn_norm_g, w_up, ffn_conv_w, ffn_conv_b, w_down, final_norm_g):
    b, s, d = x.shape
    depth = w_in.shape[0]
    x2 = x.reshape(b * s, d)
    assert depth == 1
    for l in range(depth):
        lambda_init = 0.8 - 0.6 * math.exp(-0.3 * l)
        q, k, vt, xr, gr = _inproj(x2, attn_norm_g[l][None], w_in[l], tm=1024)
        lw = xr.shape[-1]
        attn, w_up_bf, w_down_bf, w_out_bf = _attention(
            q.reshape(b, s, -1), k.reshape(b, s, -1), vt,
            lambda_q1[l][None], lambda_k1[l][None], lambda_q2[l][None], lambda_k2[l][None],
            subln_g[l][:, None], lambda_init, tq=256, nsub=4, cast_along=(w_up[l], w_down[l], w_out[l]))
        wg, bg = _gate_weights(lru_w_a[l], lru_b_a[l], lru_w_x[l], lru_b_x[l])
        lru = _lru(xr.reshape(b, s, lw), gr.reshape(b, s, lw),
                   jnp.repeat(lru_conv_w[l], SUBLANES, axis=0),
                   jnp.broadcast_to(lru_conv_b[l][None], (SUBLANES, lw)),
                   wg, bg, lru_lambda[l], rows=512)
        x2 = _ffn(x2.reshape(b, s, d), attn, lru, w_out_bf, ffn_norm_g[l][None], w_up_bf,
                  jnp.repeat(ffn_conv_w[l], SUBLANES, axis=0),
                  jnp.broadcast_to(ffn_conv_b[l][None], (SUBLANES, ffn_conv_b.shape[-1])),
                  w_down_bf, final_norm_g[None],
                  tile=512, chunk=256).reshape(b * s, d)
    return x2.reshape(b, s, d)
```

```python
import functools
import math

import jax
import jax.numpy as jnp
from jax import lax
from jax.experimental import pallas as pl
from jax.experimental.pallas import tpu as pltpu

F32 = jnp.float32
BF16 = jnp.bfloat16

N_HEADS = 4
HEAD_DIM = 64
V_DIM = 2 * HEAD_DIM
ATTN_WIDTH = N_HEADS * V_DIM
LRU_CONV_WIDTH = 4
LRU_CONV_LEFT = 2
LRU_C = 8.0
NORM_EPS = 1e-6
LOG2_E = math.log2(math.e)
LANES = 128
SUBLANES = 8
BF16_ROWS = 16
VMEM_LIMIT = 56 * 1024 * 1024


def _rms(x, g):
    return (x * lax.rsqrt(jnp.mean(x * x, axis=-1, keepdims=True) + NORM_EPS)) * g


def _dot(a, b):
    return jnp.dot(a, b, preferred_element_type=F32)


_NT = (((1,), (1,)), ((), ()))


def _inproj_kernel(x_ref, g_ref, w_ref, q_ref, k_ref, vt_ref, xr_ref, gr_ref, wb_ref, wvt_ref):
    aw = ATTN_WIDTH
    d = x_ref.shape[-1]

    @pl.when(pl.program_id(0) == 0)
    def _cast_weights():
        rows = 2 * LANES
        for r in range(0, d, rows):
            w = w_ref[r:r + rows, :]
            wb_ref[r:r + rows, 0:aw] = (w[:, 0:aw] * (HEAD_DIM ** -0.5 * LOG2_E)).astype(BF16)
            wb_ref[r:r + rows, aw:] = w[:, aw:].astype(BF16)
        for c in range(aw // LANES):
            cols = slice(2 * aw + c * LANES, 2 * aw + (c + 1) * LANES)
            wvt_ref[c * LANES:(c + 1) * LANES, :] = w_ref[:, cols].T.astype(BF16)

    hb = _rms(x_ref[...], g_ref[...]).astype(BF16)
    q_ref[...] = _dot(hb, wb_ref[:, 0:aw]).astype(BF16)
    k_ref[...] = _dot(hb, wb_ref[:, aw:2 * aw]).astype(BF16)
    vt_ref[...] = lax.dot_general(wvt_ref[...], hb, _NT, preferred_element_type=F32).astype(BF16)
    lw = xr_ref.shape[-1]
    xr_ref[...] = _dot(hb, wb_ref[:, 3 * aw:3 * aw + lw])
    gr_ref[...] = _dot(hb, wb_ref[:, 3 * aw + lw:3 * aw + 2 * lw])


def _inproj(x2, g, w, tm):
    n, d = x2.shape
    lw = (w.shape[1] - 3 * ATTN_WIDTH) // 2
    row = lambda i: (i, 0)
    const = lambda i: (0, 0)
    return pl.pallas_call(
        _inproj_kernel,
        grid=(n // tm,),
        in_specs=[pl.BlockSpec((tm, d), row),
                  pl.BlockSpec((1, d), const),
                  pl.BlockSpec(w.shape, const, pipeline_mode=pl.Buffered(1))],
        out_specs=[pl.BlockSpec((tm, ATTN_WIDTH), row)] * 2
        + [pl.BlockSpec((ATTN_WIDTH, tm), lambda i: (0, i))]
        + [pl.BlockSpec((tm, lw), row)] * 2,
        out_shape=[jax.ShapeDtypeStruct((n, ATTN_WIDTH), BF16)] * 2
        + [jax.ShapeDtypeStruct((ATTN_WIDTH, n), BF16)]
        + [jax.ShapeDtypeStruct((n, lw), F32)] * 2,
        scratch_shapes=[pltpu.VMEM(w.shape, BF16), pltpu.VMEM((ATTN_WIDTH, d), BF16)],
        compiler_params=pltpu.CompilerParams(
            dimension_semantics=("arbitrary",), vmem_limit_bytes=VMEM_LIMIT),
        name="inproj",
    )(x2, g, w)


def _attn_kernel(lq1_ref, lk1_ref, lq2_ref, lk2_ref, sg_ref, q_ref, k_ref, vt_ref, *rest,
                 tq, nsub, seq, lambda_init):
    nside = (len(rest) - 4 - 3 * nsub) // 2
    side_in, o_ref, side_out = rest[:nside], rest[nside], rest[nside + 1:2 * nside + 1]
    kf_ref, dist_ref, qf_ref, *se_refs = rest[2 * nside + 1:]
    for src, dst in zip(side_in, side_out):
        dst[...] = src[...].astype(dst.dtype)

    h = pl.program_id(1)
    qi = pl.program_id(2)
    s_refs, e_refs, m_refs = se_refs[:nsub], se_refs[nsub:2 * nsub], se_refs[2 * nsub:]
    nblk = seq // tq
    assert nblk & (nblk - 1) == 0 and 9 * nblk <= LANES and tq <= 256
    shift = nblk.bit_length() - 1

    def slope_of(head):
        return LOG2_E * jnp.where(head == 0, 2.0 ** -2, jnp.where(head == 1, 2.0 ** -4,
                                  jnp.where(head == 2, 2.0 ** -6, 2.0 ** -8))).astype(F32)

    @pl.when((pl.program_id(0) == 0) & (h == 0) & (qi == 0))
    def _init():
        lane = lax.broadcasted_iota(jnp.int32, (seq, LANES), 1)
        row = lax.broadcasted_iota(jnp.int32, (seq, LANES), 0)
        grp = lane >> shift
        hit = (row // tq) == (lane & (nblk - 1))
        dj = (row % tq).astype(F32)
        kf_ref[...] = jnp.where(hit & (grp < 6), 1.0,
                                jnp.where(hit & (grp < 9), dj, 0.0)).astype(BF16)
        r = lax.broadcasted_iota(jnp.int32, (tq, tq), 0)
        c = lax.broadcasted_iota(jnp.int32, (tq, tq), 1)
        dist_ref[...] = jnp.abs(r - c).astype(F32)

        lane = lax.broadcasted_iota(jnp.int32, (tq, LANES), 1)
        di = lax.broadcasted_iota(jnp.int32, (tq, LANES), 0).astype(F32)
        grp = lane >> shift
        term = (grp >= 3).astype(jnp.int32) + (grp >= 6).astype(jnp.int32)
        piece = grp - 3 * term

        def factors(idx, carry):
            slope = slope_of(idx // nblk)
            diff = idx % nblk - (lane & (nblk - 1))
            sign = jnp.where(diff > 0, 1.0, jnp.where(diff < 0, -1.0, 0.0))
            fac = jnp.where(term == 0, -slope * sign * di,
                            jnp.where(term == 1, -slope * tq * jnp.abs(diff).astype(F32),
                                      slope * sign))
            hi = fac.astype(BF16)
            rest1 = fac - hi.astype(F32)
            mid = rest1.astype(BF16)
            lo = (rest1 - mid.astype(F32)).astype(BF16)
            qf_ref[idx] = jnp.where(grp >= 9, jnp.zeros_like(hi),
                                    jnp.where(piece == 0, hi, jnp.where(piece == 1, mid, lo)))
            return carry

        lax.fori_loop(0, N_HEADS * nblk, factors, 0)

    lam = (jnp.exp(jnp.sum(lq1_ref[...] * lk1_ref[...], axis=-1, keepdims=True))
           - jnp.exp(jnp.sum(lq2_ref[...] * lk2_ref[...], axis=-1, keepdims=True))
           + lambda_init)
    slope = slope_of(h)
    kaug = jnp.concatenate([k_ref[...], kf_ref[...]], axis=1)
    vt_ones = jnp.concatenate([vt_ref[...], jnp.ones((BF16_ROWS, seq), BF16)], axis=0)
    lane = lax.broadcasted_iota(jnp.int32, (tq, LANES), 1)
    strip = 4 * SUBLANES

    def scores(sb):
        blk = qi * nsub + sb
        qf = qf_ref[h * nblk + blk]
        q = q_ref[sb * tq:(sb + 1) * tq, :]
        zero = jnp.zeros_like(q)
        qaug = jnp.concatenate(
            [jnp.concatenate([jnp.where(lane < HEAD_DIM, q, zero), qf], axis=1),
             jnp.concatenate([jnp.where(lane >= HEAD_DIM, q, zero), qf], axis=1)], axis=0)
        s_ref = s_refs[sb]
        s = lax.dot_general(kaug, qaug, _NT, preferred_element_type=F32)
        s_ref[...] = s
        mx = s[0:strip, :]
        for r in range(strip, seq, strip):
            mx = jnp.maximum(mx, s[r:r + strip, :])
        m_refs[sb][...] = mx
        diag = pl.ds(pl.multiple_of(blk * tq, tq), tq)
        diag_bias = slope * dist_ref[...]
        s_ref[diag, 0:tq] = s_ref[diag, 0:tq] - diag_bias
        s_ref[diag, tq:2 * tq] = s_ref[diag, tq:2 * tq] - diag_bias

    def finish(sb):
        s_ref, e_ref = s_refs[sb], e_refs[sb]
        mx = jnp.max(m_refs[sb][...], axis=0, keepdims=True)
        for r in range(0, seq, strip):
            e_ref[r:r + strip, :] = jnp.exp2(s_ref[r:r + strip, :] - mx).astype(BF16)
        o12 = _dot(vt_ones, e_ref[...])
        norm = o12[V_DIM:V_DIM + 1, :]
        o12 = o12[0:V_DIM, :]
        o = o12[:, 0:tq] * (1.0 / norm[:, 0:tq]) - o12[:, tq:2 * tq] * (lam / norm[:, tq:2 * tq])
        o = o * lax.rsqrt(jnp.mean(o * o, axis=0, keepdims=True) + NORM_EPS)
        o = o * sg_ref[...] * (1.0 - lambda_init)
        o_ref[sb * tq:(sb + 1) * tq, :] = o.T.astype(o_ref.dtype)

    scores(0)
    for sb in range(nsub):
        if sb + 1 < nsub:
            scores(sb + 1)
        finish(sb)


def _attention(q, k, vt, lq1, lk1, lq2, lk2, subln_g, lambda_init, tq, nsub, cast_along=()):
    b, s, _ = q.shape
    vec = lambda bi, h, qi: (0, 0)
    tstep = tq * nsub
    nq = s // tstep
    nsteps = b * N_HEADS * nq

    def row_block(last):
        return lambda bi, h, qi: (jnp.minimum((bi * N_HEADS + h) * nq + qi, last), 0)

    side_in, side_out, side_shapes = [], [], []
    for w in cast_along:
        rows = -(-w.shape[0] // (nsteps * BF16_ROWS)) * BF16_ROWS
        while w.shape[0] % rows:
            rows += BF16_ROWS
        for specs in (side_in, side_out):
            specs.append(pl.BlockSpec((rows, w.shape[1]), row_block(w.shape[0] // rows - 1)))
        side_shapes.append(jax.ShapeDtypeStruct(w.shape, BF16))
    kernel = functools.partial(_attn_kernel, tq=tq, nsub=nsub, seq=s, lambda_init=lambda_init)
    return pl.pallas_call(
        kernel,
        grid=(b, N_HEADS, s // tstep),
        in_specs=[pl.BlockSpec((1, HEAD_DIM), vec)] * 4
        + [pl.BlockSpec((V_DIM, 1), vec),
           pl.BlockSpec((None, tstep, V_DIM), lambda bi, h, qi: (bi, qi, h)),
           pl.BlockSpec((None, s, V_DIM), lambda bi, h, qi: (bi, 0, h)),
           pl.BlockSpec((V_DIM, s), lambda bi, h, qi: (h, bi))] + side_in,
        out_specs=[pl.BlockSpec((None, tstep, V_DIM), lambda bi, h, qi: (bi, qi, h))] + side_out,
        out_shape=[jax.ShapeDtypeStruct((b, s, ATTN_WIDTH), BF16)] + side_shapes,
        scratch_shapes=[pltpu.VMEM((s, LANES), BF16), pltpu.VMEM((tq, tq), F32),
                        pltpu.VMEM((N_HEADS * (s // tq), tq, LANES), BF16)]
        + [pltpu.VMEM((s, 2 * tq), F32)] * nsub + [pltpu.VMEM((s, 2 * tq), BF16)] * nsub
        + [pltpu.VMEM((4 * SUBLANES, 2 * tq), F32)] * nsub,
        compiler_params=pltpu.CompilerParams(
            dimension_semantics=("arbitrary",) * 3, vmem_limit_bytes=VMEM_LIMIT),
        name="attn",
    )(lq1, lk1, lq2, lk2, subln_g, q, k, vt, *cast_along)


def _local_scan(a, u, reverse):
    row = lax.broadcasted_iota(jnp.int32, a.shape, 0)
    for d in (1, 2, 4):
        shift = SUBLANES - d if reverse else d
        valid = (row < SUBLANES - d) if reverse else (row >= d)
        a_s = jnp.where(valid, pltpu.roll(a, shift, 0), 1.0)
        u_s = jnp.where(valid, pltpu.roll(u, shift, 0), 0.0)
        u = a * u_s + u
        a = a * a_s
    return a, u


def _lru_kernel(xr_ref, gr_ref, cw_ref, cb_ref, wg_ref, bg_ref, lam_ref, o_ref,
                xp_ref, xc_ref, gates_ref, h_ref, cum_ref, *, seq, rows):
    width = xr_ref.shape[-1]
    ntile = width // LANES
    grp = seq // SUBLANES
    left = LRU_CONV_LEFT
    right = LRU_CONV_WIDTH - 1 - LRU_CONV_LEFT
    top = left * SUBLANES
    sub = lax.broadcasted_iota(jnp.int32, (SUBLANES, LANES), 0)

    for s in range(SUBLANES):
        blk = xr_ref[s * grp:(s + 1) * grp, :]
        for p in range(ntile):
            xp_ref[p, pl.ds(top + s, grp, stride=SUBLANES), :] = blk[:, p * LANES:(p + 1) * LANES]
    for p in range(ntile):
        for k in range(left):
            src = xp_ref[p, top + (grp - 1 - k) * SUBLANES:top + (grp - k) * SUBLANES, :]
            xp_ref[p, top - (k + 1) * SUBLANES:top - k * SUBLANES, :] = jnp.where(
                sub == 0, 0.0, pltpu.roll(src, 1, 0))
        for k in range(right):
            src = xp_ref[p, top + k * SUBLANES:top + (k + 1) * SUBLANES, :]
            xp_ref[p, top + (grp + k) * SUBLANES:top + (grp + k + 1) * SUBLANES, :] = jnp.where(
                sub == SUBLANES - 1, 0.0, pltpu.roll(src, SUBLANES - 1, 0))

    neg_lam = -lam_ref[...]
    softplus = jnp.maximum(neg_lam, 0.0) + jnp.log1p(jnp.exp(-jnp.abs(neg_lam)))
    rate = LRU_C * softplus
    rate_log2 = -rate * math.log2(math.e)

    nchunk = seq // rows
    for c in range(nchunk):
        r0 = c * rows
        for p in range(ntile):
            cols = slice(p * LANES, (p + 1) * LANES)
            reps = (rows // SUBLANES, 1)
            xc = jnp.tile(cb_ref[:, cols], reps)
            for tap in range(LRU_CONV_WIDTH):
                start = top + r0 + (tap - left) * SUBLANES
                w = jnp.tile(cw_ref[tap * SUBLANES:(tap + 1) * SUBLANES, cols], reps)
                xc = xc + xp_ref[p, start:start + rows, :] * w
            xc_ref[p, r0:r0 + rows, :] = xc

    zero = jnp.zeros((SUBLANES, LANES), F32)
    one = jnp.ones((SUBLANES, LANES), F32)
    nvr = rows // SUBLANES
    tile8 = lambda row: jnp.broadcast_to(row, (SUBLANES, LANES))

    def chunk(c, ends):
        ends = [list(ends[2 * k:2 * k + 2]) for k in range(2 * ntile)]
        base = (pl.multiple_of(c * rows, rows), pl.multiple_of((nchunk - 1 - c) * rows, rows))
        for p in range(ntile):
            for d in range(2):
                gcols = slice(2 * d * LANES, (2 * d + 2) * LANES)
                xcb = xc_ref[p, pl.ds(base[d], rows), :].astype(BF16)
                gates_ref[2 * p + d] = _dot(xcb, wg_ref[p, :, gcols]) + bg_ref[p:p + 1, gcols]
        for step in range(nvr):
            for p in range(ntile):
                cols = slice(p * LANES, (p + 1) * LANES)
                for d in range(2):
                    j = step if d == 0 else nvr - 1 - step
                    row = pl.ds(base[d] + j * SUBLANES, SUBLANES)
                    g = gates_ref[2 * p + d, j * SUBLANES:(j + 1) * SUBLANES, :]
                    th = jnp.tanh(g)
                    r = 0.5 + 0.5 * th[:, 0:LANES]
                    i = 0.5 + 0.5 * th[:, LANES:2 * LANES]
                    t = jnp.tanh(r * tile8(rate[d:d + 1, cols]))
                    tt = t + t
                    prod = tt * (1.0 + t)
                    mult = jnp.where(prod > 0.0, tt * lax.rsqrt(prod), 0.0)
                    a = jnp.exp2(r * tile8(rate_log2[d:d + 1, cols]))
                    u = mult * (i * xc_ref[p, row, :])
                    h, cum = ends[2 * p + d]
                    h = a * h + u
                    cum = a * cum
                    h_ref[d, p, row, :] = h
                    cum_ref[d, p, row, :] = cum
                    ends[2 * p + d] = [h, cum]
        return tuple(x for pair in ends for x in pair)

    flat = lax.fori_loop(0, nchunk, chunk, (zero, one) * (2 * ntile))
    ends = [[(flat[2 * (2 * p + d)], flat[2 * (2 * p + d) + 1]) for d in range(2)]
            for p in range(ntile)]

    enter = []
    for p in range(ntile):
        for d in range(2):
            h_end, cum_end = ends[p][d]
            _, chained = _local_scan(cum_end, h_end, reverse=(d == 1))
            if d == 0:
                enter.append(jnp.where(sub == 0, 0.0, pltpu.roll(chained, 1, 0)))
            else:
                enter.append(jnp.where(sub == SUBLANES - 1, 0.0,
                                       pltpu.roll(chained, SUBLANES - 1, 0)))

    for p in range(ntile):
        for j in range(grp):
            rs = slice(j * SUBLANES, (j + 1) * SUBLANES)
            xc_ref[p, rs, :] = ((h_ref[0, p, rs, :] + cum_ref[0, p, rs, :] * enter[2 * p])
                                + (h_ref[1, p, rs, :] + cum_ref[1, p, rs, :] * enter[2 * p + 1]))

    blk = BF16_ROWS

    def segment(s, carry):
        for p in range(ntile):
            cols = slice(p * LANES, (p + 1) * LANES)
            for j0 in range(0, grp, blk):
                rs = pl.ds(pl.multiple_of(s * grp + j0, blk), blk)
                y = xc_ref[p, pl.ds(s + j0 * SUBLANES, blk, stride=SUBLANES), :]
                o_ref[rs, cols] = (jax.nn.gelu(gr_ref[rs, cols], approximate=True)
                                   * y).astype(o_ref.dtype)
        return carry

    lax.fori_loop(0, SUBLANES, segment, 0)


def _lru(xr, gr, conv_w, conv_b, wg, bg, lru_lambda, rows):
    b, s, width = xr.shape
    ntile = width // LANES
    seqblk = pl.BlockSpec((None, s, width), lambda bi: (bi, 0, 0))
    full = lambda a: pl.BlockSpec(a.shape, lambda bi: (0,) * a.ndim)
    kernel = functools.partial(_lru_kernel, seq=s, rows=rows)
    halo_rows = (LRU_CONV_WIDTH - 1) * SUBLANES
    return pl.pallas_call(
        kernel,
        grid=(b,),
        in_specs=[seqblk, seqblk, full(conv_w), full(conv_b), full(wg), full(bg), full(lru_lambda)],
        out_specs=seqblk,
        out_shape=jax.ShapeDtypeStruct((b, s, width), BF16),
        scratch_shapes=[pltpu.VMEM((ntile, s + halo_rows, LANES), F32),
                        pltpu.VMEM((ntile, s, LANES), F32),
                        pltpu.VMEM((2 * ntile, rows, 2 * LANES), F32),
                        pltpu.VMEM((2, ntile, s, LANES), F32),
                        pltpu.VMEM((2, ntile, s, LANES), F32)],
        compiler_params=pltpu.CompilerParams(
            dimension_semantics=("arbitrary",), vmem_limit_bytes=VMEM_LIMIT),
        name="lru",
    )(xr, gr, conv_w, conv_b, wg, bg, lru_lambda)


def _ffn_kernel(xp_ref, xc_ref, xn_ref, ap_ref, ac_ref, an_ref, lp_ref, lc_ref, ln_ref,
                wout_ref, og_ref, wup_ref, cw_ref, cb_ref, wdn_ref, fg_ref, o_ref,
                perm_ref, hext_ref, ua_ref, ub_ref, act_ref, y_ref, xm_ref,
                *, tile, chunk, d_ff):
    i = pl.program_id(1)
    nchunks = d_ff // chunk
    ngrp = tile // SUBLANES
    nslab = perm_ref.shape[0]
    d = nslab * LANES
    aw = ac_ref.shape[-1]

    halo_rows = BF16_ROWS
    heads = jnp.concatenate([ap_ref[...], ac_ref[...], an_ref[...]], axis=0)
    lrus = jnp.concatenate([lp_ref[...], lc_ref[...], ln_ref[...]], axis=0)
    proj = _dot(heads, wout_ref[0:aw, :]) + _dot(lrus, wout_ref[aw:, :])
    xm_ref[...] = xc_ref[...] + proj[halo_rows:halo_rows + tile, :]
    for s in range(SUBLANES):
        rows = _rms(xm_ref[s * ngrp:(s + 1) * ngrp, :], og_ref[...])
        for n in range(nslab):
            perm_ref[n, pl.ds(s, ngrp, stride=SUBLANES), :] = rows[:, n * LANES:(n + 1) * LANES]
    for n in range(nslab):
        hext_ref[0:tile, n * LANES:(n + 1) * LANES] = perm_ref[n].astype(BF16)
    h_prev = _rms(xp_ref[...] + proj[0:halo_rows, :], og_ref[...])
    h_next = _rms(xn_ref[...] + proj[halo_rows + tile:, :], og_ref[...])
    prev = jnp.where(i > 0, h_prev[BF16_ROWS - 1:BF16_ROWS, :], 0.0)
    nxt = jnp.where(i < pl.num_programs(1) - 1, h_next[0:1, :], 0.0)
    hrow = lax.broadcasted_iota(jnp.int32, (BF16_ROWS, d), 0)
    halo = jnp.where(hrow == 0, prev, jnp.where(hrow == 1, nxt, 0.0))
    hext_ref[tile:tile + BF16_ROWS, :] = halo.astype(BF16)

    def offsets(c):
        og, ov = c * chunk, d_ff + c * chunk
        if isinstance(c, int):
            return og, ov
        return pl.multiple_of(og, chunk), pl.multiple_of(ov, chunk)

    def up(c, u_ref):
        og, ov = offsets(c)
        hext = hext_ref[...]
        u_ref[:, 0:chunk] = _dot(hext, wup_ref[:, pl.ds(og, chunk)])
        u_ref[:, chunk:2 * chunk] = _dot(hext, wup_ref[:, pl.ds(ov, chunk)])

    def glu(c, u_ref, act_ref):
        sub = lax.broadcasted_iota(jnp.int32, (SUBLANES, chunk), 0)
        blk = BF16_ROWS

        def conv(col0, off, r0):
            cols = slice(col0, col0 + chunk)
            reps = (blk // SUBLANES, 1)
            w = [jnp.tile(cw_ref[k * SUBLANES:(k + 1) * SUBLANES, pl.ds(off, chunk)], reps)
                 for k in range(3)]
            bias = jnp.tile(cb_ref[:, pl.ds(off, chunk)], reps)
            cur = u_ref[r0:r0 + blk, cols]
            if r0 == 0:
                first = jnp.where(sub == 0, u_ref[tile:tile + 1, cols],
                                  pltpu.roll(u_ref[tile - SUBLANES:tile, cols], 1, 0))
                um1 = jnp.concatenate([first, cur[0:blk - SUBLANES]], axis=0)
            else:
                um1 = u_ref[r0 - SUBLANES:r0 + blk - SUBLANES, cols]
            if r0 + blk == tile:
                last = jnp.where(sub == SUBLANES - 1, u_ref[tile + 1:tile + 2, cols],
                                 pltpu.roll(u_ref[0:SUBLANES, cols], SUBLANES - 1, 0))
                up1 = jnp.concatenate([cur[SUBLANES:blk], last], axis=0)
            else:
                up1 = u_ref[r0 + SUBLANES:r0 + blk + SUBLANES, cols]
            return bias + um1 * w[0] + cur * w[1] + up1 * w[2]

        og, ov = offsets(c)
        for r0 in range(0, tile, blk):
            gate = conv(0, og, r0)
            val = conv(chunk, ov, r0)
            act_ref[r0:r0 + blk, :] = (jax.nn.gelu(gate, approximate=True) * val).astype(BF16)

    u_refs = (ua_ref, ub_ref)
    up(0, ua_ref)
    for c in range(nchunks):
        if c + 1 < nchunks:
            up(c + 1, u_refs[(c + 1) % 2])
        glu(c, u_refs[c % 2], act_ref.at[:, c * chunk:(c + 1) * chunk])
    y = _dot(act_ref[...], wdn_ref[...])
    for n in range(nslab):
        y_ref[n, 0:tile, :] = y[:, n * LANES:(n + 1) * LANES]
    for s in range(SUBLANES):
        rs = slice(s * ngrp, (s + 1) * ngrp)
        y = jnp.concatenate([y_ref[n, pl.ds(s, ngrp, stride=SUBLANES), :] for n in range(nslab)],
                            axis=1)
        o_ref[rs, :] = _rms(xm_ref[rs, :] + y, fg_ref[...])


def _ffn(x, attn, lru, w_out, out_g, w_up, conv_w, conv_b, w_down, final_g, tile, chunk):
    b, s, d = x.shape
    aw, lw = attn.shape[-1], lru.shape[-1]
    d_ff = w_down.shape[0]
    nh = tile // BF16_ROWS
    last_halo = s // BF16_ROWS - 1
    main = lambda bi, i: (bi, i, 0)
    const = lambda bi, i: (0, 0)
    single = dict(pipeline_mode=pl.Buffered(1))
    kernel = functools.partial(_ffn_kernel, tile=tile, chunk=chunk, d_ff=d_ff)
    return pl.pallas_call(
        kernel,
        grid=(b, s // tile),
        in_specs=[spec for width in (d, aw, lw) for spec in (
                      pl.BlockSpec((None, BF16_ROWS, width),
                                   lambda bi, i: (bi, jnp.maximum(i * nh - 1, 0), 0)),
                      pl.BlockSpec((None, tile, width), main),
                      pl.BlockSpec((None, BF16_ROWS, width),
                                   lambda bi, i: (bi, jnp.minimum((i + 1) * nh, last_halo), 0)))]
        + [pl.BlockSpec(w_out.shape, const, **single),
           pl.BlockSpec((1, d), const),
           pl.BlockSpec(w_up.shape, const, **single),
           pl.BlockSpec(conv_w.shape, const),
           pl.BlockSpec(conv_b.shape, const),
           pl.BlockSpec(w_down.shape, const, **single),
           pl.BlockSpec((1, d), const)],
        out_specs=pl.BlockSpec((None, tile, d), main),
        out_shape=jax.ShapeDtypeStruct((b, s, d), F32),
        scratch_shapes=[pltpu.VMEM((d // LANES, tile, LANES), F32),
                        pltpu.VMEM((tile + BF16_ROWS, d), BF16)]
        + [pltpu.VMEM((tile + BF16_ROWS, 2 * chunk), F32)] * 2
        + [pltpu.VMEM((tile, d_ff), BF16),
           pltpu.VMEM((d // LANES, tile + SUBLANES, LANES), F32),
           pltpu.VMEM((tile, d), F32)],
        compiler_params=pltpu.CompilerParams(
            dimension_semantics=("arbitrary",) * 2, vmem_limit_bytes=VMEM_LIMIT),
        name="ffn",
    )(x, x, x, attn, attn, attn, lru, lru, lru, w_out, out_g, w_up, conv_w, conv_b, w_down, final_g)


def _gate_weights(w_a, b_a, w_x, b_x):
    ndir, nblk, bd, _ = w_a.shape
    per = LANES // bd
    npair = nblk // per

    def blockdiag(w):
        w = w.reshape(npair, per, bd, bd)
        eye = jnp.eye(per, dtype=w.dtype)
        return jnp.einsum('pbij,bc->pbicj', w, eye).reshape(npair, LANES, LANES)

    ws, bs = [], []
    for d in range(ndir):
        for w, bias in ((w_a, b_a), (w_x, b_x)):
            ws.append(blockdiag(w[d]))
            bs.append(bias[d].reshape(npair, LANES))
    return ((0.5 * jnp.concatenate(ws, axis=-1)).astype(BF16),
            (0.5 * jnp.concatenate(bs, axis=-1)).astype(F32))


def kernel(x, attn_norm_g, w_in, lambda_q1, lambda_k1, lambda_q2, lambda_k2, subln_g,
           lru_conv_w, lru_conv_b, lru_w_a, lru_b_a, lru_w_x, lru_b_x, lru_lambda,
           w_out, ffn_norm_g, w_up, ffn_conv_w, ffn_conv_b, w_down, final_norm_g):
    b, s, d = x.shape
    depth = w_in.shape[0]
    x2 = x.reshape(b * s, d)
    assert depth == 1
    for l in range(depth):
        lambda_init = 0.8 - 0.6 * math.exp(-0.3 * l)
        q, k, vt, xr, gr = _inproj(x2, attn_norm_g[l][None], w_in[l], tm=1024)
        lw = xr.shape[-1]
        attn, w_up_bf, w_down_bf, w_out_bf = _attention(
            q.reshape(b, s, -1), k.reshape(b, s, -1), vt,
            lambda_q1[l][None], lambda_k1[l][None], lambda_q2[l][None], lambda_k2[l][None],
            subln_g[l][:, None], lambda_init, tq=256, nsub=4, cast_along=(w_up[l], w_down[l], w_out[l]))
        wg, bg = _gate_weights(lru_w_a[l], lru_b_a[l], lru_w_x[l], lru_b_x[l])
        lru = _lru(xr.reshape(b, s, lw), gr.reshape(b, s, lw),
                   jnp.repeat(lru_conv_w[l], SUBLANES, axis=0),
                   jnp.broadcast_to(lru_conv_b[l][None], (SUBLANES, lw)),
                   wg, bg, lru_lambda[l], rows=512)
        x2 = _ffn(x2.reshape(b, s, d), attn, lru, w_out_bf, ffn_norm_g[l][None], w_up_bf,
                  jnp.repeat(ffn_conv_w[l], SUBLANES, axis=0),
                  jnp.broadcast_to(ffn_conv_b[l][None], (SUBLANES, ffn_conv_b.shape[-1])),
                  w_down_bf, final_norm_g[None],
                  tile=512, chunk=256).reshape(b * s, d)
    return x2.reshape(b, s, d)
```
